```python
import math
import jax, jax.numpy as jnp
from jax import lax
import numpy as np

D_MODEL = 1024
BATCH = 16
SEQ = 4096
DEPTH = 4
DEC_BATCH = 32
DEC_SEQ = 64
PAST_LEN = 2048

CHUNK = 64
N_MIXERS = 3
N_LAYERS_A = (DEPTH + 2) // 3
N_LAYERS_B = (DEPTH + 1) // 3
N_LAYERS_C = DEPTH // 3
EPS = 1e-6

HEAD_DIM = 64
A_HEADS = 16
A_KV_HEADS = 4
IDX_HEADS = 8
IDX_DIM = 64
TOPK_MAX = 256
A_QBLOCK = 64
REL_BUCKETS = 32
REL_MAX_DIST = 128
A_IN = A_HEADS * HEAD_DIM + 2 * A_KV_HEADS * HEAD_DIM + IDX_HEADS * IDX_DIM + IDX_DIM + IDX_HEADS
B_HEADS = 16
B_QBLOCK = 128
B_IN = 4 * B_HEADS * HEAD_DIM + B_HEADS
C_HEADS = 8
C_DK = 128
C_DV = 128
C_BLOCK = 16
C_IN = 2 * C_HEADS * C_DK + 2 * C_HEADS * C_DV
N_GROUPS = 4
EXPERTS_PER_GROUP = 4
N_EXPERTS = N_GROUPS * EXPERTS_PER_GROUP
TOP_K_EXPERTS = 2
D_EXPERT = 512

kernel_name = 'hybrid_dsa_fox_hgrn2_hmoe_stream_step'


def rms_norm(x, gain=None):
    xf = x.astype(jnp.float32)
    y = xf * lax.rsqrt(jnp.mean(xf * xf, axis=-1, keepdims=True) + EPS)
    if gain is not None:
        y = y * gain.astype(jnp.float32)
    return y.astype(x.dtype)


def split_cols(x, sizes):
    out, off = [], 0
    for s in sizes:
        out.append(x[..., off:off + s])
        off += s
    return out


def to_blocks(x, nblk):
    b, t = x.shape[:2]
    return jnp.moveaxis(x.reshape((b, nblk, t // nblk) + x.shape[2:]), 1, 0)


def from_blocks(y):
    nblk, b, blk = y.shape[:3]
    return jnp.moveaxis(y, 0, 1).reshape((b, nblk * blk) + y.shape[3:])


def rel_bucket(rel):
    half = REL_BUCKETS // 2
    max_exact = half // 2
    ret = jnp.where(rel > 0, half, 0)
    n = jnp.abs(rel)
    nf = jnp.maximum(n, 1).astype(jnp.float32)
    large = max_exact + (jnp.log(nf / max_exact) / math.log(REL_MAX_DIST / max_exact) * (half - max_exact)).astype(jnp.int32)
    large = jnp.minimum(large, half - 1)
    return ret + jnp.where(n < max_exact, n, large)


def dsa_mixer(h, past_k, past_v, past_kidx, w_in, q_gain, k_gain, rel_table, w_out):
    f32 = jnp.float32
    b, t, _ = h.shape
    p = past_k.shape[1]
    n_keys = p + t
    topk = min(TOPK_MAX, n_keys // 4)
    group = A_HEADS // A_KV_HEADS
    q, k, v, qi, ki, wi = split_cols(h @ w_in, [A_HEADS * HEAD_DIM, A_KV_HEADS * HEAD_DIM, A_KV_HEADS * HEAD_DIM,
                                                IDX_HEADS * IDX_DIM, IDX_DIM, IDX_HEADS])
    q = rms_norm(q.reshape(b, t, A_HEADS, HEAD_DIM), q_gain)
    k_new = rms_norm(k.reshape(b, t, A_KV_HEADS, HEAD_DIM), k_gain)
    v_new = v.reshape(b, t, A_KV_HEADS, HEAD_DIM)
    qi = qi.reshape(b, t, IDX_HEADS, IDX_DIM)
    k_all = jnp.concatenate([past_k, k_new], axis=1)
    v_all = jnp.concatenate([past_v, v_new], axis=1)
    ki_all = jnp.concatenate([past_kidx, ki], axis=1)
    key_chunk = jnp.arange(n_keys) // CHUNK
    q_pos = p + jnp.arange(t)
    blk = A_QBLOCK if t % A_QBLOCK == 0 else t
    nblk = t // blk
    gather = jax.vmap(lambda rows, idx: rows[idx])

    def block(args):
        qb, qib, wib, posb = args
        s = jnp.einsum('bqhd,bsd->bqhs', qib, ki_all, preferred_element_type=f32) * IDX_DIM ** -0.5
        score = jnp.einsum('bqh,bqhs->bqs', wib.astype(f32) * IDX_HEADS ** -0.5, jax.nn.relu(s))
        admissible = key_chunk[None, :] <= (posb // CHUNK)[:, None]
        score = jnp.where(admissible[None], score, -jnp.inf)
        top_val, idx = lax.top_k(score, topk)
        valid = jnp.isfinite(top_val)
        k_sel = gather(k_all, idx)
        v_sel = gather(v_all, idx)
        bias = rel_table[rel_bucket(idx - posb[None, :, None])]
        bias = bias.reshape(b, blk, topk, A_KV_HEADS, group).transpose(0, 1, 3, 4, 2)
        qg = qb.reshape(b, blk, A_KV_HEADS, group, HEAD_DIM)
        logits = jnp.einsum('bqgrd,bqkgd->bqgrk', qg, k_sel, preferred_element_type=f32) * HEAD_DIM ** -0.5 + bias.astype(f32)
        logits = jnp.where(valid[:, :, None, None, :], logits, -jnp.inf)
        prob = jax.nn.softmax(logits, axis=-1).astype(v_all.dtype)
        o = jnp.einsum('bqgrk,bqkgd->bqgrd', prob, v_sel)
        return o.reshape(b, blk, A_HEADS * HEAD_DIM)

    o = lax.map(block, (to_blocks(q, nblk), to_blocks(qi, nblk), to_blocks(wi, nblk), q_pos.reshape(nblk, blk)))
    y = from_blocks(o) @ w_out
    return y, k_new, v_new, ki


def fox_mixer(h, past_k, past_v, past_logf, w_in, b_f, q_gain, k_gain, w_out):
    f32 = jnp.float32
    b, t, _ = h.shape
    p = past_k.shape[1]
    n_keys = p + t
    hd = B_HEADS * HEAD_DIM
    q, k, v, g, fz = split_cols(h @ w_in, [hd, hd, hd, hd, B_HEADS])
    q = rms_norm(q.reshape(b, t, B_HEADS, HEAD_DIM), q_gain)
    k_new = rms_norm(k.reshape(b, t, B_HEADS, HEAD_DIM), k_gain)
    v_new = v.reshape(b, t, B_HEADS, HEAD_DIM)
    logf_new = jax.nn.log_sigmoid((fz + b_f).astype(f32))
    k_all = jnp.concatenate([past_k, k_new], axis=1)
    v_all = jnp.concatenate([past_v, v_new], axis=1)
    logf_all = jnp.concatenate([past_logf.astype(f32), logf_new], axis=1)
    cum = jnp.cumsum(logf_all, axis=1)
    cum_keys = cum.transpose(0, 2, 1)
    cum_q = cum[:, p:]
    k_pos = jnp.arange(n_keys)
    q_pos = p + jnp.arange(t)
    blk = B_QBLOCK if t % B_QBLOCK == 0 else t
    nblk = t // blk

    def block(args):
        qb, cqb, posb = args
        logits = jnp.einsum('bqhd,bshd->bhqs', qb, k_all, preferred_element_type=f32) * HEAD_DIM ** -0.5
        logits = logits + cqb.transpose(0, 2, 1)[:, :, :, None] - cum_keys[:, :, None, :]
        causal = k_pos[None, :] <= posb[:, None]
        logits = jnp.where(causal[None, None], logits, -jnp.inf)
        prob = jax.nn.softmax(logits, axis=-1).astype(v_all.dtype)
        return jnp.einsum('bhqs,bshd->bqhd', prob, v_all)

    o = from_blocks(lax.map(block, (to_blocks(q, nblk), to_blocks(cum_q, nblk), q_pos.reshape(nblk, blk))))
    o = o * jax.nn.sigmoid(g.reshape(b, t, B_HEADS, HEAD_DIM))
    y = o.reshape(b, t, hd) @ w_out
    return y, k_new, v_new, logf_new.astype(past_logf.dtype)


def hgrn2_mixer(h, s0, w_in, lb, out_gain, w_out):
    f32 = jnp.float32
    b, t, _ = h.shape
    q, fz, inp, g = split_cols(h @ w_in, [C_HEADS * C_DK, C_HEADS * C_DK, C_HEADS * C_DV, C_HEADS * C_DV])
    lb = lb.reshape(C_HEADS, C_DK)
    z = fz.reshape(b, t, C_HEADS, C_DK).astype(f32)
    logf = jnp.logaddexp(jnp.log(lb), jnp.log1p(-lb) + jax.nn.log_sigmoid(z))
    kk = (1.0 - lb) * jax.nn.sigmoid(-z)
    q = q.reshape(b, t, C_HEADS, C_DK).astype(f32)
    v = inp.reshape(b, t, C_HEADS, C_DV).astype(f32)
    pad = (-t) % C_BLOCK
    nc = (t + pad) // C_BLOCK

    def chunks(a):
        a = jnp.pad(a, ((0, 0), (0, pad), (0, 0), (0, 0)))
        return a.reshape(b, nc, C_BLOCK, C_HEADS, a.shape[-1]).transpose(1, 0, 3, 2, 4)

    tril = jnp.tril(jnp.ones((C_BLOCK, C_BLOCK), bool))

    def step(S, xs):
        qc, kc, vc, lfc = xs
        A = jnp.cumsum(lfc, axis=2)
        o_inter = jnp.einsum('bhcd,bhde->bhce', qc * jnp.exp(A), S)
        diff = A[:, :, :, None, :] - A[:, :, None, :, :]
        decay = jnp.exp(jnp.where(tril[None, None, :, :, None], diff, -jnp.inf))
        scores = jnp.einsum('bhtd,bhtsd,bhsd->bhts', qc, decay, kc)
        o = o_inter + jnp.einsum('bhts,bhse->bhte', scores, vc)
        A_last = A[:, :, -1:, :]
        S = jnp.exp(A_last[:, :, 0, :])[..., None] * S + jnp.einsum('bhsd,bhse->bhde', kc * jnp.exp(A_last - A), vc)
        return S, o

    s_final, o = lax.scan(step, s0.astype(f32), (chunks(q), chunks(kk), chunks(v), chunks(logf)))
    o = o.transpose(1, 0, 3, 2, 4).reshape(b, nc * C_BLOCK, C_HEADS, C_DV)[:, :t]
    o = rms_norm(o, out_gain) * jax.nn.silu(g.reshape(b, t, C_HEADS, C_DV).astype(f32))
    y = o.astype(h.dtype).reshape(b, t, C_HEADS * C_DV) @ w_out
    return y, s_final.astype(s0.dtype)


def hier_moe(h, w_group, b_group, w_expert, b_expert, w_gate, w_up, w_down):
    f32 = jnp.float32
    b, t, _ = h.shape
    lg = (h @ w_group + b_group).astype(f32)
    grp = jnp.argmax(lg, axis=-1)
    grp_oh = jax.nn.one_hot(grp, N_GROUPS, dtype=f32)
    p_grp = jnp.sum(jax.nn.softmax(lg, axis=-1) * grp_oh, axis=-1, keepdims=True)
    le = (h @ w_expert + b_expert).astype(f32).reshape(b, t, N_GROUPS, EXPERTS_PER_GROUP)
    le_sel = jnp.einsum('btge,btg->bte', le, grp_oh)
    top_val, top_idx = lax.top_k(le_sel, TOP_K_EXPERTS)
    w_sel = jax.nn.softmax(top_val, axis=-1) * p_grp
    expert_id = grp[..., None] * EXPERTS_PER_GROUP + top_idx
    gates = jnp.einsum('btk,btke->bte', w_sel, jax.nn.one_hot(expert_id, N_EXPERTS, dtype=f32)).astype(h.dtype)
    y = jnp.zeros_like(h)
    for e in range(N_EXPERTS):
        he = jax.nn.silu(h @ w_gate[e]) * (h @ w_up[e])
        y = y + gates[..., e:e + 1] * (he @ w_down[e])
    return y


def trunk(x, c, a_k, a_v, a_kidx, b_k, b_v, b_logf, c_state, prm):
    lb_all = jnp.cumsum(jax.nn.softmax(prm['c_lower_bound'].astype(jnp.float32), axis=0), axis=0)
    lb_all = lb_all - lb_all[0]
    out_a_k, out_a_v, out_a_kidx = [], [], []
    out_b_k, out_b_v, out_b_logf = [], [], []
    out_c = []
    for i in range(DEPTH):
        j = i // N_MIXERS
        kind = i % N_MIXERS
        mod = (c @ prm['w_ada'][i] + prm['b_ada'][i])[:, None, :]
        sh1, sc1, g1, sh2, sc2, g2 = jnp.split(mod, 6, axis=-1)
        h = rms_norm(x, prm['norm_mix'][i]) * (1 + sc1) + sh1
        if kind == 0:
            y, kn, vn, kin = dsa_mixer(h, a_k[j], a_v[j], a_kidx[j], prm['a_w_in'][j], prm['a_q_norm'][j],
                                       prm['a_k_norm'][j], prm['rel_table'], prm['a_w_out'][j])
            out_a_k.append(kn)
            out_a_v.append(vn)
            out_a_kidx.append(kin)
        elif kind == 1:
            y, kn, vn, lfn = fox_mixer(h, b_k[j], b_v[j], b_logf[j], prm['b_w_in'][j], prm['b_forget_bias'][j],
                                       prm['b_q_norm'][j], prm['b_k_norm'][j], prm['b_w_out'][j])
            out_b_k.append(kn)
            out_b_v.append(vn)
            out_b_logf.append(lfn)
        else:
            y, sn = hgrn2_mixer(h, c_state[j], prm['c_w_in'][j], lb_all[i], prm['c_out_norm'][j], prm['c_w_out'][j])
            out_c.append(sn)
        x = x + g1 * y
        h = rms_norm(x, prm['norm_ffn'][i]) * (1 + sc2) + sh2
        x = x + g2 * hier_moe(h, prm['moe_w_group'][i], prm['moe_b_group'][i], prm['moe_w_expert'][i],
                              prm['moe_b_expert'][i], prm['moe_w_gate'][i], prm['moe_w_up'][i], prm['moe_w_down'][i])
    return (x, jnp.stack(out_a_k), jnp.stack(out_a_v), jnp.stack(out_a_kidx),
            jnp.stack(out_b_k), jnp.stack(out_b_v), jnp.stack(out_b_logf), jnp.stack(out_c))


def setup_inputs(seed: int = 0) -> dict:
    key = jax.random.key(seed)
    ks = iter(jax.random.split(key, 64))

    def nrm(shape, scale=1.0):
        return scale * jax.random.normal(next(ks), shape, jnp.float32)

    def unif(shape, lo, hi):
        return jax.random.uniform(next(ks), shape, jnp.float32, lo, hi)

    D = D_MODEL
    return {
        'x_prompt': nrm((BATCH, SEQ, D)),
        'x_sample': nrm((DEC_BATCH, DEC_SEQ, D)),
        'cache_a_k': nrm((N_LAYERS_A, DEC_BATCH, PAST_LEN, A_KV_HEADS, HEAD_DIM)),
        'cache_a_v': nrm((N_LAYERS_A, DEC_BATCH, PAST_LEN, A_KV_HEADS, HEAD_DIM)),
        'cache_a_kidx': nrm((N_LAYERS_A, DEC_BATCH, PAST_LEN, IDX_DIM)),
        'cache_b_k': nrm((N_LAYERS_B, DEC_BATCH, PAST_LEN, B_HEADS, HEAD_DIM)),
        'cache_b_v': nrm((N_LAYERS_B, DEC_BATCH, PAST_LEN, B_HEADS, HEAD_DIM)),
        'cache_b_logf': jax.nn.log_sigmoid(unif((N_LAYERS_B, 1, 1, B_HEADS), 1.0, 4.0)
                                           + nrm((N_LAYERS_B, DEC_BATCH, PAST_LEN, B_HEADS), 0.3)),
        'state_c': nrm((N_LAYERS_C, DEC_BATCH, C_HEADS, C_DK, C_DV), 0.5),
        'c_prompt': nrm((BATCH, D)),
        'c_sample': nrm((DEC_BATCH, D)),
        'rel_table': nrm((REL_BUCKETS, A_HEADS), 0.5),
        'w_ada': nrm((DEPTH, D, 6 * D), 0.3 * D ** -0.5),
        'b_ada': nrm((DEPTH, 6 * D), 0.02),
        'norm_mix': 1.0 + nrm((DEPTH, D), 0.05),
        'norm_ffn': 1.0 + nrm((DEPTH, D), 0.05),
        'a_w_in': nrm((N_LAYERS_A, D, A_IN), D ** -0.5),
        'a_q_norm': 1.0 + nrm((N_LAYERS_A, HEAD_DIM), 0.05),
        'a_k_norm': 1.0 + nrm((N_LAYERS_A, HEAD_DIM), 0.05),
        'a_w_out': nrm((N_LAYERS_A, A_HEADS * HEAD_DIM, D), (A_HEADS * HEAD_DIM) ** -0.5),
        'b_w_in': nrm((N_LAYERS_B, D, B_IN), D ** -0.5),
        'b_forget_bias': unif((N_LAYERS_B, B_HEADS), 1.0, 4.0),
        'b_q_norm': 1.0 + nrm((N_LAYERS_B, HEAD_DIM), 0.05),
        'b_k_norm': 1.0 + nrm((N_LAYERS_B, HEAD_DIM), 0.05),
        'b_w_out': nrm((N_LAYERS_B, B_HEADS * HEAD_DIM, D), (B_HEADS * HEAD_DIM) ** -0.5),
        'c_w_in': nrm((N_LAYERS_C, D, C_IN), D ** -0.5),
        'c_lower_bound': nrm((DEPTH, C_HEADS * C_DK)),
        'c_out_norm': 1.0 + nrm((N_LAYERS_C, C_DV), 0.05),
        'c_w_out': nrm((N_LAYERS_C, C_HEADS * C_DV, D), (C_HEADS * C_DV) ** -0.5),
        'moe_w_group': nrm((DEPTH, D, N_GROUPS), D ** -0.5),
        'moe_b_group': nrm((DEPTH, N_GROUPS), 0.01),
        'moe_w_expert': nrm((DEPTH, D, N_EXPERTS), D ** -0.5),
        'moe_b_expert': nrm((DEPTH, N_EXPERTS), 0.01),
        'moe_w_gate': nrm((DEPTH, N_EXPERTS, D, D_EXPERT), D ** -0.5),
        'moe_w_up': nrm((DEPTH, N_EXPERTS, D, D_EXPERT), D ** -0.5),
        'moe_w_down': nrm((DEPTH, N_EXPERTS, D_EXPERT, D), D_EXPERT ** -0.5),
    }


def reference(x_prompt, x_sample, cache_a_k, cache_a_v, cache_a_kidx, cache_b_k, cache_b_v, cache_b_logf, state_c,
              c_prompt, c_sample, rel_table, w_ada, b_ada, norm_mix, norm_ffn, a_w_in, a_q_norm, a_k_norm, a_w_out,
              b_w_in, b_forget_bias, b_q_norm, b_k_norm, b_w_out, c_w_in, c_lower_bound, c_out_norm, c_w_out,
              moe_w_group, moe_b_group, moe_w_expert, moe_b_expert, moe_w_gate, moe_w_up, moe_w_down):
    prm = {'rel_table': rel_table, 'w_ada': w_ada, 'b_ada': b_ada, 'norm_mix': norm_mix, 'norm_ffn': norm_ffn,
           'a_w_in': a_w_in, 'a_q_norm': a_q_norm, 'a_k_norm': a_k_norm, 'a_w_out': a_w_out,
           'b_w_in': b_w_in, 'b_forget_bias': b_forget_bias, 'b_q_norm': b_q_norm, 'b_k_norm': b_k_norm,
           'b_w_out': b_w_out, 'c_w_in': c_w_in, 'c_lower_bound': c_lower_bound, 'c_out_norm': c_out_norm,
           'c_w_out': c_w_out, 'moe_w_group': moe_w_group, 'moe_b_group': moe_b_group,
           'moe_w_expert': moe_w_expert, 'moe_b_expert': moe_b_expert, 'moe_w_gate': moe_w_gate,
           'moe_w_up': moe_w_up, 'moe_w_down': moe_w_down}
    bp = x_prompt.shape[0]
    dt = x_prompt.dtype
    (y_prompt, new_a_k_prompt, new_a_v_prompt, new_a_kidx_prompt, new_b_k_prompt, new_b_v_prompt,
     new_b_logf_prompt, new_c_state_prompt) = trunk(
        x_prompt, c_prompt,
        jnp.zeros((N_LAYERS_A, bp, 0, A_KV_HEADS, HEAD_DIM), dt),
        jnp.zeros((N_LAYERS_A, bp, 0, A_KV_HEADS, HEAD_DIM), dt),
        jnp.zeros((N_LAYERS_A, bp, 0, IDX_DIM), dt),
        jnp.zeros((N_LAYERS_B, bp, 0, B_HEADS, HEAD_DIM), dt),
        jnp.zeros((N_LAYERS_B, bp, 0, B_HEADS, HEAD_DIM), dt),
        jnp.zeros((N_LAYERS_B, bp, 0, B_HEADS), cache_b_logf.dtype),
        jnp.zeros((N_LAYERS_C, bp, C_HEADS, C_DK, C_DV), state_c.dtype),
        prm)
    (y_sample, new_a_k_sample, new_a_v_sample, new_a_kidx_sample, new_b_k_sample, new_b_v_sample,
     new_b_logf_sample, new_c_state_sample) = trunk(
        x_sample, c_sample, cache_a_k, cache_a_v, cache_a_kidx, cache_b_k, cache_b_v, cache_b_logf, state_c, prm)
    return (y_prompt, y_sample,
            new_a_k_prompt, new_a_v_prompt, new_a_kidx_prompt,
            new_a_k_sample, new_a_v_sample, new_a_kidx_sample,
            new_b_k_prompt, new_b_v_prompt, new_b_logf_prompt,
            new_b_k_sample, new_b_v_sample, new_b_logf_sample,
            new_c_state_prompt, new_c_state_sample)
```

```python
import functools

import jax
import jax.numpy as jnp
from jax import lax
from jax.experimental import pallas as pl
from jax.experimental.pallas import tpu as pltpu

F32 = jnp.float32
BF16 = jnp.bfloat16
I32 = jnp.int32

LANES = 128
VMEM_LIMIT_BYTES = 56 * 1024 * 1024

DEPTH = 4
N_MIXERS = 3
CHUNK = 64
EPS = 1e-6
HEAD_DIM = 64
A_HEADS = 16
A_KV_HEADS = 4
IDX_HEADS = 8
IDX_DIM = 64
TOPK_MAX = 256
REL_BUCKETS = 32
B_HEADS = 16
C_HEADS = 8
C_DK = 128
C_DV = 128
N_GROUPS = 4
EXPERTS_PER_GROUP = 4
N_EXPERTS = 16
D_EXPERT = 512

NEG_BIG = -1e30
M_FLOOR = -1e20
INT_MIN = -2 ** 31
KEY_TILE = 512
PAD_SLABS = 3


def _cparams(*sem):
    return pltpu.CompilerParams(dimension_semantics=sem, vmem_limit_bytes=VMEM_LIMIT_BYTES)


def _nt(a, b):
    return lax.dot_general(a, b, (((1,), (1,)), ((), ())), preferred_element_type=F32)


def _split3(x):
    hi = x.astype(BF16)
    r = x - hi.astype(F32)
    mid = r.astype(BF16)
    lo = (r - mid.astype(F32)).astype(BF16)
    return hi, mid, lo


def _dot(a, b):
    return jnp.dot(a, b, preferred_element_type=F32)


def _dot_x01(x, m01):
    hi, mid, lo = _split3(x)
    return _dot(hi, m01) + _dot(mid, m01) + _dot(lo, m01)


def _dot_01x(m01, x):
    hi, mid, lo = _split3(x)
    return _dot(m01, hi) + _dot(m01, mid) + _dot(m01, lo)


def _dot_f32(a, b):
    ah, am, al = _split3(a)
    bh, bm, bl = _split3(b)
    return _dot(ah, bh) + (_dot(ah, bm) + _dot(am, bh)) + (_dot(ah, bl) + _dot(am, bm) + _dot(al, bh))


def _mod_kernel(c_ref, w_ref, b_ref, o_ref):
    o_ref[0] = _dot(c_ref[...], w_ref[0]) + b_ref[0]


def ada_mod(c, w_ada, b_ada):
    nl, d, n6 = w_ada.shape
    bsz = c.shape[0]
    tn = 512
    return pl.pallas_call(
        _mod_kernel,
        grid=(nl, n6 // tn),
        in_specs=[pl.BlockSpec((bsz, d), lambda l, j: (0, 0)),
                  pl.BlockSpec((1, d, tn), lambda l, j: (l, 0, j)),
                  pl.BlockSpec((1, 1, tn), lambda l, j: (l, 0, j))],
        out_specs=pl.BlockSpec((1, bsz, tn), lambda l, j: (l, 0, j)),
        out_shape=jax.ShapeDtypeStruct((nl, bsz, n6), F32),
        compiler_params=_cparams("parallel", "parallel"),
        name="ada_mod",
    )(c, w_ada, b_ada.reshape(nl, 1, n6))


def _ln_mod(x, gain, sc, sh):
    ms = jnp.mean(x * x, axis=-1, keepdims=True)
    return (x * lax.rsqrt(ms + EPS) * gain) * (1.0 + sc) + sh


def _ln_proj_kernel(x_ref, gain_ref, sc_ref, sh_ref, w_ref, cgain_ref, cflag_ref, bd_ref, o_ref, h_ref,
                    *, n_norm_tiles, tn):
    j = pl.program_id(2)

    @pl.when(j == 0)
    def _():
        h_ref[...] = _ln_mod(x_ref[0], gain_ref[...], sc_ref[0], sh_ref[0]).astype(BF16)

    y = _dot(h_ref[...], w_ref[...])

    def plain():
        o_ref[0] = y

    def normed():
        y2 = y * y
        hi = y2.astype(BF16)
        lo = (y2 - hi.astype(F32)).astype(BF16)
        bd = bd_ref[...]
        segs = []
        for s in range(tn // LANES):
            sl = slice(s * LANES, (s + 1) * LANES)
            segs.append(_dot(hi[:, sl], bd) + _dot(lo[:, sl], bd))
        seg = jnp.concatenate(segs, axis=1)
        yn = y * lax.rsqrt(seg * (1.0 / HEAD_DIM) + EPS) * cgain_ref[...]
        o_ref[0] = jnp.where(cflag_ref[...] > 0.0, yn, y)

    if n_norm_tiles == 0:
        plain()
    else:
        pl.when(j < n_norm_tiles)(normed)
        pl.when(j >= n_norm_tiles)(plain)


def ln_proj(x, gain, sc, sh, w, cgain, cflag, n_norm_tiles, tn=256):
    bsz, t, d = x.shape
    npad = w.shape[1]
    tm = min(t, 1024)
    bd = (jnp.arange(LANES)[:, None] // HEAD_DIM == jnp.arange(LANES)[None, :] // HEAD_DIM).astype(BF16)
    kern = functools.partial(_ln_proj_kernel, n_norm_tiles=n_norm_tiles, tn=tn)
    return pl.pallas_call(
        kern,
        grid=(bsz, t // tm, npad // tn),
        in_specs=[pl.BlockSpec((1, tm, d), lambda b, i, j: (b, i, 0)),
                  pl.BlockSpec((1, d), lambda b, i, j: (0, 0)),
                  pl.BlockSpec((1, 1, d), lambda b, i, j: (b, 0, 0)),
                  pl.BlockSpec((1, 1, d), lambda b, i, j: (b, 0, 0)),
                  pl.BlockSpec((d, tn), lambda b, i, j: (0, j)),
                  pl.BlockSpec((1, tn), lambda b, i, j: (0, j)),
                  pl.BlockSpec((1, tn), lambda b, i, j: (0, j)),
                  pl.BlockSpec((LANES, LANES), lambda b, i, j: (0, 0))],
        out_specs=pl.BlockSpec((1, tm, tn), lambda b, i, j: (b, i, j)),
        out_shape=jax.ShapeDtypeStruct((bsz, t, npad), F32),
        scratch_shapes=[pltpu.VMEM((tm, d), BF16)],
        compiler_params=_cparams("parallel", "parallel", "arbitrary"),
        name="ln_proj",
    )(x, gain.reshape(1, d), sc, sh, w, cgain, cflag, bd)


def _out_proj_kernel(a_ref, w_ref, x_ref, g_ref, o_ref):
    y = _dot(a_ref[0].astype(BF16), w_ref[...])
    o_ref[0] = x_ref[0] + g_ref[0] * y


def out_proj_residual(a, w, x, gate):
    bsz, t, k = a.shape
    d = w.shape[1]
    tm = min(t, 512)
    return pl.pallas_call(
        _out_proj_kernel,
        grid=(bsz, t // tm),
        in_specs=[pl.BlockSpec((1, tm, k), lambda b, i: (b, i, 0)),
                  pl.BlockSpec((k, d), lambda b, i: (0, 0)),
                  pl.BlockSpec((1, tm, d), lambda b, i: (b, i, 0)),
                  pl.BlockSpec((1, 1, d), lambda b, i: (b, 0, 0))],
        out_specs=pl.BlockSpec((1, tm, d), lambda b, i: (b, i, 0)),
        out_shape=jax.ShapeDtypeStruct((bsz, t, d), F32),
        compiler_params=_cparams("parallel", "parallel"),
        name="out_proj",
    )(a, w, x, gate)


def _dsa_index_kernel(qi_ref, w_ref, ki_ref, o_ref, key_ref, *, tq, n_slabs, q_off, topk, idx_bits):
    a = pl.program_id(1)
    tk = KEY_TILE
    spt = tk // LANES
    q0 = q_off + a * tq
    n_kt = (q0 + tq + tk - 1) // tk
    lane = lax.broadcasted_iota(I32, (tq, LANES), 1)
    half = lane < IDX_DIM
    row = lax.broadcasted_iota(I32, (tq, tk), 0)
    col = lax.broadcasted_iota(I32, (tq, tk), 1)
    qchunk = (q0 + row) >> 6
    w = w_ref[0] * (IDX_HEADS ** -0.5)
    qs = []
    for p in range(IDX_HEADS // 2):
        qp = qi_ref[0, :, p * LANES:(p + 1) * LANES] * (IDX_DIM ** -0.5)
        qs.append(jnp.where(half, qp, 0.0).astype(BF16))
        qs.append(jnp.where(half, 0.0, qp).astype(BF16))

    def score_tile(c, carry):
        kt = ki_ref[0, pl.ds(pl.multiple_of(c * tk, tk), tk), :]
        sc = jnp.zeros((tq, tk), F32)
        for h in range(IDX_HEADS):
            s = _nt(qs[h], kt)
            sc = sc + w[:, IDX_DIM + h:IDX_DIM + h + 1] * jnp.maximum(s, 0.0)
        bits = lax.bitcast_convert_type(sc, I32)
        key = jnp.where(bits < 0, bits ^ 0x7FFFFFFF, bits)
        key = jnp.where(sc == 0.0, 0, key)
        adm = ((c * tk + col) >> 6) <= qchunk
        key = jnp.where(adm, key, INT_MIN)
        for s_ in range(spt):
            key_ref[c * spt + s_] = key[:, s_ * LANES:(s_ + 1) * LANES]
        return carry

    lax.fori_loop(0, n_kt, score_tile, 0)

    def count(pred):
        def body(c, acc):
            for s_ in range(spt):
                sidx = c * spt + s_
                acc = acc + jnp.where(pred(key_ref[sidx], sidx), 1.0, 0.0)
            return acc
        acc = lax.fori_loop(0, n_kt, body, jnp.zeros((tq, LANES), F32))
        return jnp.sum(acc, axis=-1, keepdims=True)

    kf = float(topk)

    def bit_body(i, t_u):
        cand_u = t_u | lax.shift_left(jnp.int32(1), 31 - i)
        cand_s = jnp.broadcast_to(cand_u ^ INT_MIN, (tq, LANES))
        cnt = count(lambda k, s: k >= cand_s)
        return jnp.where(cnt >= kf, cand_u, t_u)

    t_u = lax.fori_loop(0, 32, bit_body, jnp.zeros((tq, 1), I32))
    thr1 = t_u ^ INT_MIN
    thr = jnp.broadcast_to(thr1, (tq, LANES))
    cnt_ge = count(lambda k, s: k >= thr)
    cnt_gt = count(lambda k, s: k > thr)
    need = kf - cnt_gt
    excess = jnp.where((cnt_ge > kf) & (thr1 > INT_MIN), 1.0, 0.0)

    def tie_search():
        def j_body(i, j):
            cand = j | lax.shift_left(jnp.int32(1), idx_bits - 1 - i)
            cand_b = jnp.broadcast_to(cand, (tq, LANES))
            c = count(lambda k, s: (k == thr) & (lane + s * LANES < cand_b))
            return jnp.where(c < need, cand, j)
        return lax.fori_loop(0, idx_bits, j_body, jnp.zeros((tq, 1), I32))

    j_last = lax.cond(jnp.max(excess) > 0.0, tie_search, lambda: jnp.full((tq, 1), 2 ** idx_bits, I32))
    j_b = jnp.broadcast_to(j_last, (tq, LANES))

    def write_active(c, carry):
        for s_ in range(spt):
            sidx = c * spt + s_
            k = key_ref[sidx]
            sel = (k > thr) | ((k == thr) & (lane + sidx * LANES <= j_b))
            sel = sel & (k > INT_MIN)
            o_ref[0, PAD_SLABS + sidx] = jnp.where(sel, 0.0, NEG_BIG)
        return carry

    lax.fori_loop(0, n_kt, write_active, 0)
    neg = jnp.full((tq, LANES), NEG_BIG, F32)

    def write_inactive(s, carry):
        o_ref[0, s] = neg
        return carry

    lax.fori_loop(PAD_SLABS + n_kt * spt, PAD_SLABS + n_slabs, write_inactive, 0)
    for s in range(PAD_SLABS):
        o_ref[0, s] = neg


def dsa_index_mask(proj, kidup, *, t, q_off, topk, qi_blk, w_blk):
    bsz = proj.shape[0]
    lp = kidup.shape[1]
    n_slabs = lp // LANES
    tq = min(t, 128)
    idx_bits = max(1, (lp - 1).bit_length())
    kern = functools.partial(_dsa_index_kernel, tq=tq, n_slabs=n_slabs, q_off=q_off, topk=topk, idx_bits=idx_bits)
    return pl.pallas_call(
        kern,
        grid=(bsz, t // tq),
        in_specs=[pl.BlockSpec((1, tq, 4 * LANES), lambda b, a: (b, a, qi_blk)),
                  pl.BlockSpec((1, tq, LANES), lambda b, a: (b, a, w_blk)),
                  pl.BlockSpec((1, lp, LANES), lambda b, a: (b, 0, 0))],
        out_specs=pl.BlockSpec((1, n_slabs + PAD_SLABS, tq, LANES), lambda b, a: (b, 0, a, 0)),
        out_shape=jax.ShapeDtypeStruct((bsz, n_slabs + PAD_SLABS, t, LANES), F32),
        scratch_shapes=[pltpu.VMEM((n_slabs, tq, LANES), I32)],
        compiler_params=_cparams("parallel", "parallel"),
        name="dsa_index",
    )(proj, proj, kidup)


def _bias_kernel(tab_ref, o_ref):
    h = pl.program_id(0)
    i = lax.broadcasted_iota(I32, (LANES, KEY_TILE), 0)
    j = lax.broadcasted_iota(I32, (LANES, KEY_TILE), 1)
    rel = j - PAD_SLABS * LANES - i
    n = jnp.abs(rel)
    large = jnp.full(rel.shape, 8, I32)
    for th in (12, 16, 23, 32, 46, 64, 91):
        large = large + jnp.where(n >= th, 1, 0)
    bucket = jnp.where(rel > 0, REL_BUCKETS // 2, 0) + jnp.where(n < 8, n, large)
    acc = jnp.zeros(rel.shape, F32)
    for bk in range(REL_BUCKETS):
        acc = jnp.where(bucket == bk, tab_ref[bk, h], acc)
    o_ref[0] = acc


def dsa_bias_tile(rel_table):
    return pl.pallas_call(
        _bias_kernel,
        grid=(A_HEADS,),
        in_specs=[pl.BlockSpec(memory_space=pltpu.SMEM)],
        out_specs=pl.BlockSpec((1, LANES, KEY_TILE), lambda h: (h, 0, 0)),
        out_shape=jax.ShapeDtypeStruct((A_HEADS, LANES, KEY_TILE), F32),
        compiler_params=_cparams("parallel"),
        name="dsa_bias",
    )(rel_table)


def _softmax_tile(qe, kt, vt, add, m, l):
    s = _nt(qe, kt) + add
    m_new = jnp.maximum(m, jnp.max(s, axis=-1, keepdims=True))
    p = jnp.exp(s - m_new)
    alpha = jnp.exp(m - m_new)
    l_new = alpha * l + jnp.sum(p, axis=-1, keepdims=True)
    return m_new, l_new, alpha, _dot(p.astype(BF16), vt)


def _dsa_attn_kernel(cfar_ref, q_ref, k_ref, v_ref, m_ref, b_ref, o_ref, *, tq, q_off):
    a = pl.program_id(1)
    g = pl.program_id(2)
    tk = KEY_TILE
    spt = tk // LANES
    group = A_HEADS // A_KV_HEADS
    ap = (q_off + a * tq) // LANES
    n_far = (ap + spt - 1) // spt
    lane = lax.broadcasted_iota(I32, (tq, LANES), 1)
    half = lane < HEAD_DIM
    qe = []
    for e in range(group):
        qc = q_ref[0, :, (e // 2) * LANES:(e // 2 + 1) * LANES] * (HEAD_DIM ** -0.5)
        qe.append((jnp.where(half, qc, 0.0) if e % 2 == 0 else jnp.where(half, 0.0, qc)).astype(BF16))

    def update(state, kt, vt, adds):
        ms, ls, accs = state
        ms, ls, accs = list(ms), list(ls), list(accs)
        for c in range(group // 2):
            r0 = _softmax_tile(qe[2 * c], kt, vt, adds[2 * c], ms[2 * c], ls[2 * c])
            r1 = _softmax_tile(qe[2 * c + 1], kt, vt, adds[2 * c + 1], ms[2 * c + 1], ls[2 * c + 1])
            ms[2 * c], ls[2 * c] = r0[0], r0[1]
            ms[2 * c + 1], ls[2 * c + 1] = r1[0], r1[1]
            accs[c] = jnp.where(half, r0[2], r1[2]) * accs[c] + jnp.where(half, r0[3], r1[3])
        return tuple(ms), tuple(ls), tuple(accs)

    def far_tile(t, state):
        r0 = pl.multiple_of(t * tk, tk)
        kt = k_ref[0, pl.ds(r0, tk), :]
        vt = v_ref[0, pl.ds(r0, tk), :]
        slabs = [jnp.where(t * spt + s < ap, m_ref[0, t * spt + s], NEG_BIG) for s in range(spt)]
        addm = jnp.concatenate(slabs, axis=1)
        adds = [addm + cfar_ref[g * group + e] for e in range(group)]
        return update(state, kt, vt, adds)

    init = (tuple(jnp.full((tq, 1), M_FLOOR, F32) for _ in range(group)),
            tuple(jnp.zeros((tq, 1), F32) for _ in range(group)),
            tuple(jnp.zeros((tq, LANES), F32) for _ in range(group // 2)))
    state = lax.fori_loop(0, n_far, far_tile, init)

    r0 = pl.multiple_of(ap * LANES, LANES)
    kt = k_ref[0, pl.ds(r0, tk), :]
    vt = v_ref[0, pl.ds(r0, tk), :]
    addm = jnp.concatenate([m_ref[0, ap + s] for s in range(spt)], axis=1)
    adds = [addm + b_ref[e, 0:tq, :] for e in range(group)]
    ms, ls, accs = update(state, kt, vt, adds)
    for c in range(group // 2):
        o_ref[0, :, c * LANES:(c + 1) * LANES] = accs[c] / jnp.where(half, ls[2 * c], ls[2 * c + 1])


def dsa_attention(proj, kdup, vdup, mask, bias4, cfar, *, t, q_off):
    bsz = proj.shape[0]
    lpp = kdup.shape[1]
    tq = min(t, 128)
    nsl = mask.shape[1]
    group = A_HEADS // A_KV_HEADS
    kern = functools.partial(_dsa_attn_kernel, tq=tq, q_off=q_off)
    return pl.pallas_call(
        kern,
        grid=(bsz, t // tq, A_KV_HEADS),
        in_specs=[pl.BlockSpec(memory_space=pltpu.SMEM),
                  pl.BlockSpec((1, tq, 2 * LANES), lambda b, a, g: (b, a, g)),
                  pl.BlockSpec((1, lpp, LANES), lambda b, a, g: (b, 0, g)),
                  pl.BlockSpec((1, lpp, LANES), lambda b, a, g: (b, 0, g)),
                  pl.BlockSpec((1, nsl, tq, LANES), lambda b, a, g: (b, 0, a, 0)),
                  pl.BlockSpec((group, LANES, KEY_TILE), lambda b, a, g: (g, 0, 0))],
        out_specs=pl.BlockSpec((1, tq, 2 * LANES), lambda b, a, g: (b, a, g)),
        out_shape=jax.ShapeDtypeStruct((bsz, t, A_HEADS * HEAD_DIM), F32),
        compiler_params=_cparams("parallel", "parallel", "arbitrary"),
        name="dsa_attn",
    )(cfar, proj, kdup, vdup, mask, bias4)


def _fox_decay_kernel(*refs, n_past, n_new, t):
    if n_past:
        fz_ref, bf_ref, past_ref, tri_ref, lf_ref, nck_ref = refs
    else:
        fz_ref, bf_ref, tri_ref, lf_ref, nck_ref = refs
        past_ref = None
    tri = tri_ref[...]
    lane = lax.broadcasted_iota(I32, (B_HEADS, LANES), 1)
    carry = jnp.zeros((B_HEADS, 1), F32)
    for blk in range(n_past + n_new):
        sl = slice(blk * LANES, (blk + 1) * LANES)
        if blk < n_past:
            lf = past_ref[0, :, sl]
        else:
            nsl = slice((blk - n_past) * LANES, (blk - n_past + 1) * LANES)
            x = fz_ref[0, :, nsl] + bf_ref[...]
            lf = jnp.minimum(x, 0.0) - jnp.log1p(jnp.exp(-jnp.abs(x)))
            lf = jnp.where(lane + (blk - n_past) * LANES < t, lf, 0.0)
            lf_ref[0, :, nsl] = lf
        cum = _dot_x01(lf, tri) + carry
        nck_ref[0, :, sl] = -cum
        carry = cum[:, LANES - 1:LANES]


def fox_decay(fz_t, b_f, past_t, *, t):
    bsz, h, tp = fz_t.shape
    p = 0 if past_t is None else past_t.shape[2]
    n_past, n_new = p // LANES, tp // LANES
    tri = (jnp.arange(LANES)[:, None] <= jnp.arange(LANES)[None, :]).astype(BF16)
    kern = functools.partial(_fox_decay_kernel, n_past=n_past, n_new=n_new, t=t)
    args = [fz_t, b_f.reshape(h, 1)]
    in_specs = [pl.BlockSpec((1, h, tp), lambda b: (b, 0, 0)),
                pl.BlockSpec((h, 1), lambda b: (0, 0))]
    if n_past:
        args.append(past_t)
        in_specs.append(pl.BlockSpec((1, h, p), lambda b: (b, 0, 0)))
    args.append(tri)
    in_specs.append(pl.BlockSpec((LANES, LANES), lambda b: (0, 0)))
    return pl.pallas_call(
        kern,
        grid=(bsz,),
        in_specs=in_specs,
        out_specs=[pl.BlockSpec((1, h, tp), lambda b: (b, 0, 0)),
                   pl.BlockSpec((1, h, p + tp), lambda b: (b, 0, 0))],
        out_shape=[jax.ShapeDtypeStruct((bsz, h, tp), F32),
                   jax.ShapeDtypeStruct((bsz, h, p + tp), F32)],
        compiler_params=_cparams("parallel"),
        name="fox_decay",
    )(*args)


def _fox_attn_kernel(q_ref, k_ref, v_ref, nck_ref, g_ref, o_ref, *, tq, q_off):
    a = pl.program_id(1)
    tk = KEY_TILE
    nh = 4
    q0 = q_off + a * tq
    n_full = q0 // tk
    lane = lax.broadcasted_iota(I32, (tq, LANES), 1)
    half = lane < HEAD_DIM
    qe = []
    for e in range(nh):
        qc = q_ref[0, :, (e // 2) * LANES:(e // 2 + 1) * LANES] * (HEAD_DIM ** -0.5)
        qe.append((jnp.where(half, qc, 0.0) if e % 2 == 0 else jnp.where(half, 0.0, qc)).astype(BF16))

    def update(state, c, extra):
        ms, ls, accs = state
        ms, ls, accs = list(ms), list(ls), list(accs)
        r0 = pl.multiple_of(c * tk, tk)
        kt = k_ref[0, pl.ds(r0, tk), :]
        vt = v_ref[0, pl.ds(r0, tk), :]
        nck = nck_ref[0, 0, c]
        for cc in range(nh // 2):
            res = []
            for e in (2 * cc, 2 * cc + 1):
                add = nck[e:e + 1, :]
                if extra is not None:
                    add = add + extra
                res.append(_softmax_tile(qe[e], kt[:, cc * LANES:(cc + 1) * LANES],
                                         vt[:, cc * LANES:(cc + 1) * LANES], add, ms[e], ls[e]))
            for i, e in enumerate((2 * cc, 2 * cc + 1)):
                ms[e], ls[e] = res[i][0], res[i][1]
            accs[cc] = jnp.where(half, res[0][2], res[1][2]) * accs[cc] + jnp.where(half, res[0][3], res[1][3])
        return tuple(ms), tuple(ls), tuple(accs)

    init = (tuple(jnp.full((tq, 1), M_FLOOR, F32) for _ in range(nh)),
            tuple(jnp.zeros((tq, 1), F32) for _ in range(nh)),
            tuple(jnp.zeros((tq, LANES), F32) for _ in range(nh // 2)))
    state = lax.fori_loop(0, n_full, lambda c, st: update(st, c, None), init)
    row = lax.broadcasted_iota(I32, (tq, tk), 0)
    col = lax.broadcasted_iota(I32, (tq, tk), 1)
    causal = jnp.where(n_full * tk + col <= q0 + row, 0.0, NEG_BIG)
    ms, ls, accs = update(state, n_full, causal)
    for cc in range(nh // 2):
        sl = slice(cc * LANES, (cc + 1) * LANES)
        o = accs[cc] / jnp.where(half, ls[2 * cc], ls[2 * cc + 1])
        o_ref[0, :, sl] = o * (1.0 / (1.0 + jnp.exp(-g_ref[0, :, sl])))


def fox_attention(proj, k_all, v_all, nck, *, t, q_off, g_blk0):
    bsz = proj.shape[0]
    lp = k_all.shape[1]
    tq = min(t, 128)
    nt = lp // KEY_TILE
    kern = functools.partial(_fox_attn_kernel, tq=tq, q_off=q_off)
    return pl.pallas_call(
        kern,
        grid=(bsz, t // tq, B_HEADS // 4),
        in_specs=[pl.BlockSpec((1, tq, 2 * LANES), lambda b, a, g: (b, a, g)),
                  pl.BlockSpec((1, lp, 2 * LANES), lambda b, a, g: (b, 0, g)),
                  pl.BlockSpec((1, lp, 2 * LANES), lambda b, a, g: (b, 0, g)),
                  pl.BlockSpec((1, 1, nt, 4, KEY_TILE), lambda b, a, g: (b, g, 0, 0, 0)),
                  pl.BlockSpec((1, tq, 2 * LANES), lambda b, a, g: (b, a, g_blk0 + g))],
        out_specs=pl.BlockSpec((1, tq, 2 * LANES), lambda b, a, g: (b, a, g)),
        out_shape=jax.ShapeDtypeStruct((bsz, t, B_HEADS * HEAD_DIM), F32),
        compiler_params=_cparams("parallel", "parallel", "arbitrary"),
        name="fox_attn",
    )(proj, k_all, v_all, nck, proj)


def _hgrn2_levels(tc):
    lv = []
    n = 8
    while n < tc:
        lv.append(n)
        n *= 2
    return lv


def _hgrn2_masks(tc):
    t = jnp.arange(tc)[:, None]
    s = jnp.arange(tc)[None, :]
    ms = [((t // (2 * n) == s // (2 * n)) & ((t // n) % 2 == 1) & ((s // n) % 2 == 0)) for n in _hgrn2_levels(tc)]
    ms.append((t // 8 == s // 8) & (s <= t))
    return jnp.stack(ms).astype(F32)


def _hgrn2_kernel(q_ref, fz_ref, v_ref, g_ref, lb_ref, og_ref, s0_ref, tri_ref, msk_ref, y_ref, so_ref, st_ref, *, tc):
    ct = pl.program_id(2)

    @pl.when(ct == 0)
    def _():
        st_ref[...] = s0_ref[0, 0]

    z = fz_ref[0]
    q = q_ref[0]
    v = v_ref[0]
    lb = lb_ref[0]
    ez = jnp.exp(-jnp.abs(z))
    den = 1.0 / (1.0 + ez)
    pos = z >= 0.0
    f = lb + (1.0 - lb) * (jnp.where(pos, 1.0, ez) * den)
    kk = (1.0 - lb) * (jnp.where(pos, ez, 1.0) * den)
    cum = _dot_01x(tri_ref[...], jnp.log(f))

    def rows(idx):
        parts = []
        for i in idx:
            parts.append(jnp.zeros((8, LANES), F32) if i < 0 else jnp.broadcast_to(cum[i:i + 1, :], (8, LANES)))
        return jnp.concatenate(parts, axis=0)

    levels = _hgrn2_levels(tc)
    ngrp = tc // 8
    scores = jnp.zeros((tc, tc), F32)
    ql8 = None
    for li, n in enumerate(levels):
        start = [((r * 8) // n) * n for r in range(ngrp)]
        a_start = rows([s - 1 for s in start])
        a_end = rows([s + n - 1 for s in start])
        ql = (q * jnp.exp(cum - a_start)).astype(BF16)
        kr = (kk * jnp.exp(a_end - cum)).astype(BF16)
        scores = scores + msk_ref[li] * _nt(ql, kr)
        if n == 8:
            ql8 = ql
            kb = (kk * jnp.exp(a_start - cum)).astype(BF16)
    if ql8 is None:
        a_start = rows([r * 8 - 1 for r in range(ngrp)])
        ql8 = (q * jnp.exp(cum - a_start)).astype(BF16)
        kb = (kk * jnp.exp(a_start - cum)).astype(BF16)
    scores = scores + msk_ref[len(levels)] * _nt(ql8, kb)

    st = st_ref[...]
    o = _nt((q * jnp.exp(cum)).astype(BF16), st.astype(BF16)) + _dot(scores.astype(BF16), v.astype(BF16))
    a_last = cum[tc - 1:tc, :]
    khat = (kk * jnp.exp(a_last - cum)).astype(BF16)
    st_new = st * jnp.exp(a_last) + _dot(v.T.astype(BF16), khat)
    st_ref[...] = st_new

    g = g_ref[0]
    on = o * lax.rsqrt(jnp.mean(o * o, axis=-1, keepdims=True) + EPS) * og_ref[...]
    y_ref[0] = on * (g * (1.0 / (1.0 + jnp.exp(-g))))

    @pl.when(ct == pl.num_programs(2) - 1)
    def _():
        so_ref[0, 0] = st_new


def hgrn2_recurrence(proj, lb, out_gain, s0_t, *, t):
    bsz = proj.shape[0]
    tc = min(t, 128)
    nlv = len(_hgrn2_levels(tc)) + 1
    tri = (jnp.arange(tc)[:, None] >= jnp.arange(tc)[None, :]).astype(BF16)
    h = C_HEADS
    kern = functools.partial(_hgrn2_kernel, tc=tc)
    blk = lambda off: pl.BlockSpec((1, tc, LANES), lambda b, hh, c: (b, c, off + hh))
    return pl.pallas_call(
        kern,
        grid=(bsz, h, t // tc),
        in_specs=[blk(0), blk(h), blk(2 * h), blk(3 * h),
                  pl.BlockSpec((1, 1, C_DK), lambda b, hh, c: (hh, 0, 0)),
                  pl.BlockSpec((1, C_DV), lambda b, hh, c: (0, 0)),
                  pl.BlockSpec((1, 1, C_DV, C_DK), lambda b, hh, c: (b, hh, 0, 0)),
                  pl.BlockSpec((tc, tc), lambda b, hh, c: (0, 0)),
                  pl.BlockSpec((nlv, tc, tc), lambda b, hh, c: (0, 0, 0))],
        out_specs=[pl.BlockSpec((1, tc, LANES), lambda b, hh, c: (b, c, hh)),
                   pl.BlockSpec((1, 1, C_DV, C_DK), lambda b, hh, c: (b, hh, 0, 0))],
        out_shape=[jax.ShapeDtypeStruct((bsz, t, h * C_DV), F32),
                   jax.ShapeDtypeStruct((bsz, h, C_DV, C_DK), F32)],
        scratch_shapes=[pltpu.VMEM((C_DV, C_DK), F32)],
        compiler_params=_cparams("parallel", "parallel", "arbitrary"),
        name="hgrn2",
    )(proj, proj, proj, proj, lb.reshape(h, 1, C_DK), out_gain.reshape(1, C_DV), s0_t, tri, _hgrn2_masks(tc))


def _moe_kernel(x_ref, gain_ref, sc_ref, sh_ref, g2_ref, wr_ref, br_ref, wg_ref, wu_ref, wd_ref, o_ref,
                hb_ref, gates_ref, acc_ref):
    e = pl.program_id(2)
    tm = x_ref.shape[1]
    lane = lax.broadcasted_iota(I32, (tm, LANES), 1)

    @pl.when(e == 0)
    def _():
        h = _ln_mod(x_ref[0], gain_ref[...], sc_ref[0], sh_ref[0])
        hb_ref[...] = h.astype(BF16)
        acc_ref[...] = jnp.zeros_like(acc_ref)
        r = _dot_f32(h, wr_ref[...]) + br_ref[...]
        lanef = lane.astype(F32)
        big = float(LANES)
        is_g = lane < N_GROUPS
        lg = jnp.where(is_g, r, -jnp.inf)
        mg = jnp.max(lg, axis=-1, keepdims=True)
        grp = jnp.min(jnp.where(lg == mg, lanef, big), axis=-1, keepdims=True)
        p_grp = 1.0 / jnp.sum(jnp.where(is_g, jnp.exp(r - mg), 0.0), axis=-1, keepdims=True)
        eg = ((lane - N_GROUPS) >> 2).astype(F32)
        in_e = (lane >= N_GROUPS) & (lane < N_GROUPS + N_EXPERTS) & (eg == grp)
        le = jnp.where(in_e, r, -jnp.inf)
        v1 = jnp.max(le, axis=-1, keepdims=True)
        i1 = jnp.min(jnp.where(le == v1, lanef, big), axis=-1, keepdims=True)
        le2 = jnp.where(lanef == i1, -jnp.inf, le)
        v2 = jnp.max(le2, axis=-1, keepdims=True)
        i2 = jnp.min(jnp.where(le2 == v2, lanef, big), axis=-1, keepdims=True)
        e2 = jnp.exp(v2 - v1)
        w1 = 1.0 / (1.0 + e2)
        gates_ref[...] = (jnp.where(lanef == i1, w1 * p_grp, 0.0)
                          + jnp.where(lanef == i2, (e2 * w1) * p_grp, 0.0))

    ge = jnp.sum(jnp.where(lane == e + N_GROUPS, gates_ref[...], 0.0), axis=-1, keepdims=True)
    hb = hb_ref[...]
    a = _dot(hb, wg_ref[0])
    u = _dot(hb, wu_ref[0])
    he = (a * (1.0 / (1.0 + jnp.exp(-a)))) * u * ge
    acc_ref[...] += _dot(he.astype(BF16), wd_ref[0])

    @pl.when(e == pl.num_programs(2) - 1)
    def _():
        o_ref[0] = x_ref[0] + g2_ref[0] * acc_ref[...]


def moe_layer(x, gain, sc, sh, g2, w_router, b_router, w_gate, w_up, w_down):
    bsz, t, d = x.shape
    tm = min(t, 1024)
    ne, _, de = w_gate.shape
    return pl.pallas_call(
        _moe_kernel,
        grid=(bsz, t // tm, ne),
        in_specs=[pl.BlockSpec((1, tm, d), lambda b, i, e: (b, i, 0)),
                  pl.BlockSpec((1, d), lambda b, i, e: (0, 0)),
                  pl.BlockSpec((1, 1, d), lambda b, i, e: (b, 0, 0)),
                  pl.BlockSpec((1, 1, d), lambda b, i, e: (b, 0, 0)),
                  pl.BlockSpec((1, 1, d), lambda b, i, e: (b, 0, 0)),
                  pl.BlockSpec((d, LANES), lambda b, i, e: (0, 0)),
                  pl.BlockSpec((1, LANES), lambda b, i, e: (0, 0)),
                  pl.BlockSpec((1, d, de), lambda b, i, e: (e, 0, 0)),
                  pl.BlockSpec((1, d, de), lambda b, i, e: (e, 0, 0)),
                  pl.BlockSpec((1, de, d), lambda b, i, e: (e, 0, 0))],
        out_specs=pl.BlockSpec((1, tm, d), lambda b, i, e: (b, i, 0)),
        out_shape=jax.ShapeDtypeStruct((bsz, t, d), F32),
        scratch_shapes=[pltpu.VMEM((tm, d), BF16), pltpu.VMEM((tm, LANES), F32), pltpu.VMEM((tm, d), F32)],
        compiler_params=_cparams("parallel", "parallel", "arbitrary"),
        name="moe",
    )(x, gain.reshape(1, d), sc, sh, g2, w_router, b_router, w_gate, w_up, w_down)


def _pad_cols(w, n):
    return jnp.pad(w, ((0, 0), (0, n - w.shape[1])))


def _pad_rows(a, n, front=0):
    return jnp.pad(a, ((0, 0), (front, n - a.shape[1] - front)) + ((0, 0),) * (a.ndim - 2))


def _round_up(n, m):
    return -(-n // m) * m


def _head_cols(gain_q, nq, gain_k, nk, npad):
    cg = jnp.concatenate([jnp.tile(gain_q, nq), jnp.tile(gain_k, nk)])
    n = cg.shape[0]
    cgain = jnp.pad(cg, (0, npad - n)).reshape(1, npad)
    cflag = (jnp.arange(npad) < n).astype(F32).reshape(1, npad)
    return cgain, cflag


def _dsa_layer(x, mod, past, prm, bias4, cfar):
    sh1, sc1, g1 = mod
    past_k, past_v, past_ki = past
    bsz, t, d = x.shape
    p = 0 if past_k is None else past_k.shape[1]
    n_keys = p + t
    topk = min(TOPK_MAX, n_keys // 4)
    tn = 256
    n_in = prm['w_in'].shape[1]
    npad = _round_up(n_in, tn)
    nqk = (A_HEADS + A_KV_HEADS) * HEAD_DIM
    w = _pad_cols(prm['w_in'], npad).astype(BF16)
    cgain, cflag = _head_cols(prm['q_norm'], A_HEADS, prm['k_norm'], A_KV_HEADS, npad)
    proj = ln_proj(x, prm['norm'], sc1, sh1, w, cgain, cflag, nqk // tn, tn)
    o_k = A_HEADS * HEAD_DIM
    o_v = o_k + A_KV_HEADS * HEAD_DIM
    o_qi = o_v + A_KV_HEADS * HEAD_DIM
    o_ki = o_qi + IDX_HEADS * IDX_DIM
    k_new = proj[..., o_k:o_v].reshape(bsz, t, A_KV_HEADS, HEAD_DIM)
    v_new = proj[..., o_v:o_qi].reshape(bsz, t, A_KV_HEADS, HEAD_DIM)
    ki_new = proj[..., o_ki:o_ki + IDX_DIM]
    if p:
        k_all = jnp.concatenate([past_k, k_new], axis=1)
        v_all = jnp.concatenate([past_v, v_new], axis=1)
        ki_all = jnp.concatenate([past_ki, ki_new], axis=1)
    else:
        k_all, v_all, ki_all = k_new, v_new, ki_new
    lp = _round_up(n_keys, KEY_TILE)
    kidup = _pad_rows(jnp.concatenate([ki_all, ki_all], axis=-1).astype(BF16), lp)
    mask = dsa_index_mask(proj, kidup, t=t, q_off=p, topk=topk, qi_blk=o_qi // (4 * LANES), w_blk=o_ki // LANES)
    lpp = lp + PAD_SLABS * LANES

    def dup(a):
        a = jnp.concatenate([a, a], axis=-1).astype(BF16).reshape(bsz, n_keys, A_KV_HEADS * LANES)
        return _pad_rows(a, lpp, front=PAD_SLABS * LANES)

    o = dsa_attention(proj, dup(k_all), dup(v_all), mask, bias4, cfar, t=t, q_off=p)
    x = out_proj_residual(o, prm['w_out'].astype(BF16), x, g1)
    return x, (k_new, v_new, ki_new)


def _fox_layer(x, mod, past, prm):
    sh1, sc1, g1 = mod
    past_k, past_v, past_lf = past
    bsz, t, d = x.shape
    p = 0 if past_k is None else past_k.shape[1]
    n_keys = p + t
    hd = B_HEADS * HEAD_DIM
    tn = 256
    npad = _round_up(prm['w_in'].shape[1], tn)
    w = _pad_cols(prm['w_in'], npad).astype(BF16)
    cgain, cflag = _head_cols(prm['q_norm'], B_HEADS, prm['k_norm'], B_HEADS, npad)
    proj = ln_proj(x, prm['norm'], sc1, sh1, w, cgain, cflag, 2 * hd // tn, tn)
    k_new = proj[..., hd:2 * hd]
    v_new = proj[..., 2 * hd:3 * hd]
    fz = proj[..., 4 * hd:4 * hd + B_HEADS]
    tp = _round_up(t, LANES)
    fz_t = _pad_rows(fz, tp).transpose(0, 2, 1)
    past_t = None if not p else past_lf.transpose(0, 2, 1)
    lf_t, nck = fox_decay(fz_t, prm['forget_bias'], past_t, t=t)
    logf_new = lf_t[:, :, :t].transpose(0, 2, 1)
    lp = _round_up(n_keys, KEY_TILE)
    if p:
        k_all = jnp.concatenate([past_k.reshape(bsz, p, hd), k_new], axis=1)
        v_all = jnp.concatenate([past_v.reshape(bsz, p, hd), v_new], axis=1)
    else:
        k_all, v_all = k_new, v_new
    k_all = _pad_rows(k_all.astype(BF16), lp)
    v_all = _pad_rows(v_all.astype(BF16), lp)
    nck = jnp.pad(nck, ((0, 0), (0, 0), (0, lp - nck.shape[2])))
    nck = nck.reshape(bsz, B_HEADS // 4, 4, lp // KEY_TILE, KEY_TILE).transpose(0, 1, 3, 2, 4)
    o = fox_attention(proj, k_all, v_all, nck, t=t, q_off=p, g_blk0=3 * hd // (2 * LANES))
    x = out_proj_residual(o, prm['w_out'].astype(BF16), x, g1)
    return x, (k_new.reshape(bsz, t, B_HEADS, HEAD_DIM), v_new.reshape(bsz, t, B_HEADS, HEAD_DIM), logf_new)


def _hgrn2_layer(x, mod, s0, prm):
    sh1, sc1, g1 = mod
    bsz, t, d = x.shape
    npad = prm['w_in'].shape[1]
    zeros = jnp.zeros((1, npad), F32)
    proj = ln_proj(x, prm['norm'], sc1, sh1, prm['w_in'].astype(BF16), zeros, zeros, 0, 256)
    y, s_t = hgrn2_recurrence(proj, prm['lb'], prm['out_norm'], jnp.swapaxes(s0, -1, -2), t=t)
    x = out_proj_residual(y, prm['w_out'].astype(BF16), x, g1)
    return x, jnp.swapaxes(s_t, -1, -2)


def _trunk(x, c, a_k, a_v, a_kidx, b_k, b_v, b_logf, c_state, prm):
    bsz, t, d = x.shape
    mod_all = ada_mod(c, prm['w_ada'], prm['b_ada'])
    lb_all = jnp.cumsum(jax.nn.softmax(prm['c_lower_bound'].astype(F32), axis=0), axis=0)
    lb_all = lb_all - lb_all[0]
    bias4 = dsa_bias_tile(prm['rel_table'])
    cfar = prm['rel_table'][REL_BUCKETS // 2 - 1]
    out_a, out_b, out_c = [], [], []
    for i in range(DEPTH):
        j = i // N_MIXERS
        kind = i % N_MIXERS
        sh1, sc1, g1, sh2, sc2, g2 = [m.reshape(bsz, 1, d) for m in jnp.split(mod_all[i], 6, axis=-1)]
        mod = (sh1, sc1, g1)
        if kind == 0:
            past = (None, None, None) if a_k is None else (a_k[j], a_v[j], a_kidx[j])
            lp = dict(norm=prm['norm_mix'][i], w_in=prm['a_w_in'][j], q_norm=prm['a_q_norm'][j],
                      k_norm=prm['a_k_norm'][j], w_out=prm['a_w_out'][j])
            x, new = _dsa_layer(x, mod, past, lp, bias4, cfar)
            out_a.append(new)
        elif kind == 1:
            past = (None, None, None) if b_k is None else (b_k[j], b_v[j], b_logf[j])
            lp = dict(norm=prm['norm_mix'][i], w_in=prm['b_w_in'][j], forget_bias=prm['b_forget_bias'][j],
                      q_norm=prm['b_q_norm'][j], k_norm=prm['b_k_norm'][j], w_out=prm['b_w_out'][j])
            x, new = _fox_layer(x, mod, past, lp)
            out_b.append(new)
        else:
            s0 = jnp.zeros((bsz, C_HEADS, C_DK, C_DV), F32) if c_state is None else c_state[j]
            lp = dict(norm=prm['norm_mix'][i], w_in=prm['c_w_in'][j], lb=lb_all[i],
                      out_norm=prm['c_out_norm'][j], w_out=prm['c_w_out'][j])
            x, new = _hgrn2_layer(x, mod, s0, lp)
            out_c.append(new)
        w_router = jnp.pad(jnp.concatenate([prm['moe_w_group'][i], prm['moe_w_expert'][i]], axis=1),
                           ((0, 0), (0, LANES - N_GROUPS - N_EXPERTS)))
        b_router = jnp.pad(jnp.concatenate([prm['moe_b_group'][i], prm['moe_b_expert'][i]]),
                           (0, LANES - N_GROUPS - N_EXPERTS)).reshape(1, LANES)
        x = moe_layer(x, prm['norm_ffn'][i], sc2, sh2, g2, w_router, b_router,
                      prm['moe_w_gate'][i].astype(BF16), prm['moe_w_up'][i].astype(BF16),
                      prm['moe_w_down'][i].astype(BF16))
    stack = lambda outs, k: jnp.stack([o[k] for o in outs])
    return (x, stack(out_a, 0), stack(out_a, 1), stack(out_a, 2),
            stack(out_b, 0), stack(out_b, 1), stack(out_b, 2), jnp.stack(out_c))


def kernel(x_prompt, x_sample, cache_a_k, cache_a_v, cache_a_kidx, cache_b_k, cache_b_v, cache_b_logf, state_c,
           c_prompt, c_sample, rel_table, w_ada, b_ada, norm_mix, norm_ffn, a_w_in, a_q_norm, a_k_norm, a_w_out,
           b_w_in, b_forget_bias, b_q_norm, b_k_norm, b_w_out, c_w_in, c_lower_bound, c_out_norm, c_w_out,
           moe_w_group, moe_b_group, moe_w_expert, moe_b_expert, moe_w_gate, moe_w_up, moe_w_down):
    prm = {'rel_table': rel_table, 'w_ada': w_ada, 'b_ada': b_ada, 'norm_mix': norm_mix, 'norm_ffn': norm_ffn,
           'a_w_in': a_w_in, 'a_q_norm': a_q_norm, 'a_k_norm': a_k_norm, 'a_w_out': a_w_out,
           'b_w_in': b_w_in, 'b_forget_bias': b_forget_bias, 'b_q_norm': b_q_norm, 'b_k_norm': b_k_norm,
           'b_w_out': b_w_out, 'c_w_in': c_w_in, 'c_lower_bound': c_lower_bound, 'c_out_norm': c_out_norm,
           'c_w_out': c_w_out, 'moe_w_group': moe_w_group, 'moe_b_group': moe_b_group,
           'moe_w_expert': moe_w_expert, 'moe_b_expert': moe_b_expert, 'moe_w_gate': moe_w_gate,
           'moe_w_up': moe_w_up, 'moe_w_down': moe_w_down}
    (y_p, ak_p, av_p, ai_p, bk_p, bv_p, bl_p, cs_p) = _trunk(
        x_prompt, c_prompt, None, None, None, None, None, None, None, prm)
    (y_s, ak_s, av_s, ai_s, bk_s, bv_s, bl_s, cs_s) = _trunk(
        x_sample, c_sample, cache_a_k, cache_a_v, cache_a_kidx, cache_b_k, cache_b_v, cache_b_logf, state_c, prm)
    return (y_p, y_s, ak_p, av_p, ai_p, ak_s, av_s, ai_s, bk_p, bv_p, bl_p, bk_s, bv_s, bl_s, cs_p, cs_s)
```

```python
import functools

import jax
import jax.numpy as jnp
from jax import lax
from jax.experimental import pallas as pl
from jax.experimental.pallas import tpu as pltpu

F32 = jnp.float32
BF16 = jnp.bfloat16
I32 = jnp.int32

LANES = 128
VMEM_LIMIT_BYTES = 56 * 1024 * 1024

DEPTH = 4
N_MIXERS = 3
CHUNK = 64
EPS = 1e-6
HEAD_DIM = 64
A_HEADS = 16
A_KV_HEADS = 4
A_GROUP = A_HEADS // A_KV_HEADS
IDX_HEADS = 8
IDX_DIM = 64
TOPK_MAX = 256
REL_BUCKETS = 32
B_HEADS = 16
C_HEADS = 8
C_DK = 128
C_DV = 128
N_GROUPS = 4
EXPERTS_PER_GROUP = 4
N_EXPERTS = 16
D_EXPERT = 512

LOG2E = 1.4426950408889634
NEG_BIG = -1e30
M_FLOOR = -1e20
INT_MIN = -2 ** 31
KEY_TILE = 512
SLABS = KEY_TILE // LANES


def _cparams(*sem):
    return pltpu.CompilerParams(dimension_semantics=sem, vmem_limit_bytes=VMEM_LIMIT_BYTES)


def _nt(a, b):
    return lax.dot_general(a, b, (((1,), (1,)), ((), ())), preferred_element_type=F32)


def _split3(x):
    hi = x.astype(BF16)
    r = x - hi.astype(F32)
    mid = r.astype(BF16)
    lo = (r - mid.astype(F32)).astype(BF16)
    return hi, mid, lo


def _dot(a, b):
    return jnp.dot(a, b, preferred_element_type=F32)


def _dot_x01(x, m01):
    hi, mid, lo = _split3(x)
    return _dot(hi, m01) + _dot(mid, m01) + _dot(lo, m01)


def _dot_01x(m01, x):
    hi, mid, lo = _split3(x)
    return _dot(m01, hi) + _dot(m01, mid) + _dot(m01, lo)


def _dot_f32(a, b):
    ah, am, al = _split3(a)
    bh, bm, bl = _split3(b)
    return _dot(ah, bh) + (_dot(ah, bm) + _dot(am, bh)) + (_dot(ah, bl) + _dot(am, bm) + _dot(al, bh))


def _pad_q_rows(q, tq, tqp):
    if tqp == tq:
        return q
    return jnp.concatenate([q, jnp.zeros((tqp - tq, q.shape[1]), q.dtype)], axis=0)


def _mod_kernel(c_ref, w_ref, b_ref, o_ref):
    o_ref[0] = _dot(c_ref[...], w_ref[0]) + b_ref[0]


def ada_mod(c, w_ada, b_ada):
    nl, d, n6 = w_ada.shape
    bsz = c.shape[0]
    tn = 512
    return pl.pallas_call(
        _mod_kernel,
        grid=(nl, n6 // tn),
        in_specs=[pl.BlockSpec((bsz, d), lambda l, j: (0, 0)),
                  pl.BlockSpec((1, d, tn), lambda l, j: (l, 0, j)),
                  pl.BlockSpec((1, 1, tn), lambda l, j: (l, 0, j))],
        out_specs=pl.BlockSpec((1, bsz, tn), lambda l, j: (l, 0, j)),
        out_shape=jax.ShapeDtypeStruct((nl, bsz, n6), F32),
        compiler_params=_cparams("parallel", "parallel"),
        name="ada_mod",
    )(c, w_ada, b_ada.reshape(nl, 1, n6))


def _ln_mod(x, gain, sc, sh):
    ms = jnp.mean(x * x, axis=-1, keepdims=True)
    return (x * lax.rsqrt(ms + EPS) * gain) * (1.0 + sc) + sh


def _ln_proj_kernel(x_ref, gain_ref, sc_ref, sh_ref, w_ref, cgain_ref, cflag_ref, bd_ref, o_ref, h_ref,
                    *, n_norm_tiles, tn):
    j = pl.program_id(2)

    @pl.when(j == 0)
    def _():
        h_ref[...] = _ln_mod(x_ref[0], gain_ref[...], sc_ref[0], sh_ref[0]).astype(BF16)

    y = _dot(h_ref[...], w_ref[...])

    def plain():
        o_ref[0] = y

    def normed():
        y2 = y * y
        hi = y2.astype(BF16)
        lo = (y2 - hi.astype(F32)).astype(BF16)
        bd = bd_ref[...]
        segs = []
        for s in range(tn // LANES):
            sl = slice(s * LANES, (s + 1) * LANES)
            segs.append(_dot(hi[:, sl], bd) + _dot(lo[:, sl], bd))
        seg = jnp.concatenate(segs, axis=1)
        yn = y * lax.rsqrt(seg * (1.0 / HEAD_DIM) + EPS) * cgain_ref[...]
        o_ref[0] = jnp.where(cflag_ref[...] > 0.0, yn, y)

    if n_norm_tiles == 0:
        plain()
    else:
        pl.when(j < n_norm_tiles)(normed)
        pl.when(j >= n_norm_tiles)(plain)


def ln_proj(x, gain, sc, sh, w, cgain, cflag, n_norm_tiles, tn=256):
    bsz, t, d = x.shape
    npad = w.shape[1]
    tm = min(t, 1024)
    bd = (jnp.arange(LANES)[:, None] // HEAD_DIM == jnp.arange(LANES)[None, :] // HEAD_DIM).astype(BF16)
    kern = functools.partial(_ln_proj_kernel, n_norm_tiles=n_norm_tiles, tn=tn)
    return pl.pallas_call(
        kern,
        grid=(bsz, t // tm, npad // tn),
        in_specs=[pl.BlockSpec((1, tm, d), lambda b, i, j: (b, i, 0)),
                  pl.BlockSpec((1, d), lambda b, i, j: (0, 0)),
                  pl.BlockSpec((1, 1, d), lambda b, i, j: (b, 0, 0)),
                  pl.BlockSpec((1, 1, d), lambda b, i, j: (b, 0, 0)),
                  pl.BlockSpec((d, tn), lambda b, i, j: (0, j)),
                  pl.BlockSpec((1, tn), lambda b, i, j: (0, j)),
                  pl.BlockSpec((1, tn), lambda b, i, j: (0, j)),
                  pl.BlockSpec((LANES, LANES), lambda b, i, j: (0, 0))],
        out_specs=pl.BlockSpec((1, tm, tn), lambda b, i, j: (b, i, j)),
        out_shape=jax.ShapeDtypeStruct((bsz, t, npad), F32),
        scratch_shapes=[pltpu.VMEM((tm, d), BF16)],
        compiler_params=_cparams("parallel", "parallel", "arbitrary"),
        name="ln_proj",
    )(x, gain.reshape(1, d), sc, sh, w, cgain, cflag, bd)


def _out_proj_kernel(a_ref, w_ref, x_ref, g_ref, o_ref):
    y = _dot(a_ref[0].astype(BF16), w_ref[...])
    o_ref[0] = x_ref[0] + g_ref[0] * y


def out_proj_residual(a, w, x, gate):
    bsz, t, k = a.shape
    d = w.shape[1]
    tm = min(t, 512)
    return pl.pallas_call(
        _out_proj_kernel,
        grid=(bsz, t // tm),
        in_specs=[pl.BlockSpec((1, tm, k), lambda b, i: (b, i, 0)),
                  pl.BlockSpec((k, d), lambda b, i: (0, 0)),
                  pl.BlockSpec((1, tm, d), lambda b, i: (b, i, 0)),
                  pl.BlockSpec((1, 1, d), lambda b, i: (b, 0, 0))],
        out_specs=pl.BlockSpec((1, tm, d), lambda b, i: (b, i, 0)),
        out_shape=jax.ShapeDtypeStruct((bsz, t, d), F32),
        compiler_params=_cparams("parallel", "parallel"),
        name="out_proj",
    )(a, w, x, gate)


def _dsa_index_kernel(qi_ref, w_ref, ki_ref, o_ref, key_ref, *, tq, tqp, n_slabs, q_off, topk, idx_bits):
    a = pl.program_id(1)
    tk = KEY_TILE
    q0 = q_off + a * tq
    n_kt = (q0 + tq + tk - 1) // tk
    lane = lax.broadcasted_iota(I32, (tqp, LANES), 1)
    half = lane < IDX_DIM
    krow = lax.broadcasted_iota(I32, (tk, tqp), 0)
    qcol = lax.broadcasted_iota(I32, (tk, tqp), 1)
    qchunk = (q0 + qcol) >> 6
    srow = lax.broadcasted_iota(I32, (LANES, tqp), 0)
    w_t = (_pad_q_rows(w_ref[0], tq, tqp) * (IDX_HEADS ** -0.5)).T
    qs = []
    for p in range(IDX_HEADS // 2):
        qp = _pad_q_rows(qi_ref[0, :, p * LANES:(p + 1) * LANES], tq, tqp) * (IDX_DIM ** -0.5)
        qs.append(jnp.where(half, qp, 0.0).astype(BF16))
        qs.append(jnp.where(half, 0.0, qp).astype(BF16))

    def score_tile(c, carry):
        kt = ki_ref[0, pl.ds(pl.multiple_of(c * tk, tk), tk), :]
        sc = jnp.zeros((tk, tqp), F32)
        for h in range(IDX_HEADS):
            sc = sc + w_t[IDX_DIM + h:IDX_DIM + h + 1, :] * jnp.maximum(_nt(kt, qs[h]), 0.0)
        bits = lax.bitcast_convert_type(sc, I32)
        key = jnp.where(bits < 0, bits ^ 0x7FFFFFFF, bits)
        key = jnp.where(sc == 0.0, 0, key)
        adm = ((c * tk + krow) >> 6) <= qchunk
        key = jnp.where(adm, key, INT_MIN)
        for s_ in range(SLABS):
            key_ref[c * SLABS + s_] = key[s_ * LANES:(s_ + 1) * LANES, :]
        return carry

    lax.fori_loop(0, n_kt, score_tile, 0)

    def count(pred):
        def body(c, acc):
            for s_ in range(SLABS):
                sidx = c * SLABS + s_
                ind = jnp.where(pred(key_ref[sidx], sidx), 1.0, 0.0)
                acc = acc + jnp.sum(ind.reshape(LANES // 8, 8, tqp), axis=0)
            return acc
        acc = lax.fori_loop(0, n_kt, body, jnp.zeros((8, tqp), F32))
        return jnp.sum(acc, axis=0, keepdims=True)

    kf = float(topk)
    n_adm = count(lambda k, s: k > INT_MIN)

    def all_done(cnt_t):
        done = (cnt_t == kf) | (n_adm < kf)
        return (jnp.min(jnp.where(done, 1.0, 0.0)) > 0.0).astype(I32)

    def bit_cond(st):
        return (st[0] < 32) & (st[3] == 0)

    def bit_body(st):
        i, t_u, cnt_t, _ = st
        cand_u = t_u | lax.shift_left(jnp.int32(1), 31 - i)
        cand_s = cand_u ^ INT_MIN
        cnt = count(lambda k, s: k >= cand_s)
        take = cnt >= kf
        cnt_t = jnp.where(take, cnt, cnt_t)
        return i + 1, jnp.where(take, cand_u, t_u), cnt_t, all_done(cnt_t)

    _, t_u, cnt_ge, _ = lax.while_loop(bit_cond, bit_body,
                                       (jnp.int32(0), jnp.zeros((1, tqp), I32), n_adm, all_done(n_adm)))
    thr = t_u ^ INT_MIN
    excess = jnp.where(cnt_ge > kf, 1.0, 0.0)

    def tie_search():
        need = kf - count(lambda k, s: k > thr)

        def j_body(i, j):
            cand = j | lax.shift_left(jnp.int32(1), idx_bits - 1 - i)
            c = count(lambda k, s: (k == thr) & (srow + s * LANES < cand))
            return jnp.where(c < need, cand, j)
        return lax.fori_loop(0, idx_bits, j_body, jnp.zeros((1, tqp), I32))

    j_last = lax.cond(jnp.max(excess) > 0.0, tie_search, lambda: jnp.full((1, tqp), 2 ** idx_bits, I32))

    def write_active(c, carry):
        for s_ in range(SLABS):
            sidx = c * SLABS + s_
            k = key_ref[sidx]
            sel = (k > thr) | ((k == thr) & (srow + sidx * LANES <= j_last))
            sel = sel & (k > INT_MIN)
            o_ref[0, sidx] = jnp.where(sel, 0.0, NEG_BIG)
        return carry

    lax.fori_loop(0, n_kt, write_active, 0)
    neg = jnp.full((LANES, tqp), NEG_BIG, F32)

    def write_inactive(s, carry):
        o_ref[0, s] = neg
        return carry

    lax.fori_loop(n_kt * SLABS, n_slabs, write_inactive, 0)


def dsa_index_mask(proj, kidup, *, t, q_off, topk, qi_blk, w_blk):
    bsz = proj.shape[0]
    lp = kidup.shape[1]
    n_slabs = lp // LANES
    tq = min(t, KEY_TILE)
    tqp = max(tq, LANES)
    idx_bits = max(1, (lp - 1).bit_length())
    kern = functools.partial(_dsa_index_kernel, tq=tq, tqp=tqp, n_slabs=n_slabs, q_off=q_off, topk=topk,
                             idx_bits=idx_bits)
    return pl.pallas_call(
        kern,
        grid=(bsz, t // tq),
        in_specs=[pl.BlockSpec((1, tq, 4 * LANES), lambda b, a: (b, a, qi_blk)),
                  pl.BlockSpec((1, tq, LANES), lambda b, a: (b, a, w_blk)),
                  pl.BlockSpec((1, lp, LANES), lambda b, a: (b, 0, 0))],
        out_specs=pl.BlockSpec((1, n_slabs, LANES, tqp), lambda b, a: (b, 0, 0, a)),
        out_shape=jax.ShapeDtypeStruct((bsz, n_slabs, LANES, (t // tq) * tqp), F32),
        scratch_shapes=[pltpu.VMEM((n_slabs, LANES, tqp), I32)],
        compiler_params=_cparams("parallel", "parallel"),
        name="dsa_index",
    )(proj, proj, kidup)


def _bias_kernel(tab_ref, diag_ref, sub_ref):
    h = pl.program_id(0)
    far = tab_ref[REL_BUCKETS // 2 - 1, h]

    def bias(rel):
        n = jnp.abs(rel)
        large = jnp.full(rel.shape, 8, I32)
        for th in (12, 16, 23, 32, 46, 64, 91):
            large = large + jnp.where(n >= th, 1, 0)
        bucket = jnp.where(rel > 0, REL_BUCKETS // 2, 0) + jnp.where(n < 8, n, large)
        acc = jnp.zeros(rel.shape, F32)
        for bk in range(REL_BUCKETS):
            acc = jnp.where(bucket == bk, tab_ref[bk, h], acc)
        return (acc - far) * LOG2E

    ik = lax.broadcasted_iota(I32, (KEY_TILE, KEY_TILE), 0)
    iq = lax.broadcasted_iota(I32, (KEY_TILE, KEY_TILE), 1)
    diag_ref[0] = bias(ik - iq)
    ik = lax.broadcasted_iota(I32, (LANES, LANES), 0)
    iq = lax.broadcasted_iota(I32, (LANES, LANES), 1)
    sub_ref[0] = bias(ik - LANES - iq)


def dsa_bias_tiles(rel_table):
    return pl.pallas_call(
        _bias_kernel,
        grid=(A_HEADS,),
        in_specs=[pl.BlockSpec(memory_space=pltpu.SMEM)],
        out_specs=[pl.BlockSpec((1, KEY_TILE, KEY_TILE), lambda h: (h, 0, 0)),
                   pl.BlockSpec((1, LANES, LANES), lambda h: (h, 0, 0))],
        out_shape=[jax.ShapeDtypeStruct((A_HEADS, KEY_TILE, KEY_TILE), F32),
                   jax.ShapeDtypeStruct((A_HEADS, LANES, LANES), F32)],
        compiler_params=_cparams("parallel"),
        name="dsa_bias",
    )(rel_table)


def _dsa_attn_kernel(q_ref, k_ref, vt_ref, msk_ref, bd_ref, bs_ref, o_ref, acc_ref, m_ref, s0_ref,
                     *, tq, tqp, q_off):
    a = pl.program_id(1)
    tk = KEY_TILE
    q0 = q_off + a * tq
    cd = q0 // tk
    cs = jnp.maximum(cd - 1, 0)
    lane = lax.broadcasted_iota(I32, (tqp, LANES), 1)
    half = lane < HEAD_DIM
    qs = []
    for e in range(A_GROUP):
        qc = _pad_q_rows(q_ref[0, :, (e // 2) * LANES:(e // 2 + 1) * LANES], tq, tqp) * (HEAD_DIM ** -0.5 * LOG2E)
        qs.append((jnp.where(half, qc, 0.0) if e % 2 == 0 else jnp.where(half, 0.0, qc)).astype(BF16))
    m_ref[...] = jnp.full(m_ref.shape, M_FLOOR, F32)
    acc_ref[...] = jnp.zeros(acc_ref.shape, F32)

    def s_tile(e, c):
        return _nt(k_ref[0, pl.ds(pl.multiple_of(c * tk, tk), tk), :], qs[e])

    def softmax_pv(e, c, s, after, kind):
        s = s + jnp.concatenate([msk_ref[0, c * SLABS + i] for i in range(SLABS)], axis=0)
        if kind == "diag":
            s = s + bd_ref[e, :, 0:tqp]
        elif kind == "sub":
            corner = bs_ref[e]
            if tqp > LANES:
                corner = jnp.concatenate([corner, jnp.zeros((LANES, tqp - LANES), F32)], axis=1)
            s = s + jnp.concatenate([jnp.zeros((tk - LANES, tqp), F32), corner], axis=0)
            s = jnp.where(cd > 0, s, NEG_BIG)
        m_old = jnp.minimum(m_ref[e], jnp.maximum(after[0:1, :], -NEG_BIG))
        m_new = jnp.maximum(m_old, jnp.max(s, axis=0, keepdims=True))
        p = jnp.exp2(s - m_new).astype(BF16)
        acc_ref[e] = jnp.exp2(m_old - m_new) * acc_ref[e] + _dot(vt_ref[0, 0, e % 2, c], p)
        m_ref[e] = m_new

    def step(c, kind):
        s_prev = s0_ref[...]
        for e in range(A_GROUP):
            if e + 1 < A_GROUP:
                s_next = s_tile(e + 1, c)
            else:
                s_next = s_tile(0, jnp.minimum(c + 1, cd))
                s0_ref[...] = s_next
            softmax_pv(e, c, s_prev, s_next, kind)
            s_prev = s_next

    def far_body(c, carry):
        step(c, "far")
        return carry

    s0_ref[...] = s_tile(0, 0)
    lax.fori_loop(0, cs, far_body, 0)
    step(cs, "sub")
    step(cd, "diag")
    for c2 in range(A_GROUP // 2):
        t_e = acc_ref[2 * c2].T
        t_o = acc_ref[2 * c2 + 1].T
        o = jnp.where(half, t_e / t_e[:, HEAD_DIM:HEAD_DIM + 1], t_o / t_o[:, 0:1])
        o_ref[0, :, c2 * LANES:(c2 + 1) * LANES] = o[:tq]


def dsa_attention(proj, kdup, vt_aug, mask, bdiag, bsub, *, t, q_off):
    bsz = proj.shape[0]
    lp = kdup.shape[1]
    tq = min(t, KEY_TILE)
    tqp = max(tq, LANES)
    nt = lp // KEY_TILE
    nsl = mask.shape[1]
    kern = functools.partial(_dsa_attn_kernel, tq=tq, tqp=tqp, q_off=q_off)
    return pl.pallas_call(
        kern,
        grid=(bsz, t // tq, A_KV_HEADS),
        in_specs=[pl.BlockSpec((1, tq, 2 * LANES), lambda b, a, g: (b, a, g)),
                  pl.BlockSpec((1, lp, LANES), lambda b, a, g: (b, 0, g)),
                  pl.BlockSpec((1, 1, 2, nt, LANES, KEY_TILE), lambda b, a, g: (b, g, 0, 0, 0, 0)),
                  pl.BlockSpec((1, nsl, LANES, tqp), lambda b, a, g: (b, 0, 0, a)),
                  pl.BlockSpec((A_GROUP, KEY_TILE, tqp), lambda b, a, g: (g, 0, 0)),
                  pl.BlockSpec((A_GROUP, LANES, LANES), lambda b, a, g: (g, 0, 0))],
        out_specs=pl.BlockSpec((1, tq, 2 * LANES), lambda b, a, g: (b, a, g)),
        out_shape=jax.ShapeDtypeStruct((bsz, t, A_HEADS * HEAD_DIM), F32),
        scratch_shapes=[pltpu.VMEM((A_GROUP, LANES, tqp), F32), pltpu.VMEM((A_GROUP, 1, tqp), F32),
                        pltpu.VMEM((KEY_TILE, tqp), F32)],
        compiler_params=_cparams("parallel", "parallel", "arbitrary"),
        name="dsa_attn",
    )(proj, kdup, vt_aug, mask, bdiag, bsub)


def _fox_decay_kernel(*refs, n_past, n_new, t):
    if n_past:
        fz_ref, bf_ref, past_ref, tri_ref, lf_ref, nck_ref = refs
    else:
        fz_ref, bf_ref, tri_ref, lf_ref, nck_ref = refs
        past_ref = None
    tri = tri_ref[...]
    lane = lax.broadcasted_iota(I32, (B_HEADS, LANES), 1)
    carry = jnp.zeros((B_HEADS, 1), F32)
    for blk in range(n_past + n_new):
        sl = slice(blk * LANES, (blk + 1) * LANES)
        if blk < n_past:
            lf = past_ref[0, :, sl]
        else:
            nsl = slice((blk - n_past) * LANES, (blk - n_past + 1) * LANES)
            x = fz_ref[0, :, nsl] + bf_ref[...]
            lf = jnp.minimum(x, 0.0) - jnp.log1p(jnp.exp(-jnp.abs(x)))
            lf = jnp.where(lane + (blk - n_past) * LANES < t, lf, 0.0)
            lf_ref[0, :, nsl] = lf
        cum = _dot_x01(lf, tri) + carry
        for i, piece in enumerate(_split3(cum * -LOG2E)):
            nck_ref[0, i, :, sl] = piece
        carry = cum[:, LANES - 1:LANES]


def fox_decay(fz_t, b_f, past_t, *, t):
    bsz, h, tp = fz_t.shape
    p = 0 if past_t is None else past_t.shape[2]
    n_past, n_new = p // LANES, tp // LANES
    tri = (jnp.arange(LANES)[:, None] <= jnp.arange(LANES)[None, :]).astype(BF16)
    kern = functools.partial(_fox_decay_kernel, n_past=n_past, n_new=n_new, t=t)
    args = [fz_t, b_f.reshape(h, 1)]
    in_specs = [pl.BlockSpec((1, h, tp), lambda b: (b, 0, 0)),
                pl.BlockSpec((h, 1), lambda b: (0, 0))]
    if n_past:
        args.append(past_t)
        in_specs.append(pl.BlockSpec((1, h, p), lambda b: (b, 0, 0)))
    args.append(tri)
    in_specs.append(pl.BlockSpec((LANES, LANES), lambda b: (0, 0)))
    return pl.pallas_call(
        kern,
        grid=(bsz,),
        in_specs=in_specs,
        out_specs=[pl.BlockSpec((1, h, tp), lambda b: (b, 0, 0)),
                   pl.BlockSpec((1, 3, h, p + tp), lambda b: (b, 0, 0, 0))],
        out_shape=[jax.ShapeDtypeStruct((bsz, h, tp), F32),
                   jax.ShapeDtypeStruct((bsz, 3, h, p + tp), BF16)],
        compiler_params=_cparams("parallel"),
        name="fox_decay",
    )(*args)


def _fox_attn_kernel(q_ref, k_ref, vt_ref, g_ref, o_ref, acc_ref, m_ref, s0_ref, *, tq, tqp, q_off):
    a = pl.program_id(2)
    tk = KEY_TILE
    q0 = q_off + a * tq
    n_full = q0 // tk
    n_need = (q0 + tq - 1) // tk + 1
    lane = lax.broadcasted_iota(I32, (tqp, LANES), 1)
    qn = _pad_q_rows(q_ref[0], tq, tqp) * (HEAD_DIM ** -0.5 * LOG2E)
    qs = (jnp.where(lane < HEAD_DIM, qn, jnp.where(lane < HEAD_DIM + 3, 1.0, 0.0)).astype(BF16),
          jnp.where(lane >= HEAD_DIM, qn, jnp.where(lane < 3, 1.0, 0.0)).astype(BF16))
    m_ref[...] = jnp.full(m_ref.shape, M_FLOOR, F32)
    acc_ref[...] = jnp.zeros(acc_ref.shape, F32)
    krow = lax.broadcasted_iota(I32, (tk, tqp), 0)
    qcol = lax.broadcasted_iota(I32, (tk, tqp), 1)

    def s_tile(e, c):
        return _nt(k_ref[0, e, pl.ds(pl.multiple_of(c * tk, tk), tk), :], qs[e])

    def softmax_pv(e, c, s, masked):
        if masked:
            s = jnp.where(c * tk + krow <= q0 + qcol, s, NEG_BIG)
        m_old = m_ref[e]
        m_new = jnp.maximum(m_old, jnp.max(s, axis=0, keepdims=True))
        p = jnp.exp2(s - m_new).astype(BF16)
        acc_ref[e] = jnp.exp2(m_old - m_new) * acc_ref[e] + _dot(vt_ref[0, e, c], p)
        m_ref[e] = m_new

    def step(c, masked):
        s1 = s_tile(1, c)
        softmax_pv(0, c, s0_ref[...], masked)
        s0_ref[...] = s_tile(0, jnp.minimum(c + 1, n_need - 1))
        softmax_pv(1, c, s1, masked)

    def full_body(c, carry):
        step(c, False)
        return carry

    def masked_body(c, carry):
        step(c, True)
        return carry

    s0_ref[...] = s_tile(0, 0)
    lax.fori_loop(0, n_full, full_body, 0)
    lax.fori_loop(n_full, n_need, masked_body, 0)
    acc_e = acc_ref[0].T
    acc_o = acc_ref[1].T
    o = jnp.where(lane < HEAD_DIM, acc_e / acc_e[:, HEAD_DIM:HEAD_DIM + 1], acc_o / acc_o[:, 0:1])
    o_ref[0] = o[:tq] * (1.0 / (1.0 + jnp.exp(-g_ref[0])))


def fox_attention(proj, k_aug, vt_aug, *, t, q_off, g_blk0):
    bsz = proj.shape[0]
    lp = k_aug.shape[2]
    tq = min(t, KEY_TILE)
    tqp = max(tq, LANES)
    nt = lp // KEY_TILE
    kern = functools.partial(_fox_attn_kernel, tq=tq, tqp=tqp, q_off=q_off)
    return pl.pallas_call(
        kern,
        grid=(bsz, B_HEADS // 2, t // tq),
        in_specs=[pl.BlockSpec((1, tq, LANES), lambda b, j, a: (b, a, j)),
                  pl.BlockSpec((1, 2, lp, LANES), lambda b, j, a: (b, j, 0, 0)),
                  pl.BlockSpec((1, 2, nt, LANES, KEY_TILE), lambda b, j, a: (b, j, 0, 0, 0)),
                  pl.BlockSpec((1, tq, LANES), lambda b, j, a: (b, a, g_blk0 + j))],
        out_specs=pl.BlockSpec((1, tq, LANES), lambda b, j, a: (b, a, j)),
        out_shape=jax.ShapeDtypeStruct((bsz, t, B_HEADS * HEAD_DIM), F32),
        scratch_shapes=[pltpu.VMEM((2, LANES, tqp), F32), pltpu.VMEM((2, 1, tqp), F32),
                        pltpu.VMEM((KEY_TILE, tqp), F32)],
        compiler_params=_cparams("parallel", "parallel", "parallel"),
        name="fox_attn",
    )(proj, k_aug, vt_aug, proj)


def _hgrn2_levels(tc):
    lv = []
    n = 8
    while n < tc:
        lv.append(n)
        n *= 2
    return lv


def _hgrn2_masks(tc):
    t = jnp.arange(tc)[:, None]
    s = jnp.arange(tc)[None, :]
    ms = [((t // (2 * n) == s // (2 * n)) & ((t // n) % 2 == 1) & ((s // n) % 2 == 0)) for n in _hgrn2_levels(tc)]
    ms.append((t // 8 == s // 8) & (s <= t))
    return jnp.stack(ms).astype(F32)


def _hgrn2_kernel(q_ref, fz_ref, v_ref, g_ref, lb_ref, og_ref, s0_ref, tri_ref, msk_ref, y_ref, so_ref, st_ref, *, tc):
    ct = pl.program_id(2)

    @pl.when(ct == 0)
    def _():
        st_ref[...] = s0_ref[0, 0]

    z = fz_ref[0]
    q = q_ref[0]
    v = v_ref[0]
    lb = lb_ref[0]
    ez = jnp.exp(-jnp.abs(z))
    den = 1.0 / (1.0 + ez)
    pos = z >= 0.0
    f = lb + (1.0 - lb) * (jnp.where(pos, 1.0, ez) * den)
    kk = (1.0 - lb) * (jnp.where(pos, ez, 1.0) * den)
    cum = _dot_01x(tri_ref[...], jnp.log(f))

    def rows(idx):
        parts = []
        for i in idx:
            parts.append(jnp.zeros((8, LANES), F32) if i < 0 else jnp.broadcast_to(cum[i:i + 1, :], (8, LANES)))
        return jnp.concatenate(parts, axis=0)

    levels = _hgrn2_levels(tc)
    ngrp = tc // 8
    scores = jnp.zeros((tc, tc), F32)
    ql8 = None
    for li, n in enumerate(levels):
        start = [((r * 8) // n) * n for r in range(ngrp)]
        a_start = rows([s - 1 for s in start])
        a_end = rows([s + n - 1 for s in start])
        ql = (q * jnp.exp(cum - a_start)).astype(BF16)
        kr = (kk * jnp.exp(a_end - cum)).astype(BF16)
        scores = scores + msk_ref[li] * _nt(ql, kr)
        if n == 8:
            ql8 = ql
            kb = (kk * jnp.exp(a_start - cum)).astype(BF16)
    if ql8 is None:
        a_start = rows([r * 8 - 1 for r in range(ngrp)])
        ql8 = (q * jnp.exp(cum - a_start)).astype(BF16)
        kb = (kk * jnp.exp(a_start - cum)).astype(BF16)
    scores = scores + msk_ref[len(levels)] * _nt(ql8, kb)

    st = st_ref[...]
    o = _nt((q * jnp.exp(cum)).astype(BF16), st.astype(BF16)) + _dot(scores.astype(BF16), v.astype(BF16))
    a_last = cum[tc - 1:tc, :]
    khat = (kk * jnp.exp(a_last - cum)).astype(BF16)
    st_new = st * jnp.exp(a_last) + _dot(v.T.astype(BF16), khat)
    st_ref[...] = st_new

    g = g_ref[0]
    on = o * lax.rsqrt(jnp.mean(o * o, axis=-1, keepdims=True) + EPS) * og_ref[...]
    y_ref[0] = on * (g * (1.0 / (1.0 + jnp.exp(-g))))

    @pl.when(ct == pl.num_programs(2) - 1)
    def _():
        so_ref[0, 0] = st_new


def hgrn2_recurrence(proj, lb, out_gain, s0_t, *, t):
    bsz = proj.shape[0]
    tc = min(t, 128)
    nlv = len(_hgrn2_levels(tc)) + 1
    tri = (jnp.arange(tc)[:, None] >= jnp.arange(tc)[None, :]).astype(BF16)
    h = C_HEADS
    kern = functools.partial(_hgrn2_kernel, tc=tc)
    blk = lambda off: pl.BlockSpec((1, tc, LANES), lambda b, hh, c: (b, c, off + hh))
    return pl.pallas_call(
        kern,
        grid=(bsz, h, t // tc),
        in_specs=[blk(0), blk(h), blk(2 * h), blk(3 * h),
                  pl.BlockSpec((1, 1, C_DK), lambda b, hh, c: (hh, 0, 0)),
                  pl.BlockSpec((1, C_DV), lambda b, hh, c: (0, 0)),
                  pl.BlockSpec((1, 1, C_DV, C_DK), lambda b, hh, c: (b, hh, 0, 0)),
                  pl.BlockSpec((tc, tc), lambda b, hh, c: (0, 0)),
                  pl.BlockSpec((nlv, tc, tc), lambda b, hh, c: (0, 0, 0))],
        out_specs=[pl.BlockSpec((1, tc, LANES), lambda b, hh, c: (b, c, hh)),
                   pl.BlockSpec((1, 1, C_DV, C_DK), lambda b, hh, c: (b, hh, 0, 0))],
        out_shape=[jax.ShapeDtypeStruct((bsz, t, h * C_DV), F32),
                   jax.ShapeDtypeStruct((bsz, h, C_DV, C_DK), F32)],
        scratch_shapes=[pltpu.VMEM((C_DV, C_DK), F32)],
        compiler_params=_cparams("parallel", "parallel", "arbitrary"),
        name="hgrn2",
    )(proj, proj, proj, proj, lb.reshape(h, 1, C_DK), out_gain.reshape(1, C_DV), s0_t, tri, _hgrn2_masks(tc))


def _moe_kernel(x_ref, gain_ref, sc_ref, sh_ref, g2_ref, wr_ref, br_ref, wg_ref, wu_ref, wd_ref, o_ref,
                hb_ref, gates_ref, acc_ref):
    e = pl.program_id(2)
    tm = x_ref.shape[1]
    lane = lax.broadcasted_iota(I32, (tm, LANES), 1)

    @pl.when(e == 0)
    def _():
        h = _ln_mod(x_ref[0], gain_ref[...], sc_ref[0], sh_ref[0])
        hb_ref[...] = h.astype(BF16)
        acc_ref[...] = jnp.zeros_like(acc_ref)
        r = _dot_f32(h, wr_ref[...]) + br_ref[...]
        lanef = lane.astype(F32)
        big = float(LANES)
        is_g = lane < N_GROUPS
        lg = jnp.where(is_g, r, -jnp.inf)
        mg = jnp.max(lg, axis=-1, keepdims=True)
        grp = jnp.min(jnp.where(lg == mg, lanef, big), axis=-1, keepdims=True)
        p_grp = 1.0 / jnp.sum(jnp.where(is_g, jnp.exp(r - mg), 0.0), axis=-1, keepdims=True)
        eg = ((lane - N_GROUPS) >> 2).astype(F32)
        in_e = (lane >= N_GROUPS) & (lane < N_GROUPS + N_EXPERTS) & (eg == grp)
        le = jnp.where(in_e, r, -jnp.inf)
        v1 = jnp.max(le, axis=-1, keepdims=True)
        i1 = jnp.min(jnp.where(le == v1, lanef, big), axis=-1, keepdims=True)
        le2 = jnp.where(lanef == i1, -jnp.inf, le)
        v2 = jnp.max(le2, axis=-1, keepdims=True)
        i2 = jnp.min(jnp.where(le2 == v2, lanef, big), axis=-1, keepdims=True)
        e2 = jnp.exp(v2 - v1)
        w1 = 1.0 / (1.0 + e2)
        gates_ref[...] = (jnp.where(lanef == i1, w1 * p_grp, 0.0)
                          + jnp.where(lanef == i2, (e2 * w1) * p_grp, 0.0))

    ge = jnp.sum(jnp.where(lane == e + N_GROUPS, gates_ref[...], 0.0), axis=-1, keepdims=True)
    hb = hb_ref[...]
    a = _dot(hb, wg_ref[0])
    u = _dot(hb, wu_ref[0])
    he = (a * (1.0 / (1.0 + jnp.exp(-a)))) * u * ge
    acc_ref[...] += _dot(he.astype(BF16), wd_ref[0])

    @pl.when(e == pl.num_programs(2) - 1)
    def _():
        o_ref[0] = x_ref[0] + g2_ref[0] * acc_ref[...]


def moe_layer(x, gain, sc, sh, g2, w_router, b_router, w_gate, w_up, w_down):
    bsz, t, d = x.shape
    tm = min(t, 512)
    ne, _, de = w_gate.shape
    return pl.pallas_call(
        _moe_kernel,
        grid=(bsz, t // tm, ne),
        in_specs=[pl.BlockSpec((1, tm, d), lambda b, i, e: (b, i, 0)),
                  pl.BlockSpec((1, d), lambda b, i, e: (0, 0)),
                  pl.BlockSpec((1, 1, d), lambda b, i, e: (b, 0, 0)),
                  pl.BlockSpec((1, 1, d), lambda b, i, e: (b, 0, 0)),
                  pl.BlockSpec((1, 1, d), lambda b, i, e: (b, 0, 0)),
                  pl.BlockSpec((d, LANES), lambda b, i, e: (0, 0)),
                  pl.BlockSpec((1, LANES), lambda b, i, e: (0, 0)),
                  pl.BlockSpec((1, d, de), lambda b, i, e: (e, 0, 0)),
                  pl.BlockSpec((1, d, de), lambda b, i, e: (e, 0, 0)),
                  pl.BlockSpec((1, de, d), lambda b, i, e: (e, 0, 0))],
        out_specs=pl.BlockSpec((1, tm, d), lambda b, i, e: (b, i, 0)),
        out_shape=jax.ShapeDtypeStruct((bsz, t, d), F32),
        scratch_shapes=[pltpu.VMEM((tm, d), BF16), pltpu.VMEM((tm, LANES), F32), pltpu.VMEM((tm, d), F32)],
        compiler_params=_cparams("parallel", "parallel", "arbitrary"),
        name="moe",
    )(x, gain.reshape(1, d), sc, sh, g2, w_router, b_router, w_gate, w_up, w_down)


def _pad_cols(w, n):
    return jnp.pad(w, ((0, 0), (0, n - w.shape[1])))


def _pad_rows(a, n):
    return jnp.pad(a, ((0, 0), (0, n - a.shape[1])) + ((0, 0),) * (a.ndim - 2))


def _round_up(n, m):
    return -(-n // m) * m


def _head_cols(gain_q, nq, gain_k, nk, npad):
    cg = jnp.concatenate([jnp.tile(gain_q, nq), jnp.tile(gain_k, nk)])
    n = cg.shape[0]
    cgain = jnp.pad(cg, (0, npad - n)).reshape(1, npad)
    cflag = (jnp.arange(npad) < n).astype(F32).reshape(1, npad)
    return cgain, cflag


def _values_t_aug(vh, lp):
    bsz, h = vh.shape[:2]
    ones = jnp.ones((bsz, h, 1, lp), BF16)
    zv = jnp.zeros((bsz, h, LANES - HEAD_DIM - 1, lp), BF16)
    both = jnp.stack([jnp.concatenate([vh, ones, zv], axis=2), jnp.concatenate([ones, zv, vh], axis=2)])
    return both.reshape(2, bsz, h, LANES, lp // KEY_TILE, KEY_TILE).transpose(0, 1, 2, 4, 3, 5)


def _dsa_layer(x, mod, past, prm, bdiag, bsub):
    sh1, sc1, g1 = mod
    past_k, past_v, past_ki = past
    bsz, t, d = x.shape
    p = 0 if past_k is None else past_k.shape[1]
    n_keys = p + t
    topk = min(TOPK_MAX, n_keys // 4)
    tn = 256
    n_in = prm['w_in'].shape[1]
    npad = _round_up(n_in, tn)
    nqk = (A_HEADS + A_KV_HEADS) * HEAD_DIM
    w = _pad_cols(prm['w_in'], npad).astype(BF16)
    cgain, cflag = _head_cols(prm['q_norm'], A_HEADS, prm['k_norm'], A_KV_HEADS, npad)
    proj = ln_proj(x, prm['norm'], sc1, sh1, w, cgain, cflag, nqk // tn, tn)
    o_k = A_HEADS * HEAD_DIM
    o_v = o_k + A_KV_HEADS * HEAD_DIM
    o_qi = o_v + A_KV_HEADS * HEAD_DIM
    o_ki = o_qi + IDX_HEADS * IDX_DIM
    k_new = proj[..., o_k:o_v].reshape(bsz, t, A_KV_HEADS, HEAD_DIM)
    v_new = proj[..., o_v:o_qi].reshape(bsz, t, A_KV_HEADS, HEAD_DIM)
    ki_new = proj[..., o_ki:o_ki + IDX_DIM]
    if p:
        k_all = jnp.concatenate([past_k, k_new], axis=1)
        v_all = jnp.concatenate([past_v, v_new], axis=1)
        ki_all = jnp.concatenate([past_ki, ki_new], axis=1)
    else:
        k_all, v_all, ki_all = k_new, v_new, ki_new
    lp = _round_up(n_keys, KEY_TILE)
    kidup = _pad_rows(jnp.concatenate([ki_all, ki_all], axis=-1).astype(BF16), lp)
    mask = dsa_index_mask(proj, kidup, t=t, q_off=p, topk=topk, qi_blk=o_qi // (4 * LANES), w_blk=o_ki // LANES)
    kdup = _pad_rows(jnp.concatenate([k_all, k_all], axis=-1).astype(BF16).reshape(bsz, n_keys, A_KV_HEADS * LANES), lp)
    vh = _pad_rows(v_all.astype(BF16), lp).transpose(0, 2, 3, 1)
    vt_aug = _values_t_aug(vh, lp).transpose(1, 2, 0, 3, 4, 5)
    o = dsa_attention(proj, kdup, vt_aug, mask, bdiag, bsub, t=t, q_off=p)
    x = out_proj_residual(o, prm['w_out'].astype(BF16), x, g1)
    return x, (k_new, v_new, ki_new)


def _fox_layer(x, mod, past, prm):
    sh1, sc1, g1 = mod
    past_k, past_v, past_lf = past
    bsz, t, d = x.shape
    p = 0 if past_k is None else past_k.shape[1]
    n_keys = p + t
    hd = B_HEADS * HEAD_DIM
    tn = 256
    npad = _round_up(prm['w_in'].shape[1], tn)
    w = _pad_cols(prm['w_in'], npad).astype(BF16)
    cgain, cflag = _head_cols(prm['q_norm'], B_HEADS, prm['k_norm'], B_HEADS, npad)
    proj = ln_proj(x, prm['norm'], sc1, sh1, w, cgain, cflag, 2 * hd // tn, tn)
    k_new = proj[..., hd:2 * hd]
    v_new = proj[..., 2 * hd:3 * hd]
    fz = proj[..., 4 * hd:4 * hd + B_HEADS]
    tp = _round_up(t, LANES)
    fz_t = _pad_rows(fz, tp).transpose(0, 2, 1)
    past_t = None if not p else past_lf.transpose(0, 2, 1)
    lf_t, nck = fox_decay(fz_t, prm['forget_bias'], past_t, t=t)
    logf_new = lf_t[:, :, :t].transpose(0, 2, 1)
    lp = _round_up(n_keys, KEY_TILE)
    if p:
        k_all = jnp.concatenate([past_k.reshape(bsz, p, hd), k_new], axis=1)
        v_all = jnp.concatenate([past_v.reshape(bsz, p, hd), v_new], axis=1)
    else:
        k_all, v_all = k_new, v_new
    kh = _pad_rows(k_all.astype(BF16), lp).reshape(bsz, lp, B_HEADS, HEAD_DIM).transpose(0, 2, 1, 3)
    vh = _pad_rows(v_all.astype(BF16), lp).reshape(bsz, lp, B_HEADS, HEAD_DIM).transpose(0, 2, 3, 1)
    pieces = jnp.pad(nck, ((0, 0), (0, 0), (0, 0), (0, lp - nck.shape[3]))).transpose(0, 2, 3, 1)
    zk = jnp.zeros((bsz, B_HEADS, lp, LANES - HEAD_DIM - 3), BF16)
    odd = (jnp.arange(B_HEADS) % 2 == 1)[None, :, None, None]
    k_aug = jnp.where(odd, jnp.concatenate([pieces, zk, kh], axis=-1), jnp.concatenate([kh, pieces, zk], axis=-1))
    both = _values_t_aug(vh, lp)
    vt_aug = jnp.where(odd[..., None], both[1], both[0])
    o = fox_attention(proj, k_aug, vt_aug, t=t, q_off=p, g_blk0=3 * hd // LANES)
    x = out_proj_residual(o, prm['w_out'].astype(BF16), x, g1)
    return x, (k_new.reshape(bsz, t, B_HEADS, HEAD_DIM), v_new.reshape(bsz, t, B_HEADS, HEAD_DIM), logf_new)


def _hgrn2_layer(x, mod, s0, prm):
    sh1, sc1, g1 = mod
    bsz, t, d = x.shape
    npad = prm['w_in'].shape[1]
    zeros = jnp.zeros((1, npad), F32)
    proj = ln_proj(x, prm['norm'], sc1, sh1, prm['w_in'].astype(BF16), zeros, zeros, 0, 256)
    y, s_t = hgrn2_recurrence(proj, prm['lb'], prm['out_norm'], jnp.swapaxes(s0, -1, -2), t=t)
    x = out_proj_residual(y, prm['w_out'].astype(BF16), x, g1)
    return x, jnp.swapaxes(s_t, -1, -2)


def _trunk(x, c, a_k, a_v, a_kidx, b_k, b_v, b_logf, c_state, prm):
    bsz, t, d = x.shape
    mod_all = ada_mod(c, prm['w_ada'], prm['b_ada'])
    lb_all = jnp.cumsum(jax.nn.softmax(prm['c_lower_bound'].astype(F32), axis=0), axis=0)
    lb_all = lb_all - lb_all[0]
    bdiag, bsub = dsa_bias_tiles(prm['rel_table'])
    out_a, out_b, out_c = [], [], []
    for i in range(DEPTH):
        j = i // N_MIXERS
        kind = i % N_MIXERS
        sh1, sc1, g1, sh2, sc2, g2 = [m.reshape(bsz, 1, d) for m in jnp.split(mod_all[i], 6, axis=-1)]
        mod = (sh1, sc1, g1)
        if kind == 0:
            past = (None, None, None) if a_k is None else (a_k[j], a_v[j], a_kidx[j])
            lp = dict(norm=prm['norm_mix'][i], w_in=prm['a_w_in'][j], q_norm=prm['a_q_norm'][j],
                      k_norm=prm['a_k_norm'][j], w_out=prm['a_w_out'][j])
            x, new = _dsa_layer(x, mod, past, lp, bdiag, bsub)
            out_a.append(new)
        elif kind == 1:
            past = (None, None, None) if b_k is None else (b_k[j], b_v[j], b_logf[j])
            lp = dict(norm=prm['norm_mix'][i], w_in=prm['b_w_in'][j], forget_bias=prm['b_forget_bias'][j],
                      q_norm=prm['b_q_norm'][j], k_norm=prm['b_k_norm'][j], w_out=prm['b_w_out'][j])
            x, new = _fox_layer(x, mod, past, lp)
            out_b.append(new)
        else:
            s0 = jnp.zeros((bsz, C_HEADS, C_DK, C_DV), F32) if c_state is None else c_state[j]
            lp = dict(norm=prm['norm_mix'][i], w_in=prm['c_w_in'][j], lb=lb_all[i],
                      out_norm=prm['c_out_norm'][j], w_out=prm['c_w_out'][j])
            x, new = _hgrn2_layer(x, mod, s0, lp)
            out_c.append(new)
        w_router = jnp.pad(jnp.concatenate([prm['moe_w_group'][i], prm['moe_w_expert'][i]], axis=1),
                           ((0, 0), (0, LANES - N_GROUPS - N_EXPERTS)))
        b_router = jnp.pad(jnp.concatenate([prm['moe_b_group'][i], prm['moe_b_expert'][i]]),
                           (0, LANES - N_GROUPS - N_EXPERTS)).reshape(1, LANES)
        x = moe_layer(x, prm['norm_ffn'][i], sc2, sh2, g2, w_router, b_router,
                      prm['moe_w_gate'][i].astype(BF16), prm['moe_w_up'][i].astype(BF16),
                      prm['moe_w_down'][i].astype(BF16))
    stack = lambda outs, k: jnp.stack([o[k] for o in outs])
    return (x, stack(out_a, 0), stack(out_a, 1), stack(out_a, 2),
            stack(out_b, 0), stack(out_b, 1), stack(out_b, 2), jnp.stack(out_c))


def kernel(x_prompt, x_sample, cache_a_k, cache_a_v, cache_a_kidx, cache_b_k, cache_b_v, cache_b_logf, state_c,
           c_prompt, c_sample, rel_table, w_ada, b_ada, norm_mix, norm_ffn, a_w_in, a_q_norm, a_k_norm, a_w_out,
           b_w_in, b_forget_bias, b_q_norm, b_k_norm, b_w_out, c_w_in, c_lower_bound, c_out_norm, c_w_out,
           moe_w_group, moe_b_group, moe_w_expert, moe_b_expert, moe_w_gate, moe_w_up, moe_w_down):
    prm = {'rel_table': rel_table, 'w_ada': w_ada, 'b_ada': b_ada, 'norm_mix': norm_mix, 'norm_ffn': norm_ffn,
           'a_w_in': a_w_in, 'a_q_norm': a_q_norm, 'a_k_norm': a_k_norm, 'a_w_out': a_w_out,
           'b_w_in': b_w_in, 'b_forget_bias': b_forget_bias, 'b_q_norm': b_q_norm, 'b_k_norm': b_k_norm,
           'b_w_out': b_w_out, 'c_w_in': c_w_in, 'c_lower_bound': c_lower_bound, 'c_out_norm': c_out_norm,
           'c_w_out': c_w_out, 'moe_w_group': moe_w_group, 'moe_b_group': moe_b_group,
           'moe_w_expert': moe_w_expert, 'moe_b_expert': moe_b_expert, 'moe_w_gate': moe_w_gate,
           'moe_w_up': moe_w_up, 'moe_w_down': moe_w_down}
    (y_p, ak_p, av_p, ai_p, bk_p, bv_p, bl_p, cs_p) = _trunk(
        x_prompt, c_prompt, None, None, None, None, None, None, None, prm)
    (y_s, ak_s, av_s, ai_s, bk_s, bv_s, bl_s, cs_s) = _trunk(
        x_sample, c_sample, cache_a_k, cache_a_v, cache_a_kidx, cache_b_k, cache_b_v, cache_b_logf, state_c, prm)
    return (y_p, y_s, ak_p, av_p, ai_p, ak_s, av_s, ai_s, bk_p, bv_p, bl_p, bk_s, bv_s, bl_s, cs_p, cs_s)
```

```python
import functools

import jax
import jax.numpy as jnp
from jax import lax
from jax.experimental import pallas as pl
from jax.experimental.pallas import tpu as pltpu

F32 = jnp.float32
BF16 = jnp.bfloat16
I32 = jnp.int32

LANES = 128
VMEM_LIMIT_BYTES = 56 * 1024 * 1024

DEPTH = 4
N_MIXERS = 3
CHUNK = 64
EPS = 1e-6
HEAD_DIM = 64
A_HEADS = 16
A_KV_HEADS = 4
A_GROUP = A_HEADS // A_KV_HEADS
IDX_HEADS = 8
IDX_DIM = 64
TOPK_MAX = 256
REL_BUCKETS = 32
B_HEADS = 16
C_HEADS = 8
C_DK = 128
C_DV = 128
N_GROUPS = 4
EXPERTS_PER_GROUP = 4
N_EXPERTS = 16
D_EXPERT = 512

LOG2E = 1.4426950408889634
NEG_BIG = -1e30
M_FLOOR = -1e20
INT_MIN = -2 ** 31
KEY_TILE = 512
SLABS = KEY_TILE // LANES
MOE_SLOT = 128


def _cparams(*sem):
    return pltpu.CompilerParams(dimension_semantics=sem, vmem_limit_bytes=VMEM_LIMIT_BYTES)


def _nt(a, b):
    return lax.dot_general(a, b, (((1,), (1,)), ((), ())), preferred_element_type=F32)


def _split3(x):
    hi = x.astype(BF16)
    r = x - hi.astype(F32)
    mid = r.astype(BF16)
    lo = (r - mid.astype(F32)).astype(BF16)
    return hi, mid, lo


def _dot(a, b):
    return jnp.dot(a, b, preferred_element_type=F32)


def _dot_x01(x, m01):
    hi, mid, lo = _split3(x)
    return _dot(hi, m01) + _dot(mid, m01) + _dot(lo, m01)


def _dot_01x(m01, x):
    hi, mid, lo = _split3(x)
    return _dot(m01, hi) + _dot(m01, mid) + _dot(m01, lo)


def _dot_f32(a, b):
    ah, am, al = _split3(a)
    bh, bm, bl = _split3(b)
    return _dot(ah, bh) + (_dot(ah, bm) + _dot(am, bh)) + (_dot(ah, bl) + _dot(am, bm) + _dot(al, bh))


def _pad_q_rows(q, tq, tqp):
    if tqp == tq:
        return q
    return jnp.concatenate([q, jnp.zeros((tqp - tq, q.shape[1]), q.dtype)], axis=0)


def _mod_kernel(c_ref, w_ref, b_ref, o_ref):
    o_ref[0] = _dot(c_ref[...], w_ref[0]) + b_ref[0]


def ada_mod(c, w_ada, b_ada):
    nl, d, n6 = w_ada.shape
    bsz = c.shape[0]
    tn = 512
    return pl.pallas_call(
        _mod_kernel,
        grid=(nl, n6 // tn),
        in_specs=[pl.BlockSpec((bsz, d), lambda l, j: (0, 0)),
                  pl.BlockSpec((1, d, tn), lambda l, j: (l, 0, j)),
                  pl.BlockSpec((1, 1, tn), lambda l, j: (l, 0, j))],
        out_specs=pl.BlockSpec((1, bsz, tn), lambda l, j: (l, 0, j)),
        out_shape=jax.ShapeDtypeStruct((nl, bsz, n6), F32),
        compiler_params=_cparams("parallel", "parallel"),
        name="ada_mod",
    )(c, w_ada, b_ada.reshape(nl, 1, n6))


def _ln_mod(x, gain, sc, sh):
    ms = jnp.mean(x * x, axis=-1, keepdims=True)
    return (x * lax.rsqrt(ms + EPS) * gain) * (1.0 + sc) + sh


def _ln_proj_kernel(x_ref, gain_ref, sc_ref, sh_ref, w_ref, cgain_ref, cflag_ref, bd_ref, o_ref, h_ref,
                    *, n_norm_tiles, tn):
    j = pl.program_id(2)

    @pl.when(j == 0)
    def _():
        h_ref[...] = _ln_mod(x_ref[0], gain_ref[...], sc_ref[0], sh_ref[0]).astype(BF16)

    y = _dot(h_ref[...], w_ref[...])

    def plain():
        o_ref[0] = y

    def normed():
        y2 = y * y
        hi = y2.astype(BF16)
        lo = (y2 - hi.astype(F32)).astype(BF16)
        bd = bd_ref[...]
        segs = []
        for s in range(tn // LANES):
            sl = slice(s * LANES, (s + 1) * LANES)
            segs.append(_dot(hi[:, sl], bd) + _dot(lo[:, sl], bd))
        seg = jnp.concatenate(segs, axis=1)
        yn = y * lax.rsqrt(seg * (1.0 / HEAD_DIM) + EPS) * cgain_ref[...]
        o_ref[0] = jnp.where(cflag_ref[...] > 0.0, yn, y)

    if n_norm_tiles == 0:
        plain()
    else:
        pl.when(j < n_norm_tiles)(normed)
        pl.when(j >= n_norm_tiles)(plain)


def ln_proj(x, gain, sc, sh, w, cgain, cflag, n_norm_tiles, tn=256):
    bsz, t, d = x.shape
    npad = w.shape[1]
    tm = min(t, 1024)
    bd = (jnp.arange(LANES)[:, None] // HEAD_DIM == jnp.arange(LANES)[None, :] // HEAD_DIM).astype(BF16)
    kern = functools.partial(_ln_proj_kernel, n_norm_tiles=n_norm_tiles, tn=tn)
    return pl.pallas_call(
        kern,
        grid=(bsz, t // tm, npad // tn),
        in_specs=[pl.BlockSpec((1, tm, d), lambda b, i, j: (b, i, 0)),
                  pl.BlockSpec((1, d), lambda b, i, j: (0, 0)),
                  pl.BlockSpec((1, 1, d), lambda b, i, j: (b, 0, 0)),
                  pl.BlockSpec((1, 1, d), lambda b, i, j: (b, 0, 0)),
                  pl.BlockSpec((d, tn), lambda b, i, j: (0, j)),
                  pl.BlockSpec((1, tn), lambda b, i, j: (0, j)),
                  pl.BlockSpec((1, tn), lambda b, i, j: (0, j)),
                  pl.BlockSpec((LANES, LANES), lambda b, i, j: (0, 0))],
        out_specs=pl.BlockSpec((1, tm, tn), lambda b, i, j: (b, i, j)),
        out_shape=jax.ShapeDtypeStruct((bsz, t, npad), F32),
        scratch_shapes=[pltpu.VMEM((tm, d), BF16)],
        compiler_params=_cparams("parallel", "parallel", "arbitrary"),
        name="ln_proj",
    )(x, gain.reshape(1, d), sc, sh, w, cgain, cflag, bd)


def _out_proj_kernel(a_ref, w_ref, x_ref, g_ref, o_ref):
    y = _dot(a_ref[0].astype(BF16), w_ref[...])
    o_ref[0] = x_ref[0] + g_ref[0] * y


def out_proj_residual(a, w, x, gate):
    bsz, t, k = a.shape
    d = w.shape[1]
    tm = min(t, 512)
    return pl.pallas_call(
        _out_proj_kernel,
        grid=(bsz, t // tm),
        in_specs=[pl.BlockSpec((1, tm, k), lambda b, i: (b, i, 0)),
                  pl.BlockSpec((k, d), lambda b, i: (0, 0)),
                  pl.BlockSpec((1, tm, d), lambda b, i: (b, i, 0)),
                  pl.BlockSpec((1, 1, d), lambda b, i: (b, 0, 0))],
        out_specs=pl.BlockSpec((1, tm, d), lambda b, i: (b, i, 0)),
        out_shape=jax.ShapeDtypeStruct((bsz, t, d), F32),
        compiler_params=_cparams("parallel", "parallel"),
        name="out_proj",
    )(a, w, x, gate)


def _dsa_index_kernel(qi_ref, w_ref, ki_ref, o_ref, key_ref, *, tq, tqp, n_slabs, q_off, topk, idx_bits):
    a = pl.program_id(1)
    tk = KEY_TILE
    q0 = q_off + a * tq
    n_kt = (q0 + tq + tk - 1) // tk
    lane = lax.broadcasted_iota(I32, (tqp, LANES), 1)
    half = lane < IDX_DIM
    krow = lax.broadcasted_iota(I32, (tk, tqp), 0)
    qcol = lax.broadcasted_iota(I32, (tk, tqp), 1)
    qchunk = (q0 + qcol) >> 6
    srow = lax.broadcasted_iota(I32, (LANES, tqp), 0)
    w_t = (_pad_q_rows(w_ref[0], tq, tqp) * (IDX_HEADS ** -0.5)).T
    qs = []
    for p in range(IDX_HEADS // 2):
        qp = _pad_q_rows(qi_ref[0, :, p * LANES:(p + 1) * LANES], tq, tqp) * (IDX_DIM ** -0.5)
        qs.append(jnp.where(half, qp, 0.0).astype(BF16))
        qs.append(jnp.where(half, 0.0, qp).astype(BF16))

    def score_tile(c, carry):
        kt = ki_ref[0, pl.ds(pl.multiple_of(c * tk, tk), tk), :]
        sc = jnp.zeros((tk, tqp), F32)
        for h in range(IDX_HEADS):
            sc = sc + w_t[IDX_DIM + h:IDX_DIM + h + 1, :] * jnp.maximum(_nt(kt, qs[h]), 0.0)
        bits = lax.bitcast_convert_type(sc, I32)
        key = jnp.where(bits < 0, bits ^ 0x7FFFFFFF, bits)
        key = jnp.where(sc == 0.0, 0, key)
        adm = ((c * tk + krow) >> 6) <= qchunk
        key = jnp.where(adm, key, INT_MIN)
        for s_ in range(SLABS):
            key_ref[c * SLABS + s_] = key[s_ * LANES:(s_ + 1) * LANES, :]
        return carry

    lax.fori_loop(0, n_kt, score_tile, 0)

    def count(pred):
        def body(c, acc):
            for s_ in range(SLABS):
                sidx = c * SLABS + s_
                ind = jnp.where(pred(key_ref[sidx], sidx), 1.0, 0.0)
                acc = acc + jnp.sum(ind.reshape(LANES // 8, 8, tqp), axis=0)
            return acc
        acc = lax.fori_loop(0, n_kt, body, jnp.zeros((8, tqp), F32))
        return jnp.sum(acc, axis=0, keepdims=True)

    kf = float(topk)
    n_adm = count(lambda k, s: k > INT_MIN)

    def all_done(cnt_t):
        done = (cnt_t == kf) | (n_adm < kf)
        return (jnp.min(jnp.where(done, 1.0, 0.0)) > 0.0).astype(I32)

    def bit_cond(st):
        return (st[0] < 32) & (st[3] == 0)

    def bit_body(st):
        i, t_u, cnt_t, _ = st
        cand_u = t_u | lax.shift_left(jnp.int32(1), 31 - i)
        cand_s = cand_u ^ INT_MIN
        cnt = count(lambda k, s: k >= cand_s)
        take = cnt >= kf
        cnt_t = jnp.where(take, cnt, cnt_t)
        return i + 1, jnp.where(take, cand_u, t_u), cnt_t, all_done(cnt_t)

    _, t_u, cnt_ge, _ = lax.while_loop(bit_cond, bit_body,
                                       (jnp.int32(0), jnp.zeros((1, tqp), I32), n_adm, all_done(n_adm)))
    thr = t_u ^ INT_MIN
    excess = jnp.where(cnt_ge > kf, 1.0, 0.0)

    def tie_search():
        need = kf - count(lambda k, s: k > thr)

        def j_body(i, j):
            cand = j | lax.shift_left(jnp.int32(1), idx_bits - 1 - i)
            c = count(lambda k, s: (k == thr) & (srow + s * LANES < cand))
            return jnp.where(c < need, cand, j)
        return lax.fori_loop(0, idx_bits, j_body, jnp.zeros((1, tqp), I32))

    j_last = lax.cond(jnp.max(excess) > 0.0, tie_search, lambda: jnp.full((1, tqp), 2 ** idx_bits, I32))

    def write_active(c, carry):
        for s_ in range(SLABS):
            sidx = c * SLABS + s_
            k = key_ref[sidx]
            sel = (k > thr) | ((k == thr) & (srow + sidx * LANES <= j_last))
            sel = sel & (k > INT_MIN)
            o_ref[0, sidx] = jnp.where(sel, 0.0, NEG_BIG)
        return carry

    lax.fori_loop(0, n_kt, write_active, 0)
    neg = jnp.full((LANES, tqp), NEG_BIG, F32)

    def write_inactive(s, carry):
        o_ref[0, s] = neg
        return carry

    lax.fori_loop(n_kt * SLABS, n_slabs, write_inactive, 0)


def dsa_index_mask(proj, kidup, *, t, q_off, topk, qi_blk, w_blk):
    bsz = proj.shape[0]
    lp = kidup.shape[1]
    n_slabs = lp // LANES
    tq = min(t, KEY_TILE)
    tqp = max(tq, LANES)
    idx_bits = max(1, (lp - 1).bit_length())
    kern = functools.partial(_dsa_index_kernel, tq=tq, tqp=tqp, n_slabs=n_slabs, q_off=q_off, topk=topk,
                             idx_bits=idx_bits)
    return pl.pallas_call(
        kern,
        grid=(bsz, t // tq),
        in_specs=[pl.BlockSpec((1, tq, 4 * LANES), lambda b, a: (b, a, qi_blk)),
                  pl.BlockSpec((1, tq, LANES), lambda b, a: (b, a, w_blk)),
                  pl.BlockSpec((1, lp, LANES), lambda b, a: (b, 0, 0))],
        out_specs=pl.BlockSpec((1, n_slabs, LANES, tqp), lambda b, a: (b, 0, 0, a)),
        out_shape=jax.ShapeDtypeStruct((bsz, n_slabs, LANES, (t // tq) * tqp), F32),
        scratch_shapes=[pltpu.VMEM((n_slabs, LANES, tqp), I32)],
        compiler_params=_cparams("parallel", "parallel"),
        name="dsa_index",
    )(proj, proj, kidup)


def _bias_kernel(tab_ref, diag_ref, sub_ref):
    h = pl.program_id(0)
    far = tab_ref[REL_BUCKETS // 2 - 1, h]

    def bias(rel):
        n = jnp.abs(rel)
        large = jnp.full(rel.shape, 8, I32)
        for th in (12, 16, 23, 32, 46, 64, 91):
            large = large + jnp.where(n >= th, 1, 0)
        bucket = jnp.where(rel > 0, REL_BUCKETS // 2, 0) + jnp.where(n < 8, n, large)
        acc = jnp.zeros(rel.shape, F32)
        for bk in range(REL_BUCKETS):
            acc = jnp.where(bucket == bk, tab_ref[bk, h], acc)
        return (acc - far) * LOG2E

    ik = lax.broadcasted_iota(I32, (KEY_TILE, KEY_TILE), 0)
    iq = lax.broadcasted_iota(I32, (KEY_TILE, KEY_TILE), 1)
    diag_ref[0] = bias(ik - iq)
    ik = lax.broadcasted_iota(I32, (LANES, LANES), 0)
    iq = lax.broadcasted_iota(I32, (LANES, LANES), 1)
    sub_ref[0] = bias(ik - LANES - iq)


def dsa_bias_tiles(rel_table):
    return pl.pallas_call(
        _bias_kernel,
        grid=(A_HEADS,),
        in_specs=[pl.BlockSpec(memory_space=pltpu.SMEM)],
        out_specs=[pl.BlockSpec((1, KEY_TILE, KEY_TILE), lambda h: (h, 0, 0)),
                   pl.BlockSpec((1, LANES, LANES), lambda h: (h, 0, 0))],
        out_shape=[jax.ShapeDtypeStruct((A_HEADS, KEY_TILE, KEY_TILE), F32),
                   jax.ShapeDtypeStruct((A_HEADS, LANES, LANES), F32)],
        compiler_params=_cparams("parallel"),
        name="dsa_bias",
    )(rel_table)


def _dsa_attn_kernel(q_ref, k_ref, vt_ref, msk_ref, bd_ref, bs_ref, o_ref, acc_ref, m_ref, s0_ref,
                     *, tq, tqp, q_off):
    a = pl.program_id(1)
    tk = KEY_TILE
    q0 = q_off + a * tq
    cd = q0 // tk
    cs = jnp.maximum(cd - 1, 0)
    lane = lax.broadcasted_iota(I32, (tqp, LANES), 1)
    half = lane < HEAD_DIM
    qs = []
    for e in range(A_GROUP):
        qc = _pad_q_rows(q_ref[0, :, (e // 2) * LANES:(e // 2 + 1) * LANES], tq, tqp) * (HEAD_DIM ** -0.5 * LOG2E)
        qs.append((jnp.where(half, qc, 0.0) if e % 2 == 0 else jnp.where(half, 0.0, qc)).astype(BF16))
    m_ref[...] = jnp.full(m_ref.shape, M_FLOOR, F32)
    acc_ref[...] = jnp.zeros(acc_ref.shape, F32)

    def s_tile(e, c):
        return _nt(k_ref[0, pl.ds(pl.multiple_of(c * tk, tk), tk), :], qs[e])

    def softmax_pv(e, c, s, after, kind):
        s = s + jnp.concatenate([msk_ref[0, c * SLABS + i] for i in range(SLABS)], axis=0)
        if kind == "diag":
            s = s + bd_ref[e, :, 0:tqp]
        elif kind == "sub":
            corner = bs_ref[e]
            if tqp > LANES:
                corner = jnp.concatenate([corner, jnp.zeros((LANES, tqp - LANES), F32)], axis=1)
            s = s + jnp.concatenate([jnp.zeros((tk - LANES, tqp), F32), corner], axis=0)
            s = jnp.where(cd > 0, s, NEG_BIG)
        m_old = jnp.minimum(m_ref[e], jnp.maximum(after[0:1, :], -NEG_BIG))
        m_new = jnp.maximum(m_old, jnp.max(s, axis=0, keepdims=True))
        p = jnp.exp2(s - m_new).astype(BF16)
        acc_ref[e] = jnp.exp2(m_old - m_new) * acc_ref[e] + _dot(vt_ref[0, 0, e % 2, c], p)
        m_ref[e] = m_new

    def step(c, kind):
        s_prev = s0_ref[...]
        for e in range(A_GROUP):
            if e + 1 < A_GROUP:
                s_next = s_tile(e + 1, c)
            else:
                s_next = s_tile(0, jnp.minimum(c + 1, cd))
                s0_ref[...] = s_next
            softmax_pv(e, c, s_prev, s_next, kind)
            s_prev = s_next

    def far_body(c, carry):
        step(c, "far")
        return carry

    s0_ref[...] = s_tile(0, 0)
    lax.fori_loop(0, cs, far_body, 0)
    step(cs, "sub")
    step(cd, "diag")
    for c2 in range(A_GROUP // 2):
        t_e = acc_ref[2 * c2].T
        t_o = acc_ref[2 * c2 + 1].T
        o = jnp.where(half, t_e / t_e[:, HEAD_DIM:HEAD_DIM + 1], t_o / t_o[:, 0:1])
        o_ref[0, :, c2 * LANES:(c2 + 1) * LANES] = o[:tq]


def dsa_attention(proj, kdup, vt_aug, mask, bdiag, bsub, *, t, q_off):
    bsz = proj.shape[0]
    lp = kdup.shape[1]
    tq = min(t, KEY_TILE)
    tqp = max(tq, LANES)
    nt = lp // KEY_TILE
    nsl = mask.shape[1]
    kern = functools.partial(_dsa_attn_kernel, tq=tq, tqp=tqp, q_off=q_off)
    return pl.pallas_call(
        kern,
        grid=(bsz, t // tq, A_KV_HEADS),
        in_specs=[pl.BlockSpec((1, tq, 2 * LANES), lambda b, a, g: (b, a, g)),
                  pl.BlockSpec((1, lp, LANES), lambda b, a, g: (b, 0, g)),
                  pl.BlockSpec((1, 1, 2, nt, LANES, KEY_TILE), lambda b, a, g: (b, g, 0, 0, 0, 0)),
                  pl.BlockSpec((1, nsl, LANES, tqp), lambda b, a, g: (b, 0, 0, a)),
                  pl.BlockSpec((A_GROUP, KEY_TILE, tqp), lambda b, a, g: (g, 0, 0)),
                  pl.BlockSpec((A_GROUP, LANES, LANES), lambda b, a, g: (g, 0, 0))],
        out_specs=pl.BlockSpec((1, tq, 2 * LANES), lambda b, a, g: (b, a, g)),
        out_shape=jax.ShapeDtypeStruct((bsz, t, A_HEADS * HEAD_DIM), F32),
        scratch_shapes=[pltpu.VMEM((A_GROUP, LANES, tqp), F32), pltpu.VMEM((A_GROUP, 1, tqp), F32),
                        pltpu.VMEM((KEY_TILE, tqp), F32)],
        compiler_params=_cparams("parallel", "parallel", "arbitrary"),
        name="dsa_attn",
    )(proj, kdup, vt_aug, mask, bdiag, bsub)


def _fox_decay_kernel(*refs, n_past, n_new, t):
    if n_past:
        fz_ref, bf_ref, past_ref, tri_ref, lf_ref, nck_ref = refs
    else:
        fz_ref, bf_ref, tri_ref, lf_ref, nck_ref = refs
        past_ref = None
    tri = tri_ref[...]
    lane = lax.broadcasted_iota(I32, (B_HEADS, LANES), 1)
    carry = jnp.zeros((B_HEADS, 1), F32)
    for blk in range(n_past + n_new):
        sl = slice(blk * LANES, (blk + 1) * LANES)
        if blk < n_past:
            lf = past_ref[0, :, sl]
        else:
            nsl = slice((blk - n_past) * LANES, (blk - n_past + 1) * LANES)
            x = fz_ref[0, :, nsl] + bf_ref[...]
            lf = jnp.minimum(x, 0.0) - jnp.log1p(jnp.exp(-jnp.abs(x)))
            lf = jnp.where(lane + (blk - n_past) * LANES < t, lf, 0.0)
            lf_ref[0, :, nsl] = lf
        cum = _dot_x01(lf, tri) + carry
        for i, piece in enumerate(_split3(cum * -LOG2E)):
            nck_ref[0, i, :, sl] = piece
        carry = cum[:, LANES - 1:LANES]


def fox_decay(fz_t, b_f, past_t, *, t):
    bsz, h, tp = fz_t.shape
    p = 0 if past_t is None else past_t.shape[2]
    n_past, n_new = p // LANES, tp // LANES
    tri = (jnp.arange(LANES)[:, None] <= jnp.arange(LANES)[None, :]).astype(BF16)
    kern = functools.partial(_fox_decay_kernel, n_past=n_past, n_new=n_new, t=t)
    args = [fz_t, b_f.reshape(h, 1)]
    in_specs = [pl.BlockSpec((1, h, tp), lambda b: (b, 0, 0)),
                pl.BlockSpec((h, 1), lambda b: (0, 0))]
    if n_past:
        args.append(past_t)
        in_specs.append(pl.BlockSpec((1, h, p), lambda b: (b, 0, 0)))
    args.append(tri)
    in_specs.append(pl.BlockSpec((LANES, LANES), lambda b: (0, 0)))
    return pl.pallas_call(
        kern,
        grid=(bsz,),
        in_specs=in_specs,
        out_specs=[pl.BlockSpec((1, h, tp), lambda b: (b, 0, 0)),
                   pl.BlockSpec((1, 3, h, p + tp), lambda b: (b, 0, 0, 0))],
        out_shape=[jax.ShapeDtypeStruct((bsz, h, tp), F32),
                   jax.ShapeDtypeStruct((bsz, 3, h, p + tp), BF16)],
        compiler_params=_cparams("parallel"),
        name="fox_decay",
    )(*args)


def _fox_attn_kernel(q_ref, k_ref, vt_ref, g_ref, o_ref, acc_ref, m_ref, s0_ref, *, tq, tqp, q_off):
    a = pl.program_id(2)
    tk = KEY_TILE
    q0 = q_off + a * tq
    n_full = q0 // tk
    n_need = (q0 + tq - 1) // tk + 1
    lane = lax.broadcasted_iota(I32, (tqp, LANES), 1)
    qn = _pad_q_rows(q_ref[0], tq, tqp) * (HEAD_DIM ** -0.5 * LOG2E)
    qs = (jnp.where(lane < HEAD_DIM, qn, jnp.where(lane < HEAD_DIM + 3, 1.0, 0.0)).astype(BF16),
          jnp.where(lane >= HEAD_DIM, qn, jnp.where(lane < 3, 1.0, 0.0)).astype(BF16))
    m_ref[...] = jnp.full(m_ref.shape, M_FLOOR, F32)
    acc_ref[...] = jnp.zeros(acc_ref.shape, F32)
    krow = lax.broadcasted_iota(I32, (tk, tqp), 0)
    qcol = lax.broadcasted_iota(I32, (tk, tqp), 1)

    def s_tile(e, c):
        return _nt(k_ref[0, e, pl.ds(pl.multiple_of(c * tk, tk), tk), :], qs[e])

    def softmax_pv(e, c, s, masked):
        if masked:
            s = jnp.where(c * tk + krow <= q0 + qcol, s, NEG_BIG)
        m_old = m_ref[e]
        m_new = jnp.maximum(m_old, jnp.max(s, axis=0, keepdims=True))
        p = jnp.exp2(s - m_new).astype(BF16)
        acc_ref[e] = jnp.exp2(m_old - m_new) * acc_ref[e] + _dot(vt_ref[0, e, c], p)
        m_ref[e] = m_new

    def step(c, masked):
        s1 = s_tile(1, c)
        softmax_pv(0, c, s0_ref[...], masked)
        s0_ref[...] = s_tile(0, jnp.minimum(c + 1, n_need - 1))
        softmax_pv(1, c, s1, masked)

    def full_body(c, carry):
        step(c, False)
        return carry

    def masked_body(c, carry):
        step(c, True)
        return carry

    s0_ref[...] = s_tile(0, 0)
    lax.fori_loop(0, n_full, full_body, 0)
    lax.fori_loop(n_full, n_need, masked_body, 0)
    acc_e = acc_ref[0].T
    acc_o = acc_ref[1].T
    o = jnp.where(lane < HEAD_DIM, acc_e / acc_e[:, HEAD_DIM:HEAD_DIM + 1], acc_o / acc_o[:, 0:1])
    o_ref[0] = o[:tq] * (1.0 / (1.0 + jnp.exp(-g_ref[0])))


def fox_attention(proj, k_aug, vt_aug, *, t, q_off, g_blk0):
    bsz = proj.shape[0]
    lp = k_aug.shape[2]
    tq = min(t, KEY_TILE)
    tqp = max(tq, LANES)
    nt = lp // KEY_TILE
    kern = functools.partial(_fox_attn_kernel, tq=tq, tqp=tqp, q_off=q_off)
    return pl.pallas_call(
        kern,
        grid=(bsz, B_HEADS // 2, t // tq),
        in_specs=[pl.BlockSpec((1, tq, LANES), lambda b, j, a: (b, a, j)),
                  pl.BlockSpec((1, 2, lp, LANES), lambda b, j, a: (b, j, 0, 0)),
                  pl.BlockSpec((1, 2, nt, LANES, KEY_TILE), lambda b, j, a: (b, j, 0, 0, 0)),
                  pl.BlockSpec((1, tq, LANES), lambda b, j, a: (b, a, g_blk0 + j))],
        out_specs=pl.BlockSpec((1, tq, LANES), lambda b, j, a: (b, a, j)),
        out_shape=jax.ShapeDtypeStruct((bsz, t, B_HEADS * HEAD_DIM), F32),
        scratch_shapes=[pltpu.VMEM((2, LANES, tqp), F32), pltpu.VMEM((2, 1, tqp), F32),
                        pltpu.VMEM((KEY_TILE, tqp), F32)],
        compiler_params=_cparams("parallel", "parallel", "parallel"),
        name="fox_attn",
    )(proj, k_aug, vt_aug, proj)


def _hgrn2_levels(tc):
    lv = []
    n = 8
    while n < tc:
        lv.append(n)
        n *= 2
    return lv


def _hgrn2_masks(tc):
    t = jnp.arange(tc)[:, None]
    s = jnp.arange(tc)[None, :]
    ms = [((t // (2 * n) == s // (2 * n)) & ((t // n) % 2 == 1) & ((s // n) % 2 == 0)) for n in _hgrn2_levels(tc)]
    ms.append((t // 8 == s // 8) & (s <= t))
    return jnp.stack(ms).astype(F32)


def _hgrn2_kernel(q_ref, fz_ref, v_ref, g_ref, lb_ref, og_ref, s0_ref, tri_ref, msk_ref, y_ref, so_ref, st_ref, *, tc):
    ct = pl.program_id(2)

    @pl.when(ct == 0)
    def _():
        st_ref[...] = s0_ref[0, 0]

    z = fz_ref[0]
    q = q_ref[0]
    v = v_ref[0]
    lb = lb_ref[0]
    ez = jnp.exp(-jnp.abs(z))
    den = 1.0 / (1.0 + ez)
    pos = z >= 0.0
    f = lb + (1.0 - lb) * (jnp.where(pos, 1.0, ez) * den)
    kk = (1.0 - lb) * (jnp.where(pos, ez, 1.0) * den)
    cum = _dot_01x(tri_ref[...], jnp.log(f))

    def rows(idx):
        parts = []
        for i in idx:
            parts.append(jnp.zeros((8, LANES), F32) if i < 0 else jnp.broadcast_to(cum[i:i + 1, :], (8, LANES)))
        return jnp.concatenate(parts, axis=0)

    levels = _hgrn2_levels(tc)
    ngrp = tc // 8
    scores = jnp.zeros((tc, tc), F32)
    ql8 = None
    for li, n in enumerate(levels):
        start = [((r * 8) // n) * n for r in range(ngrp)]
        a_start = rows([s - 1 for s in start])
        a_end = rows([s + n - 1 for s in start])
        ql = (q * jnp.exp(cum - a_start)).astype(BF16)
        kr = (kk * jnp.exp(a_end - cum)).astype(BF16)
        scores = scores + msk_ref[li] * _nt(ql, kr)
        if n == 8:
            ql8 = ql
            kb = (kk * jnp.exp(a_start - cum)).astype(BF16)
    if ql8 is None:
        a_start = rows([r * 8 - 1 for r in range(ngrp)])
        ql8 = (q * jnp.exp(cum - a_start)).astype(BF16)
        kb = (kk * jnp.exp(a_start - cum)).astype(BF16)
    scores = scores + msk_ref[len(levels)] * _nt(ql8, kb)

    st = st_ref[...]
    o = _nt((q * jnp.exp(cum)).astype(BF16), st.astype(BF16)) + _dot(scores.astype(BF16), v.astype(BF16))
    a_last = cum[tc - 1:tc, :]
    khat = (kk * jnp.exp(a_last - cum)).astype(BF16)
    st_new = st * jnp.exp(a_last) + _dot(v.T.astype(BF16), khat)
    st_ref[...] = st_new

    g = g_ref[0]
    on = o * lax.rsqrt(jnp.mean(o * o, axis=-1, keepdims=True) + EPS) * og_ref[...]
    y_ref[0] = on * (g * (1.0 / (1.0 + jnp.exp(-g))))

    @pl.when(ct == pl.num_programs(2) - 1)
    def _():
        so_ref[0, 0] = st_new


def hgrn2_recurrence(proj, lb, out_gain, s0_t, *, t):
    bsz = proj.shape[0]
    tc = min(t, 128)
    nlv = len(_hgrn2_levels(tc)) + 1
    tri = (jnp.arange(tc)[:, None] >= jnp.arange(tc)[None, :]).astype(BF16)
    h = C_HEADS
    kern = functools.partial(_hgrn2_kernel, tc=tc)
    blk = lambda off: pl.BlockSpec((1, tc, LANES), lambda b, hh, c: (b, c, off + hh))
    return pl.pallas_call(
        kern,
        grid=(bsz, h, t // tc),
        in_specs=[blk(0), blk(h), blk(2 * h), blk(3 * h),
                  pl.BlockSpec((1, 1, C_DK), lambda b, hh, c: (hh, 0, 0)),
                  pl.BlockSpec((1, C_DV), lambda b, hh, c: (0, 0)),
                  pl.BlockSpec((1, 1, C_DV, C_DK), lambda b, hh, c: (b, hh, 0, 0)),
                  pl.BlockSpec((tc, tc), lambda b, hh, c: (0, 0)),
                  pl.BlockSpec((nlv, tc, tc), lambda b, hh, c: (0, 0, 0))],
        out_specs=[pl.BlockSpec((1, tc, LANES), lambda b, hh, c: (b, c, hh)),
                   pl.BlockSpec((1, 1, C_DV, C_DK), lambda b, hh, c: (b, hh, 0, 0))],
        out_shape=[jax.ShapeDtypeStruct((bsz, t, h * C_DV), F32),
                   jax.ShapeDtypeStruct((bsz, h, C_DV, C_DK), F32)],
        scratch_shapes=[pltpu.VMEM((C_DV, C_DK), F32)],
        compiler_params=_cparams("parallel", "parallel", "arbitrary"),
        name="hgrn2",
    )(proj, proj, proj, proj, lb.reshape(h, 1, C_DK), out_gain.reshape(1, C_DV), s0_t, tri, _hgrn2_masks(tc))


def _route(h, wr, br):
    r = _dot_f32(h, wr) + br
    lane = lax.broadcasted_iota(I32, r.shape, 1)
    lanef = lane.astype(F32)
    big = float(LANES)
    is_g = lane < N_GROUPS
    lg = jnp.where(is_g, r, -jnp.inf)
    mg = jnp.max(lg, axis=-1, keepdims=True)
    grp = jnp.min(jnp.where(lg == mg, lanef, big), axis=-1, keepdims=True)
    p_grp = 1.0 / jnp.sum(jnp.where(is_g, jnp.exp(r - mg), 0.0), axis=-1, keepdims=True)
    eg = ((lane - N_GROUPS) >> 2).astype(F32)
    in_e = (lane >= N_GROUPS) & (lane < N_GROUPS + N_EXPERTS) & (eg == grp)
    le = jnp.where(in_e, r, -jnp.inf)
    v1 = jnp.max(le, axis=-1, keepdims=True)
    i1 = jnp.min(jnp.where(le == v1, lanef, big), axis=-1, keepdims=True)
    le2 = jnp.where(lanef == i1, -jnp.inf, le)
    v2 = jnp.max(le2, axis=-1, keepdims=True)
    i2 = jnp.min(jnp.where(le2 == v2, lanef, big), axis=-1, keepdims=True)
    e2 = jnp.exp(v2 - v1)
    w1 = 1.0 / (1.0 + e2)
    gates = jnp.where(lanef == i1, w1 * p_grp, 0.0) + jnp.where(lanef == i2, (e2 * w1) * p_grp, 0.0)
    member = jnp.where((lanef == i1) | (lanef == i2), 1.0, 0.0)
    return gates, member


def _moe_kernel(x_ref, gain_ref, sc_ref, sh_ref, g2_ref, wr_ref, br_ref, tri_ref, wg_ref, wu_ref, wd_ref, o_ref,
                hb_ref, rank_ref, rt_ref, gt_ref, y_ref, acc_ref, *, ts, slot):
    e = pl.program_id(2)
    tm = x_ref.shape[1]
    n_sub = tm // ts
    tsp = rt_ref.shape[2]
    n_chunk = tsp // slot
    row = e + N_GROUPS

    @pl.when(e == 0)
    def _():
        for s in range(n_sub):
            h = _ln_mod(x_ref[0, s * ts:(s + 1) * ts, :], gain_ref[...], sc_ref[0], sh_ref[0])
            gates, member = _route(h, wr_ref[...], br_ref[...])
            hb_ref[s * tsp:(s + 1) * tsp, :] = _pad_q_rows(h.astype(BF16), ts, tsp)
            m_t = _pad_q_rows(member, ts, tsp).T
            r_t = _dot(m_t.astype(BF16), tri_ref[...])
            r_t = jnp.where(m_t > 0.0, r_t, -1.0)
            rt_ref[s] = r_t
            gt_ref[s] = _pad_q_rows(gates, ts, tsp).T
            rank_ref[s] = r_t.T
        if n_chunk > 1:
            acc_ref[...] = jnp.zeros_like(acc_ref)

    def expert_rows(k):
        xs, ges, sels = [], [], []
        slot_i = lax.broadcasted_iota(I32, (slot, tsp), 0).astype(F32) + float(k * slot)
        for s in range(n_sub):
            sel = jnp.where(rt_ref[s, pl.ds(row, 1), :] == slot_i, 1.0, 0.0)
            ges.append(jnp.sum(sel * gt_ref[s, pl.ds(row, 1), :], axis=-1, keepdims=True))
            xs.append(_dot(sel.astype(BF16), hb_ref[s * tsp:(s + 1) * tsp, :]).astype(BF16))
            sels.append(sel)
        xa = jnp.concatenate(xs, axis=0)
        a = _dot(xa, wg_ref[0])
        u = _dot(xa, wu_ref[0])
        he = (a * (1.0 / (1.0 + jnp.exp(-a)))) * u * jnp.concatenate(ges, axis=0)
        return _dot(he.astype(BF16), wd_ref[0]), sels

    ye, _ = expert_rows(0)
    for s in range(n_sub):
        y_ref[s, pl.ds(pl.multiple_of(e * slot, slot), slot), :] = ye[s * slot:(s + 1) * slot].astype(BF16)

    for k in range(1, n_chunk):
        last_rank = jnp.max(rt_ref[0, pl.ds(row, 1), :])
        for s in range(1, n_sub):
            last_rank = jnp.maximum(last_rank, jnp.max(rt_ref[s, pl.ds(row, 1), :]))

        @pl.when(last_rank >= float(k * slot))
        def _():
            ye_k, sels = expert_rows(k)
            for s in range(n_sub):
                acc_ref[s * tsp:(s + 1) * tsp, :] += _dot(sels[s].T.astype(BF16),
                                                          ye_k[s * slot:(s + 1) * slot].astype(BF16))

    @pl.when(e == pl.num_programs(2) - 1)
    def _():
        slot_l = lax.broadcasted_iota(I32, (1, slot), 1).astype(F32)
        for s in range(n_sub):
            rows = slice(s * ts, (s + 1) * ts)
            rank = rank_ref[s]
            sel_t = jnp.concatenate(
                [jnp.where(rank[:, N_GROUPS + j:N_GROUPS + j + 1] == slot_l, 1.0, 0.0).astype(BF16)
                 for j in range(N_EXPERTS)], axis=1)
            y = _dot(sel_t, y_ref[s])
            if n_chunk > 1:
                y = y + acc_ref[s * tsp:(s + 1) * tsp, :]
            o_ref[0, rows, :] = x_ref[0, rows, :] + g2_ref[0] * y[:ts]


def moe_layer(x, gain, sc, sh, g2, w_router, b_router, w_gate, w_up, w_down):
    bsz, t, d = x.shape
    tm = min(t, 1024)
    ts = min(tm, 512)
    tsp = max(ts, LANES)
    slot = MOE_SLOT
    ne, _, de = w_gate.shape
    tri = (jnp.arange(tsp)[:, None] < jnp.arange(tsp)[None, :]).astype(BF16)
    kern = functools.partial(_moe_kernel, ts=ts, slot=slot)
    n_sub = tm // ts
    return pl.pallas_call(
        kern,
        grid=(bsz, t // tm, ne),
        in_specs=[pl.BlockSpec((1, tm, d), lambda b, i, e: (b, i, 0)),
                  pl.BlockSpec((1, d), lambda b, i, e: (0, 0)),
                  pl.BlockSpec((1, 1, d), lambda b, i, e: (b, 0, 0)),
                  pl.BlockSpec((1, 1, d), lambda b, i, e: (b, 0, 0)),
                  pl.BlockSpec((1, 1, d), lambda b, i, e: (b, 0, 0)),
                  pl.BlockSpec((d, LANES), lambda b, i, e: (0, 0)),
                  pl.BlockSpec((1, LANES), lambda b, i, e: (0, 0)),
                  pl.BlockSpec((tsp, tsp), lambda b, i, e: (0, 0)),
                  pl.BlockSpec((1, d, de), lambda b, i, e: (e, 0, 0)),
                  pl.BlockSpec((1, d, de), lambda b, i, e: (e, 0, 0)),
                  pl.BlockSpec((1, de, d), lambda b, i, e: (e, 0, 0))],
        out_specs=pl.BlockSpec((1, tm, d), lambda b, i, e: (b, i, 0)),
        out_shape=jax.ShapeDtypeStruct((bsz, t, d), F32),
        scratch_shapes=[pltpu.VMEM((n_sub * tsp, d), BF16),
                        pltpu.VMEM((n_sub, tsp, LANES), F32),
                        pltpu.VMEM((n_sub, LANES, tsp), F32),
                        pltpu.VMEM((n_sub, LANES, tsp), F32),
                        pltpu.VMEM((n_sub, ne * slot, d), BF16),
                        pltpu.VMEM((n_sub * tsp if tsp > slot else 8, d), F32)],
        compiler_params=_cparams("parallel", "parallel", "arbitrary"),
        name="moe",
    )(x, gain.reshape(1, d), sc, sh, g2, w_router, b_router, tri, w_gate, w_up, w_down)


def _pad_cols(w, n):
    return jnp.pad(w, ((0, 0), (0, n - w.shape[1])))


def _pad_rows(a, n):
    return jnp.pad(a, ((0, 0), (0, n - a.shape[1])) + ((0, 0),) * (a.ndim - 2))


def _round_up(n, m):
    return -(-n // m) * m


def _head_cols(gain_q, nq, gain_k, nk, npad):
    cg = jnp.concatenate([jnp.tile(gain_q, nq), jnp.tile(gain_k, nk)])
    n = cg.shape[0]
    cgain = jnp.pad(cg, (0, npad - n)).reshape(1, npad)
    cflag = (jnp.arange(npad) < n).astype(F32).reshape(1, npad)
    return cgain, cflag


def _values_t_aug(vh, lp):
    bsz, h = vh.shape[:2]
    ones = jnp.ones((bsz, h, 1, lp), BF16)
    zv = jnp.zeros((bsz, h, LANES - HEAD_DIM - 1, lp), BF16)
    both = jnp.stack([jnp.concatenate([vh, ones, zv], axis=2), jnp.concatenate([ones, zv, vh], axis=2)])
    return both.reshape(2, bsz, h, LANES, lp // KEY_TILE, KEY_TILE).transpose(0, 1, 2, 4, 3, 5)


def _dsa_layer(x, mod, past, prm, bdiag, bsub):
    sh1, sc1, g1 = mod
    past_k, past_v, past_ki = past
    bsz, t, d = x.shape
    p = 0 if past_k is None else past_k.shape[1]
    n_keys = p + t
    topk = min(TOPK_MAX, n_keys // 4)
    tn = 256
    n_in = prm['w_in'].shape[1]
    npad = _round_up(n_in, tn)
    nqk = (A_HEADS + A_KV_HEADS) * HEAD_DIM
    w = _pad_cols(prm['w_in'], npad).astype(BF16)
    cgain, cflag = _head_cols(prm['q_norm'], A_HEADS, prm['k_norm'], A_KV_HEADS, npad)
    proj = ln_proj(x, prm['norm'], sc1, sh1, w, cgain, cflag, nqk // tn, tn)
    o_k = A_HEADS * HEAD_DIM
    o_v = o_k + A_KV_HEADS * HEAD_DIM
    o_qi = o_v + A_KV_HEADS * HEAD_DIM
    o_ki = o_qi + IDX_HEADS * IDX_DIM
    k_new = proj[..., o_k:o_v].reshape(bsz, t, A_KV_HEADS, HEAD_DIM)
    v_new = proj[..., o_v:o_qi].reshape(bsz, t, A_KV_HEADS, HEAD_DIM)
    ki_new = proj[..., o_ki:o_ki + IDX_DIM]
    if p:
        k_all = jnp.concatenate([past_k, k_new], axis=1)
        v_all = jnp.concatenate([past_v, v_new], axis=1)
        ki_all = jnp.concatenate([past_ki, ki_new], axis=1)
    else:
        k_all, v_all, ki_all = k_new, v_new, ki_new
    lp = _round_up(n_keys, KEY_TILE)
    kidup = _pad_rows(jnp.concatenate([ki_all, ki_all], axis=-1).astype(BF16), lp)
    mask = dsa_index_mask(proj, kidup, t=t, q_off=p, topk=topk, qi_blk=o_qi // (4 * LANES), w_blk=o_ki // LANES)
    kdup = _pad_rows(jnp.concatenate([k_all, k_all], axis=-1).astype(BF16).reshape(bsz, n_keys, A_KV_HEADS * LANES), lp)
    vh = _pad_rows(v_all.astype(BF16), lp).transpose(0, 2, 3, 1)
    vt_aug = _values_t_aug(vh, lp).transpose(1, 2, 0, 3, 4, 5)
    o = dsa_attention(proj, kdup, vt_aug, mask, bdiag, bsub, t=t, q_off=p)
    x = out_proj_residual(o, prm['w_out'].astype(BF16), x, g1)
    return x, (k_new, v_new, ki_new)


def _fox_layer(x, mod, past, prm):
    sh1, sc1, g1 = mod
    past_k, past_v, past_lf = past
    bsz, t, d = x.shape
    p = 0 if past_k is None else past_k.shape[1]
    n_keys = p + t
    hd = B_HEADS * HEAD_DIM
    tn = 256
    npad = _round_up(prm['w_in'].shape[1], tn)
    w = _pad_cols(prm['w_in'], npad).astype(BF16)
    cgain, cflag = _head_cols(prm['q_norm'], B_HEADS, prm['k_norm'], B_HEADS, npad)
    proj = ln_proj(x, prm['norm'], sc1, sh1, w, cgain, cflag, 2 * hd // tn, tn)
    k_new = proj[..., hd:2 * hd]
    v_new = proj[..., 2 * hd:3 * hd]
    fz = proj[..., 4 * hd:4 * hd + B_HEADS]
    tp = _round_up(t, LANES)
    fz_t = _pad_rows(fz, tp).transpose(0, 2, 1)
    past_t = None if not p else past_lf.transpose(0, 2, 1)
    lf_t, nck = fox_decay(fz_t, prm['forget_bias'], past_t, t=t)
    logf_new = lf_t[:, :, :t].transpose(0, 2, 1)
    lp = _round_up(n_keys, KEY_TILE)
    if p:
        k_all = jnp.concatenate([past_k.reshape(bsz, p, hd), k_new], axis=1)
        v_all = jnp.concatenate([past_v.reshape(bsz, p, hd), v_new], axis=1)
    else:
        k_all, v_all = k_new, v_new
    kh = _pad_rows(k_all.astype(BF16), lp).reshape(bsz, lp, B_HEADS, HEAD_DIM).transpose(0, 2, 1, 3)
    vh = _pad_rows(v_all.astype(BF16), lp).reshape(bsz, lp, B_HEADS, HEAD_DIM).transpose(0, 2, 3, 1)
    pieces = jnp.pad(nck, ((0, 0), (0, 0), (0, 0), (0, lp - nck.shape[3]))).transpose(0, 2, 3, 1)
    zk = jnp.zeros((bsz, B_HEADS, lp, LANES - HEAD_DIM - 3), BF16)
    odd = (jnp.arange(B_HEADS) % 2 == 1)[None, :, None, None]
    k_aug = jnp.where(odd, jnp.concatenate([pieces, zk, kh], axis=-1), jnp.concatenate([kh, pieces, zk], axis=-1))
    both = _values_t_aug(vh, lp)
    vt_aug = jnp.where(odd[..., None], both[1], both[0])
    o = fox_attention(proj, k_aug, vt_aug, t=t, q_off=p, g_blk0=3 * hd // LANES)
    x = out_proj_residual(o, prm['w_out'].astype(BF16), x, g1)
    return x, (k_new.reshape(bsz, t, B_HEADS, HEAD_DIM), v_new.reshape(bsz, t, B_HEADS, HEAD_DIM), logf_new)


def _hgrn2_layer(x, mod, s0, prm):
    sh1, sc1, g1 = mod
    bsz, t, d = x.shape
    npad = prm['w_in'].shape[1]
    zeros = jnp.zeros((1, npad), F32)
    proj = ln_proj(x, prm['norm'], sc1, sh1, prm['w_in'].astype(BF16), zeros, zeros, 0, 256)
    y, s_t = hgrn2_recurrence(proj, prm['lb'], prm['out_norm'], jnp.swapaxes(s0, -1, -2), t=t)
    x = out_proj_residual(y, prm['w_out'].astype(BF16), x, g1)
    return x, jnp.swapaxes(s_t, -1, -2)


def _trunk(x, c, a_k, a_v, a_kidx, b_k, b_v, b_logf, c_state, prm):
    bsz, t, d = x.shape
    mod_all = ada_mod(c, prm['w_ada'], prm['b_ada'])
    lb_all = jnp.cumsum(jax.nn.softmax(prm['c_lower_bound'].astype(F32), axis=0), axis=0)
    lb_all = lb_all - lb_all[0]
    bdiag, bsub = dsa_bias_tiles(prm['rel_table'])
    out_a, out_b, out_c = [], [], []
    for i in range(DEPTH):
        j = i // N_MIXERS
        kind = i % N_MIXERS
        sh1, sc1, g1, sh2, sc2, g2 = [m.reshape(bsz, 1, d) for m in jnp.split(mod_all[i], 6, axis=-1)]
        mod = (sh1, sc1, g1)
        if kind == 0:
            past = (None, None, None) if a_k is None else (a_k[j], a_v[j], a_kidx[j])
            lp = dict(norm=prm['norm_mix'][i], w_in=prm['a_w_in'][j], q_norm=prm['a_q_norm'][j],
                      k_norm=prm['a_k_norm'][j], w_out=prm['a_w_out'][j])
            x, new = _dsa_layer(x, mod, past, lp, bdiag, bsub)
            out_a.append(new)
        elif kind == 1:
            past = (None, None, None) if b_k is None else (b_k[j], b_v[j], b_logf[j])
            lp = dict(norm=prm['norm_mix'][i], w_in=prm['b_w_in'][j], forget_bias=prm['b_forget_bias'][j],
                      q_norm=prm['b_q_norm'][j], k_norm=prm['b_k_norm'][j], w_out=prm['b_w_out'][j])
            x, new = _fox_layer(x, mod, past, lp)
            out_b.append(new)
        else:
            s0 = jnp.zeros((bsz, C_HEADS, C_DK, C_DV), F32) if c_state is None else c_state[j]
            lp = dict(norm=prm['norm_mix'][i], w_in=prm['c_w_in'][j], lb=lb_all[i],
                      out_norm=prm['c_out_norm'][j], w_out=prm['c_w_out'][j])
            x, new = _hgrn2_layer(x, mod, s0, lp)
            out_c.append(new)
        w_router = jnp.pad(jnp.concatenate([prm['moe_w_group'][i], prm['moe_w_expert'][i]], axis=1),
                           ((0, 0), (0, LANES - N_GROUPS - N_EXPERTS)))
        b_router = jnp.pad(jnp.concatenate([prm['moe_b_group'][i], prm['moe_b_expert'][i]]),
                           (0, LANES - N_GROUPS - N_EXPERTS)).reshape(1, LANES)
        x = moe_layer(x, prm['norm_ffn'][i], sc2, sh2, g2, w_router, b_router,
                      prm['moe_w_gate'][i].astype(BF16), prm['moe_w_up'][i].astype(BF16),
                      prm['moe_w_down'][i].astype(BF16))
    stack = lambda outs, k: jnp.stack([o[k] for o in outs])
    return (x, stack(out_a, 0), stack(out_a, 1), stack(out_a, 2),
            stack(out_b, 0), stack(out_b, 1), stack(out_b, 2), jnp.stack(out_c))


def kernel(x_prompt, x_sample, cache_a_k, cache_a_v, cache_a_kidx, cache_b_k, cache_b_v, cache_b_logf, state_c,
           c_prompt, c_sample, rel_table, w_ada, b_ada, norm_mix, norm_ffn, a_w_in, a_q_norm, a_k_norm, a_w_out,
           b_w_in, b_forget_bias, b_q_norm, b_k_norm, b_w_out, c_w_in, c_lower_bound, c_out_norm, c_w_out,
           moe_w_group, moe_b_group, moe_w_expert, moe_b_expert, moe_w_gate, moe_w_up, moe_w_down):
    prm = {'rel_table': rel_table, 'w_ada': w_ada, 'b_ada': b_ada, 'norm_mix': norm_mix, 'norm_ffn': norm_ffn,
           'a_w_in': a_w_in, 'a_q_norm': a_q_norm, 'a_k_norm': a_k_norm, 'a_w_out': a_w_out,
           'b_w_in': b_w_in, 'b_forget_bias': b_forget_bias, 'b_q_norm': b_q_norm, 'b_k_norm': b_k_norm,
           'b_w_out': b_w_out, 'c_w_in': c_w_in, 'c_lower_bound': c_lower_bound, 'c_out_norm': c_out_norm,
           'c_w_out': c_w_out, 'moe_w_group': moe_w_group, 'moe_b_group': moe_b_group,
           'moe_w_expert': moe_w_expert, 'moe_b_expert': moe_b_expert, 'moe_w_gate': moe_w_gate,
           'moe_w_up': moe_w_up, 'moe_w_down': moe_w_down}
    (y_p, ak_p, av_p, ai_p, bk_p, bv_p, bl_p, cs_p) = _trunk(
        x_prompt, c_prompt, None, None, None, None, None, None, None, prm)
    (y_s, ak_s, av_s, ai_s, bk_s, bv_s, bl_s, cs_s) = _trunk(
        x_sample, c_sample, cache_a_k, cache_a_v, cache_a_kidx, cache_b_k, cache_b_v, cache_b_logf, state_c, prm)
    return (y_p, y_s, ak_p, av_p, ai_p, ak_s, av_s, ai_s, bk_p, bv_p, bl_p, bk_s, bv_s, bl_s, cs_p, cs_s)
```

```python
import functools

import jax
import jax.numpy as jnp
from jax import lax
from jax.experimental import pallas as pl
from jax.experimental.pallas import tpu as pltpu

F32 = jnp.float32
BF16 = jnp.bfloat16
I32 = jnp.int32

LANES = 128
VMEM_LIMIT_BYTES = 56 * 1024 * 1024

DEPTH = 4
N_MIXERS = 3
CHUNK = 64
EPS = 1e-6
HEAD_DIM = 64
A_HEADS = 16
A_KV_HEADS = 4
A_GROUP = A_HEADS // A_KV_HEADS
IDX_HEADS = 8
IDX_DIM = 64
TOPK_MAX = 256
REL_BUCKETS = 32
B_HEADS = 16
C_HEADS = 8
C_DK = 128
C_DV = 128
N_GROUPS = 4
EXPERTS_PER_GROUP = 4
N_EXPERTS = 16
D_EXPERT = 512

LOG2E = 1.4426950408889634
NEG_BIG = -1e30
M_FLOOR = -1e20
INT_MIN = -2 ** 31
KEY_TILE = 512
SLABS = KEY_TILE // LANES
MOE_SLOT = 128
VT_ROWS = LANES + 16


def _cparams(*sem):
    return pltpu.CompilerParams(dimension_semantics=sem, vmem_limit_bytes=VMEM_LIMIT_BYTES)


def _nt(a, b):
    return lax.dot_general(a, b, (((1,), (1,)), ((), ())), preferred_element_type=F32)


def _split3(x):
    hi = x.astype(BF16)
    r = x - hi.astype(F32)
    mid = r.astype(BF16)
    lo = (r - mid.astype(F32)).astype(BF16)
    return hi, mid, lo


def _dot(a, b):
    return jnp.dot(a, b, preferred_element_type=F32)


def _dot_x01(x, m01):
    hi, mid, lo = _split3(x)
    return _dot(hi, m01) + _dot(mid, m01) + _dot(lo, m01)


def _dot_01x(m01, x):
    hi, mid, lo = _split3(x)
    return _dot(m01, hi) + _dot(m01, mid) + _dot(m01, lo)


def _dot_f32(a, b):
    ah, am, al = _split3(a)
    bh, bm, bl = _split3(b)
    return _dot(ah, bh) + (_dot(ah, bm) + _dot(am, bh)) + (_dot(ah, bl) + _dot(am, bm) + _dot(al, bh))


def _pad_q_rows(q, tq, tqp):
    if tqp == tq:
        return q
    return jnp.concatenate([q, jnp.zeros((tqp - tq, q.shape[1]), q.dtype)], axis=0)


def _mod_kernel(c_ref, w_ref, b_ref, o_ref):
    o_ref[0] = _dot(c_ref[...], w_ref[0]) + b_ref[0]


def ada_mod(c, w_ada, b_ada):
    nl, d, n6 = w_ada.shape
    bsz = c.shape[0]
    tn = 512
    return pl.pallas_call(
        _mod_kernel,
        grid=(nl, n6 // tn),
        in_specs=[pl.BlockSpec((bsz, d), lambda l, j: (0, 0)),
                  pl.BlockSpec((1, d, tn), lambda l, j: (l, 0, j)),
                  pl.BlockSpec((1, 1, tn), lambda l, j: (l, 0, j))],
        out_specs=pl.BlockSpec((1, bsz, tn), lambda l, j: (l, 0, j)),
        out_shape=jax.ShapeDtypeStruct((nl, bsz, n6), F32),
        compiler_params=_cparams("parallel", "parallel"),
        name="ada_mod",
    )(c, w_ada, b_ada.reshape(nl, 1, n6))


def _ln_mod(x, gain, sc, sh):
    ms = jnp.mean(x * x, axis=-1, keepdims=True)
    return (x * lax.rsqrt(ms + EPS) * gain) * (1.0 + sc) + sh


def _ln_proj_kernel(x_ref, gain_ref, sc_ref, sh_ref, w_ref, cgain_ref, cflag_ref, bd_ref, o_ref, h_ref,
                    *, n_norm_tiles, tn):
    j = pl.program_id(2)

    @pl.when(j == 0)
    def _():
        h_ref[...] = _ln_mod(x_ref[0], gain_ref[...], sc_ref[0], sh_ref[0]).astype(BF16)

    y = _dot(h_ref[...], w_ref[...])

    def plain():
        o_ref[0] = y

    def normed():
        y2 = y * y
        hi = y2.astype(BF16)
        lo = (y2 - hi.astype(F32)).astype(BF16)
        bd = bd_ref[...]
        segs = []
        for s in range(tn // LANES):
            sl = slice(s * LANES, (s + 1) * LANES)
            segs.append(_dot(hi[:, sl], bd) + _dot(lo[:, sl], bd))
        seg = jnp.concatenate(segs, axis=1)
        yn = y * lax.rsqrt(seg * (1.0 / HEAD_DIM) + EPS) * cgain_ref[...]
        o_ref[0] = jnp.where(cflag_ref[...] > 0.0, yn, y)

    if n_norm_tiles == 0:
        plain()
    else:
        pl.when(j < n_norm_tiles)(normed)
        pl.when(j >= n_norm_tiles)(plain)


def ln_proj(x, gain, sc, sh, w, cgain, cflag, n_norm_tiles, tn=256):
    bsz, t, d = x.shape
    npad = w.shape[1]
    tm = min(t, 1024)
    bd = (jnp.arange(LANES)[:, None] // HEAD_DIM == jnp.arange(LANES)[None, :] // HEAD_DIM).astype(BF16)
    kern = functools.partial(_ln_proj_kernel, n_norm_tiles=n_norm_tiles, tn=tn)
    return pl.pallas_call(
        kern,
        grid=(bsz, t // tm, npad // tn),
        in_specs=[pl.BlockSpec((1, tm, d), lambda b, i, j: (b, i, 0)),
                  pl.BlockSpec((1, d), lambda b, i, j: (0, 0)),
                  pl.BlockSpec((1, 1, d), lambda b, i, j: (b, 0, 0)),
                  pl.BlockSpec((1, 1, d), lambda b, i, j: (b, 0, 0)),
                  pl.BlockSpec((d, tn), lambda b, i, j: (0, j)),
                  pl.BlockSpec((1, tn), lambda b, i, j: (0, j)),
                  pl.BlockSpec((1, tn), lambda b, i, j: (0, j)),
                  pl.BlockSpec((LANES, LANES), lambda b, i, j: (0, 0))],
        out_specs=pl.BlockSpec((1, tm, tn), lambda b, i, j: (b, i, j)),
        out_shape=jax.ShapeDtypeStruct((bsz, t, npad), F32),
        scratch_shapes=[pltpu.VMEM((tm, d), BF16)],
        compiler_params=_cparams("parallel", "parallel", "arbitrary"),
        name="ln_proj",
    )(x, gain.reshape(1, d), sc, sh, w, cgain, cflag, bd)


def _out_proj_kernel(a_ref, w_ref, x_ref, g_ref, o_ref):
    y = _dot(a_ref[0].astype(BF16), w_ref[...])
    o_ref[0] = x_ref[0] + g_ref[0] * y


def out_proj_residual(a, w, x, gate):
    bsz, t, k = a.shape
    d = w.shape[1]
    tm = min(t, 512)
    return pl.pallas_call(
        _out_proj_kernel,
        grid=(bsz, t // tm),
        in_specs=[pl.BlockSpec((1, tm, k), lambda b, i: (b, i, 0)),
                  pl.BlockSpec((k, d), lambda b, i: (0, 0)),
                  pl.BlockSpec((1, tm, d), lambda b, i: (b, i, 0)),
                  pl.BlockSpec((1, 1, d), lambda b, i: (b, 0, 0))],
        out_specs=pl.BlockSpec((1, tm, d), lambda b, i: (b, i, 0)),
        out_shape=jax.ShapeDtypeStruct((bsz, t, d), F32),
        compiler_params=_cparams("parallel", "parallel"),
        name="out_proj",
    )(a, w, x, gate)


def _dsa_index_kernel(qi_ref, w_ref, ki_ref, o_ref, key_ref, *, tq, tqp, n_slabs, q_off, topk, idx_bits):
    a = pl.program_id(1)
    tk = KEY_TILE
    q0 = q_off + a * tq
    n_kt = (q0 + tq + tk - 1) // tk
    lane = lax.broadcasted_iota(I32, (tqp, LANES), 1)
    half = lane < IDX_DIM
    krow = lax.broadcasted_iota(I32, (tk, tqp), 0)
    qcol = lax.broadcasted_iota(I32, (tk, tqp), 1)
    qchunk = (q0 + qcol) >> 6
    srow = lax.broadcasted_iota(I32, (LANES, tqp), 0)
    w_t = (_pad_q_rows(w_ref[0], tq, tqp) * (IDX_HEADS ** -0.5)).T
    qs = []
    for p in range(IDX_HEADS // 2):
        qp = _pad_q_rows(qi_ref[0, :, p * LANES:(p + 1) * LANES], tq, tqp) * (IDX_DIM ** -0.5)
        qs.append(jnp.where(half, qp, 0.0).astype(BF16))
        qs.append(pltpu.roll(jnp.where(half, 0.0, qp), IDX_DIM, axis=1).astype(BF16))

    def score_tile(c, carry):
        kt = ki_ref[0, pl.ds(pl.multiple_of(c * tk, tk), tk), :].astype(BF16)
        sc = jnp.zeros((tk, tqp), F32)
        for h in range(IDX_HEADS):
            sc = sc + w_t[IDX_DIM + h:IDX_DIM + h + 1, :] * jnp.maximum(_nt(kt, qs[h]), 0.0)
        bits = lax.bitcast_convert_type(sc, I32)
        key = jnp.where(bits < 0, bits ^ 0x7FFFFFFF, bits)
        key = jnp.where(sc == 0.0, 0, key)
        adm = ((c * tk + krow) >> 6) <= qchunk
        key = jnp.where(adm, key, INT_MIN)
        for s_ in range(SLABS):
            key_ref[c * SLABS + s_] = key[s_ * LANES:(s_ + 1) * LANES, :]
        return carry

    lax.fori_loop(0, n_kt, score_tile, 0)

    def count(pred):
        def body(c, acc):
            for s_ in range(SLABS):
                sidx = c * SLABS + s_
                ind = jnp.where(pred(key_ref[sidx], sidx), 1.0, 0.0)
                acc = acc + jnp.sum(ind.reshape(LANES // 8, 8, tqp), axis=0)
            return acc
        acc = lax.fori_loop(0, n_kt, body, jnp.zeros((8, tqp), F32))
        return jnp.sum(acc, axis=0, keepdims=True)

    kf = float(topk)
    n_adm = count(lambda k, s: k > INT_MIN)

    def all_done(cnt_t):
        done = (cnt_t == kf) | (n_adm < kf)
        return (jnp.min(jnp.where(done, 1.0, 0.0)) > 0.0).astype(I32)

    def bit_cond(st):
        return (st[0] < 32) & (st[3] == 0)

    def bit_body(st):
        i, t_u, cnt_t, _ = st
        cand_u = t_u | lax.shift_left(jnp.int32(1), 31 - i)
        cand_s = cand_u ^ INT_MIN
        cnt = count(lambda k, s: k >= cand_s)
        take = cnt >= kf
        cnt_t = jnp.where(take, cnt, cnt_t)
        return i + 1, jnp.where(take, cand_u, t_u), cnt_t, all_done(cnt_t)

    _, t_u, cnt_ge, _ = lax.while_loop(bit_cond, bit_body,
                                       (jnp.int32(0), jnp.zeros((1, tqp), I32), n_adm, all_done(n_adm)))
    thr = t_u ^ INT_MIN
    excess = jnp.where(cnt_ge > kf, 1.0, 0.0)

    def tie_search():
        need = kf - count(lambda k, s: k > thr)

        def j_body(i, j):
            cand = j | lax.shift_left(jnp.int32(1), idx_bits - 1 - i)
            c = count(lambda k, s: (k == thr) & (srow + s * LANES < cand))
            return jnp.where(c < need, cand, j)
        return lax.fori_loop(0, idx_bits, j_body, jnp.zeros((1, tqp), I32))

    j_last = lax.cond(jnp.max(excess) > 0.0, tie_search, lambda: jnp.full((1, tqp), 2 ** idx_bits, I32))

    def write_active(c, carry):
        for s_ in range(SLABS):
            sidx = c * SLABS + s_
            k = key_ref[sidx]
            sel = (k > thr) | ((k == thr) & (srow + sidx * LANES <= j_last))
            sel = sel & (k > INT_MIN)
            o_ref[0, sidx] = jnp.where(sel, 0.0, NEG_BIG)
        return carry

    lax.fori_loop(0, n_kt, write_active, 0)
    neg = jnp.full((LANES, tqp), NEG_BIG, F32)

    def write_inactive(s, carry):
        o_ref[0, s] = neg
        return carry

    lax.fori_loop(n_kt * SLABS, n_slabs, write_inactive, 0)


def dsa_index_mask(proj, keys, *, lp, t, q_off, topk, qi_blk, w_blk, ki_blk):
    bsz = proj.shape[0]
    n_slabs = lp // LANES
    tq = min(t, KEY_TILE)
    tqp = max(tq, LANES)
    idx_bits = max(1, (lp - 1).bit_length())
    kern = functools.partial(_dsa_index_kernel, tq=tq, tqp=tqp, n_slabs=n_slabs, q_off=q_off, topk=topk,
                             idx_bits=idx_bits)
    return pl.pallas_call(
        kern,
        grid=(bsz, t // tq),
        in_specs=[pl.BlockSpec((1, tq, 4 * LANES), lambda b, a: (b, a, qi_blk)),
                  pl.BlockSpec((1, tq, LANES), lambda b, a: (b, a, w_blk)),
                  pl.BlockSpec((1, lp, LANES), lambda b, a: (b, 0, ki_blk))],
        out_specs=pl.BlockSpec((1, n_slabs, LANES, tqp), lambda b, a: (b, 0, 0, a)),
        out_shape=jax.ShapeDtypeStruct((bsz, n_slabs, LANES, (t // tq) * tqp), F32),
        scratch_shapes=[pltpu.VMEM((n_slabs, LANES, tqp), I32)],
        compiler_params=_cparams("parallel", "parallel"),
        name="dsa_index",
    )(proj, proj, keys)


def _bias_kernel(tab_ref, diag_ref, sub_ref):
    h = pl.program_id(0)
    far = tab_ref[REL_BUCKETS // 2 - 1, h]

    def bias(rel):
        n = jnp.abs(rel)
        large = jnp.full(rel.shape, 8, I32)
        for th in (12, 16, 23, 32, 46, 64, 91):
            large = large + jnp.where(n >= th, 1, 0)
        bucket = jnp.where(rel > 0, REL_BUCKETS // 2, 0) + jnp.where(n < 8, n, large)
        acc = jnp.zeros(rel.shape, F32)
        for bk in range(REL_BUCKETS):
            acc = jnp.where(bucket == bk, tab_ref[bk, h], acc)
        return (acc - far) * LOG2E

    ik = lax.broadcasted_iota(I32, (KEY_TILE, KEY_TILE), 0)
    iq = lax.broadcasted_iota(I32, (KEY_TILE, KEY_TILE), 1)
    diag_ref[0] = bias(ik - iq)
    ik = lax.broadcasted_iota(I32, (LANES, LANES), 0)
    iq = lax.broadcasted_iota(I32, (LANES, LANES), 1)
    sub_ref[0] = bias(ik - LANES - iq)


def dsa_bias_tiles(rel_table):
    return pl.pallas_call(
        _bias_kernel,
        grid=(A_HEADS,),
        in_specs=[pl.BlockSpec(memory_space=pltpu.SMEM)],
        out_specs=[pl.BlockSpec((1, KEY_TILE, KEY_TILE), lambda h: (h, 0, 0)),
                   pl.BlockSpec((1, LANES, LANES), lambda h: (h, 0, 0))],
        out_shape=[jax.ShapeDtypeStruct((A_HEADS, KEY_TILE, KEY_TILE), F32),
                   jax.ShapeDtypeStruct((A_HEADS, LANES, LANES), F32)],
        compiler_params=_cparams("parallel"),
        name="dsa_bias",
    )(rel_table)


def _vt_rows(v_tile):
    return jnp.concatenate([v_tile.T.astype(BF16), jnp.ones((VT_ROWS - LANES, v_tile.shape[0]), BF16)], axis=0)


def _dsa_attn_kernel(q_ref, k_ref, v_ref, msk_ref, bd_ref, bs_ref, o_ref, acc_ref, m_ref, s0_ref, vt_ref,
                     *, tq, tqp, q_off, nt):
    a = pl.program_id(1)
    g = pl.program_id(2)
    tk = KEY_TILE
    q0 = q_off + a * tq
    cd = q0 // tk
    cs = jnp.maximum(cd - 1, 0)
    par = g % 2
    lane = lax.broadcasted_iota(I32, (tqp, LANES), 1)
    half = lane < HEAD_DIM

    @pl.when(a == 0)
    def _():
        for c in range(nt):
            vt_ref[g, c] = _vt_rows(v_ref[0, c * tk:(c + 1) * tk, :])

    qs = []
    for e in range(A_GROUP):
        qc = _pad_q_rows(q_ref[0, :, (e // 2) * LANES:(e // 2 + 1) * LANES], tq, tqp) * (HEAD_DIM ** -0.5 * LOG2E)
        own = jnp.where(half, qc, 0.0) if e % 2 == 0 else jnp.where(half, 0.0, qc)
        qs.append(jnp.where(par == e % 2, own, pltpu.roll(own, HEAD_DIM, axis=1)).astype(BF16))
    m_ref[...] = jnp.full(m_ref.shape, M_FLOOR, F32)
    acc_ref[...] = jnp.zeros(acc_ref.shape, F32)

    def s_tile(e, c):
        return _nt(k_ref[0, pl.ds(pl.multiple_of(c * tk, tk), tk), :].astype(BF16), qs[e])

    def softmax_pv(e, c, s, after, kind):
        s = s + jnp.concatenate([msk_ref[0, c * SLABS + i] for i in range(SLABS)], axis=0)
        if kind == "diag":
            s = s + bd_ref[e, :, 0:tqp]
        elif kind == "sub":
            corner = bs_ref[e]
            if tqp > LANES:
                corner = jnp.concatenate([corner, jnp.zeros((LANES, tqp - LANES), F32)], axis=1)
            s = s + jnp.concatenate([jnp.zeros((tk - LANES, tqp), F32), corner], axis=0)
            s = jnp.where(cd > 0, s, NEG_BIG)
        m_old = jnp.minimum(m_ref[e], jnp.maximum(after[0:1, :], -NEG_BIG))
        m_new = jnp.maximum(m_old, jnp.max(s, axis=0, keepdims=True))
        p = jnp.exp2(s - m_new).astype(BF16)
        acc_ref[e] = jnp.exp2(m_old - m_new) * acc_ref[e] + _dot(vt_ref[g, c], p)
        m_ref[e] = m_new

    def step(c, kind):
        s_prev = s0_ref[...]
        for e in range(A_GROUP):
            if e + 1 < A_GROUP:
                s_next = s_tile(e + 1, c)
            else:
                s_next = s_tile(0, jnp.minimum(c + 1, cd))
                s0_ref[...] = s_next
            softmax_pv(e, c, s_prev, s_next, kind)
            s_prev = s_next

    def far_body(c, carry):
        step(c, "far")
        return carry

    s0_ref[...] = s_tile(0, 0)
    lax.fori_loop(0, cs, far_body, 0)
    step(cs, "sub")
    step(cd, "diag")
    outs = []
    for e in range(A_GROUP):
        o_t = (acc_ref[e, 0:LANES, :] * (1.0 / acc_ref[e, LANES:LANES + 1, :])).T
        outs.append(jnp.where(par == e % 2, o_t, pltpu.roll(o_t, HEAD_DIM, axis=1)))
    for c2 in range(A_GROUP // 2):
        o_ref[0, :, c2 * LANES:(c2 + 1) * LANES] = jnp.where(half, outs[2 * c2], outs[2 * c2 + 1])[:tq]


def dsa_attention(proj, keys, values, mask, bdiag, bsub, *, lp, t, q_off, k_blk0, v_blk0):
    bsz = proj.shape[0]
    tq = min(t, KEY_TILE)
    tqp = max(tq, LANES)
    nt = lp // KEY_TILE
    nsl = mask.shape[1]
    kern = functools.partial(_dsa_attn_kernel, tq=tq, tqp=tqp, q_off=q_off, nt=nt)
    return pl.pallas_call(
        kern,
        grid=(bsz, t // tq, A_KV_HEADS),
        in_specs=[pl.BlockSpec((1, tq, 2 * LANES), lambda b, a, g: (b, a, g)),
                  pl.BlockSpec((1, lp, LANES), lambda b, a, g: (b, 0, k_blk0 + g // 2)),
                  pl.BlockSpec((1, lp, LANES), lambda b, a, g: (b, 0, v_blk0 + g // 2)),
                  pl.BlockSpec((1, nsl, LANES, tqp), lambda b, a, g: (b, 0, 0, a)),
                  pl.BlockSpec((A_GROUP, KEY_TILE, tqp), lambda b, a, g: (g, 0, 0)),
                  pl.BlockSpec((A_GROUP, LANES, LANES), lambda b, a, g: (g, 0, 0))],
        out_specs=pl.BlockSpec((1, tq, 2 * LANES), lambda b, a, g: (b, a, g)),
        out_shape=jax.ShapeDtypeStruct((bsz, t, A_HEADS * HEAD_DIM), F32),
        scratch_shapes=[pltpu.VMEM((A_GROUP, VT_ROWS, tqp), F32), pltpu.VMEM((A_GROUP, 1, tqp), F32),
                        pltpu.VMEM((KEY_TILE, tqp), F32),
                        pltpu.VMEM((A_KV_HEADS, nt, VT_ROWS, KEY_TILE), BF16)],
        compiler_params=_cparams("parallel", "arbitrary", "arbitrary"),
        name="dsa_attn",
    )(proj, keys, values, mask, bdiag, bsub)


def _fox_decay_kernel(*refs, n_past, n_new, t):
    if n_past:
        fz_ref, bf_ref, past_ref, tri_ref, lf_ref, nck_ref = refs
    else:
        fz_ref, bf_ref, tri_ref, lf_ref, nck_ref = refs
        past_ref = None
    tri = tri_ref[...]
    lane = lax.broadcasted_iota(I32, (B_HEADS, LANES), 1)
    carry = jnp.zeros((B_HEADS, 1), F32)
    for blk in range(n_past + n_new):
        sl = slice(blk * LANES, (blk + 1) * LANES)
        if blk < n_past:
            lf = past_ref[0, :, sl]
        else:
            nsl = slice((blk - n_past) * LANES, (blk - n_past + 1) * LANES)
            x = fz_ref[0, :, nsl] + bf_ref[...]
            lf = jnp.minimum(x, 0.0) - jnp.log1p(jnp.exp(-jnp.abs(x)))
            lf = jnp.where(lane + (blk - n_past) * LANES < t, lf, 0.0)
            lf_ref[0, :, nsl] = lf
        cum = _dot_x01(lf, tri) + carry
        for i, piece in enumerate(_split3(cum * -LOG2E)):
            nck_ref[0, i, :, sl] = piece
        carry = cum[:, LANES - 1:LANES]


def fox_decay(fz_t, b_f, past_t, *, t):
    bsz, h, tp = fz_t.shape
    p = 0 if past_t is None else past_t.shape[2]
    n_past, n_new = p // LANES, tp // LANES
    tri = (jnp.arange(LANES)[:, None] <= jnp.arange(LANES)[None, :]).astype(BF16)
    kern = functools.partial(_fox_decay_kernel, n_past=n_past, n_new=n_new, t=t)
    args = [fz_t, b_f.reshape(h, 1)]
    in_specs = [pl.BlockSpec((1, h, tp), lambda b: (b, 0, 0)),
                pl.BlockSpec((h, 1), lambda b: (0, 0))]
    if n_past:
        args.append(past_t)
        in_specs.append(pl.BlockSpec((1, h, p), lambda b: (b, 0, 0)))
    args.append(tri)
    in_specs.append(pl.BlockSpec((LANES, LANES), lambda b: (0, 0)))
    return pl.pallas_call(
        kern,
        grid=(bsz,),
        in_specs=in_specs,
        out_specs=[pl.BlockSpec((1, h, tp), lambda b: (b, 0, 0)),
                   pl.BlockSpec((1, 3, h, p + tp), lambda b: (b, 0, 0, 0))],
        out_shape=[jax.ShapeDtypeStruct((bsz, h, tp), F32),
                   jax.ShapeDtypeStruct((bsz, 3, h, p + tp), BF16)],
        compiler_params=_cparams("parallel"),
        name="fox_decay",
    )(*args)


def _fox_attn_kernel(q_ref, k_ref, v_ref, pz_ref, g_ref, o_ref, acc_ref, m_ref, s0_ref, ka_ref, vt_ref,
                     *, tq, tqp, q_off, nt):
    j = pl.program_id(1)
    a = pl.program_id(2)
    tk = KEY_TILE
    q0 = q_off + a * tq
    n_full = q0 // tk
    n_need = (q0 + tq - 1) // tk + 1

    @pl.when(a == 0)
    def _():
        klane = lax.broadcasted_iota(I32, (tk, LANES), 1)
        for c in range(nt):
            rows = slice(c * tk, (c + 1) * tk)
            kp = k_ref[0, rows, :]
            pz = pz_ref[0, rows, :].astype(F32)
            ka_ref[0, rows, :] = jnp.where(klane < HEAD_DIM, kp, pz).astype(BF16)
            ka_ref[1, rows, :] = jnp.where(klane >= HEAD_DIM, kp, pz).astype(BF16)
            vt_ref[c] = _vt_rows(v_ref[0, rows, :])

    lane = lax.broadcasted_iota(I32, (tqp, LANES), 1)
    qn = _pad_q_rows(q_ref[0], tq, tqp) * (HEAD_DIM ** -0.5 * LOG2E)
    ones_e = (lane >= HEAD_DIM + 3 * j) & (lane < HEAD_DIM + 3 * j + 3)
    ones_o = (lane >= 3 * j) & (lane < 3 * j + 3)
    qs = (jnp.where(lane < HEAD_DIM, qn, jnp.where(ones_e, 1.0, 0.0)).astype(BF16),
          jnp.where(lane >= HEAD_DIM, qn, jnp.where(ones_o, 1.0, 0.0)).astype(BF16))
    m_ref[...] = jnp.full(m_ref.shape, M_FLOOR, F32)
    acc_ref[...] = jnp.zeros(acc_ref.shape, F32)
    krow = lax.broadcasted_iota(I32, (tk, tqp), 0)
    qcol = lax.broadcasted_iota(I32, (tk, tqp), 1)

    def s_tile(e, c):
        return _nt(ka_ref[e, pl.ds(pl.multiple_of(c * tk, tk), tk), :], qs[e])

    def softmax_pv(e, c, s, masked):
        if masked:
            s = jnp.where(c * tk + krow <= q0 + qcol, s, NEG_BIG)
        m_old = m_ref[e]
        m_new = jnp.maximum(m_old, jnp.max(s, axis=0, keepdims=True))
        p = jnp.exp2(s - m_new).astype(BF16)
        acc_ref[e] = jnp.exp2(m_old - m_new) * acc_ref[e] + _dot(vt_ref[c], p)
        m_ref[e] = m_new

    def step(c, masked):
        s1 = s_tile(1, c)
        softmax_pv(0, c, s0_ref[...], masked)
        s0_ref[...] = s_tile(0, jnp.minimum(c + 1, n_need - 1))
        softmax_pv(1, c, s1, masked)

    def full_body(c, carry):
        step(c, False)
        return carry

    def masked_body(c, carry):
        step(c, True)
        return carry

    s0_ref[...] = s_tile(0, 0)
    lax.fori_loop(0, n_full, full_body, 0)
    lax.fori_loop(n_full, n_need, masked_body, 0)
    o_e = (acc_ref[0, 0:LANES, :] * (1.0 / acc_ref[0, LANES:LANES + 1, :])).T
    o_o = (acc_ref[1, 0:LANES, :] * (1.0 / acc_ref[1, LANES:LANES + 1, :])).T
    o = jnp.where(lane < HEAD_DIM, o_e, o_o)
    o_ref[0] = o[:tq] * (1.0 / (1.0 + jnp.exp(-g_ref[0])))


def fox_attention(proj, keys, values, pz, *, lp, t, q_off, g_blk0, k_blk0, v_blk0):
    bsz = proj.shape[0]
    tq = min(t, KEY_TILE)
    tqp = max(tq, LANES)
    nt = lp // KEY_TILE
    kern = functools.partial(_fox_attn_kernel, tq=tq, tqp=tqp, q_off=q_off, nt=nt)
    return pl.pallas_call(
        kern,
        grid=(bsz, B_HEADS // 2, t // tq),
        in_specs=[pl.BlockSpec((1, tq, LANES), lambda b, j, a: (b, a, j)),
                  pl.BlockSpec((1, lp, LANES), lambda b, j, a: (b, 0, k_blk0 + j)),
                  pl.BlockSpec((1, lp, LANES), lambda b, j, a: (b, 0, v_blk0 + j)),
                  pl.BlockSpec((1, lp, LANES), lambda b, j, a: (b, 0, 0)),
                  pl.BlockSpec((1, tq, LANES), lambda b, j, a: (b, a, g_blk0 + j))],
        out_specs=pl.BlockSpec((1, tq, LANES), lambda b, j, a: (b, a, j)),
        out_shape=jax.ShapeDtypeStruct((bsz, t, B_HEADS * HEAD_DIM), F32),
        scratch_shapes=[pltpu.VMEM((2, VT_ROWS, tqp), F32), pltpu.VMEM((2, 1, tqp), F32),
                        pltpu.VMEM((KEY_TILE, tqp), F32),
                        pltpu.VMEM((2, lp, LANES), BF16),
                        pltpu.VMEM((nt, VT_ROWS, KEY_TILE), BF16)],
        compiler_params=_cparams("parallel", "parallel", "arbitrary"),
        name="fox_attn",
    )(proj, keys, values, pz, proj)


def _hgrn2_levels(tc):
    lv = []
    n = 8
    while n < tc:
        lv.append(n)
        n *= 2
    return lv


def _hgrn2_masks(tc):
    t = jnp.arange(tc)[:, None]
    s = jnp.arange(tc)[None, :]
    ms = [((t // (2 * n) == s // (2 * n)) & ((t // n) % 2 == 1) & ((s // n) % 2 == 0)) for n in _hgrn2_levels(tc)]
    ms.append((t // 8 == s // 8) & (s <= t))
    return jnp.stack(ms).astype(F32)


def _hgrn2_kernel(q_ref, fz_ref, v_ref, g_ref, lb_ref, og_ref, s0_ref, tri_ref, msk_ref, y_ref, so_ref, st_ref, *, tc):
    ct = pl.program_id(2)

    @pl.when(ct == 0)
    def _():
        st_ref[...] = s0_ref[0, 0]

    z = fz_ref[0]
    q = q_ref[0]
    v = v_ref[0]
    lb = lb_ref[0]
    ez = jnp.exp(-jnp.abs(z))
    den = 1.0 / (1.0 + ez)
    pos = z >= 0.0
    f = lb + (1.0 - lb) * (jnp.where(pos, 1.0, ez) * den)
    kk = (1.0 - lb) * (jnp.where(pos, ez, 1.0) * den)
    cum = _dot_01x(tri_ref[...], jnp.log(f))

    def rows(idx):
        parts = []
        for i in idx:
            parts.append(jnp.zeros((8, LANES), F32) if i < 0 else jnp.broadcast_to(cum[i:i + 1, :], (8, LANES)))
        return jnp.concatenate(parts, axis=0)

    levels = _hgrn2_levels(tc)
    ngrp = tc // 8
    scores = jnp.zeros((tc, tc), F32)
    ql8 = None
    for li, n in enumerate(levels):
        start = [((r * 8) // n) * n for r in range(ngrp)]
        a_start = rows([s - 1 for s in start])
        a_end = rows([s + n - 1 for s in start])
        ql = (q * jnp.exp(cum - a_start)).astype(BF16)
        kr = (kk * jnp.exp(a_end - cum)).astype(BF16)
        scores = scores + msk_ref[li] * _nt(ql, kr)
        if n == 8:
            ql8 = ql
            kb = (kk * jnp.exp(a_start - cum)).astype(BF16)
    if ql8 is None:
        a_start = rows([r * 8 - 1 for r in range(ngrp)])
        ql8 = (q * jnp.exp(cum - a_start)).astype(BF16)
        kb = (kk * jnp.exp(a_start - cum)).astype(BF16)
    scores = scores + msk_ref[len(levels)] * _nt(ql8, kb)

    st = st_ref[...]
    o = _nt((q * jnp.exp(cum)).astype(BF16), st.astype(BF16)) + _dot(scores.astype(BF16), v.astype(BF16))
    a_last = cum[tc - 1:tc, :]
    khat = (kk * jnp.exp(a_last - cum)).astype(BF16)
    st_new = st * jnp.exp(a_last) + _dot(v.T.astype(BF16), khat)
    st_ref[...] = st_new

    g = g_ref[0]
    on = o * lax.rsqrt(jnp.mean(o * o, axis=-1, keepdims=True) + EPS) * og_ref[...]
    y_ref[0] = on * (g * (1.0 / (1.0 + jnp.exp(-g))))

    @pl.when(ct == pl.num_programs(2) - 1)
    def _():
        so_ref[0, 0] = st_new


def hgrn2_recurrence(proj, lb, out_gain, s0_t, *, t):
    bsz = proj.shape[0]
    tc = min(t, 128)
    nlv = len(_hgrn2_levels(tc)) + 1
    tri = (jnp.arange(tc)[:, None] >= jnp.arange(tc)[None, :]).astype(BF16)
    h = C_HEADS
    kern = functools.partial(_hgrn2_kernel, tc=tc)
    blk = lambda off: pl.BlockSpec((1, tc, LANES), lambda b, hh, c: (b, c, off + hh))
    return pl.pallas_call(
        kern,
        grid=(bsz, h, t // tc),
        in_specs=[blk(0), blk(h), blk(2 * h), blk(3 * h),
                  pl.BlockSpec((1, 1, C_DK), lambda b, hh, c: (hh, 0, 0)),
                  pl.BlockSpec((1, C_DV), lambda b, hh, c: (0, 0)),
                  pl.BlockSpec((1, 1, C_DV, C_DK), lambda b, hh, c: (b, hh, 0, 0)),
                  pl.BlockSpec((tc, tc), lambda b, hh, c: (0, 0)),
                  pl.BlockSpec((nlv, tc, tc), lambda b, hh, c: (0, 0, 0))],
        out_specs=[pl.BlockSpec((1, tc, LANES), lambda b, hh, c: (b, c, hh)),
                   pl.BlockSpec((1, 1, C_DV, C_DK), lambda b, hh, c: (b, hh, 0, 0))],
        out_shape=[jax.ShapeDtypeStruct((bsz, t, h * C_DV), F32),
                   jax.ShapeDtypeStruct((bsz, h, C_DV, C_DK), F32)],
        scratch_shapes=[pltpu.VMEM((C_DV, C_DK), F32)],
        compiler_params=_cparams("parallel", "parallel", "arbitrary"),
        name="hgrn2",
    )(proj, proj, proj, proj, lb.reshape(h, 1, C_DK), out_gain.reshape(1, C_DV), s0_t, tri, _hgrn2_masks(tc))


def _route(h, wr, br):
    r = _dot_f32(h, wr) + br
    lane = lax.broadcasted_iota(I32, r.shape, 1)
    lanef = lane.astype(F32)
    big = float(LANES)
    is_g = lane < N_GROUPS
    lg = jnp.where(is_g, r, -jnp.inf)
    mg = jnp.max(lg, axis=-1, keepdims=True)
    grp = jnp.min(jnp.where(lg == mg, lanef, big), axis=-1, keepdims=True)
    p_grp = 1.0 / jnp.sum(jnp.where(is_g, jnp.exp(r - mg), 0.0), axis=-1, keepdims=True)
    eg = ((lane - N_GROUPS) >> 2).astype(F32)
    in_e = (lane >= N_GROUPS) & (lane < N_GROUPS + N_EXPERTS) & (eg == grp)
    le = jnp.where(in_e, r, -jnp.inf)
    v1 = jnp.max(le, axis=-1, keepdims=True)
    i1 = jnp.min(jnp.where(le == v1, lanef, big), axis=-1, keepdims=True)
    le2 = jnp.where(lanef == i1, -jnp.inf, le)
    v2 = jnp.max(le2, axis=-1, keepdims=True)
    i2 = jnp.min(jnp.where(le2 == v2, lanef, big), axis=-1, keepdims=True)
    e2 = jnp.exp(v2 - v1)
    w1 = 1.0 / (1.0 + e2)
    gates = jnp.where(lanef == i1, w1 * p_grp, 0.0) + jnp.where(lanef == i2, (e2 * w1) * p_grp, 0.0)
    member = jnp.where((lanef == i1) | (lanef == i2), 1.0, 0.0)
    return gates, member


def _moe_kernel(x_ref, gain_ref, sc_ref, sh_ref, g2_ref, wr_ref, br_ref, tri_ref, wg_ref, wu_ref, wd_ref, o_ref,
                hb_ref, rank_ref, rt_ref, gt_ref, y_ref, acc_ref, *, ts, slot):
    e = pl.program_id(2)
    tm = x_ref.shape[1]
    n_sub = tm // ts
    tsp = rt_ref.shape[2]
    n_chunk = tsp // slot
    row = e + N_GROUPS

    @pl.when(e == 0)
    def _():
        for s in range(n_sub):
            h = _ln_mod(x_ref[0, s * ts:(s + 1) * ts, :], gain_ref[...], sc_ref[0], sh_ref[0])
            gates, member = _route(h, wr_ref[...], br_ref[...])
            hb_ref[s * tsp:(s + 1) * tsp, :] = _pad_q_rows(h.astype(BF16), ts, tsp)
            m_t = _pad_q_rows(member, ts, tsp).T
            r_t = _dot(m_t.astype(BF16), tri_ref[...])
            r_t = jnp.where(m_t > 0.0, r_t, -1.0)
            rt_ref[s] = r_t
            gt_ref[s] = _pad_q_rows(gates, ts, tsp).T
            rank_ref[s] = r_t.T
        if n_chunk > 1:
            acc_ref[...] = jnp.zeros_like(acc_ref)

    def expert_rows(k):
        xs, ges, sels = [], [], []
        slot_i = lax.broadcasted_iota(I32, (slot, tsp), 0).astype(F32) + float(k * slot)
        for s in range(n_sub):
            sel = jnp.where(rt_ref[s, pl.ds(row, 1), :] == slot_i, 1.0, 0.0)
            ges.append(jnp.sum(sel * gt_ref[s, pl.ds(row, 1), :], axis=-1, keepdims=True))
            xs.append(_dot(sel.astype(BF16), hb_ref[s * tsp:(s + 1) * tsp, :]).astype(BF16))
            sels.append(sel)
        xa = jnp.concatenate(xs, axis=0)
        a = _dot(xa, wg_ref[0])
        u = _dot(xa, wu_ref[0])
        he = (a * (1.0 / (1.0 + jnp.exp(-a)))) * u * jnp.concatenate(ges, axis=0)
        return _dot(he.astype(BF16), wd_ref[0]), sels

    ye, _ = expert_rows(0)
    for s in range(n_sub):
        y_ref[s, pl.ds(pl.multiple_of(e * slot, slot), slot), :] = ye[s * slot:(s + 1) * slot].astype(BF16)

    for k in range(1, n_chunk):
        last_rank = jnp.max(rt_ref[0, pl.ds(row, 1), :])
        for s in range(1, n_sub):
            last_rank = jnp.maximum(last_rank, jnp.max(rt_ref[s, pl.ds(row, 1), :]))

        @pl.when(last_rank >= float(k * slot))
        def _():
            ye_k, sels = expert_rows(k)
            for s in range(n_sub):
                acc_ref[s * tsp:(s + 1) * tsp, :] += _dot(sels[s].T.astype(BF16),
                                                          ye_k[s * slot:(s + 1) * slot].astype(BF16))

    @pl.when(e == pl.num_programs(2) - 1)
    def _():
        slot_l = lax.broadcasted_iota(I32, (1, slot), 1).astype(F32)
        for s in range(n_sub):
            rows = slice(s * ts, (s + 1) * ts)
            rank = rank_ref[s]
            sel_t = jnp.concatenate(
                [jnp.where(rank[:, N_GROUPS + j:N_GROUPS + j + 1] == slot_l, 1.0, 0.0).astype(BF16)
                 for j in range(N_EXPERTS)], axis=1)
            y = _dot(sel_t, y_ref[s])
            if n_chunk > 1:
                y = y + acc_ref[s * tsp:(s + 1) * tsp, :]
            o_ref[0, rows, :] = x_ref[0, rows, :] + g2_ref[0] * y[:ts]


def moe_layer(x, gain, sc, sh, g2, w_router, b_router, w_gate, w_up, w_down):
    bsz, t, d = x.shape
    tm = min(t, 1024)
    ts = min(tm, 512)
    tsp = max(ts, LANES)
    slot = MOE_SLOT
    ne, _, de = w_gate.shape
    tri = (jnp.arange(tsp)[:, None] < jnp.arange(tsp)[None, :]).astype(BF16)
    kern = functools.partial(_moe_kernel, ts=ts, slot=slot)
    n_sub = tm // ts
    return pl.pallas_call(
        kern,
        grid=(bsz, t // tm, ne),
        in_specs=[pl.BlockSpec((1, tm, d), lambda b, i, e: (b, i, 0)),
                  pl.BlockSpec((1, d), lambda b, i, e: (0, 0)),
                  pl.BlockSpec((1, 1, d), lambda b, i, e: (b, 0, 0)),
                  pl.BlockSpec((1, 1, d), lambda b, i, e: (b, 0, 0)),
                  pl.BlockSpec((1, 1, d), lambda b, i, e: (b, 0, 0)),
                  pl.BlockSpec((d, LANES), lambda b, i, e: (0, 0)),
                  pl.BlockSpec((1, LANES), lambda b, i, e: (0, 0)),
                  pl.BlockSpec((tsp, tsp), lambda b, i, e: (0, 0)),
                  pl.BlockSpec((1, d, de), lambda b, i, e: (e, 0, 0)),
                  pl.BlockSpec((1, d, de), lambda b, i, e: (e, 0, 0)),
                  pl.BlockSpec((1, de, d), lambda b, i, e: (e, 0, 0))],
        out_specs=pl.BlockSpec((1, tm, d), lambda b, i, e: (b, i, 0)),
        out_shape=jax.ShapeDtypeStruct((bsz, t, d), F32),
        scratch_shapes=[pltpu.VMEM((n_sub * tsp, d), BF16),
                        pltpu.VMEM((n_sub, tsp, LANES), F32),
                        pltpu.VMEM((n_sub, LANES, tsp), F32),
                        pltpu.VMEM((n_sub, LANES, tsp), F32),
                        pltpu.VMEM((n_sub, ne * slot, d), BF16),
                        pltpu.VMEM((n_sub * tsp if tsp > slot else 8, d), F32)],
        compiler_params=_cparams("parallel", "parallel", "arbitrary"),
        name="moe",
    )(x, gain.reshape(1, d), sc, sh, g2, w_router, b_router, tri, w_gate, w_up, w_down)


def _pad_cols(w, n):
    return jnp.pad(w, ((0, 0), (0, n - w.shape[1])))


def _pad_rows(a, n):
    return jnp.pad(a, ((0, 0), (0, n - a.shape[1])) + ((0, 0),) * (a.ndim - 2))


def _round_up(n, m):
    return -(-n // m) * m


def _head_cols(gain_q, nq, gain_k, nk, npad):
    cg = jnp.concatenate([jnp.tile(gain_q, nq), jnp.tile(gain_k, nk)])
    n = cg.shape[0]
    cgain = jnp.pad(cg, (0, npad - n)).reshape(1, npad)
    cflag = (jnp.arange(npad) < n).astype(F32).reshape(1, npad)
    return cgain, cflag


def _dsa_layer(x, mod, past, prm, bdiag, bsub):
    sh1, sc1, g1 = mod
    past_k, past_v, past_ki = past
    bsz, t, d = x.shape
    p = 0 if past_k is None else past_k.shape[1]
    n_keys = p + t
    topk = min(TOPK_MAX, n_keys // 4)
    tn = 256
    n_in = prm['w_in'].shape[1]
    npad = _round_up(n_in, tn)
    nqk = (A_HEADS + A_KV_HEADS) * HEAD_DIM
    w = _pad_cols(prm['w_in'], npad).astype(BF16)
    cgain, cflag = _head_cols(prm['q_norm'], A_HEADS, prm['k_norm'], A_KV_HEADS, npad)
    proj = ln_proj(x, prm['norm'], sc1, sh1, w, cgain, cflag, nqk // tn, tn)
    o_k = A_HEADS * HEAD_DIM
    o_v = o_k + A_KV_HEADS * HEAD_DIM
    o_qi = o_v + A_KV_HEADS * HEAD_DIM
    o_ki = o_qi + IDX_HEADS * IDX_DIM
    k_new = proj[..., o_k:o_v].reshape(bsz, t, A_KV_HEADS, HEAD_DIM)
    v_new = proj[..., o_v:o_qi].reshape(bsz, t, A_KV_HEADS, HEAD_DIM)
    ki_new = proj[..., o_ki:o_ki + IDX_DIM]
    lp = _round_up(n_keys, KEY_TILE)
    kvw = A_KV_HEADS * HEAD_DIM
    if p:
        keys = _pad_rows(jnp.concatenate([past_k.reshape(bsz, p, kvw), proj[..., o_k:o_v]], axis=1), lp)
        values = _pad_rows(jnp.concatenate([past_v.reshape(bsz, p, kvw), proj[..., o_v:o_qi]], axis=1), lp)
        ikeys = _pad_rows(jnp.concatenate([past_ki, ki_new], axis=1), lp)
        ikeys = jnp.pad(ikeys, ((0, 0), (0, 0), (0, LANES - IDX_DIM)))
        k_blk0, v_blk0, ki_blk = 0, 0, 0
    else:
        keys = values = ikeys = proj
        k_blk0, v_blk0, ki_blk = o_k // LANES, o_v // LANES, o_ki // LANES
    mask = dsa_index_mask(proj, ikeys, lp=lp, t=t, q_off=p, topk=topk, qi_blk=o_qi // (4 * LANES),
                          w_blk=o_ki // LANES, ki_blk=ki_blk)
    o = dsa_attention(proj, keys, values, mask, bdiag, bsub, lp=lp, t=t, q_off=p, k_blk0=k_blk0, v_blk0=v_blk0)
    x = out_proj_residual(o, prm['w_out'].astype(BF16), x, g1)
    return x, (k_new, v_new, ki_new)


def _fox_layer(x, mod, past, prm):
    sh1, sc1, g1 = mod
    past_k, past_v, past_lf = past
    bsz, t, d = x.shape
    p = 0 if past_k is None else past_k.shape[1]
    n_keys = p + t
    hd = B_HEADS * HEAD_DIM
    tn = 256
    npad = _round_up(prm['w_in'].shape[1], tn)
    w = _pad_cols(prm['w_in'], npad).astype(BF16)
    cgain, cflag = _head_cols(prm['q_norm'], B_HEADS, prm['k_norm'], B_HEADS, npad)
    proj = ln_proj(x, prm['norm'], sc1, sh1, w, cgain, cflag, 2 * hd // tn, tn)
    k_new = proj[..., hd:2 * hd]
    v_new = proj[..., 2 * hd:3 * hd]
    fz = proj[..., 4 * hd:4 * hd + B_HEADS]
    tp = _round_up(t, LANES)
    fz_t = _pad_rows(fz, tp).transpose(0, 2, 1)
    past_t = None if not p else past_lf.transpose(0, 2, 1)
    lf_t, nck = fox_decay(fz_t, prm['forget_bias'], past_t, t=t)
    logf_new = lf_t[:, :, :t].transpose(0, 2, 1)
    lp = _round_up(n_keys, KEY_TILE)
    if p:
        keys = _pad_rows(jnp.concatenate([past_k.reshape(bsz, p, hd), k_new], axis=1), lp)
        values = _pad_rows(jnp.concatenate([past_v.reshape(bsz, p, hd), v_new], axis=1), lp)
        k_blk0, v_blk0 = 0, 0
    else:
        keys = values = proj
        k_blk0, v_blk0 = hd // LANES, 2 * hd // LANES
    pieces = jnp.pad(nck, ((0, 0), (0, 0), (0, 0), (0, lp - nck.shape[3]))).transpose(0, 3, 2, 1)
    zl = jnp.zeros((bsz, lp, HEAD_DIM - 3 * B_HEADS // 2), pieces.dtype)
    pz = jnp.concatenate([pieces[:, :, 1::2].reshape(bsz, lp, -1), zl,
                          pieces[:, :, 0::2].reshape(bsz, lp, -1), zl], axis=-1)
    o = fox_attention(proj, keys, values, pz, lp=lp, t=t, q_off=p, g_blk0=3 * hd // LANES,
                      k_blk0=k_blk0, v_blk0=v_blk0)
    x = out_proj_residual(o, prm['w_out'].astype(BF16), x, g1)
    return x, (k_new.reshape(bsz, t, B_HEADS, HEAD_DIM), v_new.reshape(bsz, t, B_HEADS, HEAD_DIM), logf_new)


def _hgrn2_layer(x, mod, s0, prm):
    sh1, sc1, g1 = mod
    bsz, t, d = x.shape
    npad = prm['w_in'].shape[1]
    zeros = jnp.zeros((1, npad), F32)
    proj = ln_proj(x, prm['norm'], sc1, sh1, prm['w_in'].astype(BF16), zeros, zeros, 0, 256)
    y, s_t = hgrn2_recurrence(proj, prm['lb'], prm['out_norm'], jnp.swapaxes(s0, -1, -2), t=t)
    x = out_proj_residual(y, prm['w_out'].astype(BF16), x, g1)
    return x, jnp.swapaxes(s_t, -1, -2)


def _trunk(x, c, a_k, a_v, a_kidx, b_k, b_v, b_logf, c_state, prm):
    bsz, t, d = x.shape
    mod_all = ada_mod(c, prm['w_ada'], prm['b_ada'])
    lb_all = jnp.cumsum(jax.nn.softmax(prm['c_lower_bound'].astype(F32), axis=0), axis=0)
    lb_all = lb_all - lb_all[0]
    bdiag, bsub = dsa_bias_tiles(prm['rel_table'])
    out_a, out_b, out_c = [], [], []
    for i in range(DEPTH):
        j = i // N_MIXERS
        kind = i % N_MIXERS
        sh1, sc1, g1, sh2, sc2, g2 = [m.reshape(bsz, 1, d) for m in jnp.split(mod_all[i], 6, axis=-1)]
        mod = (sh1, sc1, g1)
        if kind == 0:
            past = (None, None, None) if a_k is None else (a_k[j], a_v[j], a_kidx[j])
            lp = dict(norm=prm['norm_mix'][i], w_in=prm['a_w_in'][j], q_norm=prm['a_q_norm'][j],
                      k_norm=prm['a_k_norm'][j], w_out=prm['a_w_out'][j])
            x, new = _dsa_layer(x, mod, past, lp, bdiag, bsub)
            out_a.append(new)
        elif kind == 1:
            past = (None, None, None) if b_k is None else (b_k[j], b_v[j], b_logf[j])
            lp = dict(norm=prm['norm_mix'][i], w_in=prm['b_w_in'][j], forget_bias=prm['b_forget_bias'][j],
                      q_norm=prm['b_q_norm'][j], k_norm=prm['b_k_norm'][j], w_out=prm['b_w_out'][j])
            x, new = _fox_layer(x, mod, past, lp)
            out_b.append(new)
        else:
            s0 = jnp.zeros((bsz, C_HEADS, C_DK, C_DV), F32) if c_state is None else c_state[j]
            lp = dict(norm=prm['norm_mix'][i], w_in=prm['c_w_in'][j], lb=lb_all[i],
                      out_norm=prm['c_out_norm'][j], w_out=prm['c_w_out'][j])
            x, new = _hgrn2_layer(x, mod, s0, lp)
            out_c.append(new)
        w_router = jnp.pad(jnp.concatenate([prm['moe_w_group'][i], prm['moe_w_expert'][i]], axis=1),
                           ((0, 0), (0, LANES - N_GROUPS - N_EXPERTS)))
        b_router = jnp.pad(jnp.concatenate([prm['moe_b_group'][i], prm['moe_b_expert'][i]]),
                           (0, LANES - N_GROUPS - N_EXPERTS)).reshape(1, LANES)
        x = moe_layer(x, prm['norm_ffn'][i], sc2, sh2, g2, w_router, b_router,
                      prm['moe_w_gate'][i].astype(BF16), prm['moe_w_up'][i].astype(BF16),
                      prm['moe_w_down'][i].astype(BF16))
    stack = lambda outs, k: jnp.stack([o[k] for o in outs])
    return (x, stack(out_a, 0), stack(out_a, 1), stack(out_a, 2),
            stack(out_b, 0), stack(out_b, 1), stack(out_b, 2), jnp.stack(out_c))


def kernel(x_prompt, x_sample, cache_a_k, cache_a_v, cache_a_kidx, cache_b_k, cache_b_v, cache_b_logf, state_c,
           c_prompt, c_sample, rel_table, w_ada, b_ada, norm_mix, norm_ffn, a_w_in, a_q_norm, a_k_norm, a_w_out,
           b_w_in, b_forget_bias, b_q_norm, b_k_norm, b_w_out, c_w_in, c_lower_bound, c_out_norm, c_w_out,
           moe_w_group, moe_b_group, moe_w_expert, moe_b_expert, moe_w_gate, moe_w_up, moe_w_down):
    prm = {'rel_table': rel_table, 'w_ada': w_ada, 'b_ada': b_ada, 'norm_mix': norm_mix, 'norm_ffn': norm_ffn,
           'a_w_in': a_w_in, 'a_q_norm': a_q_norm, 'a_k_norm': a_k_norm, 'a_w_out': a_w_out,
           'b_w_in': b_w_in, 'b_forget_bias': b_forget_bias, 'b_q_norm': b_q_norm, 'b_k_norm': b_k_norm,
           'b_w_out': b_w_out, 'c_w_in': c_w_in, 'c_lower_bound': c_lower_bound, 'c_out_norm': c_out_norm,
           'c_w_out': c_w_out, 'moe_w_group': moe_w_group, 'moe_b_group': moe_b_group,
           'moe_w_expert': moe_w_expert, 'moe_b_expert': moe_b_expert, 'moe_w_gate': moe_w_gate,
           'moe_w_up': moe_w_up, 'moe_w_down': moe_w_down}
    (y_p, ak_p, av_p, ai_p, bk_p, bv_p, bl_p, cs_p) = _trunk(
        x_prompt, c_prompt, None, None, None, None, None, None, None, prm)
    (y_s, ak_s, av_s, ai_s, bk_s, bv_s, bl_s, cs_s) = _trunk(
        x_sample, c_sample, cache_a_k, cache_a_v, cache_a_kidx, cache_b_k, cache_b_v, cache_b_logf, state_c, prm)
    return (y_p, y_s, ak_p, av_p, ai_p, ak_s, av_s, ai_s, bk_p, bv_p, bl_p, bk_s, bv_s, bl_s, cs_p, cs_s)
```

```python
import functools

import jax
import jax.numpy as jnp
from jax import lax
from jax.experimental import pallas as pl
from jax.experimental.pallas import tpu as pltpu

F32 = jnp.float32
BF16 = jnp.bfloat16
I32 = jnp.int32

LANES = 128
VMEM_LIMIT_BYTES = 56 * 1024 * 1024

DEPTH = 4
N_MIXERS = 3
CHUNK = 64
EPS = 1e-6
HEAD_DIM = 64
A_HEADS = 16
A_KV_HEADS = 4
A_GROUP = A_HEADS // A_KV_HEADS
IDX_HEADS = 8
IDX_DIM = 64
TOPK_MAX = 256
REL_BUCKETS = 32
B_HEADS = 16
C_HEADS = 8
C_DK = 128
C_DV = 128
N_GROUPS = 4
EXPERTS_PER_GROUP = 4
N_EXPERTS = 16
D_EXPERT = 512

LOG2E = 1.4426950408889634
NEG_BIG = -1e30
M_FLOOR = -1e20
INT_MIN = -2 ** 31
KEY_TILE = 512
SLABS = KEY_TILE // LANES
MOE_SLOT = 128
VT_ROWS = LANES + 16
ROUTER_ROWS = 32


def _cparams(*sem):
    return pltpu.CompilerParams(dimension_semantics=sem, vmem_limit_bytes=VMEM_LIMIT_BYTES)


def _nt(a, b):
    return lax.dot_general(a, b, (((1,), (1,)), ((), ())), preferred_element_type=F32)


def _split3(x):
    hi = x.astype(BF16)
    r = x - hi.astype(F32)
    mid = r.astype(BF16)
    lo = (r - mid.astype(F32)).astype(BF16)
    return hi, mid, lo


def _dot(a, b):
    return jnp.dot(a, b, preferred_element_type=F32)


def _dot_x01(x, m01):
    hi, mid, lo = _split3(x)
    return _dot(hi, m01) + _dot(mid, m01) + _dot(lo, m01)


def _dot_01x(m01, x):
    hi, mid, lo = _split3(x)
    return _dot(m01, hi) + _dot(m01, mid) + _dot(m01, lo)


def _dot_f32(a, b):
    ah, am, al = _split3(a)
    bh, bm, bl = _split3(b)
    return _dot(ah, bh) + (_dot(ah, bm) + _dot(am, bh)) + (_dot(ah, bl) + _dot(am, bm) + _dot(al, bh))


def _pad_q_rows(q, tq, tqp):
    if tqp == tq:
        return q
    return jnp.concatenate([q, jnp.zeros((tqp - tq, q.shape[1]), q.dtype)], axis=0)


def _mod_kernel(c_ref, w_ref, b_ref, o_ref):
    o_ref[0] = _dot(c_ref[...], w_ref[0]) + b_ref[0]


def ada_mod(c, w_ada, b_ada):
    nl, d, n6 = w_ada.shape
    bsz = c.shape[0]
    tn = 512
    return pl.pallas_call(
        _mod_kernel,
        grid=(nl, n6 // tn),
        in_specs=[pl.BlockSpec((bsz, d), lambda l, j: (0, 0)),
                  pl.BlockSpec((1, d, tn), lambda l, j: (l, 0, j)),
                  pl.BlockSpec((1, 1, tn), lambda l, j: (l, 0, j))],
        out_specs=pl.BlockSpec((1, bsz, tn), lambda l, j: (l, 0, j)),
        out_shape=jax.ShapeDtypeStruct((nl, bsz, n6), F32),
        compiler_params=_cparams("parallel", "parallel"),
        name="ada_mod",
    )(c, w_ada, b_ada.reshape(nl, 1, n6))


def _mod_spec(m, tm, grid_rank):
    d = m.shape[2]
    if m.shape[1] == 1:
        return pl.BlockSpec((1, 1, d), (lambda b, i, j: (b, 0, 0)) if grid_rank == 3 else (lambda b, i: (b, 0, 0)))
    return pl.BlockSpec((1, tm, d), (lambda b, i, j: (b, i, 0)) if grid_rank == 3 else (lambda b, i: (b, i, 0)))


def _mod_rows(ref, rows):
    return ref[0] if ref.shape[1] == 1 else ref[0, rows, :]


def _merge(x):
    return x.reshape(1, x.shape[0] * x.shape[1], x.shape[2])


def _per_token(m, t):
    bsz, _, d = m.shape
    return jnp.broadcast_to(m, (bsz, t, d)).reshape(1, bsz * t, d)


def _ln_mod(x, gain, sc, sh):
    ms = jnp.mean(x * x, axis=-1, keepdims=True)
    return (x * lax.rsqrt(ms + EPS) * gain) * (1.0 + sc) + sh


def _ln_proj_kernel(x_ref, gain_ref, sc_ref, sh_ref, w_ref, cgain_ref, cflag_ref, bd_ref, o_ref, h_ref,
                    *, n_norm_tiles, tn):
    j = pl.program_id(2)

    @pl.when(j == 0)
    def _():
        h_ref[...] = _ln_mod(x_ref[0], gain_ref[...], sc_ref[0], sh_ref[0]).astype(BF16)

    y = _dot(h_ref[...], w_ref[...])

    def plain():
        o_ref[0] = y

    def normed():
        y2 = y * y
        hi = y2.astype(BF16)
        lo = (y2 - hi.astype(F32)).astype(BF16)
        bd = bd_ref[...]
        segs = []
        for s in range(tn // LANES):
            sl = slice(s * LANES, (s + 1) * LANES)
            segs.append(_dot(hi[:, sl], bd) + _dot(lo[:, sl], bd))
        seg = jnp.concatenate(segs, axis=1)
        yn = y * lax.rsqrt(seg * (1.0 / HEAD_DIM) + EPS) * cgain_ref[...]
        o_ref[0] = jnp.where(cflag_ref[...] > 0.0, yn, y)

    if n_norm_tiles == 0:
        plain()
    else:
        pl.when(j < n_norm_tiles)(normed)
        pl.when(j >= n_norm_tiles)(plain)


def ln_proj(x, gain, sc, sh, w, cgain, cflag, n_norm_tiles, tn=256):
    bsz, t, d = x.shape
    if t % LANES:
        y = ln_proj(_merge(x), gain, _per_token(sc, t), _per_token(sh, t), w, cgain, cflag, n_norm_tiles, tn)
        return y.reshape(bsz, t, -1)
    npad = w.shape[1]
    tm = min(t, 1024)
    bd = (jnp.arange(LANES)[:, None] // HEAD_DIM == jnp.arange(LANES)[None, :] // HEAD_DIM).astype(BF16)
    kern = functools.partial(_ln_proj_kernel, n_norm_tiles=n_norm_tiles, tn=tn)
    return pl.pallas_call(
        kern,
        grid=(bsz, t // tm, npad // tn),
        in_specs=[pl.BlockSpec((1, tm, d), lambda b, i, j: (b, i, 0)),
                  pl.BlockSpec((1, d), lambda b, i, j: (0, 0)),
                  _mod_spec(sc, tm, 3),
                  _mod_spec(sh, tm, 3),
                  pl.BlockSpec((d, tn), lambda b, i, j: (0, j)),
                  pl.BlockSpec((1, tn), lambda b, i, j: (0, j)),
                  pl.BlockSpec((1, tn), lambda b, i, j: (0, j)),
                  pl.BlockSpec((LANES, LANES), lambda b, i, j: (0, 0))],
        out_specs=pl.BlockSpec((1, tm, tn), lambda b, i, j: (b, i, j)),
        out_shape=jax.ShapeDtypeStruct((bsz, t, npad), F32),
        scratch_shapes=[pltpu.VMEM((tm, d), BF16)],
        compiler_params=_cparams("parallel", "parallel", "arbitrary"),
        name="ln_proj",
    )(x, gain.reshape(1, d), sc, sh, w, cgain, cflag, bd)


def _out_proj_kernel(a_ref, w_ref, x_ref, g_ref, o_ref):
    y = _dot(a_ref[0].astype(BF16), w_ref[...])
    o_ref[0] = x_ref[0] + g_ref[0] * y


def out_proj_residual(a, w, x, gate):
    bsz, t, k = a.shape
    if t % LANES:
        return out_proj_residual(_merge(a), w, _merge(x), _per_token(gate, t)).reshape(x.shape)
    d = w.shape[1]
    tm = min(t, 512)
    return pl.pallas_call(
        _out_proj_kernel,
        grid=(bsz, t // tm),
        in_specs=[pl.BlockSpec((1, tm, k), lambda b, i: (b, i, 0)),
                  pl.BlockSpec((k, d), lambda b, i: (0, 0)),
                  pl.BlockSpec((1, tm, d), lambda b, i: (b, i, 0)),
                  _mod_spec(gate, tm, 2)],
        out_specs=pl.BlockSpec((1, tm, d), lambda b, i: (b, i, 0)),
        out_shape=jax.ShapeDtypeStruct((bsz, t, d), F32),
        compiler_params=_cparams("parallel", "parallel"),
        name="out_proj",
    )(a, w, x, gate)


def _dsa_index_kernel(qi_ref, w_ref, ki_ref, o_ref, key_ref, *, tq, tqp, n_slabs, q_off, topk, idx_bits):
    a = pl.program_id(1)
    tk = KEY_TILE
    q0 = q_off + a * tq
    n_kt = (q0 + tq + tk - 1) // tk
    lane = lax.broadcasted_iota(I32, (tqp, LANES), 1)
    half = lane < IDX_DIM
    krow = lax.broadcasted_iota(I32, (tk, tqp), 0)
    qcol = lax.broadcasted_iota(I32, (tk, tqp), 1)
    qchunk = (q0 + qcol) >> 6
    srow = lax.broadcasted_iota(I32, (LANES, tqp), 0)
    w_t = (_pad_q_rows(w_ref[0], tq, tqp) * (IDX_HEADS ** -0.5)).T
    qs = []
    for p in range(IDX_HEADS // 2):
        qp = _pad_q_rows(qi_ref[0, :, p * LANES:(p + 1) * LANES], tq, tqp) * (IDX_DIM ** -0.5)
        qs.append(jnp.where(half, qp, 0.0).astype(BF16))
        qs.append(pltpu.roll(jnp.where(half, 0.0, qp), IDX_DIM, axis=1).astype(BF16))

    def score_tile(c, carry):
        kt = ki_ref[0, pl.ds(pl.multiple_of(c * tk, tk), tk), :].astype(BF16)
        sc = jnp.zeros((tk, tqp), F32)
        for h in range(IDX_HEADS):
            sc = sc + w_t[IDX_DIM + h:IDX_DIM + h + 1, :] * jnp.maximum(_nt(kt, qs[h]), 0.0)
        bits = lax.bitcast_convert_type(sc, I32)
        key = jnp.where(bits < 0, bits ^ 0x7FFFFFFF, bits)
        key = jnp.where(sc == 0.0, 0, key)
        adm = ((c * tk + krow) >> 6) <= qchunk
        key = jnp.where(adm, key, INT_MIN)
        for s_ in range(SLABS):
            key_ref[c * SLABS + s_] = key[s_ * LANES:(s_ + 1) * LANES, :]
        return carry

    lax.fori_loop(0, n_kt, score_tile, 0)

    def count(pred):
        def body(c, acc):
            for s_ in range(SLABS):
                sidx = c * SLABS + s_
                ind = jnp.where(pred(key_ref[sidx], sidx), 1.0, 0.0)
                acc = acc + jnp.sum(ind.reshape(LANES // 8, 8, tqp), axis=0)
            return acc
        acc = lax.fori_loop(0, n_kt, body, jnp.zeros((8, tqp), F32))
        return jnp.sum(acc, axis=0, keepdims=True)

    kf = float(topk)
    n_adm = count(lambda k, s: k > INT_MIN)

    def all_done(cnt_t):
        done = (cnt_t == kf) | (n_adm < kf)
        return (jnp.min(jnp.where(done, 1.0, 0.0)) > 0.0).astype(I32)

    def bit_cond(st):
        return (st[0] < 32) & (st[3] == 0)

    def bit_body(st):
        i, t_u, cnt_t, _ = st
        cand_u = t_u | lax.shift_left(jnp.int32(1), 31 - i)
        cand_s = cand_u ^ INT_MIN
        cnt = count(lambda k, s: k >= cand_s)
        take = cnt >= kf
        cnt_t = jnp.where(take, cnt, cnt_t)
        return i + 1, jnp.where(take, cand_u, t_u), cnt_t, all_done(cnt_t)

    _, t_u, cnt_ge, _ = lax.while_loop(bit_cond, bit_body,
                                       (jnp.int32(0), jnp.zeros((1, tqp), I32), n_adm, all_done(n_adm)))
    thr = t_u ^ INT_MIN
    excess = jnp.where(cnt_ge > kf, 1.0, 0.0)

    def tie_search():
        need = kf - count(lambda k, s: k > thr)

        def j_body(i, j):
            cand = j | lax.shift_left(jnp.int32(1), idx_bits - 1 - i)
            c = count(lambda k, s: (k == thr) & (srow + s * LANES < cand))
            return jnp.where(c < need, cand, j)
        return lax.fori_loop(0, idx_bits, j_body, jnp.zeros((1, tqp), I32))

    j_last = lax.cond(jnp.max(excess) > 0.0, tie_search, lambda: jnp.full((1, tqp), 2 ** idx_bits, I32))

    def write_active(c, carry):
        for s_ in range(SLABS):
            sidx = c * SLABS + s_
            k = key_ref[sidx]
            sel = (k > thr) | ((k == thr) & (srow + sidx * LANES <= j_last))
            sel = sel & (k > INT_MIN)
            o_ref[0, sidx] = jnp.where(sel, 0.0, NEG_BIG)
        return carry

    lax.fori_loop(0, n_kt, write_active, 0)
    neg = jnp.full((LANES, tqp), NEG_BIG, F32)

    def write_inactive(s, carry):
        o_ref[0, s] = neg
        return carry

    lax.fori_loop(n_kt * SLABS, n_slabs, write_inactive, 0)


def dsa_index_mask(proj, keys, *, lp, t, q_off, topk, qi_blk, w_blk, ki_blk):
    bsz = proj.shape[0]
    n_slabs = lp // LANES
    tq = min(t, KEY_TILE)
    tqp = max(tq, LANES)
    idx_bits = max(1, (lp - 1).bit_length())
    kern = functools.partial(_dsa_index_kernel, tq=tq, tqp=tqp, n_slabs=n_slabs, q_off=q_off, topk=topk,
                             idx_bits=idx_bits)
    return pl.pallas_call(
        kern,
        grid=(bsz, t // tq),
        in_specs=[pl.BlockSpec((1, tq, 4 * LANES), lambda b, a: (b, a, qi_blk)),
                  pl.BlockSpec((1, tq, LANES), lambda b, a: (b, a, w_blk)),
                  pl.BlockSpec((1, lp, LANES), lambda b, a: (b, 0, ki_blk))],
        out_specs=pl.BlockSpec((1, n_slabs, LANES, tqp), lambda b, a: (b, 0, 0, a)),
        out_shape=jax.ShapeDtypeStruct((bsz, n_slabs, LANES, (t // tq) * tqp), F32),
        scratch_shapes=[pltpu.VMEM((n_slabs, LANES, tqp), I32)],
        compiler_params=_cparams("parallel", "parallel"),
        name="dsa_index",
    )(proj, proj, keys)


def _bias_kernel(tab_ref, diag_ref, sub_ref):
    h = pl.program_id(0)
    far = tab_ref[REL_BUCKETS // 2 - 1, h]

    def bias(rel):
        n = jnp.abs(rel)
        large = jnp.full(rel.shape, 8, I32)
        for th in (12, 16, 23, 32, 46, 64, 91):
            large = large + jnp.where(n >= th, 1, 0)
        bucket = jnp.where(rel > 0, REL_BUCKETS // 2, 0) + jnp.where(n < 8, n, large)
        acc = jnp.zeros(rel.shape, F32)
        for bk in range(REL_BUCKETS):
            acc = jnp.where(bucket == bk, tab_ref[bk, h], acc)
        return (acc - far) * LOG2E

    ik = lax.broadcasted_iota(I32, (KEY_TILE, KEY_TILE), 0)
    iq = lax.broadcasted_iota(I32, (KEY_TILE, KEY_TILE), 1)
    diag_ref[0] = bias(ik - iq)
    ik = lax.broadcasted_iota(I32, (LANES, LANES), 0)
    iq = lax.broadcasted_iota(I32, (LANES, LANES), 1)
    sub_ref[0] = bias(ik - LANES - iq)


def dsa_bias_tiles(rel_table):
    return pl.pallas_call(
        _bias_kernel,
        grid=(A_HEADS,),
        in_specs=[pl.BlockSpec(memory_space=pltpu.SMEM)],
        out_specs=[pl.BlockSpec((1, KEY_TILE, KEY_TILE), lambda h: (h, 0, 0)),
                   pl.BlockSpec((1, LANES, LANES), lambda h: (h, 0, 0))],
        out_shape=[jax.ShapeDtypeStruct((A_HEADS, KEY_TILE, KEY_TILE), F32),
                   jax.ShapeDtypeStruct((A_HEADS, LANES, LANES), F32)],
        compiler_params=_cparams("parallel"),
        name="dsa_bias",
    )(rel_table)


def _vt_rows(v_tile):
    return jnp.concatenate([v_tile.T.astype(BF16), jnp.ones((VT_ROWS - LANES, v_tile.shape[0]), BF16)], axis=0)


def _dsa_attn_kernel(q_ref, k_ref, v_ref, msk_ref, bd_ref, bs_ref, o_ref, acc_ref, m_ref, s0_ref, vt_ref,
                     *, tq, tqp, q_off, nt):
    a = pl.program_id(1)
    g = pl.program_id(2)
    tk = KEY_TILE
    q0 = q_off + a * tq
    cd = q0 // tk
    cs = jnp.maximum(cd - 1, 0)
    par = g % 2
    lane = lax.broadcasted_iota(I32, (tqp, LANES), 1)
    half = lane < HEAD_DIM

    @pl.when(a == 0)
    def _():
        for c in range(nt):
            vt_ref[g, c] = _vt_rows(v_ref[0, c * tk:(c + 1) * tk, :])

    qs = []
    for e in range(A_GROUP):
        qc = _pad_q_rows(q_ref[0, :, (e // 2) * LANES:(e // 2 + 1) * LANES], tq, tqp) * (HEAD_DIM ** -0.5 * LOG2E)
        own = jnp.where(half, qc, 0.0) if e % 2 == 0 else jnp.where(half, 0.0, qc)
        qs.append(jnp.where(par == e % 2, own, pltpu.roll(own, HEAD_DIM, axis=1)).astype(BF16))
    m_ref[...] = jnp.full(m_ref.shape, M_FLOOR, F32)
    acc_ref[...] = jnp.zeros(acc_ref.shape, F32)

    def s_tile(e, c):
        return _nt(k_ref[0, pl.ds(pl.multiple_of(c * tk, tk), tk), :].astype(BF16), qs[e])

    def softmax_pv(e, c, s, after, kind):
        s = s + jnp.concatenate([msk_ref[0, c * SLABS + i] for i in range(SLABS)], axis=0)
        if kind == "diag":
            s = s + bd_ref[e, :, 0:tqp]
        elif kind == "sub":
            corner = bs_ref[e]
            if tqp > LANES:
                corner = jnp.concatenate([corner, jnp.zeros((LANES, tqp - LANES), F32)], axis=1)
            s = s + jnp.concatenate([jnp.zeros((tk - LANES, tqp), F32), corner], axis=0)
            s = jnp.where(cd > 0, s, NEG_BIG)
        m_old = jnp.minimum(m_ref[e], jnp.maximum(after[0:1, :], -NEG_BIG))
        m_new = jnp.maximum(m_old, jnp.max(s, axis=0, keepdims=True))
        p = jnp.exp2(s - m_new).astype(BF16)
        acc_ref[e] = jnp.exp2(m_old - m_new) * acc_ref[e] + _dot(vt_ref[g, c], p)
        m_ref[e] = m_new

    def step(c, kind):
        s_prev = s0_ref[...]
        for e in range(A_GROUP):
            if e + 1 < A_GROUP:
                s_next = s_tile(e + 1, c)
            else:
                s_next = s_tile(0, jnp.minimum(c + 1, cd))
                s0_ref[...] = s_next
            softmax_pv(e, c, s_prev, s_next, kind)
            s_prev = s_next

    def far_body(c, carry):
        step(c, "far")
        return carry

    s0_ref[...] = s_tile(0, 0)
    lax.fori_loop(0, cs, far_body, 0)
    step(cs, "sub")
    step(cd, "diag")
    outs = []
    for e in range(A_GROUP):
        o_t = (acc_ref[e, 0:LANES, :] * (1.0 / acc_ref[e, LANES:LANES + 1, :])).T
        outs.append(jnp.where(par == e % 2, o_t, pltpu.roll(o_t, HEAD_DIM, axis=1)))
    for c2 in range(A_GROUP // 2):
        o_ref[0, :, c2 * LANES:(c2 + 1) * LANES] = jnp.where(half, outs[2 * c2], outs[2 * c2 + 1])[:tq]


def dsa_attention(proj, keys, values, mask, bdiag, bsub, *, lp, t, q_off, k_blk0, v_blk0):
    bsz = proj.shape[0]
    tq = min(t, KEY_TILE)
    tqp = max(tq, LANES)
    nt = lp // KEY_TILE
    nsl = mask.shape[1]
    kern = functools.partial(_dsa_attn_kernel, tq=tq, tqp=tqp, q_off=q_off, nt=nt)
    return pl.pallas_call(
        kern,
        grid=(bsz, t // tq, A_KV_HEADS),
        in_specs=[pl.BlockSpec((1, tq, 2 * LANES), lambda b, a, g: (b, a, g)),
                  pl.BlockSpec((1, lp, LANES), lambda b, a, g: (b, 0, k_blk0 + g // 2)),
                  pl.BlockSpec((1, lp, LANES), lambda b, a, g: (b, 0, v_blk0 + g // 2)),
                  pl.BlockSpec((1, nsl, LANES, tqp), lambda b, a, g: (b, 0, 0, a)),
                  pl.BlockSpec((A_GROUP, KEY_TILE, tqp), lambda b, a, g: (g, 0, 0)),
                  pl.BlockSpec((A_GROUP, LANES, LANES), lambda b, a, g: (g, 0, 0))],
        out_specs=pl.BlockSpec((1, tq, 2 * LANES), lambda b, a, g: (b, a, g)),
        out_shape=jax.ShapeDtypeStruct((bsz, t, A_HEADS * HEAD_DIM), F32),
        scratch_shapes=[pltpu.VMEM((A_GROUP, VT_ROWS, tqp), F32), pltpu.VMEM((A_GROUP, 1, tqp), F32),
                        pltpu.VMEM((KEY_TILE, tqp), F32),
                        pltpu.VMEM((A_KV_HEADS, nt, VT_ROWS, KEY_TILE), BF16)],
        compiler_params=_cparams("parallel", "arbitrary", "arbitrary"),
        name="dsa_attn",
    )(proj, keys, values, mask, bdiag, bsub)


def _fox_decay_kernel(*refs, n_past, n_new, t):
    if n_past:
        fz_ref, bf_ref, past_ref, tri_ref, lf_ref, nck_ref = refs
    else:
        fz_ref, bf_ref, tri_ref, lf_ref, nck_ref = refs
        past_ref = None
    tri = tri_ref[...]
    lane = lax.broadcasted_iota(I32, (B_HEADS, LANES), 1)
    carry = jnp.zeros((B_HEADS, 1), F32)
    for blk in range(n_past + n_new):
        sl = slice(blk * LANES, (blk + 1) * LANES)
        if blk < n_past:
            lf = past_ref[0, :, sl]
        else:
            nsl = slice((blk - n_past) * LANES, (blk - n_past + 1) * LANES)
            x = fz_ref[0, :, nsl] + bf_ref[...]
            lf = jnp.minimum(x, 0.0) - jnp.log1p(jnp.exp(-jnp.abs(x)))
            lf = jnp.where(lane + (blk - n_past) * LANES < t, lf, 0.0)
            lf_ref[0, :, nsl] = lf
        cum = _dot_x01(lf, tri) + carry
        for i, piece in enumerate(_split3(cum * -LOG2E)):
            nck_ref[0, i, :, sl] = piece
        carry = cum[:, LANES - 1:LANES]


def fox_decay(fz_t, b_f, past_t, *, t):
    bsz, h, tp = fz_t.shape
    p = 0 if past_t is None else past_t.shape[2]
    n_past, n_new = p // LANES, tp // LANES
    tri = (jnp.arange(LANES)[:, None] <= jnp.arange(LANES)[None, :]).astype(BF16)
    kern = functools.partial(_fox_decay_kernel, n_past=n_past, n_new=n_new, t=t)
    args = [fz_t, b_f.reshape(h, 1)]
    in_specs = [pl.BlockSpec((1, h, tp), lambda b: (b, 0, 0)),
                pl.BlockSpec((h, 1), lambda b: (0, 0))]
    if n_past:
        args.append(past_t)
        in_specs.append(pl.BlockSpec((1, h, p), lambda b: (b, 0, 0)))
    args.append(tri)
    in_specs.append(pl.BlockSpec((LANES, LANES), lambda b: (0, 0)))
    return pl.pallas_call(
        kern,
        grid=(bsz,),
        in_specs=in_specs,
        out_specs=[pl.BlockSpec((1, h, tp), lambda b: (b, 0, 0)),
                   pl.BlockSpec((1, 3, h, p + tp), lambda b: (b, 0, 0, 0))],
        out_shape=[jax.ShapeDtypeStruct((bsz, h, tp), F32),
                   jax.ShapeDtypeStruct((bsz, 3, h, p + tp), BF16)],
        compiler_params=_cparams("parallel"),
        name="fox_decay",
    )(*args)


def _fox_attn_kernel(q_ref, k_ref, v_ref, pz_ref, g_ref, o_ref, acc_ref, m_ref, s0_ref, ka_ref, vt_ref,
                     *, tq, tqp, q_off, nt):
    j = pl.program_id(1)
    a = pl.program_id(2)
    tk = KEY_TILE
    q0 = q_off + a * tq
    n_full = q0 // tk
    n_need = (q0 + tq - 1) // tk + 1

    @pl.when(a == 0)
    def _():
        klane = lax.broadcasted_iota(I32, (tk, LANES), 1)
        for c in range(nt):
            rows = slice(c * tk, (c + 1) * tk)
            kp = k_ref[0, rows, :]
            pz = pz_ref[0, rows, :].astype(F32)
            ka_ref[0, rows, :] = jnp.where(klane < HEAD_DIM, kp, pz).astype(BF16)
            ka_ref[1, rows, :] = jnp.where(klane >= HEAD_DIM, kp, pz).astype(BF16)
            vt_ref[c] = _vt_rows(v_ref[0, rows, :])

    lane = lax.broadcasted_iota(I32, (tqp, LANES), 1)
    qn = _pad_q_rows(q_ref[0], tq, tqp) * (HEAD_DIM ** -0.5 * LOG2E)
    ones_e = (lane >= HEAD_DIM + 3 * j) & (lane < HEAD_DIM + 3 * j + 3)
    ones_o = (lane >= 3 * j) & (lane < 3 * j + 3)
    qs = (jnp.where(lane < HEAD_DIM, qn, jnp.where(ones_e, 1.0, 0.0)).astype(BF16),
          jnp.where(lane >= HEAD_DIM, qn, jnp.where(ones_o, 1.0, 0.0)).astype(BF16))
    m_ref[...] = jnp.full(m_ref.shape, M_FLOOR, F32)
    acc_ref[...] = jnp.zeros(acc_ref.shape, F32)
    krow = lax.broadcasted_iota(I32, (tk, tqp), 0)
    qcol = lax.broadcasted_iota(I32, (tk, tqp), 1)

    def s_tile(e, c):
        return _nt(ka_ref[e, pl.ds(pl.multiple_of(c * tk, tk), tk), :], qs[e])

    def softmax_pv(e, c, s, masked):
        if masked:
            s = jnp.where(c * tk + krow <= q0 + qcol, s, NEG_BIG)
        m_old = m_ref[e]
        m_new = jnp.maximum(m_old, jnp.max(s, axis=0, keepdims=True))
        p = jnp.exp2(s - m_new).astype(BF16)
        acc_ref[e] = jnp.exp2(m_old - m_new) * acc_ref[e] + _dot(vt_ref[c], p)
        m_ref[e] = m_new

    def step(c, masked):
        s1 = s_tile(1, c)
        softmax_pv(0, c, s0_ref[...], masked)
        s0_ref[...] = s_tile(0, jnp.minimum(c + 1, n_need - 1))
        softmax_pv(1, c, s1, masked)

    def full_body(c, carry):
        step(c, False)
        return carry

    def masked_body(c, carry):
        step(c, True)
        return carry

    s0_ref[...] = s_tile(0, 0)
    lax.fori_loop(0, n_full, full_body, 0)
    lax.fori_loop(n_full, n_need, masked_body, 0)
    o_e = (acc_ref[0, 0:LANES, :] * (1.0 / acc_ref[0, LANES:LANES + 1, :])).T
    o_o = (acc_ref[1, 0:LANES, :] * (1.0 / acc_ref[1, LANES:LANES + 1, :])).T
    o = jnp.where(lane < HEAD_DIM, o_e, o_o)
    o_ref[0] = o[:tq] * (1.0 / (1.0 + jnp.exp(-g_ref[0])))


def fox_attention(proj, keys, values, pz, *, lp, t, q_off, g_blk0, k_blk0, v_blk0):
    bsz = proj.shape[0]
    tq = min(t, KEY_TILE)
    tqp = max(tq, LANES)
    nt = lp // KEY_TILE
    kern = functools.partial(_fox_attn_kernel, tq=tq, tqp=tqp, q_off=q_off, nt=nt)
    return pl.pallas_call(
        kern,
        grid=(bsz, B_HEADS // 2, t // tq),
        in_specs=[pl.BlockSpec((1, tq, LANES), lambda b, j, a: (b, a, j)),
                  pl.BlockSpec((1, lp, LANES), lambda b, j, a: (b, 0, k_blk0 + j)),
                  pl.BlockSpec((1, lp, LANES), lambda b, j, a: (b, 0, v_blk0 + j)),
                  pl.BlockSpec((1, lp, LANES), lambda b, j, a: (b, 0, 0)),
                  pl.BlockSpec((1, tq, LANES), lambda b, j, a: (b, a, g_blk0 + j))],
        out_specs=pl.BlockSpec((1, tq, LANES), lambda b, j, a: (b, a, j)),
        out_shape=jax.ShapeDtypeStruct((bsz, t, B_HEADS * HEAD_DIM), F32),
        scratch_shapes=[pltpu.VMEM((2, VT_ROWS, tqp), F32), pltpu.VMEM((2, 1, tqp), F32),
                        pltpu.VMEM((KEY_TILE, tqp), F32),
                        pltpu.VMEM((2, lp, LANES), BF16),
                        pltpu.VMEM((nt, VT_ROWS, KEY_TILE), BF16)],
        compiler_params=_cparams("parallel", "parallel", "arbitrary"),
        name="fox_attn",
    )(proj, keys, values, pz, proj)


def _hgrn2_levels(tc):
    lv = []
    n = 8
    while n < tc:
        lv.append(n)
        n *= 2
    return lv


def _hgrn2_masks(tc):
    t = jnp.arange(tc)[:, None]
    s = jnp.arange(tc)[None, :]
    ms = [((t // (2 * n) == s // (2 * n)) & ((t // n) % 2 == 1) & ((s // n) % 2 == 0)) for n in _hgrn2_levels(tc)]
    ms.append((t // 8 == s // 8) & (s <= t))
    return jnp.stack(ms).astype(F32)


def _hgrn2_kernel(q_ref, fz_ref, v_ref, g_ref, lb_ref, og_ref, s0_ref, tri_ref, msk_ref, y_ref, so_ref, st_ref, *, tc):
    ct = pl.program_id(2)

    @pl.when(ct == 0)
    def _():
        st_ref[...] = s0_ref[0, 0]

    z = fz_ref[0]
    q = q_ref[0]
    v = v_ref[0]
    lb = lb_ref[0]
    ez = jnp.exp(-jnp.abs(z))
    den = 1.0 / (1.0 + ez)
    pos = z >= 0.0
    f = lb + (1.0 - lb) * (jnp.where(pos, 1.0, ez) * den)
    kk = (1.0 - lb) * (jnp.where(pos, ez, 1.0) * den)
    cum = _dot_01x(tri_ref[...], jnp.log(f))

    def rows(idx):
        parts = []
        for i in idx:
            parts.append(jnp.zeros((8, LANES), F32) if i < 0 else jnp.broadcast_to(cum[i:i + 1, :], (8, LANES)))
        return jnp.concatenate(parts, axis=0)

    levels = _hgrn2_levels(tc)
    ngrp = tc // 8
    scores = jnp.zeros((tc, tc), F32)
    ql8 = None
    for li, n in enumerate(levels):
        start = [((r * 8) // n) * n for r in range(ngrp)]
        a_start = rows([s - 1 for s in start])
        a_end = rows([s + n - 1 for s in start])
        ql = (q * jnp.exp(cum - a_start)).astype(BF16)
        kr = (kk * jnp.exp(a_end - cum)).astype(BF16)
        scores = scores + msk_ref[li] * _nt(ql, kr)
        if n == 8:
            ql8 = ql
            kb = (kk * jnp.exp(a_start - cum)).astype(BF16)
    if ql8 is None:
        a_start = rows([r * 8 - 1 for r in range(ngrp)])
        ql8 = (q * jnp.exp(cum - a_start)).astype(BF16)
        kb = (kk * jnp.exp(a_start - cum)).astype(BF16)
    scores = scores + msk_ref[len(levels)] * _nt(ql8, kb)

    st = st_ref[...]
    o = _nt((q * jnp.exp(cum)).astype(BF16), st.astype(BF16)) + _dot(scores.astype(BF16), v.astype(BF16))
    a_last = cum[tc - 1:tc, :]
    khat = (kk * jnp.exp(a_last - cum)).astype(BF16)
    st_new = st * jnp.exp(a_last) + _dot(v.T.astype(BF16), khat)
    st_ref[...] = st_new

    g = g_ref[0]
    on = o * lax.rsqrt(jnp.mean(o * o, axis=-1, keepdims=True) + EPS) * og_ref[...]
    y_ref[0] = on * (g * (1.0 / (1.0 + jnp.exp(-g))))

    @pl.when(ct == pl.num_programs(2) - 1)
    def _():
        so_ref[0, 0] = st_new


def hgrn2_recurrence(proj, lb, out_gain, s0_t, *, t):
    bsz = proj.shape[0]
    tc = min(t, 128)
    nlv = len(_hgrn2_levels(tc)) + 1
    tri = (jnp.arange(tc)[:, None] >= jnp.arange(tc)[None, :]).astype(BF16)
    h = C_HEADS
    kern = functools.partial(_hgrn2_kernel, tc=tc)
    blk = lambda off: pl.BlockSpec((1, tc, LANES), lambda b, hh, c: (b, c, off + hh))
    return pl.pallas_call(
        kern,
        grid=(bsz, h, t // tc),
        in_specs=[blk(0), blk(h), blk(2 * h), blk(3 * h),
                  pl.BlockSpec((1, 1, C_DK), lambda b, hh, c: (hh, 0, 0)),
                  pl.BlockSpec((1, C_DV), lambda b, hh, c: (0, 0)),
                  pl.BlockSpec((1, 1, C_DV, C_DK), lambda b, hh, c: (b, hh, 0, 0)),
                  pl.BlockSpec((tc, tc), lambda b, hh, c: (0, 0)),
                  pl.BlockSpec((nlv, tc, tc), lambda b, hh, c: (0, 0, 0))],
        out_specs=[pl.BlockSpec((1, tc, LANES), lambda b, hh, c: (b, c, hh)),
                   pl.BlockSpec((1, 1, C_DV, C_DK), lambda b, hh, c: (b, hh, 0, 0))],
        out_shape=[jax.ShapeDtypeStruct((bsz, t, h * C_DV), F32),
                   jax.ShapeDtypeStruct((bsz, h, C_DV, C_DK), F32)],
        scratch_shapes=[pltpu.VMEM((C_DV, C_DK), F32)],
        compiler_params=_cparams("parallel", "parallel", "arbitrary"),
        name="hgrn2",
    )(proj, proj, proj, proj, lb.reshape(h, 1, C_DK), out_gain.reshape(1, C_DV), s0_t, tri, _hgrn2_masks(tc))


def _route_t(h, wr_t, br_t):
    ah, am, al = _split3(wr_t)
    bh, bm, bl = _split3(h)
    r = _nt(ah, bh) + (_nt(ah, bm) + _nt(am, bh)) + (_nt(ah, bl) + _nt(am, bm) + _nt(al, bh)) + br_t
    row = lax.broadcasted_iota(I32, r.shape, 0)
    rowf = row.astype(F32)
    big = float(ROUTER_ROWS)
    is_g = row < N_GROUPS
    lg = jnp.where(is_g, r, -jnp.inf)
    mg = jnp.max(lg, axis=0, keepdims=True)
    grp = jnp.min(jnp.where(lg == mg, rowf, big), axis=0, keepdims=True)
    p_grp = 1.0 / jnp.sum(jnp.where(is_g, jnp.exp(r - mg), 0.0), axis=0, keepdims=True)
    eg = ((row - N_GROUPS) >> 2).astype(F32)
    in_e = (row >= N_GROUPS) & (row < N_GROUPS + N_EXPERTS) & (eg == grp)
    le = jnp.where(in_e, r, -jnp.inf)
    v1 = jnp.max(le, axis=0, keepdims=True)
    i1 = jnp.min(jnp.where(le == v1, rowf, big), axis=0, keepdims=True)
    le2 = jnp.where(rowf == i1, -jnp.inf, le)
    v2 = jnp.max(le2, axis=0, keepdims=True)
    i2 = jnp.min(jnp.where(le2 == v2, rowf, big), axis=0, keepdims=True)
    e2 = jnp.exp(v2 - v1)
    w1 = 1.0 / (1.0 + e2)
    gates = jnp.where(rowf == i1, w1 * p_grp, 0.0) + jnp.where(rowf == i2, (e2 * w1) * p_grp, 0.0)
    member = jnp.where((rowf == i1) | (rowf == i2), 1.0, 0.0)
    return gates, member


def _moe_kernel(x_ref, gain_ref, sc_ref, sh_ref, g2_ref, wr_ref, br_ref, tri_ref, wg_ref, wu_ref, wd_ref, o_ref,
                hb_ref, rank_ref, rt_ref, gt_ref, y_ref, acc_ref, *, ts, slot):
    e = pl.program_id(2)
    tm = x_ref.shape[1]
    n_sub = tm // ts
    tsp = ts
    n_chunk = tsp // slot
    row = e + N_GROUPS

    @pl.when(e == 0)
    def _():
        for s in range(n_sub):
            rows = slice(s * ts, (s + 1) * ts)
            h = _ln_mod(x_ref[0, rows, :], gain_ref[...], _mod_rows(sc_ref, rows), _mod_rows(sh_ref, rows))
            hb_ref[rows, :] = h.astype(BF16)
            gates_t, member_t = _route_t(h, wr_ref[...], br_ref[...])
            r_t = _dot(member_t.astype(BF16), tri_ref[...])
            r_t = jnp.where(member_t > 0.0, r_t, -1.0)
            rt_ref[s] = r_t
            gt_ref[s] = gates_t
            rank_ref[s] = jnp.concatenate([r_t, jnp.full((LANES - ROUTER_ROWS, ts), -1.0, F32)], axis=0).T
        if n_chunk > 1:
            acc_ref[...] = jnp.zeros_like(acc_ref)

    def expert_rows(k):
        xs, ges, sels = [], [], []
        slot_i = lax.broadcasted_iota(I32, (slot, tsp), 0).astype(F32) + float(k * slot)
        for s in range(n_sub):
            sel = jnp.where(rt_ref[s, pl.ds(row, 1), :] == slot_i, 1.0, 0.0)
            ges.append(jnp.sum(sel * gt_ref[s, pl.ds(row, 1), :], axis=-1, keepdims=True))
            xs.append(_dot(sel.astype(BF16), hb_ref[s * tsp:(s + 1) * tsp, :]).astype(BF16))
            sels.append(sel)
        xa = jnp.concatenate(xs, axis=0)
        a = _dot(xa, wg_ref[0])
        u = _dot(xa, wu_ref[0])
        he = (a * (1.0 / (1.0 + jnp.exp(-a)))) * u * jnp.concatenate(ges, axis=0)
        return _dot(he.astype(BF16), wd_ref[0]), sels

    ye, _ = expert_rows(0)
    for s in range(n_sub):
        y_ref[s, pl.ds(pl.multiple_of(e * slot, slot), slot), :] = ye[s * slot:(s + 1) * slot].astype(BF16)

    for k in range(1, n_chunk):
        last_rank = jnp.max(rt_ref[0, pl.ds(row, 1), :])
        for s in range(1, n_sub):
            last_rank = jnp.maximum(last_rank, jnp.max(rt_ref[s, pl.ds(row, 1), :]))

        @pl.when(last_rank >= float(k * slot))
        def _():
            ye_k, sels = expert_rows(k)
            for s in range(n_sub):
                acc_ref[s * tsp:(s + 1) * tsp, :] += _dot(sels[s].T.astype(BF16),
                                                          ye_k[s * slot:(s + 1) * slot].astype(BF16))

    @pl.when(e == pl.num_programs(2) - 1)
    def _():
        slot_l = lax.broadcasted_iota(I32, (1, slot), 1).astype(F32)
        for s in range(n_sub):
            rows = slice(s * ts, (s + 1) * ts)
            rank = rank_ref[s]
            sel_t = jnp.concatenate(
                [jnp.where(rank[:, N_GROUPS + j:N_GROUPS + j + 1] == slot_l, 1.0, 0.0).astype(BF16)
                 for j in range(N_EXPERTS)], axis=1)
            y = _dot(sel_t, y_ref[s])
            if n_chunk > 1:
                y = y + acc_ref[s * tsp:(s + 1) * tsp, :]
            o_ref[0, rows, :] = x_ref[0, rows, :] + _mod_rows(g2_ref, rows) * y


def moe_layer(x, gain, sc, sh, g2, w_router, b_router, w_gate, w_up, w_down):
    bsz, t, d = x.shape
    if t % LANES:
        y = moe_layer(_merge(x), gain, _per_token(sc, t), _per_token(sh, t), _per_token(g2, t),
                      w_router, b_router, w_gate, w_up, w_down)
        return y.reshape(x.shape)
    tm = min(t, 1024 if sc.shape[1] == 1 else 512)
    ts = min(tm, 512)
    assert ts % LANES == 0 and t % tm == 0, (t, tm, ts)
    tsp = ts
    slot = MOE_SLOT
    ne, _, de = w_gate.shape
    tri = (jnp.arange(tsp)[:, None] < jnp.arange(tsp)[None, :]).astype(BF16)
    kern = functools.partial(_moe_kernel, ts=ts, slot=slot)
    n_sub = tm // ts
    return pl.pallas_call(
        kern,
        grid=(bsz, t // tm, ne),
        in_specs=[pl.BlockSpec((1, tm, d), lambda b, i, e: (b, i, 0)),
                  pl.BlockSpec((1, d), lambda b, i, e: (0, 0)),
                  _mod_spec(sc, tm, 3),
                  _mod_spec(sh, tm, 3),
                  _mod_spec(g2, tm, 3),
                  pl.BlockSpec((ROUTER_ROWS, d), lambda b, i, e: (0, 0)),
                  pl.BlockSpec((ROUTER_ROWS, 1), lambda b, i, e: (0, 0)),
                  pl.BlockSpec((tsp, tsp), lambda b, i, e: (0, 0)),
                  pl.BlockSpec((1, d, de), lambda b, i, e: (e, 0, 0)),
                  pl.BlockSpec((1, d, de), lambda b, i, e: (e, 0, 0)),
                  pl.BlockSpec((1, de, d), lambda b, i, e: (e, 0, 0))],
        out_specs=pl.BlockSpec((1, tm, d), lambda b, i, e: (b, i, 0)),
        out_shape=jax.ShapeDtypeStruct((bsz, t, d), F32),
        scratch_shapes=[pltpu.VMEM((n_sub * tsp, d), BF16),
                        pltpu.VMEM((n_sub, tsp, LANES), F32),
                        pltpu.VMEM((n_sub, ROUTER_ROWS, tsp), F32),
                        pltpu.VMEM((n_sub, ROUTER_ROWS, tsp), F32),
                        pltpu.VMEM((n_sub, ne * slot, d), BF16),
                        pltpu.VMEM((n_sub * tsp if tsp > slot else 8, d), F32)],
        compiler_params=_cparams("parallel", "parallel", "arbitrary"),
        name="moe",
    )(x, gain.reshape(1, d), sc, sh, g2, w_router, b_router, tri, w_gate, w_up, w_down)


def _pad_cols(w, n):
    return jnp.pad(w, ((0, 0), (0, n - w.shape[1])))


def _pad_rows(a, n):
    return jnp.pad(a, ((0, 0), (0, n - a.shape[1])) + ((0, 0),) * (a.ndim - 2))


def _round_up(n, m):
    return -(-n // m) * m


def _head_cols(gain_q, nq, gain_k, nk, npad):
    cg = jnp.concatenate([jnp.tile(gain_q, nq), jnp.tile(gain_k, nk)])
    n = cg.shape[0]
    cgain = jnp.pad(cg, (0, npad - n)).reshape(1, npad)
    cflag = (jnp.arange(npad) < n).astype(F32).reshape(1, npad)
    return cgain, cflag


def _dsa_layer(x, mod, past, prm, bdiag, bsub):
    sh1, sc1, g1 = mod
    past_k, past_v, past_ki = past
    bsz, t, d = x.shape
    p = 0 if past_k is None else past_k.shape[1]
    n_keys = p + t
    topk = min(TOPK_MAX, n_keys // 4)
    tn = 768
    n_in = prm['w_in'].shape[1]
    npad = _round_up(n_in, tn)
    nqk = (A_HEADS + A_KV_HEADS) * HEAD_DIM
    w = _pad_cols(prm['w_in'], npad).astype(BF16)
    cgain, cflag = _head_cols(prm['q_norm'], A_HEADS, prm['k_norm'], A_KV_HEADS, npad)
    proj = ln_proj(x, prm['norm'], sc1, sh1, w, cgain, cflag, -(-nqk // tn), tn)
    o_k = A_HEADS * HEAD_DIM
    o_v = o_k + A_KV_HEADS * HEAD_DIM
    o_qi = o_v + A_KV_HEADS * HEAD_DIM
    o_ki = o_qi + IDX_HEADS * IDX_DIM
    k_new = proj[..., o_k:o_v].reshape(bsz, t, A_KV_HEADS, HEAD_DIM)
    v_new = proj[..., o_v:o_qi].reshape(bsz, t, A_KV_HEADS, HEAD_DIM)
    ki_new = proj[..., o_ki:o_ki + IDX_DIM]
    lp = _round_up(n_keys, KEY_TILE)
    kvw = A_KV_HEADS * HEAD_DIM
    if p:
        keys = _pad_rows(jnp.concatenate([past_k.reshape(bsz, p, kvw), proj[..., o_k:o_v]], axis=1), lp)
        values = _pad_rows(jnp.concatenate([past_v.reshape(bsz, p, kvw), proj[..., o_v:o_qi]], axis=1), lp)
        ikeys = _pad_rows(jnp.concatenate([past_ki, ki_new], axis=1), lp)
        ikeys = jnp.pad(ikeys, ((0, 0), (0, 0), (0, LANES - IDX_DIM)))
        k_blk0, v_blk0, ki_blk = 0, 0, 0
    else:
        keys = values = ikeys = proj
        k_blk0, v_blk0, ki_blk = o_k // LANES, o_v // LANES, o_ki // LANES
    mask = dsa_index_mask(proj, ikeys, lp=lp, t=t, q_off=p, topk=topk, qi_blk=o_qi // (4 * LANES),
                          w_blk=o_ki // LANES, ki_blk=ki_blk)
    o = dsa_attention(proj, keys, values, mask, bdiag, bsub, lp=lp, t=t, q_off=p, k_blk0=k_blk0, v_blk0=v_blk0)
    x = out_proj_residual(o, prm['w_out'].astype(BF16), x, g1)
    return x, (k_new, v_new, ki_new)


def _fox_layer(x, mod, past, prm):
    sh1, sc1, g1 = mod
    past_k, past_v, past_lf = past
    bsz, t, d = x.shape
    p = 0 if past_k is None else past_k.shape[1]
    n_keys = p + t
    hd = B_HEADS * HEAD_DIM
    tn = 512
    npad = _round_up(prm['w_in'].shape[1], tn)
    w = _pad_cols(prm['w_in'], npad).astype(BF16)
    cgain, cflag = _head_cols(prm['q_norm'], B_HEADS, prm['k_norm'], B_HEADS, npad)
    proj = ln_proj(x, prm['norm'], sc1, sh1, w, cgain, cflag, 2 * hd // tn, tn)
    k_new = proj[..., hd:2 * hd]
    v_new = proj[..., 2 * hd:3 * hd]
    fz = proj[..., 4 * hd:4 * hd + B_HEADS]
    tp = _round_up(t, LANES)
    fz_t = _pad_rows(fz, tp).transpose(0, 2, 1)
    past_t = None if not p else past_lf.transpose(0, 2, 1)
    lf_t, nck = fox_decay(fz_t, prm['forget_bias'], past_t, t=t)
    logf_new = lf_t[:, :, :t].transpose(0, 2, 1)
    lp = _round_up(n_keys, KEY_TILE)
    if p:
        keys = _pad_rows(jnp.concatenate([past_k.reshape(bsz, p, hd), k_new], axis=1), lp)
        values = _pad_rows(jnp.concatenate([past_v.reshape(bsz, p, hd), v_new], axis=1), lp)
        k_blk0, v_blk0 = 0, 0
    else:
        keys = values = proj
        k_blk0, v_blk0 = hd // LANES, 2 * hd // LANES
    pieces = jnp.pad(nck, ((0, 0), (0, 0), (0, 0), (0, lp - nck.shape[3]))).transpose(0, 3, 2, 1)
    zl = jnp.zeros((bsz, lp, HEAD_DIM - 3 * B_HEADS // 2), pieces.dtype)
    pz = jnp.concatenate([pieces[:, :, 1::2].reshape(bsz, lp, -1), zl,
                          pieces[:, :, 0::2].reshape(bsz, lp, -1), zl], axis=-1)
    o = fox_attention(proj, keys, values, pz, lp=lp, t=t, q_off=p, g_blk0=3 * hd // LANES,
                      k_blk0=k_blk0, v_blk0=v_blk0)
    x = out_proj_residual(o, prm['w_out'].astype(BF16), x, g1)
    return x, (k_new.reshape(bsz, t, B_HEADS, HEAD_DIM), v_new.reshape(bsz, t, B_HEADS, HEAD_DIM), logf_new)


def _hgrn2_layer(x, mod, s0, prm):
    sh1, sc1, g1 = mod
    bsz, t, d = x.shape
    npad = prm['w_in'].shape[1]
    zeros = jnp.zeros((1, npad), F32)
    proj = ln_proj(x, prm['norm'], sc1, sh1, prm['w_in'].astype(BF16), zeros, zeros, 0, 512)
    y, s_t = hgrn2_recurrence(proj, prm['lb'], prm['out_norm'], jnp.swapaxes(s0, -1, -2), t=t)
    x = out_proj_residual(y, prm['w_out'].astype(BF16), x, g1)
    return x, jnp.swapaxes(s_t, -1, -2)


def _trunk(x, c, a_k, a_v, a_kidx, b_k, b_v, b_logf, c_state, prm):
    bsz, t, d = x.shape
    mod_all = ada_mod(c, prm['w_ada'], prm['b_ada'])
    lb_all = jnp.cumsum(jax.nn.softmax(prm['c_lower_bound'].astype(F32), axis=0), axis=0)
    lb_all = lb_all - lb_all[0]
    bdiag, bsub = dsa_bias_tiles(prm['rel_table'])
    out_a, out_b, out_c = [], [], []
    for i in range(DEPTH):
        j = i // N_MIXERS
        kind = i % N_MIXERS
        sh1, sc1, g1, sh2, sc2, g2 = [m.reshape(bsz, 1, d) for m in jnp.split(mod_all[i], 6, axis=-1)]
        mod = (sh1, sc1, g1)
        if kind == 0:
            past = (None, None, None) if a_k is None else (a_k[j], a_v[j], a_kidx[j])
            lp = dict(norm=prm['norm_mix'][i], w_in=prm['a_w_in'][j], q_norm=prm['a_q_norm'][j],
                      k_norm=prm['a_k_norm'][j], w_out=prm['a_w_out'][j])
            x, new = _dsa_layer(x, mod, past, lp, bdiag, bsub)
            out_a.append(new)
        elif kind == 1:
            past = (None, None, None) if b_k is None else (b_k[j], b_v[j], b_logf[j])
            lp = dict(norm=prm['norm_mix'][i], w_in=prm['b_w_in'][j], forget_bias=prm['b_forget_bias'][j],
                      q_norm=prm['b_q_norm'][j], k_norm=prm['b_k_norm'][j], w_out=prm['b_w_out'][j])
            x, new = _fox_layer(x, mod, past, lp)
            out_b.append(new)
        else:
            s0 = jnp.zeros((bsz, C_HEADS, C_DK, C_DV), F32) if c_state is None else c_state[j]
            lp = dict(norm=prm['norm_mix'][i], w_in=prm['c_w_in'][j], lb=lb_all[i],
                      out_norm=prm['c_out_norm'][j], w_out=prm['c_w_out'][j])
            x, new = _hgrn2_layer(x, mod, s0, lp)
            out_c.append(new)
        w_router = jnp.pad(jnp.concatenate([prm['moe_w_group'][i], prm['moe_w_expert'][i]], axis=1).T,
                           ((0, ROUTER_ROWS - N_GROUPS - N_EXPERTS), (0, 0)))
        b_router = jnp.pad(jnp.concatenate([prm['moe_b_group'][i], prm['moe_b_expert'][i]]),
                           (0, ROUTER_ROWS - N_GROUPS - N_EXPERTS)).reshape(ROUTER_ROWS, 1)
        x = moe_layer(x, prm['norm_ffn'][i], sc2, sh2, g2, w_router, b_router,
                      prm['moe_w_gate'][i].astype(BF16), prm['moe_w_up'][i].astype(BF16),
                      prm['moe_w_down'][i].astype(BF16))
    stack = lambda outs, k: jnp.stack([o[k] for o in outs])
    return (x, stack(out_a, 0), stack(out_a, 1), stack(out_a, 2),
            stack(out_b, 0), stack(out_b, 1), stack(out_b, 2), jnp.stack(out_c))


def kernel(x_prompt, x_sample, cache_a_k, cache_a_v, cache_a_kidx, cache_b_k, cache_b_v, cache_b_logf, state_c,
           c_prompt, c_sample, rel_table, w_ada, b_ada, norm_mix, norm_ffn, a_w_in, a_q_norm, a_k_norm, a_w_out,
           b_w_in, b_forget_bias, b_q_norm, b_k_norm, b_w_out, c_w_in, c_lower_bound, c_out_norm, c_w_out,
           moe_w_group, moe_b_group, moe_w_expert, moe_b_expert, moe_w_gate, moe_w_up, moe_w_down):
    prm = {'rel_table': rel_table, 'w_ada': w_ada, 'b_ada': b_ada, 'norm_mix': norm_mix, 'norm_ffn': norm_ffn,
           'a_w_in': a_w_in, 'a_q_norm': a_q_norm, 'a_k_norm': a_k_norm, 'a_w_out': a_w_out,
           'b_w_in': b_w_in, 'b_forget_bias': b_forget_bias, 'b_q_norm': b_q_norm, 'b_k_norm': b_k_norm,
           'b_w_out': b_w_out, 'c_w_in': c_w_in, 'c_lower_bound': c_lower_bound, 'c_out_norm': c_out_norm,
           'c_w_out': c_w_out, 'moe_w_group': moe_w_group, 'moe_b_group': moe_b_group,
           'moe_w_expert': moe_w_expert, 'moe_b_expert': moe_b_expert, 'moe_w_gate': moe_w_gate,
           'moe_w_up': moe_w_up, 'moe_w_down': moe_w_down}
    (y_p, ak_p, av_p, ai_p, bk_p, bv_p, bl_p, cs_p) = _trunk(
        x_prompt, c_prompt, None, None, None, None, None, None, None, prm)
    (y_s, ak_s, av_s, ai_s, bk_s, bv_s, bl_s, cs_s) = _trunk(
        x_sample, c_sample, cache_a_k, cache_a_v, cache_a_kidx, cache_b_k, cache_b_v, cache_b_logf, state_c, prm)
    return (y_p, y_s, ak_p, av_p, ai_p, ak_s, av_s, ai_s, bk_p, bv_p, bl_p, bk_s, bv_s, bl_s, cs_p, cs_s)
```

```python
import functools

import jax
import jax.numpy as jnp
from jax import lax
from jax.experimental import pallas as pl
from jax.experimental.pallas import tpu as pltpu

F32 = jnp.float32
BF16 = jnp.bfloat16
I32 = jnp.int32

LANES = 128
VMEM_LIMIT_BYTES = 56 * 1024 * 1024

DEPTH = 4
N_MIXERS = 3
CHUNK = 64
EPS = 1e-6
HEAD_DIM = 64
A_HEADS = 16
A_KV_HEADS = 4
A_GROUP = A_HEADS // A_KV_HEADS
IDX_HEADS = 8
IDX_DIM = 64
TOPK_MAX = 256
REL_BUCKETS = 32
B_HEADS = 16
C_HEADS = 8
C_DK = 128
C_DV = 128
N_GROUPS = 4
EXPERTS_PER_GROUP = 4
N_EXPERTS = 16
D_EXPERT = 512

LOG2E = 1.4426950408889634
NEG_BIG = -1e30
M_FLOOR = -1e20
INT_MIN = -2 ** 31
KEY_TILE = 512
SLABS = KEY_TILE // LANES
MOE_SLOT = 128
VT_ROWS = LANES + 16
ROUTER_ROWS = 32
HGRN2_HEADS_PER_STEP = 4


def _cparams(*sem):
    return pltpu.CompilerParams(dimension_semantics=sem, vmem_limit_bytes=VMEM_LIMIT_BYTES)


def _nt(a, b):
    return lax.dot_general(a, b, (((1,), (1,)), ((), ())), preferred_element_type=F32)


def _split3(x):
    hi = x.astype(BF16)
    r = x - hi.astype(F32)
    mid = r.astype(BF16)
    lo = (r - mid.astype(F32)).astype(BF16)
    return hi, mid, lo


def _dot(a, b):
    return jnp.dot(a, b, preferred_element_type=F32)


def _dot_x01(x, m01):
    hi, mid, lo = _split3(x)
    return _dot(hi, m01) + _dot(mid, m01) + _dot(lo, m01)


def _dot_01x(m01, x):
    hi, mid, lo = _split3(x)
    return _dot(m01, hi) + _dot(m01, mid) + _dot(m01, lo)


def _dot_f32(a, b):
    ah, am, al = _split3(a)
    bh, bm, bl = _split3(b)
    return _dot(ah, bh) + (_dot(ah, bm) + _dot(am, bh)) + (_dot(ah, bl) + _dot(am, bm) + _dot(al, bh))


def _pad_q_rows(q, tq, tqp):
    if tqp == tq:
        return q
    return jnp.concatenate([q, jnp.zeros((tqp - tq, q.shape[1]), q.dtype)], axis=0)


def _mod_kernel(c_ref, w_ref, b_ref, o_ref):
    o_ref[0] = _dot(c_ref[...], w_ref[0]) + b_ref[0]


def ada_mod(c, w_ada, b_ada):
    nl, d, n6 = w_ada.shape
    bsz = c.shape[0]
    tn = 512
    return pl.pallas_call(
        _mod_kernel,
        grid=(nl, n6 // tn),
        in_specs=[pl.BlockSpec((bsz, d), lambda l, j: (0, 0)),
                  pl.BlockSpec((1, d, tn), lambda l, j: (l, 0, j)),
                  pl.BlockSpec((1, 1, tn), lambda l, j: (l, 0, j))],
        out_specs=pl.BlockSpec((1, bsz, tn), lambda l, j: (l, 0, j)),
        out_shape=jax.ShapeDtypeStruct((nl, bsz, n6), F32),
        compiler_params=_cparams("parallel", "parallel"),
        name="ada_mod",
    )(c, w_ada, b_ada.reshape(nl, 1, n6))


def _mod_spec(m, tm, grid_rank):
    d = m.shape[2]
    if m.shape[1] == 1:
        return pl.BlockSpec((1, 1, d), (lambda b, i, j: (b, 0, 0)) if grid_rank == 3 else (lambda b, i: (b, 0, 0)))
    return pl.BlockSpec((1, tm, d), (lambda b, i, j: (b, i, 0)) if grid_rank == 3 else (lambda b, i: (b, i, 0)))


def _mod_rows(ref, rows):
    return ref[0] if ref.shape[1] == 1 else ref[0, rows, :]


def _merge(x):
    return x.reshape(1, x.shape[0] * x.shape[1], x.shape[2])


def _per_token(m, t):
    bsz, _, d = m.shape
    return jnp.broadcast_to(m, (bsz, t, d)).reshape(1, bsz * t, d)


def _ln_mod(x, gain, sc, sh):
    ms = jnp.mean(x * x, axis=-1, keepdims=True)
    return (x * lax.rsqrt(ms + EPS) * gain) * (1.0 + sc) + sh


def _ln_proj_kernel(x_ref, gain_ref, sc_ref, sh_ref, w_ref, cgain_ref, cflag_ref, bd_ref, o_ref, h_ref,
                    *, n_norm_tiles, tn):
    j = pl.program_id(2)

    @pl.when(j == 0)
    def _():
        h_ref[...] = _ln_mod(x_ref[0], gain_ref[...], sc_ref[0], sh_ref[0]).astype(BF16)

    y = _dot(h_ref[...], w_ref[...])

    def plain():
        o_ref[0] = y

    def normed():
        y2 = y * y
        hi = y2.astype(BF16)
        lo = (y2 - hi.astype(F32)).astype(BF16)
        bd = bd_ref[...]
        segs = []
        for s in range(tn // LANES):
            sl = slice(s * LANES, (s + 1) * LANES)
            segs.append(_dot(hi[:, sl], bd) + _dot(lo[:, sl], bd))
        seg = jnp.concatenate(segs, axis=1)
        yn = y * lax.rsqrt(seg * (1.0 / HEAD_DIM) + EPS) * cgain_ref[...]
        o_ref[0] = jnp.where(cflag_ref[...] > 0.0, yn, y)

    if n_norm_tiles == 0:
        plain()
    else:
        pl.when(j < n_norm_tiles)(normed)
        pl.when(j >= n_norm_tiles)(plain)


def ln_proj(x, gain, sc, sh, w, cgain, cflag, n_norm_tiles, tn=256):
    bsz, t, d = x.shape
    if t % LANES:
        y = ln_proj(_merge(x), gain, _per_token(sc, t), _per_token(sh, t), w, cgain, cflag, n_norm_tiles, tn)
        return y.reshape(bsz, t, -1)
    npad = w.shape[1]
    tm = min(t, 1024)
    bd = (jnp.arange(LANES)[:, None] // HEAD_DIM == jnp.arange(LANES)[None, :] // HEAD_DIM).astype(BF16)
    kern = functools.partial(_ln_proj_kernel, n_norm_tiles=n_norm_tiles, tn=tn)
    return pl.pallas_call(
        kern,
        grid=(bsz, t // tm, npad // tn),
        in_specs=[pl.BlockSpec((1, tm, d), lambda b, i, j: (b, i, 0)),
                  pl.BlockSpec((1, d), lambda b, i, j: (0, 0)),
                  _mod_spec(sc, tm, 3),
                  _mod_spec(sh, tm, 3),
                  pl.BlockSpec((d, tn), lambda b, i, j: (0, j)),
                  pl.BlockSpec((1, tn), lambda b, i, j: (0, j)),
                  pl.BlockSpec((1, tn), lambda b, i, j: (0, j)),
                  pl.BlockSpec((LANES, LANES), lambda b, i, j: (0, 0))],
        out_specs=pl.BlockSpec((1, tm, tn), lambda b, i, j: (b, i, j)),
        out_shape=jax.ShapeDtypeStruct((bsz, t, npad), F32),
        scratch_shapes=[pltpu.VMEM((tm, d), BF16)],
        compiler_params=_cparams("parallel", "parallel", "arbitrary"),
        name="ln_proj",
    )(x, gain.reshape(1, d), sc, sh, w, cgain, cflag, bd)


def _out_proj_kernel(a_ref, w_ref, x_ref, g_ref, o_ref):
    y = _dot(a_ref[0].astype(BF16), w_ref[...])
    o_ref[0] = x_ref[0] + g_ref[0] * y


def out_proj_residual(a, w, x, gate):
    bsz, t, k = a.shape
    if t % LANES:
        return out_proj_residual(_merge(a), w, _merge(x), _per_token(gate, t)).reshape(x.shape)
    d = w.shape[1]
    tm = min(t, 512)
    return pl.pallas_call(
        _out_proj_kernel,
        grid=(bsz, t // tm),
        in_specs=[pl.BlockSpec((1, tm, k), lambda b, i: (b, i, 0)),
                  pl.BlockSpec((k, d), lambda b, i: (0, 0)),
                  pl.BlockSpec((1, tm, d), lambda b, i: (b, i, 0)),
                  _mod_spec(gate, tm, 2)],
        out_specs=pl.BlockSpec((1, tm, d), lambda b, i: (b, i, 0)),
        out_shape=jax.ShapeDtypeStruct((bsz, t, d), F32),
        compiler_params=_cparams("parallel", "parallel"),
        name="out_proj",
    )(a, w, x, gate)


def _dsa_index_kernel(qi_ref, w_ref, ki_ref, o_ref, key_ref, *, tq, tqp, n_slabs, q_off, topk, idx_bits):
    a = pl.program_id(1)
    tk = KEY_TILE
    q0 = q_off + a * tq
    n_kt = (q0 + tq + tk - 1) // tk
    lane = lax.broadcasted_iota(I32, (tqp, LANES), 1)
    half = lane < IDX_DIM
    krow = lax.broadcasted_iota(I32, (tk, tqp), 0)
    qcol = lax.broadcasted_iota(I32, (tk, tqp), 1)
    qchunk = (q0 + qcol) >> 6
    srow = lax.broadcasted_iota(I32, (LANES, tqp), 0)
    w_t = (_pad_q_rows(w_ref[0], tq, tqp) * (IDX_HEADS ** -0.5)).T
    qs = []
    for p in range(IDX_HEADS // 2):
        qp = _pad_q_rows(qi_ref[0, :, p * LANES:(p + 1) * LANES], tq, tqp) * (IDX_DIM ** -0.5)
        qs.append(jnp.where(half, qp, 0.0).astype(BF16))
        qs.append(pltpu.roll(jnp.where(half, 0.0, qp), IDX_DIM, axis=1).astype(BF16))

    def score_tile(c, carry):
        kt = ki_ref[0, pl.ds(pl.multiple_of(c * tk, tk), tk), :].astype(BF16)
        sc = jnp.zeros((tk, tqp), F32)
        for h in range(IDX_HEADS):
            sc = sc + w_t[IDX_DIM + h:IDX_DIM + h + 1, :] * jnp.maximum(_nt(kt, qs[h]), 0.0)
        bits = lax.bitcast_convert_type(sc, I32)
        key = jnp.where(bits < 0, bits ^ 0x7FFFFFFF, bits)
        key = jnp.where(sc == 0.0, 0, key)
        adm = ((c * tk + krow) >> 6) <= qchunk
        key = jnp.where(adm, key, INT_MIN)
        for s_ in range(SLABS):
            key_ref[c * SLABS + s_] = key[s_ * LANES:(s_ + 1) * LANES, :]
        return carry

    lax.fori_loop(0, n_kt, score_tile, 0)

    def count(pred):
        def body(c, acc):
            for s_ in range(SLABS):
                sidx = c * SLABS + s_
                ind = jnp.where(pred(key_ref[sidx], sidx), 1.0, 0.0)
                acc = acc + jnp.sum(ind.reshape(LANES // 8, 8, tqp), axis=0)
            return acc
        acc = lax.fori_loop(0, n_kt, body, jnp.zeros((8, tqp), F32))
        return jnp.sum(acc, axis=0, keepdims=True)

    kf = float(topk)
    n_adm = count(lambda k, s: k > INT_MIN)

    def all_done(cnt_t):
        done = (cnt_t == kf) | (n_adm < kf)
        return (jnp.min(jnp.where(done, 1.0, 0.0)) > 0.0).astype(I32)

    def bit_cond(st):
        return (st[0] < 32) & (st[3] == 0)

    def bit_body(st):
        i, t_u, cnt_t, _ = st
        cand_u = t_u | lax.shift_left(jnp.int32(1), 31 - i)
        cand_s = cand_u ^ INT_MIN
        cnt = count(lambda k, s: k >= cand_s)
        take = cnt >= kf
        cnt_t = jnp.where(take, cnt, cnt_t)
        return i + 1, jnp.where(take, cand_u, t_u), cnt_t, all_done(cnt_t)

    _, t_u, cnt_ge, _ = lax.while_loop(bit_cond, bit_body,
                                       (jnp.int32(0), jnp.zeros((1, tqp), I32), n_adm, all_done(n_adm)))
    thr = t_u ^ INT_MIN
    excess = jnp.where(cnt_ge > kf, 1.0, 0.0)

    def tie_search():
        need = kf - count(lambda k, s: k > thr)

        def j_body(i, j):
            cand = j | lax.shift_left(jnp.int32(1), idx_bits - 1 - i)
            c = count(lambda k, s: (k == thr) & (srow + s * LANES < cand))
            return jnp.where(c < need, cand, j)
        return lax.fori_loop(0, idx_bits, j_body, jnp.zeros((1, tqp), I32))

    j_last = lax.cond(jnp.max(excess) > 0.0, tie_search, lambda: jnp.full((1, tqp), 2 ** idx_bits, I32))

    def write_active(c, carry):
        for s_ in range(SLABS):
            sidx = c * SLABS + s_
            k = key_ref[sidx]
            sel = (k > thr) | ((k == thr) & (srow + sidx * LANES <= j_last))
            sel = sel & (k > INT_MIN)
            o_ref[0, sidx] = jnp.where(sel, 0.0, NEG_BIG)
        return carry

    lax.fori_loop(0, n_kt, write_active, 0)
    neg = jnp.full((LANES, tqp), NEG_BIG, F32)

    def write_inactive(s, carry):
        o_ref[0, s] = neg
        return carry

    lax.fori_loop(n_kt * SLABS, n_slabs, write_inactive, 0)


def dsa_index_mask(proj, keys, *, lp, t, q_off, topk, qi_blk, w_blk, ki_blk):
    bsz = proj.shape[0]
    n_slabs = lp // LANES
    tq = min(t, KEY_TILE)
    tqp = max(tq, LANES)
    idx_bits = max(1, (lp - 1).bit_length())
    kern = functools.partial(_dsa_index_kernel, tq=tq, tqp=tqp, n_slabs=n_slabs, q_off=q_off, topk=topk,
                             idx_bits=idx_bits)
    return pl.pallas_call(
        kern,
        grid=(bsz, t // tq),
        in_specs=[pl.BlockSpec((1, tq, 4 * LANES), lambda b, a: (b, a, qi_blk)),
                  pl.BlockSpec((1, tq, LANES), lambda b, a: (b, a, w_blk)),
                  pl.BlockSpec((1, lp, LANES), lambda b, a: (b, 0, ki_blk))],
        out_specs=pl.BlockSpec((1, n_slabs, LANES, tqp), lambda b, a: (b, 0, 0, a)),
        out_shape=jax.ShapeDtypeStruct((bsz, n_slabs, LANES, (t // tq) * tqp), F32),
        scratch_shapes=[pltpu.VMEM((n_slabs, LANES, tqp), I32)],
        compiler_params=_cparams("parallel", "parallel"),
        name="dsa_index",
    )(proj, proj, keys)


def _bias_kernel(tab_ref, diag_ref, sub_ref):
    h = pl.program_id(0)
    far = tab_ref[REL_BUCKETS // 2 - 1, h]

    def bias(rel):
        n = jnp.abs(rel)
        large = jnp.full(rel.shape, 8, I32)
        for th in (12, 16, 23, 32, 46, 64, 91):
            large = large + jnp.where(n >= th, 1, 0)
        bucket = jnp.where(rel > 0, REL_BUCKETS // 2, 0) + jnp.where(n < 8, n, large)
        acc = jnp.zeros(rel.shape, F32)
        for bk in range(REL_BUCKETS):
            acc = jnp.where(bucket == bk, tab_ref[bk, h], acc)
        return (acc - far) * LOG2E

    ik = lax.broadcasted_iota(I32, (KEY_TILE, KEY_TILE), 0)
    iq = lax.broadcasted_iota(I32, (KEY_TILE, KEY_TILE), 1)
    diag_ref[0] = bias(ik - iq)
    ik = lax.broadcasted_iota(I32, (LANES, LANES), 0)
    iq = lax.broadcasted_iota(I32, (LANES, LANES), 1)
    sub_ref[0] = bias(ik - LANES - iq)


def dsa_bias_tiles(rel_table):
    return pl.pallas_call(
        _bias_kernel,
        grid=(A_HEADS,),
        in_specs=[pl.BlockSpec(memory_space=pltpu.SMEM)],
        out_specs=[pl.BlockSpec((1, KEY_TILE, KEY_TILE), lambda h: (h, 0, 0)),
                   pl.BlockSpec((1, LANES, LANES), lambda h: (h, 0, 0))],
        out_shape=[jax.ShapeDtypeStruct((A_HEADS, KEY_TILE, KEY_TILE), F32),
                   jax.ShapeDtypeStruct((A_HEADS, LANES, LANES), F32)],
        compiler_params=_cparams("parallel"),
        name="dsa_bias",
    )(rel_table)


def _vt_rows(v_tile):
    return jnp.concatenate([v_tile.T.astype(BF16), jnp.ones((VT_ROWS - LANES, v_tile.shape[0]), BF16)], axis=0)


def _dsa_attn_kernel(q_ref, k_ref, v_ref, msk_ref, bd_ref, bs_ref, o_ref, acc_ref, m_ref, s0_ref, vt_ref,
                     *, tq, tqp, q_off, nt):
    a = pl.program_id(1)
    g = pl.program_id(2)
    tk = KEY_TILE
    q0 = q_off + a * tq
    cd = q0 // tk
    cs = jnp.maximum(cd - 1, 0)
    par = g % 2
    lane = lax.broadcasted_iota(I32, (tqp, LANES), 1)
    half = lane < HEAD_DIM

    @pl.when(a == 0)
    def _():
        for c in range(nt):
            vt_ref[g, c] = _vt_rows(v_ref[0, c * tk:(c + 1) * tk, :])

    qs = []
    for e in range(A_GROUP):
        qc = _pad_q_rows(q_ref[0, :, (e // 2) * LANES:(e // 2 + 1) * LANES], tq, tqp) * (HEAD_DIM ** -0.5 * LOG2E)
        own = jnp.where(half, qc, 0.0) if e % 2 == 0 else jnp.where(half, 0.0, qc)
        qs.append(jnp.where(par == e % 2, own, pltpu.roll(own, HEAD_DIM, axis=1)).astype(BF16))
    m_ref[...] = jnp.full(m_ref.shape, M_FLOOR, F32)
    acc_ref[...] = jnp.zeros(acc_ref.shape, F32)

    def s_tile(e, c):
        return _nt(k_ref[0, pl.ds(pl.multiple_of(c * tk, tk), tk), :].astype(BF16), qs[e])

    def softmax_pv(e, c, s, after, kind):
        s = s + jnp.concatenate([msk_ref[0, c * SLABS + i] for i in range(SLABS)], axis=0)
        if kind == "diag":
            s = s + bd_ref[e, :, 0:tqp]
        elif kind == "sub":
            corner = bs_ref[e]
            if tqp > LANES:
                corner = jnp.concatenate([corner, jnp.zeros((LANES, tqp - LANES), F32)], axis=1)
            s = s + jnp.concatenate([jnp.zeros((tk - LANES, tqp), F32), corner], axis=0)
            s = jnp.where(cd > 0, s, NEG_BIG)
        m_old = jnp.minimum(m_ref[e], jnp.maximum(after[0:1, :], -NEG_BIG))
        m_new = jnp.maximum(m_old, jnp.max(s, axis=0, keepdims=True))
        p = jnp.exp2(s - m_new).astype(BF16)
        acc_ref[e] = jnp.exp2(m_old - m_new) * acc_ref[e] + _dot(vt_ref[g, c], p)
        m_ref[e] = m_new

    def step(c, kind):
        s_prev = s0_ref[...]
        for e in range(A_GROUP):
            if e + 1 < A_GROUP:
                s_next = s_tile(e + 1, c)
            else:
                s_next = s_tile(0, jnp.minimum(c + 1, cd))
                s0_ref[...] = s_next
            softmax_pv(e, c, s_prev, s_next, kind)
            s_prev = s_next

    def far_body(c, carry):
        step(c, "far")
        return carry

    def far_pair(i, carry):
        step(2 * i, "far")
        step(2 * i + 1, "far")
        return carry

    s0_ref[...] = s_tile(0, 0)
    lax.fori_loop(0, cs // 2, far_pair, 0)
    lax.fori_loop((cs // 2) * 2, cs, far_body, 0)
    step(cs, "sub")
    step(cd, "diag")
    outs = []
    for e in range(A_GROUP):
        o_t = (acc_ref[e, 0:LANES, :] * (1.0 / acc_ref[e, LANES:LANES + 1, :])).T
        outs.append(jnp.where(par == e % 2, o_t, pltpu.roll(o_t, HEAD_DIM, axis=1)))
    for c2 in range(A_GROUP // 2):
        o_ref[0, :, c2 * LANES:(c2 + 1) * LANES] = jnp.where(half, outs[2 * c2], outs[2 * c2 + 1])[:tq]


def dsa_attention(proj, keys, values, mask, bdiag, bsub, *, lp, t, q_off, k_blk0, v_blk0):
    bsz = proj.shape[0]
    tq = min(t, KEY_TILE)
    tqp = max(tq, LANES)
    nt = lp // KEY_TILE
    nsl = mask.shape[1]
    kern = functools.partial(_dsa_attn_kernel, tq=tq, tqp=tqp, q_off=q_off, nt=nt)
    return pl.pallas_call(
        kern,
        grid=(bsz, t // tq, A_KV_HEADS),
        in_specs=[pl.BlockSpec((1, tq, 2 * LANES), lambda b, a, g: (b, a, g)),
                  pl.BlockSpec((1, lp, LANES), lambda b, a, g: (b, 0, k_blk0 + g // 2)),
                  pl.BlockSpec((1, lp, LANES), lambda b, a, g: (b, 0, v_blk0 + g // 2)),
                  pl.BlockSpec((1, nsl, LANES, tqp), lambda b, a, g: (b, 0, 0, a)),
                  pl.BlockSpec((A_GROUP, KEY_TILE, tqp), lambda b, a, g: (g, 0, 0)),
                  pl.BlockSpec((A_GROUP, LANES, LANES), lambda b, a, g: (g, 0, 0))],
        out_specs=pl.BlockSpec((1, tq, 2 * LANES), lambda b, a, g: (b, a, g)),
        out_shape=jax.ShapeDtypeStruct((bsz, t, A_HEADS * HEAD_DIM), F32),
        scratch_shapes=[pltpu.VMEM((A_GROUP, VT_ROWS, tqp), F32), pltpu.VMEM((A_GROUP, 1, tqp), F32),
                        pltpu.VMEM((KEY_TILE, tqp), F32),
                        pltpu.VMEM((A_KV_HEADS, nt, VT_ROWS, KEY_TILE), BF16)],
        compiler_params=_cparams("parallel", "arbitrary", "arbitrary"),
        name="dsa_attn",
    )(proj, keys, values, mask, bdiag, bsub)


def _fox_decay_kernel(*refs, n_past, n_new, t):
    if n_past:
        fz_ref, bf_ref, past_ref, tri_ref, lf_ref, nck_ref = refs
    else:
        fz_ref, bf_ref, tri_ref, lf_ref, nck_ref = refs
        past_ref = None
    tri = tri_ref[...]
    lane = lax.broadcasted_iota(I32, (B_HEADS, LANES), 1)
    carry = jnp.zeros((B_HEADS, 1), F32)
    for blk in range(n_past + n_new):
        sl = slice(blk * LANES, (blk + 1) * LANES)
        if blk < n_past:
            lf = past_ref[0, :, sl]
        else:
            nsl = slice((blk - n_past) * LANES, (blk - n_past + 1) * LANES)
            x = fz_ref[0, :, nsl] + bf_ref[...]
            lf = jnp.minimum(x, 0.0) - jnp.log1p(jnp.exp(-jnp.abs(x)))
            lf = jnp.where(lane + (blk - n_past) * LANES < t, lf, 0.0)
            lf_ref[0, :, nsl] = lf
        cum = _dot_x01(lf, tri) + carry
        for i, piece in enumerate(_split3(cum * -LOG2E)):
            nck_ref[0, i, :, sl] = piece
        carry = cum[:, LANES - 1:LANES]


def fox_decay(fz_t, b_f, past_t, *, t):
    bsz, h, tp = fz_t.shape
    p = 0 if past_t is None else past_t.shape[2]
    n_past, n_new = p // LANES, tp // LANES
    tri = (jnp.arange(LANES)[:, None] <= jnp.arange(LANES)[None, :]).astype(BF16)
    kern = functools.partial(_fox_decay_kernel, n_past=n_past, n_new=n_new, t=t)
    args = [fz_t, b_f.reshape(h, 1)]
    in_specs = [pl.BlockSpec((1, h, tp), lambda b: (b, 0, 0)),
                pl.BlockSpec((h, 1), lambda b: (0, 0))]
    if n_past:
        args.append(past_t)
        in_specs.append(pl.BlockSpec((1, h, p), lambda b: (b, 0, 0)))
    args.append(tri)
    in_specs.append(pl.BlockSpec((LANES, LANES), lambda b: (0, 0)))
    return pl.pallas_call(
        kern,
        grid=(bsz,),
        in_specs=in_specs,
        out_specs=[pl.BlockSpec((1, h, tp), lambda b: (b, 0, 0)),
                   pl.BlockSpec((1, 3, h, p + tp), lambda b: (b, 0, 0, 0))],
        out_shape=[jax.ShapeDtypeStruct((bsz, h, tp), F32),
                   jax.ShapeDtypeStruct((bsz, 3, h, p + tp), BF16)],
        compiler_params=_cparams("parallel"),
        name="fox_decay",
    )(*args)


def _fox_attn_kernel(q_ref, k_ref, v_ref, pz_ref, g_ref, o_ref, acc_ref, m_ref, s0_ref, ka_ref, vt_ref,
                     *, tq, tqp, q_off, nt):
    j = pl.program_id(1)
    a = pl.program_id(2)
    tk = KEY_TILE
    q0 = q_off + a * tq
    n_full = q0 // tk
    n_need = (q0 + tq - 1) // tk + 1

    @pl.when(a == 0)
    def _():
        klane = lax.broadcasted_iota(I32, (tk, LANES), 1)
        for c in range(nt):
            rows = slice(c * tk, (c + 1) * tk)
            kp = k_ref[0, rows, :]
            pz = pz_ref[0, rows, :].astype(F32)
            ka_ref[0, rows, :] = jnp.where(klane < HEAD_DIM, kp, pz).astype(BF16)
            ka_ref[1, rows, :] = jnp.where(klane >= HEAD_DIM, kp, pz).astype(BF16)
            vt_ref[c] = _vt_rows(v_ref[0, rows, :])

    lane = lax.broadcasted_iota(I32, (tqp, LANES), 1)
    qn = _pad_q_rows(q_ref[0], tq, tqp) * (HEAD_DIM ** -0.5 * LOG2E)
    ones_e = (lane >= HEAD_DIM + 3 * j) & (lane < HEAD_DIM + 3 * j + 3)
    ones_o = (lane >= 3 * j) & (lane < 3 * j + 3)
    qs = (jnp.where(lane < HEAD_DIM, qn, jnp.where(ones_e, 1.0, 0.0)).astype(BF16),
          jnp.where(lane >= HEAD_DIM, qn, jnp.where(ones_o, 1.0, 0.0)).astype(BF16))
    m_ref[...] = jnp.full(m_ref.shape, M_FLOOR, F32)
    acc_ref[...] = jnp.zeros(acc_ref.shape, F32)
    krow = lax.broadcasted_iota(I32, (tk, tqp), 0)
    qcol = lax.broadcasted_iota(I32, (tk, tqp), 1)

    def s_tile(e, c):
        return _nt(ka_ref[e, pl.ds(pl.multiple_of(c * tk, tk), tk), :], qs[e])

    def softmax_pv(e, c, s, masked):
        if masked:
            s = jnp.where(c * tk + krow <= q0 + qcol, s, NEG_BIG)
        m_old = m_ref[e]
        m_new = jnp.maximum(m_old, jnp.max(s, axis=0, keepdims=True))
        p = jnp.exp2(s - m_new).astype(BF16)
        acc_ref[e] = jnp.exp2(m_old - m_new) * acc_ref[e] + _dot(vt_ref[c], p)
        m_ref[e] = m_new

    def step(c, masked):
        s1 = s_tile(1, c)
        softmax_pv(0, c, s0_ref[...], masked)
        s0_ref[...] = s_tile(0, jnp.minimum(c + 1, n_need - 1))
        softmax_pv(1, c, s1, masked)

    def full_body(c, carry):
        step(c, False)
        return carry

    def masked_body(c, carry):
        step(c, True)
        return carry

    def pair_body(i, carry):
        step(2 * i, False)
        step(2 * i + 1, False)
        return carry

    s0_ref[...] = s_tile(0, 0)
    lax.fori_loop(0, n_full // 2, pair_body, 0)
    lax.fori_loop((n_full // 2) * 2, n_full, full_body, 0)
    lax.fori_loop(n_full, n_need, masked_body, 0)
    o_e = (acc_ref[0, 0:LANES, :] * (1.0 / acc_ref[0, LANES:LANES + 1, :])).T
    o_o = (acc_ref[1, 0:LANES, :] * (1.0 / acc_ref[1, LANES:LANES + 1, :])).T
    o = jnp.where(lane < HEAD_DIM, o_e, o_o)
    o_ref[0] = o[:tq] * (1.0 / (1.0 + jnp.exp(-g_ref[0])))


def fox_attention(proj, keys, values, pz, *, lp, t, q_off, g_blk0, k_blk0, v_blk0):
    bsz = proj.shape[0]
    tq = min(t, KEY_TILE)
    tqp = max(tq, LANES)
    nt = lp // KEY_TILE
    kern = functools.partial(_fox_attn_kernel, tq=tq, tqp=tqp, q_off=q_off, nt=nt)
    return pl.pallas_call(
        kern,
        grid=(bsz, B_HEADS // 2, t // tq),
        in_specs=[pl.BlockSpec((1, tq, LANES), lambda b, j, a: (b, a, j)),
                  pl.BlockSpec((1, lp, LANES), lambda b, j, a: (b, 0, k_blk0 + j)),
                  pl.BlockSpec((1, lp, LANES), lambda b, j, a: (b, 0, v_blk0 + j)),
                  pl.BlockSpec((1, lp, LANES), lambda b, j, a: (b, 0, 0)),
                  pl.BlockSpec((1, tq, LANES), lambda b, j, a: (b, a, g_blk0 + j))],
        out_specs=pl.BlockSpec((1, tq, LANES), lambda b, j, a: (b, a, j)),
        out_shape=jax.ShapeDtypeStruct((bsz, t, B_HEADS * HEAD_DIM), F32),
        scratch_shapes=[pltpu.VMEM((2, VT_ROWS, tqp), F32), pltpu.VMEM((2, 1, tqp), F32),
                        pltpu.VMEM((KEY_TILE, tqp), F32),
                        pltpu.VMEM((2, lp, LANES), BF16),
                        pltpu.VMEM((nt, VT_ROWS, KEY_TILE), BF16)],
        compiler_params=_cparams("parallel", "parallel", "arbitrary"),
        name="fox_attn",
    )(proj, keys, values, pz, proj)


def _hgrn2_levels(tc):
    lv = []
    n = 8
    while n < tc:
        lv.append(n)
        n *= 2
    return lv


def _hgrn2_masks(tc):
    t = jnp.arange(tc)[:, None]
    s = jnp.arange(tc)[None, :]
    ms = [((t // (2 * n) == s // (2 * n)) & ((t // n) % 2 == 1) & ((s // n) % 2 == 0)) for n in _hgrn2_levels(tc)]
    ms.append((t // 8 == s // 8) & (s <= t))
    return jnp.stack(ms).astype(F32)


def _hgrn2_head(q, z, v, g, lb, og, st, tri_ref, msk_ref, tc):
    ez = jnp.exp(-jnp.abs(z))
    den = 1.0 / (1.0 + ez)
    pos = z >= 0.0
    f = lb + (1.0 - lb) * (jnp.where(pos, 1.0, ez) * den)
    kk = (1.0 - lb) * (jnp.where(pos, ez, 1.0) * den)
    cum = _dot_01x(tri_ref[...], jnp.log(f))

    def rows(idx):
        parts = []
        for i in idx:
            parts.append(jnp.zeros((8, LANES), F32) if i < 0 else jnp.broadcast_to(cum[i:i + 1, :], (8, LANES)))
        return jnp.concatenate(parts, axis=0)

    levels = _hgrn2_levels(tc)
    ngrp = tc // 8
    scores = jnp.zeros((tc, tc), F32)
    ql8 = None
    for li, n in enumerate(levels):
        start = [((r * 8) // n) * n for r in range(ngrp)]
        a_start = rows([s - 1 for s in start])
        a_end = rows([s + n - 1 for s in start])
        ql = (q * jnp.exp(cum - a_start)).astype(BF16)
        kr = (kk * jnp.exp(a_end - cum)).astype(BF16)
        scores = scores + msk_ref[li] * _nt(ql, kr)
        if n == 8:
            ql8 = ql
            kb = (kk * jnp.exp(a_start - cum)).astype(BF16)
    if ql8 is None:
        a_start = rows([r * 8 - 1 for r in range(ngrp)])
        ql8 = (q * jnp.exp(cum - a_start)).astype(BF16)
        kb = (kk * jnp.exp(a_start - cum)).astype(BF16)
    scores = scores + msk_ref[len(levels)] * _nt(ql8, kb)

    o = _nt((q * jnp.exp(cum)).astype(BF16), st.astype(BF16)) + _dot(scores.astype(BF16), v.astype(BF16))
    a_last = cum[tc - 1:tc, :]
    khat = (kk * jnp.exp(a_last - cum)).astype(BF16)
    st_new = st * jnp.exp(a_last) + _dot(v.T.astype(BF16), khat)
    on = o * lax.rsqrt(jnp.mean(o * o, axis=-1, keepdims=True) + EPS) * og
    return on * (g * (1.0 / (1.0 + jnp.exp(-g)))), st_new


def _hgrn2_kernel(q_ref, fz_ref, v_ref, g_ref, lb_ref, og_ref, s0_ref, tri_ref, msk_ref, y_ref, so_ref, st_ref, *, tc):
    ct = pl.program_id(2)

    @pl.when(ct == 0)
    def _():
        st_ref[...] = s0_ref[0]

    for e in range(HGRN2_HEADS_PER_STEP):
        sl = slice(e * LANES, (e + 1) * LANES)
        y, st_new = _hgrn2_head(q_ref[0, :, sl], fz_ref[0, :, sl], v_ref[0, :, sl], g_ref[0, :, sl], lb_ref[e],
                                og_ref[...], st_ref[e], tri_ref, msk_ref, tc)
        y_ref[0, :, sl] = y
        st_ref[e] = st_new

    @pl.when(ct == pl.num_programs(2) - 1)
    def _():
        so_ref[0] = st_ref[...]


def hgrn2_recurrence(proj, lb, out_gain, s0_t, *, t):
    bsz = proj.shape[0]
    tc = min(t, 128)
    nlv = len(_hgrn2_levels(tc)) + 1
    tri = (jnp.arange(tc)[:, None] >= jnp.arange(tc)[None, :]).astype(BF16)
    h = C_HEADS
    hps = HGRN2_HEADS_PER_STEP
    ng = h // hps
    kern = functools.partial(_hgrn2_kernel, tc=tc)
    blk = lambda off: pl.BlockSpec((1, tc, hps * LANES), lambda b, hh, c: (b, c, off + hh))
    return pl.pallas_call(
        kern,
        grid=(bsz, ng, t // tc),
        in_specs=[blk(0), blk(ng), blk(2 * ng), blk(3 * ng),
                  pl.BlockSpec((hps, 1, C_DK), lambda b, hh, c: (hh, 0, 0)),
                  pl.BlockSpec((1, C_DV), lambda b, hh, c: (0, 0)),
                  pl.BlockSpec((1, hps, C_DV, C_DK), lambda b, hh, c: (b, hh, 0, 0)),
                  pl.BlockSpec((tc, tc), lambda b, hh, c: (0, 0)),
                  pl.BlockSpec((nlv, tc, tc), lambda b, hh, c: (0, 0, 0))],
        out_specs=[pl.BlockSpec((1, tc, hps * LANES), lambda b, hh, c: (b, c, hh)),
                   pl.BlockSpec((1, hps, C_DV, C_DK), lambda b, hh, c: (b, hh, 0, 0))],
        out_shape=[jax.ShapeDtypeStruct((bsz, t, h * C_DV), F32),
                   jax.ShapeDtypeStruct((bsz, h, C_DV, C_DK), F32)],
        scratch_shapes=[pltpu.VMEM((hps, C_DV, C_DK), F32)],
        compiler_params=_cparams("parallel", "parallel", "arbitrary"),
        name="hgrn2",
    )(proj, proj, proj, proj, lb.reshape(h, 1, C_DK), out_gain.reshape(1, C_DV), s0_t, tri, _hgrn2_masks(tc))


def _route_t(h, wr_t, br_t):
    ah, am, al = _split3(wr_t)
    bh, bm, bl = _split3(h)
    r = _nt(ah, bh) + (_nt(ah, bm) + _nt(am, bh)) + (_nt(ah, bl) + _nt(am, bm) + _nt(al, bh)) + br_t
    row = lax.broadcasted_iota(I32, r.shape, 0)
    rowf = row.astype(F32)
    big = float(ROUTER_ROWS)
    is_g = row < N_GROUPS
    lg = jnp.where(is_g, r, -jnp.inf)
    mg = jnp.max(lg, axis=0, keepdims=True)
    grp = jnp.min(jnp.where(lg == mg, rowf, big), axis=0, keepdims=True)
    p_grp = 1.0 / jnp.sum(jnp.where(is_g, jnp.exp(r - mg), 0.0), axis=0, keepdims=True)
    eg = ((row - N_GROUPS) >> 2).astype(F32)
    in_e = (row >= N_GROUPS) & (row < N_GROUPS + N_EXPERTS) & (eg == grp)
    le = jnp.where(in_e, r, -jnp.inf)
    v1 = jnp.max(le, axis=0, keepdims=True)
    i1 = jnp.min(jnp.where(le == v1, rowf, big), axis=0, keepdims=True)
    le2 = jnp.where(rowf == i1, -jnp.inf, le)
    v2 = jnp.max(le2, axis=0, keepdims=True)
    i2 = jnp.min(jnp.where(le2 == v2, rowf, big), axis=0, keepdims=True)
    e2 = jnp.exp(v2 - v1)
    w1 = 1.0 / (1.0 + e2)
    gates = jnp.where(rowf == i1, w1 * p_grp, 0.0) + jnp.where(rowf == i2, (e2 * w1) * p_grp, 0.0)
    member = jnp.where((rowf == i1) | (rowf == i2), 1.0, 0.0)
    return gates, member


def _moe_kernel(x_ref, gain_ref, sc_ref, sh_ref, g2_ref, wr_ref, br_ref, tri_ref, wg_ref, wu_ref, wd_ref, o_ref,
                hb_ref, rank_ref, rt_ref, gt_ref, y_ref, acc_ref, *, ts, slot):
    e = pl.program_id(2)
    tm = x_ref.shape[1]
    n_sub = tm // ts
    tsp = ts
    n_chunk = tsp // slot
    row = e + N_GROUPS

    @pl.when(e == 0)
    def _():
        for s in range(n_sub):
            rows = slice(s * ts, (s + 1) * ts)
            h = _ln_mod(x_ref[0, rows, :], gain_ref[...], _mod_rows(sc_ref, rows), _mod_rows(sh_ref, rows))
            hb_ref[rows, :] = h.astype(BF16)
            gates_t, member_t = _route_t(h, wr_ref[...], br_ref[...])
            r_t = _dot(member_t.astype(BF16), tri_ref[...])
            r_t = jnp.where(member_t > 0.0, r_t, -1.0)
            rt_ref[s] = r_t
            gt_ref[s] = gates_t
            rank_ref[s] = jnp.concatenate([r_t, jnp.full((LANES - ROUTER_ROWS, ts), -1.0, F32)], axis=0).T
        if n_chunk > 1:
            acc_ref[...] = jnp.zeros_like(acc_ref)

    def expert_rows(k):
        xs, ges, sels = [], [], []
        slot_i = lax.broadcasted_iota(I32, (slot, tsp), 0).astype(F32) + float(k * slot)
        for s in range(n_sub):
            sel = jnp.where(rt_ref[s, pl.ds(row, 1), :] == slot_i, 1.0, 0.0)
            ges.append(jnp.sum(sel * gt_ref[s, pl.ds(row, 1), :], axis=-1, keepdims=True))
            xs.append(_dot(sel.astype(BF16), hb_ref[s * tsp:(s + 1) * tsp, :]).astype(BF16))
            sels.append(sel)
        xa = jnp.concatenate(xs, axis=0)
        a = _dot(xa, wg_ref[0])
        u = _dot(xa, wu_ref[0])
        he = (a * (1.0 / (1.0 + jnp.exp(-a)))) * u * jnp.concatenate(ges, axis=0)
        return _dot(he.astype(BF16), wd_ref[0]), sels

    ye, _ = expert_rows(0)
    for s in range(n_sub):
        y_ref[s, pl.ds(pl.multiple_of(e * slot, slot), slot), :] = ye[s * slot:(s + 1) * slot].astype(BF16)

    for k in range(1, n_chunk):
        last_rank = jnp.max(rt_ref[0, pl.ds(row, 1), :])
        for s in range(1, n_sub):
            last_rank = jnp.maximum(last_rank, jnp.max(rt_ref[s, pl.ds(row, 1), :]))

        @pl.when(last_rank >= float(k * slot))
        def _():
            ye_k, sels = expert_rows(k)
            for s in range(n_sub):
                acc_ref[s * tsp:(s + 1) * tsp, :] += _dot(sels[s].T.astype(BF16),
                                                          ye_k[s * slot:(s + 1) * slot].astype(BF16))

    @pl.when(e == pl.num_programs(2) - 1)
    def _():
        slot_l = lax.broadcasted_iota(I32, (1, slot), 1).astype(F32)
        for s in range(n_sub):
            rows = slice(s * ts, (s + 1) * ts)
            rank = rank_ref[s]
            sel_t = jnp.concatenate(
                [jnp.where(rank[:, N_GROUPS + j:N_GROUPS + j + 1] == slot_l, 1.0, 0.0).astype(BF16)
                 for j in range(N_EXPERTS)], axis=1)
            y = _dot(sel_t, y_ref[s])
            if n_chunk > 1:
                y = y + acc_ref[s * tsp:(s + 1) * tsp, :]
            o_ref[0, rows, :] = x_ref[0, rows, :] + _mod_rows(g2_ref, rows) * y


def moe_layer(x, gain, sc, sh, g2, w_router, b_router, w_gate, w_up, w_down):
    bsz, t, d = x.shape
    if t % LANES:
        y = moe_layer(_merge(x), gain, _per_token(sc, t), _per_token(sh, t), _per_token(g2, t),
                      w_router, b_router, w_gate, w_up, w_down)
        return y.reshape(x.shape)
    tm = min(t, 1024 if sc.shape[1] == 1 else 512)
    ts = min(tm, 512)
    assert ts % LANES == 0 and t % tm == 0, (t, tm, ts)
    tsp = ts
    slot = MOE_SLOT
    ne, _, de = w_gate.shape
    tri = (jnp.arange(tsp)[:, None] < jnp.arange(tsp)[None, :]).astype(BF16)
    kern = functools.partial(_moe_kernel, ts=ts, slot=slot)
    n_sub = tm // ts
    return pl.pallas_call(
        kern,
        grid=(bsz, t // tm, ne),
        in_specs=[pl.BlockSpec((1, tm, d), lambda b, i, e: (b, i, 0)),
                  pl.BlockSpec((1, d), lambda b, i, e: (0, 0)),
                  _mod_spec(sc, tm, 3),
                  _mod_spec(sh, tm, 3),
                  _mod_spec(g2, tm, 3),
                  pl.BlockSpec((ROUTER_ROWS, d), lambda b, i, e: (0, 0)),
                  pl.BlockSpec((ROUTER_ROWS, 1), lambda b, i, e: (0, 0)),
                  pl.BlockSpec((tsp, tsp), lambda b, i, e: (0, 0)),
                  pl.BlockSpec((1, d, de), lambda b, i, e: (e, 0, 0)),
                  pl.BlockSpec((1, d, de), lambda b, i, e: (e, 0, 0)),
                  pl.BlockSpec((1, de, d), lambda b, i, e: (e, 0, 0))],
        out_specs=pl.BlockSpec((1, tm, d), lambda b, i, e: (b, i, 0)),
        out_shape=jax.ShapeDtypeStruct((bsz, t, d), F32),
        scratch_shapes=[pltpu.VMEM((n_sub * tsp, d), BF16),
                        pltpu.VMEM((n_sub, tsp, LANES), F32),
                        pltpu.VMEM((n_sub, ROUTER_ROWS, tsp), F32),
                        pltpu.VMEM((n_sub, ROUTER_ROWS, tsp), F32),
                        pltpu.VMEM((n_sub, ne * slot, d), BF16),
                        pltpu.VMEM((n_sub * tsp if tsp > slot else 8, d), F32)],
        compiler_params=_cparams("parallel", "parallel", "arbitrary"),
        name="moe",
    )(x, gain.reshape(1, d), sc, sh, g2, w_router, b_router, tri, w_gate, w_up, w_down)


def _pad_cols(w, n):
    return jnp.pad(w, ((0, 0), (0, n - w.shape[1])))


def _pad_rows(a, n):
    return jnp.pad(a, ((0, 0), (0, n - a.shape[1])) + ((0, 0),) * (a.ndim - 2))


def _round_up(n, m):
    return -(-n // m) * m


def _head_cols(gain_q, nq, gain_k, nk, npad):
    cg = jnp.concatenate([jnp.tile(gain_q, nq), jnp.tile(gain_k, nk)])
    n = cg.shape[0]
    cgain = jnp.pad(cg, (0, npad - n)).reshape(1, npad)
    cflag = (jnp.arange(npad) < n).astype(F32).reshape(1, npad)
    return cgain, cflag


def _dsa_layer(x, mod, past, prm, bdiag, bsub):
    sh1, sc1, g1 = mod
    past_k, past_v, past_ki = past
    bsz, t, d = x.shape
    p = 0 if past_k is None else past_k.shape[1]
    n_keys = p + t
    topk = min(TOPK_MAX, n_keys // 4)
    tn = 768
    n_in = prm['w_in'].shape[1]
    npad = _round_up(n_in, tn)
    nqk = (A_HEADS + A_KV_HEADS) * HEAD_DIM
    w = _pad_cols(prm['w_in'], npad).astype(BF16)
    cgain, cflag = _head_cols(prm['q_norm'], A_HEADS, prm['k_norm'], A_KV_HEADS, npad)
    proj = ln_proj(x, prm['norm'], sc1, sh1, w, cgain, cflag, -(-nqk // tn), tn)
    o_k = A_HEADS * HEAD_DIM
    o_v = o_k + A_KV_HEADS * HEAD_DIM
    o_qi = o_v + A_KV_HEADS * HEAD_DIM
    o_ki = o_qi + IDX_HEADS * IDX_DIM
    k_new = proj[..., o_k:o_v].reshape(bsz, t, A_KV_HEADS, HEAD_DIM)
    v_new = proj[..., o_v:o_qi].reshape(bsz, t, A_KV_HEADS, HEAD_DIM)
    ki_new = proj[..., o_ki:o_ki + IDX_DIM]
    lp = _round_up(n_keys, KEY_TILE)
    kvw = A_KV_HEADS * HEAD_DIM
    if p:
        keys = _pad_rows(jnp.concatenate([past_k.reshape(bsz, p, kvw), proj[..., o_k:o_v]], axis=1), lp)
        values = _pad_rows(jnp.concatenate([past_v.reshape(bsz, p, kvw), proj[..., o_v:o_qi]], axis=1), lp)
        ikeys = _pad_rows(jnp.concatenate([past_ki, ki_new], axis=1), lp)
        ikeys = jnp.pad(ikeys, ((0, 0), (0, 0), (0, LANES - IDX_DIM)))
        k_blk0, v_blk0, ki_blk = 0, 0, 0
    else:
        keys = values = ikeys = proj
        k_blk0, v_blk0, ki_blk = o_k // LANES, o_v // LANES, o_ki // LANES
    mask = dsa_index_mask(proj, ikeys, lp=lp, t=t, q_off=p, topk=topk, qi_blk=o_qi // (4 * LANES),
                          w_blk=o_ki // LANES, ki_blk=ki_blk)
    o = dsa_attention(proj, keys, values, mask, bdiag, bsub, lp=lp, t=t, q_off=p, k_blk0=k_blk0, v_blk0=v_blk0)
    x = out_proj_residual(o, prm['w_out'].astype(BF16), x, g1)
    return x, (k_new, v_new, ki_new)


def _fox_layer(x, mod, past, prm):
    sh1, sc1, g1 = mod
    past_k, past_v, past_lf = past
    bsz, t, d = x.shape
    p = 0 if past_k is None else past_k.shape[1]
    n_keys = p + t
    hd = B_HEADS * HEAD_DIM
    tn = 512
    npad = _round_up(prm['w_in'].shape[1], tn)
    w = _pad_cols(prm['w_in'], npad).astype(BF16)
    cgain, cflag = _head_cols(prm['q_norm'], B_HEADS, prm['k_norm'], B_HEADS, npad)
    proj = ln_proj(x, prm['norm'], sc1, sh1, w, cgain, cflag, 2 * hd // tn, tn)
    k_new = proj[..., hd:2 * hd]
    v_new = proj[..., 2 * hd:3 * hd]
    fz = proj[..., 4 * hd:4 * hd + B_HEADS]
    tp = _round_up(t, LANES)
    fz_t = _pad_rows(fz, tp).transpose(0, 2, 1)
    past_t = None if not p else past_lf.transpose(0, 2, 1)
    lf_t, nck = fox_decay(fz_t, prm['forget_bias'], past_t, t=t)
    logf_new = lf_t[:, :, :t].transpose(0, 2, 1)
    lp = _round_up(n_keys, KEY_TILE)
    if p:
        keys = _pad_rows(jnp.concatenate([past_k.reshape(bsz, p, hd), k_new], axis=1), lp)
        values = _pad_rows(jnp.concatenate([past_v.reshape(bsz, p, hd), v_new], axis=1), lp)
        k_blk0, v_blk0 = 0, 0
    else:
        keys = values = proj
        k_blk0, v_blk0 = hd // LANES, 2 * hd // LANES
    pieces = jnp.pad(nck, ((0, 0), (0, 0), (0, 0), (0, lp - nck.shape[3]))).transpose(0, 3, 2, 1)
    zl = jnp.zeros((bsz, lp, HEAD_DIM - 3 * B_HEADS // 2), pieces.dtype)
    pz = jnp.concatenate([pieces[:, :, 1::2].reshape(bsz, lp, -1), zl,
                          pieces[:, :, 0::2].reshape(bsz, lp, -1), zl], axis=-1)
    o = fox_attention(proj, keys, values, pz, lp=lp, t=t, q_off=p, g_blk0=3 * hd // LANES,
                      k_blk0=k_blk0, v_blk0=v_blk0)
    x = out_proj_residual(o, prm['w_out'].astype(BF16), x, g1)
    return x, (k_new.reshape(bsz, t, B_HEADS, HEAD_DIM), v_new.reshape(bsz, t, B_HEADS, HEAD_DIM), logf_new)


def _hgrn2_layer(x, mod, s0, prm):
    sh1, sc1, g1 = mod
    bsz, t, d = x.shape
    npad = prm['w_in'].shape[1]
    zeros = jnp.zeros((1, npad), F32)
    proj = ln_proj(x, prm['norm'], sc1, sh1, prm['w_in'].astype(BF16), zeros, zeros, 0, 512)
    y, s_t = hgrn2_recurrence(proj, prm['lb'], prm['out_norm'], jnp.swapaxes(s0, -1, -2), t=t)
    x = out_proj_residual(y, prm['w_out'].astype(BF16), x, g1)
    return x, jnp.swapaxes(s_t, -1, -2)


def _trunk(x, c, a_k, a_v, a_kidx, b_k, b_v, b_logf, c_state, prm):
    bsz, t, d = x.shape
    mod_all = ada_mod(c, prm['w_ada'], prm['b_ada'])
    lb_all = jnp.cumsum(jax.nn.softmax(prm['c_lower_bound'].astype(F32), axis=0), axis=0)
    lb_all = lb_all - lb_all[0]
    bdiag, bsub = dsa_bias_tiles(prm['rel_table'])
    out_a, out_b, out_c = [], [], []
    for i in range(DEPTH):
        j = i // N_MIXERS
        kind = i % N_MIXERS
        sh1, sc1, g1, sh2, sc2, g2 = [m.reshape(bsz, 1, d) for m in jnp.split(mod_all[i], 6, axis=-1)]
        mod = (sh1, sc1, g1)
        if kind == 0:
            past = (None, None, None) if a_k is None else (a_k[j], a_v[j], a_kidx[j])
            lp = dict(norm=prm['norm_mix'][i], w_in=prm['a_w_in'][j], q_norm=prm['a_q_norm'][j],
                      k_norm=prm['a_k_norm'][j], w_out=prm['a_w_out'][j])
            x, new = _dsa_layer(x, mod, past, lp, bdiag, bsub)
            out_a.append(new)
        elif kind == 1:
            past = (None, None, None) if b_k is None else (b_k[j], b_v[j], b_logf[j])
            lp = dict(norm=prm['norm_mix'][i], w_in=prm['b_w_in'][j], forget_bias=prm['b_forget_bias'][j],
                      q_norm=prm['b_q_norm'][j], k_norm=prm['b_k_norm'][j], w_out=prm['b_w_out'][j])
            x, new = _fox_layer(x, mod, past, lp)
            out_b.append(new)
        else:
            s0 = jnp.zeros((bsz, C_HEADS, C_DK, C_DV), F32) if c_state is None else c_state[j]
            lp = dict(norm=prm['norm_mix'][i], w_in=prm['c_w_in'][j], lb=lb_all[i],
                      out_norm=prm['c_out_norm'][j], w_out=prm['c_w_out'][j])
            x, new = _hgrn2_layer(x, mod, s0, lp)
            out_c.append(new)
        w_router = jnp.pad(jnp.concatenate([prm['moe_w_group'][i], prm['moe_w_expert'][i]], axis=1).T,
                           ((0, ROUTER_ROWS - N_GROUPS - N_EXPERTS), (0, 0)))
        b_router = jnp.pad(jnp.concatenate([prm['moe_b_group'][i], prm['moe_b_expert'][i]]),
                           (0, ROUTER_ROWS - N_GROUPS - N_EXPERTS)).reshape(ROUTER_ROWS, 1)
        x = moe_layer(x, prm['norm_ffn'][i], sc2, sh2, g2, w_router, b_router,
                      prm['moe_w_gate'][i].astype(BF16), prm['moe_w_up'][i].astype(BF16),
                      prm['moe_w_down'][i].astype(BF16))
    stack = lambda outs, k: jnp.stack([o[k] for o in outs])
    return (x, stack(out_a, 0), stack(out_a, 1), stack(out_a, 2),
            stack(out_b, 0), stack(out_b, 1), stack(out_b, 2), jnp.stack(out_c))


def kernel(x_prompt, x_sample, cache_a_k, cache_a_v, cache_a_kidx, cache_b_k, cache_b_v, cache_b_logf, state_c,
           c_prompt, c_sample, rel_table, w_ada, b_ada, norm_mix, norm_ffn, a_w_in, a_q_norm, a_k_norm, a_w_out,
           b_w_in, b_forget_bias, b_q_norm, b_k_norm, b_w_out, c_w_in, c_lower_bound, c_out_norm, c_w_out,
           moe_w_group, moe_b_group, moe_w_expert, moe_b_expert, moe_w_gate, moe_w_up, moe_w_down):
    prm = {'rel_table': rel_table, 'w_ada': w_ada, 'b_ada': b_ada, 'norm_mix': norm_mix, 'norm_ffn': norm_ffn,
           'a_w_in': a_w_in, 'a_q_norm': a_q_norm, 'a_k_norm': a_k_norm, 'a_w_out': a_w_out,
           'b_w_in': b_w_in, 'b_forget_bias': b_forget_bias, 'b_q_norm': b_q_norm, 'b_k_norm': b_k_norm,
           'b_w_out': b_w_out, 'c_w_in': c_w_in, 'c_lower_bound': c_lower_bound, 'c_out_norm': c_out_norm,
           'c_w_out': c_w_out, 'moe_w_group': moe_w_group, 'moe_b_group': moe_b_group,
           'moe_w_expert': moe_w_expert, 'moe_b_expert': moe_b_expert, 'moe_w_gate': moe_w_gate,
           'moe_w_up': moe_w_up, 'moe_w_down': moe_w_down}
    (y_p, ak_p, av_p, ai_p, bk_p, bv_p, bl_p, cs_p) = _trunk(
        x_prompt, c_prompt, None, None, None, None, None, None, None, prm)
    (y_s, ak_s, av_s, ai_s, bk_s, bv_s, bl_s, cs_s) = _trunk(
        x_sample, c_sample, cache_a_k, cache_a_v, cache_a_kidx, cache_b_k, cache_b_v, cache_b_logf, state_c, prm)
    return (y_p, y_s, ak_p, av_p, ai_p, ak_s, av_s, ai_s, bk_p, bv_p, bl_p, bk_s, bv_s, bl_s, cs_p, cs_s)
```

```python
import functools

import jax
import jax.numpy as jnp
from jax import lax
from jax.experimental import pallas as pl
from jax.experimental.pallas import tpu as pltpu

F32 = jnp.float32
BF16 = jnp.bfloat16
I32 = jnp.int32

LANES = 128
VMEM_LIMIT_BYTES = 56 * 1024 * 1024

DEPTH = 4
N_MIXERS = 3
CHUNK = 64
EPS = 1e-6
HEAD_DIM = 64
A_HEADS = 16
A_KV_HEADS = 4
A_GROUP = A_HEADS // A_KV_HEADS
IDX_HEADS = 8
IDX_DIM = 64
TOPK_MAX = 256
REL_BUCKETS = 32
B_HEADS = 16
C_HEADS = 8
C_DK = 128
C_DV = 128
N_GROUPS = 4
EXPERTS_PER_GROUP = 4
N_EXPERTS = 16
D_EXPERT = 512

LOG2E = 1.4426950408889634
NEG_BIG = -1e30
M_FLOOR = -1e20
INT_MIN = -2 ** 31
KEY_TILE = 512
SLABS = KEY_TILE // LANES
MOE_SLOT = 128
VT_ROWS = LANES + 16
ROUTER_ROWS = 32
HGRN2_HEADS_PER_STEP = 4
SEG_ROWS = 16


def _cparams(*sem):
    return pltpu.CompilerParams(dimension_semantics=sem, vmem_limit_bytes=VMEM_LIMIT_BYTES)


def _nt(a, b):
    return lax.dot_general(a, b, (((1,), (1,)), ((), ())), preferred_element_type=F32)


def _split3(x):
    hi = x.astype(BF16)
    r = x - hi.astype(F32)
    mid = r.astype(BF16)
    lo = (r - mid.astype(F32)).astype(BF16)
    return hi, mid, lo


def _dot(a, b):
    return jnp.dot(a, b, preferred_element_type=F32)


def _dot_x01(x, m01):
    hi, mid, lo = _split3(x)
    return _dot(hi, m01) + _dot(mid, m01) + _dot(lo, m01)


def _dot_01x(m01, x):
    hi, mid, lo = _split3(x)
    return _dot(m01, hi) + _dot(m01, mid) + _dot(m01, lo)


def _dot_f32(a, b):
    ah, am, al = _split3(a)
    bh, bm, bl = _split3(b)
    return _dot(ah, bh) + (_dot(ah, bm) + _dot(am, bh)) + (_dot(ah, bl) + _dot(am, bm) + _dot(al, bh))


def _pad_q_rows(q, tq, tqp):
    if tqp == tq:
        return q
    return jnp.concatenate([q, jnp.zeros((tqp - tq, q.shape[1]), q.dtype)], axis=0)


def _mod_kernel(c_ref, w_ref, b_ref, o_ref):
    o_ref[0] = _dot(c_ref[...], w_ref[0]) + b_ref[0]


def ada_mod(c, w_ada, b_ada):
    nl, d, n6 = w_ada.shape
    bsz = c.shape[0]
    tn = 512
    return pl.pallas_call(
        _mod_kernel,
        grid=(nl, n6 // tn),
        in_specs=[pl.BlockSpec((bsz, d), lambda l, j: (0, 0)),
                  pl.BlockSpec((1, d, tn), lambda l, j: (l, 0, j)),
                  pl.BlockSpec((1, 1, tn), lambda l, j: (l, 0, j))],
        out_specs=pl.BlockSpec((1, bsz, tn), lambda l, j: (l, 0, j)),
        out_shape=jax.ShapeDtypeStruct((nl, bsz, n6), F32),
        compiler_params=_cparams("parallel", "parallel"),
        name="ada_mod",
    )(c, w_ada, b_ada.reshape(nl, 1, n6))


def _mod_spec(m, tm, grid_rank):
    d = m.shape[2]
    if m.shape[1] == 1:
        return pl.BlockSpec((1, 1, d), (lambda b, i, j: (b, 0, 0)) if grid_rank == 3 else (lambda b, i: (b, 0, 0)))
    return pl.BlockSpec((1, tm, d), (lambda b, i, j: (b, i, 0)) if grid_rank == 3 else (lambda b, i: (b, i, 0)))


def _mod_rows(ref, rows):
    return ref[0] if ref.shape[1] == 1 else ref[0, rows, :]


def _merge(x):
    return x.reshape(1, x.shape[0] * x.shape[1], x.shape[2])


def _per_token(m, t):
    bsz, _, d = m.shape
    return jnp.broadcast_to(m, (bsz, t, d)).reshape(1, bsz * t, d)


def _ln_mod(x, gain, sc, sh):
    ms = jnp.mean(x * x, axis=-1, keepdims=True)
    return (x * lax.rsqrt(ms + EPS) * gain) * (1.0 + sc) + sh


def _ln_proj_kernel(x_ref, gain_ref, sc_ref, sh_ref, w_ref, cgain_ref, cflag_ref, bd_ref, o_ref, h_ref,
                    *, n_norm_tiles, tn):
    j = pl.program_id(2)

    @pl.when(j == 0)
    def _():
        h_ref[...] = _ln_mod(x_ref[0], gain_ref[...], sc_ref[0], sh_ref[0]).astype(BF16)

    y = _dot(h_ref[...], w_ref[...])

    def plain():
        o_ref[0] = y

    def normed():
        y2 = y * y
        hi = y2.astype(BF16)
        lo = (y2 - hi.astype(F32)).astype(BF16)
        seg = bd_ref[...]
        tm = y.shape[0]
        ss_t = _nt(seg[0:SEG_ROWS], hi) + _nt(seg[0:SEG_ROWS], lo)
        r_t = lax.rsqrt(ss_t * (1.0 / HEAD_DIM) + EPS)
        r = jnp.concatenate([r_t, jnp.zeros((LANES - SEG_ROWS, tm), F32)], axis=0).T
        rh = r.astype(BF16)
        rl = (r - rh.astype(F32)).astype(BF16)
        yn = y * (_dot(rh, seg) + _dot(rl, seg)) * cgain_ref[...]
        o_ref[0] = jnp.where(cflag_ref[...] > 0.0, yn, y)

    if n_norm_tiles == 0:
        plain()
    else:
        pl.when(j < n_norm_tiles)(normed)
        pl.when(j >= n_norm_tiles)(plain)


def ln_proj(x, gain, sc, sh, w, cgain, cflag, n_norm_tiles, tn=256):
    bsz, t, d = x.shape
    if t % LANES:
        y = ln_proj(_merge(x), gain, _per_token(sc, t), _per_token(sh, t), w, cgain, cflag, n_norm_tiles, tn)
        return y.reshape(bsz, t, -1)
    npad = w.shape[1]
    tm = min(t, 1024)
    assert tn // HEAD_DIM <= SEG_ROWS
    bd = (jnp.arange(LANES)[:, None] == jnp.arange(tn)[None, :] // HEAD_DIM).astype(BF16)
    kern = functools.partial(_ln_proj_kernel, n_norm_tiles=n_norm_tiles, tn=tn)
    return pl.pallas_call(
        kern,
        grid=(bsz, t // tm, npad // tn),
        in_specs=[pl.BlockSpec((1, tm, d), lambda b, i, j: (b, i, 0)),
                  pl.BlockSpec((1, d), lambda b, i, j: (0, 0)),
                  _mod_spec(sc, tm, 3),
                  _mod_spec(sh, tm, 3),
                  pl.BlockSpec((d, tn), lambda b, i, j: (0, j)),
                  pl.BlockSpec((1, tn), lambda b, i, j: (0, j)),
                  pl.BlockSpec((1, tn), lambda b, i, j: (0, j)),
                  pl.BlockSpec((LANES, tn), lambda b, i, j: (0, 0))],
        out_specs=pl.BlockSpec((1, tm, tn), lambda b, i, j: (b, i, j)),
        out_shape=jax.ShapeDtypeStruct((bsz, t, npad), F32),
        scratch_shapes=[pltpu.VMEM((tm, d), BF16)],
        compiler_params=_cparams("parallel", "parallel", "arbitrary"),
        name="ln_proj",
    )(x, gain.reshape(1, d), sc, sh, w, cgain, cflag, bd)


def _out_proj_kernel(a_ref, w_ref, x_ref, g_ref, o_ref):
    y = _dot(a_ref[0].astype(BF16), w_ref[...])
    o_ref[0] = x_ref[0] + g_ref[0] * y


def out_proj_residual(a, w, x, gate):
    bsz, t, k = a.shape
    if t % LANES:
        return out_proj_residual(_merge(a), w, _merge(x), _per_token(gate, t)).reshape(x.shape)
    d = w.shape[1]
    tm = min(t, 512)
    return pl.pallas_call(
        _out_proj_kernel,
        grid=(bsz, t // tm),
        in_specs=[pl.BlockSpec((1, tm, k), lambda b, i: (b, i, 0)),
                  pl.BlockSpec((k, d), lambda b, i: (0, 0)),
                  pl.BlockSpec((1, tm, d), lambda b, i: (b, i, 0)),
                  _mod_spec(gate, tm, 2)],
        out_specs=pl.BlockSpec((1, tm, d), lambda b, i: (b, i, 0)),
        out_shape=jax.ShapeDtypeStruct((bsz, t, d), F32),
        compiler_params=_cparams("parallel", "parallel"),
        name="out_proj",
    )(a, w, x, gate)


def _dsa_index_kernel(qi_ref, w_ref, ki_ref, o_ref, key_ref, *, tq, tqp, n_slabs, q_off, topk, idx_bits):
    a = pl.program_id(1)
    tk = KEY_TILE
    q0 = q_off + a * tq
    n_kt = (q0 + tq + tk - 1) // tk
    lane = lax.broadcasted_iota(I32, (tqp, LANES), 1)
    half = lane < IDX_DIM
    krow = lax.broadcasted_iota(I32, (tk, tqp), 0)
    qcol = lax.broadcasted_iota(I32, (tk, tqp), 1)
    qchunk = (q0 + qcol) >> 6
    srow = lax.broadcasted_iota(I32, (LANES, tqp), 0)
    w_t = (_pad_q_rows(w_ref[0], tq, tqp) * (IDX_HEADS ** -0.5)).T
    qs = []
    for p in range(IDX_HEADS // 2):
        qp = _pad_q_rows(qi_ref[0, :, p * LANES:(p + 1) * LANES], tq, tqp) * (IDX_DIM ** -0.5)
        qs.append(jnp.where(half, qp, 0.0).astype(BF16))
        qs.append(pltpu.roll(jnp.where(half, 0.0, qp), IDX_DIM, axis=1).astype(BF16))

    def score_tile(c, carry):
        kt = ki_ref[0, pl.ds(pl.multiple_of(c * tk, tk), tk), :].astype(BF16)
        sc = jnp.zeros((tk, tqp), F32)
        for h in range(IDX_HEADS):
            sc = sc + w_t[IDX_DIM + h:IDX_DIM + h + 1, :] * jnp.maximum(_nt(kt, qs[h]), 0.0)
        bits = lax.bitcast_convert_type(sc, I32)
        key = jnp.where(bits < 0, bits ^ 0x7FFFFFFF, bits)
        key = jnp.where(sc == 0.0, 0, key)
        adm = ((c * tk + krow) >> 6) <= qchunk
        key = jnp.where(adm, key, INT_MIN)
        for s_ in range(SLABS):
            key_ref[c * SLABS + s_] = key[s_ * LANES:(s_ + 1) * LANES, :]
        return carry

    lax.fori_loop(0, n_kt, score_tile, 0)

    def count(pred):
        def body(c, acc):
            for s_ in range(SLABS):
                sidx = c * SLABS + s_
                ind = jnp.where(pred(key_ref[sidx], sidx), 1.0, 0.0)
                acc = acc + jnp.sum(ind.reshape(LANES // 8, 8, tqp), axis=0)
            return acc
        acc = lax.fori_loop(0, n_kt, body, jnp.zeros((8, tqp), F32))
        return jnp.sum(acc, axis=0, keepdims=True)

    kf = float(topk)
    n_adm = count(lambda k, s: k > INT_MIN)

    def all_done(cnt_t):
        done = (cnt_t == kf) | (n_adm < kf)
        return (jnp.min(jnp.where(done, 1.0, 0.0)) > 0.0).astype(I32)

    def bit_cond(st):
        return (st[0] < 32) & (st[3] == 0)

    def bit_body(st):
        i, t_u, cnt_t, _ = st
        cand_u = t_u | lax.shift_left(jnp.int32(1), 31 - i)
        cand_s = cand_u ^ INT_MIN
        cnt = count(lambda k, s: k >= cand_s)
        take = cnt >= kf
        cnt_t = jnp.where(take, cnt, cnt_t)
        return i + 1, jnp.where(take, cand_u, t_u), cnt_t, all_done(cnt_t)

    _, t_u, cnt_ge, _ = lax.while_loop(bit_cond, bit_body,
                                       (jnp.int32(0), jnp.zeros((1, tqp), I32), n_adm, all_done(n_adm)))
    thr = t_u ^ INT_MIN
    excess = jnp.where(cnt_ge > kf, 1.0, 0.0)

    def tie_search():
        need = kf - count(lambda k, s: k > thr)

        def j_body(i, j):
            cand = j | lax.shift_left(jnp.int32(1), idx_bits - 1 - i)
            c = count(lambda k, s: (k == thr) & (srow + s * LANES < cand))
            return jnp.where(c < need, cand, j)
        return lax.fori_loop(0, idx_bits, j_body, jnp.zeros((1, tqp), I32))

    has_ties = jnp.max(excess) > 0.0

    @pl.when(has_ties)
    def _():
        j_last = tie_search()

        def write_ties(c, carry):
            for s_ in range(SLABS):
                sidx = c * SLABS + s_
                k = key_ref[sidx]
                sel = (k > thr) | ((k == thr) & (srow + sidx * LANES <= j_last))
                sel = sel & (k > INT_MIN)
                o_ref[0, sidx] = jnp.where(sel, 0.0, NEG_BIG)
            return carry

        lax.fori_loop(0, n_kt, write_ties, 0)

    @pl.when(jnp.logical_not(has_ties))
    def _():
        thr_adm = jnp.maximum(thr, INT_MIN + 1)

        def write_plain(c, carry):
            for s_ in range(SLABS):
                sidx = c * SLABS + s_
                o_ref[0, sidx] = jnp.where(key_ref[sidx] >= thr_adm, 0.0, NEG_BIG)
            return carry

        lax.fori_loop(0, n_kt, write_plain, 0)

    neg = jnp.full((LANES, tqp), NEG_BIG, F32)

    def write_inactive(s, carry):
        o_ref[0, s] = neg
        return carry

    lax.fori_loop(n_kt * SLABS, n_slabs, write_inactive, 0)


def dsa_index_mask(proj, keys, *, lp, t, q_off, topk, qi_blk, w_blk, ki_blk):
    bsz = proj.shape[0]
    n_slabs = lp // LANES
    tq = min(t, KEY_TILE)
    tqp = max(tq, LANES)
    idx_bits = max(1, (lp - 1).bit_length())
    kern = functools.partial(_dsa_index_kernel, tq=tq, tqp=tqp, n_slabs=n_slabs, q_off=q_off, topk=topk,
                             idx_bits=idx_bits)
    return pl.pallas_call(
        kern,
        grid=(bsz, t // tq),
        in_specs=[pl.BlockSpec((1, tq, 4 * LANES), lambda b, a: (b, a, qi_blk)),
                  pl.BlockSpec((1, tq, LANES), lambda b, a: (b, a, w_blk)),
                  pl.BlockSpec((1, lp, LANES), lambda b, a: (b, 0, ki_blk))],
        out_specs=pl.BlockSpec((1, n_slabs, LANES, tqp), lambda b, a: (b, 0, 0, a)),
        out_shape=jax.ShapeDtypeStruct((bsz, n_slabs, LANES, (t // tq) * tqp), F32),
        scratch_shapes=[pltpu.VMEM((n_slabs, LANES, tqp), I32)],
        compiler_params=_cparams("parallel", "parallel"),
        name="dsa_index",
    )(proj, proj, keys)


def _bias_kernel(tab_ref, diag_ref, sub_ref):
    h = pl.program_id(0)
    far = tab_ref[REL_BUCKETS // 2 - 1, h]

    def bias(rel):
        n = jnp.abs(rel)
        large = jnp.full(rel.shape, 8, I32)
        for th in (12, 16, 23, 32, 46, 64, 91):
            large = large + jnp.where(n >= th, 1, 0)
        bucket = jnp.where(rel > 0, REL_BUCKETS // 2, 0) + jnp.where(n < 8, n, large)
        acc = jnp.zeros(rel.shape, F32)
        for bk in range(REL_BUCKETS):
            acc = jnp.where(bucket == bk, tab_ref[bk, h], acc)
        return (acc - far) * LOG2E

    ik = lax.broadcasted_iota(I32, (KEY_TILE, KEY_TILE), 0)
    iq = lax.broadcasted_iota(I32, (KEY_TILE, KEY_TILE), 1)
    diag_ref[0] = bias(ik - iq)
    ik = lax.broadcasted_iota(I32, (LANES, LANES), 0)
    iq = lax.broadcasted_iota(I32, (LANES, LANES), 1)
    sub_ref[0] = bias(ik - LANES - iq)


def dsa_bias_tiles(rel_table):
    return pl.pallas_call(
        _bias_kernel,
        grid=(A_HEADS,),
        in_specs=[pl.BlockSpec(memory_space=pltpu.SMEM)],
        out_specs=[pl.BlockSpec((1, KEY_TILE, KEY_TILE), lambda h: (h, 0, 0)),
                   pl.BlockSpec((1, LANES, LANES), lambda h: (h, 0, 0))],
        out_shape=[jax.ShapeDtypeStruct((A_HEADS, KEY_TILE, KEY_TILE), F32),
                   jax.ShapeDtypeStruct((A_HEADS, LANES, LANES), F32)],
        compiler_params=_cparams("parallel"),
        name="dsa_bias",
    )(rel_table)


def _vt_rows(v_tile):
    return jnp.concatenate([v_tile.T.astype(BF16), jnp.ones((VT_ROWS - LANES, v_tile.shape[0]), BF16)], axis=0)


def _dsa_attn_kernel(q_ref, k_ref, v_ref, msk_ref, bd_ref, bs_ref, o_ref, acc_ref, m_ref, s0_ref, vt_ref,
                     *, tq, tqp, q_off, nt):
    a = pl.program_id(1)
    g = pl.program_id(2)
    tk = KEY_TILE
    q0 = q_off + a * tq
    cd = q0 // tk
    cs = jnp.maximum(cd - 1, 0)
    par = g % 2
    lane = lax.broadcasted_iota(I32, (tqp, LANES), 1)
    half = lane < HEAD_DIM

    @pl.when(a == 0)
    def _():
        for c in range(nt):
            vt_ref[g, c] = _vt_rows(v_ref[0, c * tk:(c + 1) * tk, :])

    qs = []
    for e in range(A_GROUP):
        qc = _pad_q_rows(q_ref[0, :, (e // 2) * LANES:(e // 2 + 1) * LANES], tq, tqp) * (HEAD_DIM ** -0.5 * LOG2E)
        own = jnp.where(half, qc, 0.0) if e % 2 == 0 else jnp.where(half, 0.0, qc)
        qs.append(jnp.where(par == e % 2, own, pltpu.roll(own, HEAD_DIM, axis=1)).astype(BF16))
    m_ref[...] = jnp.full(m_ref.shape, M_FLOOR, F32)
    acc_ref[...] = jnp.zeros(acc_ref.shape, F32)

    def s_tile(e, c):
        return _nt(k_ref[0, pl.ds(pl.multiple_of(c * tk, tk), tk), :].astype(BF16), qs[e])

    def softmax_pv(e, c, s, after, kind):
        s = s + jnp.concatenate([msk_ref[0, c * SLABS + i] for i in range(SLABS)], axis=0)
        if kind == "diag":
            s = s + bd_ref[e, :, 0:tqp]
        elif kind == "sub":
            corner = bs_ref[e]
            if tqp > LANES:
                corner = jnp.concatenate([corner, jnp.zeros((LANES, tqp - LANES), F32)], axis=1)
            s = s + jnp.concatenate([jnp.zeros((tk - LANES, tqp), F32), corner], axis=0)
            s = jnp.where(cd > 0, s, NEG_BIG)
        m_old = jnp.minimum(m_ref[e], jnp.maximum(after[0:1, :], -NEG_BIG))
        m_new = jnp.maximum(m_old, jnp.max(s, axis=0, keepdims=True))
        p = jnp.exp2(s - m_new).astype(BF16)
        acc_ref[e] = jnp.exp2(m_old - m_new) * acc_ref[e] + _dot(vt_ref[g, c], p)
        m_ref[e] = m_new

    def step(c, kind):
        s_prev = s0_ref[...]
        for e in range(A_GROUP):
            if e + 1 < A_GROUP:
                s_next = s_tile(e + 1, c)
            else:
                s_next = s_tile(0, jnp.minimum(c + 1, cd))
                s0_ref[...] = s_next
            softmax_pv(e, c, s_prev, s_next, kind)
            s_prev = s_next

    def far_body(c, carry):
        step(c, "far")
        return carry

    def far_pair(i, carry):
        step(2 * i, "far")
        step(2 * i + 1, "far")
        return carry

    s0_ref[...] = s_tile(0, 0)
    lax.fori_loop(0, cs // 2, far_pair, 0)
    lax.fori_loop((cs // 2) * 2, cs, far_body, 0)
    step(cs, "sub")
    step(cd, "diag")
    outs = []
    for e in range(A_GROUP):
        o_t = (acc_ref[e, 0:LANES, :] * (1.0 / acc_ref[e, LANES:LANES + 1, :])).T
        outs.append(jnp.where(par == e % 2, o_t, pltpu.roll(o_t, HEAD_DIM, axis=1)))
    for c2 in range(A_GROUP // 2):
        o_ref[0, :, c2 * LANES:(c2 + 1) * LANES] = jnp.where(half, outs[2 * c2], outs[2 * c2 + 1])[:tq]


def dsa_attention(proj, keys, values, mask, bdiag, bsub, *, lp, t, q_off, k_blk0, v_blk0):
    bsz = proj.shape[0]
    tq = min(t, KEY_TILE)
    tqp = max(tq, LANES)
    nt = lp // KEY_TILE
    nsl = mask.shape[1]
    kern = functools.partial(_dsa_attn_kernel, tq=tq, tqp=tqp, q_off=q_off, nt=nt)
    return pl.pallas_call(
        kern,
        grid=(bsz, t // tq, A_KV_HEADS),
        in_specs=[pl.BlockSpec((1, tq, 2 * LANES), lambda b, a, g: (b, a, g)),
                  pl.BlockSpec((1, lp, LANES), lambda b, a, g: (b, 0, k_blk0 + g // 2)),
                  pl.BlockSpec((1, lp, LANES), lambda b, a, g: (b, 0, v_blk0 + g // 2)),
                  pl.BlockSpec((1, nsl, LANES, tqp), lambda b, a, g: (b, 0, 0, a)),
                  pl.BlockSpec((A_GROUP, KEY_TILE, tqp), lambda b, a, g: (g, 0, 0)),
                  pl.BlockSpec((A_GROUP, LANES, LANES), lambda b, a, g: (g, 0, 0))],
        out_specs=pl.BlockSpec((1, tq, 2 * LANES), lambda b, a, g: (b, a, g)),
        out_shape=jax.ShapeDtypeStruct((bsz, t, A_HEADS * HEAD_DIM), F32),
        scratch_shapes=[pltpu.VMEM((A_GROUP, VT_ROWS, tqp), F32), pltpu.VMEM((A_GROUP, 1, tqp), F32),
                        pltpu.VMEM((KEY_TILE, tqp), F32),
                        pltpu.VMEM((A_KV_HEADS, nt, VT_ROWS, KEY_TILE), BF16)],
        compiler_params=_cparams("parallel", "arbitrary", "arbitrary"),
        name="dsa_attn",
    )(proj, keys, values, mask, bdiag, bsub)


def _fox_decay_kernel(*refs, n_past, n_new, t):
    if n_past:
        fz_ref, bf_ref, past_ref, tri_ref, lf_ref, nck_ref = refs
    else:
        fz_ref, bf_ref, tri_ref, lf_ref, nck_ref = refs
        past_ref = None
    tri = tri_ref[...]
    lane = lax.broadcasted_iota(I32, (B_HEADS, LANES), 1)
    carry = jnp.zeros((B_HEADS, 1), F32)
    for blk in range(n_past + n_new):
        sl = slice(blk * LANES, (blk + 1) * LANES)
        if blk < n_past:
            lf = past_ref[0, :, sl]
        else:
            nsl = slice((blk - n_past) * LANES, (blk - n_past + 1) * LANES)
            x = fz_ref[0, :, nsl] + bf_ref[...]
            lf = jnp.minimum(x, 0.0) - jnp.log1p(jnp.exp(-jnp.abs(x)))
            lf = jnp.where(lane + (blk - n_past) * LANES < t, lf, 0.0)
            lf_ref[0, :, nsl] = lf
        cum = _dot_x01(lf, tri) + carry
        for i, piece in enumerate(_split3(cum * -LOG2E)):
            nck_ref[0, i, :, sl] = piece
        carry = cum[:, LANES - 1:LANES]


def fox_decay(fz_t, b_f, past_t, *, t):
    bsz, h, tp = fz_t.shape
    p = 0 if past_t is None else past_t.shape[2]
    n_past, n_new = p // LANES, tp // LANES
    tri = (jnp.arange(LANES)[:, None] <= jnp.arange(LANES)[None, :]).astype(BF16)
    kern = functools.partial(_fox_decay_kernel, n_past=n_past, n_new=n_new, t=t)
    args = [fz_t, b_f.reshape(h, 1)]
    in_specs = [pl.BlockSpec((1, h, tp), lambda b: (b, 0, 0)),
                pl.BlockSpec((h, 1), lambda b: (0, 0))]
    if n_past:
        args.append(past_t)
        in_specs.append(pl.BlockSpec((1, h, p), lambda b: (b, 0, 0)))
    args.append(tri)
    in_specs.append(pl.BlockSpec((LANES, LANES), lambda b: (0, 0)))
    return pl.pallas_call(
        kern,
        grid=(bsz,),
        in_specs=in_specs,
        out_specs=[pl.BlockSpec((1, h, tp), lambda b: (b, 0, 0)),
                   pl.BlockSpec((1, 3, h, p + tp), lambda b: (b, 0, 0, 0))],
        out_shape=[jax.ShapeDtypeStruct((bsz, h, tp), F32),
                   jax.ShapeDtypeStruct((bsz, 3, h, p + tp), BF16)],
        compiler_params=_cparams("parallel"),
        name="fox_decay",
    )(*args)


def _fox_attn_kernel(q_ref, k_ref, v_ref, pz_ref, g_ref, o_ref, acc_ref, m_ref, s0_ref, ka_ref, vt_ref,
                     *, tq, tqp, q_off, nt):
    j = pl.program_id(1)
    a = pl.program_id(2)
    tk = KEY_TILE
    q0 = q_off + a * tq
    n_full = q0 // tk
    n_need = (q0 + tq - 1) // tk + 1

    @pl.when(a == 0)
    def _():
        klane = lax.broadcasted_iota(I32, (tk, LANES), 1)
        for c in range(nt):
            rows = slice(c * tk, (c + 1) * tk)
            kp = k_ref[0, rows, :]
            pz = pz_ref[0, rows, :].astype(F32)
            ka_ref[0, rows, :] = jnp.where(klane < HEAD_DIM, kp, pz).astype(BF16)
            ka_ref[1, rows, :] = jnp.where(klane >= HEAD_DIM, kp, pz).astype(BF16)
            vt_ref[c] = _vt_rows(v_ref[0, rows, :])

    lane = lax.broadcasted_iota(I32, (tqp, LANES), 1)
    qn = _pad_q_rows(q_ref[0], tq, tqp) * (HEAD_DIM ** -0.5 * LOG2E)
    ones_e = (lane >= HEAD_DIM + 3 * j) & (lane < HEAD_DIM + 3 * j + 3)
    ones_o = (lane >= 3 * j) & (lane < 3 * j + 3)
    qs = (jnp.where(lane < HEAD_DIM, qn, jnp.where(ones_e, 1.0, 0.0)).astype(BF16),
          jnp.where(lane >= HEAD_DIM, qn, jnp.where(ones_o, 1.0, 0.0)).astype(BF16))
    m_ref[...] = jnp.full(m_ref.shape, M_FLOOR, F32)
    acc_ref[...] = jnp.zeros(acc_ref.shape, F32)
    krow = lax.broadcasted_iota(I32, (tk, tqp), 0)
    qcol = lax.broadcasted_iota(I32, (tk, tqp), 1)

    def s_tile(e, c):
        return _nt(ka_ref[e, pl.ds(pl.multiple_of(c * tk, tk), tk), :], qs[e])

    def softmax_pv(e, c, s, masked):
        if masked:
            s = jnp.where(c * tk + krow <= q0 + qcol, s, NEG_BIG)
        m_old = m_ref[e]
        m_new = jnp.maximum(m_old, jnp.max(s, axis=0, keepdims=True))
        p = jnp.exp2(s - m_new).astype(BF16)
        acc_ref[e] = jnp.exp2(m_old - m_new) * acc_ref[e] + _dot(vt_ref[c], p)
        m_ref[e] = m_new

    def step(c, masked):
        s1 = s_tile(1, c)
        softmax_pv(0, c, s0_ref[...], masked)
        s0_ref[...] = s_tile(0, jnp.minimum(c + 1, n_need - 1))
        softmax_pv(1, c, s1, masked)

    def full_body(c, carry):
        step(c, False)
        return carry

    def masked_body(c, carry):
        step(c, True)
        return carry

    def pair_body(i, carry):
        step(2 * i, False)
        step(2 * i + 1, False)
        return carry

    s0_ref[...] = s_tile(0, 0)
    lax.fori_loop(0, n_full // 2, pair_body, 0)
    lax.fori_loop((n_full // 2) * 2, n_full, full_body, 0)
    lax.fori_loop(n_full, n_need, masked_body, 0)
    o_e = (acc_ref[0, 0:LANES, :] * (1.0 / acc_ref[0, LANES:LANES + 1, :])).T
    o_o = (acc_ref[1, 0:LANES, :] * (1.0 / acc_ref[1, LANES:LANES + 1, :])).T
    o = jnp.where(lane < HEAD_DIM, o_e, o_o)
    o_ref[0] = o[:tq] * (1.0 / (1.0 + jnp.exp(-g_ref[0])))


def fox_attention(proj, keys, values, pz, *, lp, t, q_off, g_blk0, k_blk0, v_blk0):
    bsz = proj.shape[0]
    tq = min(t, KEY_TILE)
    tqp = max(tq, LANES)
    nt = lp // KEY_TILE
    kern = functools.partial(_fox_attn_kernel, tq=tq, tqp=tqp, q_off=q_off, nt=nt)
    return pl.pallas_call(
        kern,
        grid=(bsz, B_HEADS // 2, t // tq),
        in_specs=[pl.BlockSpec((1, tq, LANES), lambda b, j, a: (b, a, j)),
                  pl.BlockSpec((1, lp, LANES), lambda b, j, a: (b, 0, k_blk0 + j)),
                  pl.BlockSpec((1, lp, LANES), lambda b, j, a: (b, 0, v_blk0 + j)),
                  pl.BlockSpec((1, lp, LANES), lambda b, j, a: (b, 0, 0)),
                  pl.BlockSpec((1, tq, LANES), lambda b, j, a: (b, a, g_blk0 + j))],
        out_specs=pl.BlockSpec((1, tq, LANES), lambda b, j, a: (b, a, j)),
        out_shape=jax.ShapeDtypeStruct((bsz, t, B_HEADS * HEAD_DIM), F32),
        scratch_shapes=[pltpu.VMEM((2, VT_ROWS, tqp), F32), pltpu.VMEM((2, 1, tqp), F32),
                        pltpu.VMEM((KEY_TILE, tqp), F32),
                        pltpu.VMEM((2, lp, LANES), BF16),
                        pltpu.VMEM((nt, VT_ROWS, KEY_TILE), BF16)],
        compiler_params=_cparams("parallel", "parallel", "arbitrary"),
        name="fox_attn",
    )(proj, keys, values, pz, proj)


def _hgrn2_levels(tc):
    lv = []
    n = 8
    while n < tc:
        lv.append(n)
        n *= 2
    return lv


def _hgrn2_masks(tc):
    t = jnp.arange(tc)[:, None]
    s = jnp.arange(tc)[None, :]
    ms = [((t // (2 * n) == s // (2 * n)) & ((t // n) % 2 == 1) & ((s // n) % 2 == 0)) for n in _hgrn2_levels(tc)]
    ms.append((t // 8 == s // 8) & (s <= t))
    return jnp.stack(ms).astype(F32)


def _hgrn2_head(q, z, v, g, lb, og, st, tri_ref, msk_ref, tc):
    ez = jnp.exp(-jnp.abs(z))
    den = 1.0 / (1.0 + ez)
    pos = z >= 0.0
    f = lb + (1.0 - lb) * (jnp.where(pos, 1.0, ez) * den)
    kk = (1.0 - lb) * (jnp.where(pos, ez, 1.0) * den)
    cum = _dot_01x(tri_ref[...], jnp.log(f))

    def rows(idx):
        parts = []
        for i in idx:
            parts.append(jnp.zeros((8, LANES), F32) if i < 0 else jnp.broadcast_to(cum[i:i + 1, :], (8, LANES)))
        return jnp.concatenate(parts, axis=0)

    levels = _hgrn2_levels(tc)
    ngrp = tc // 8
    scores = jnp.zeros((tc, tc), F32)
    ql8 = None
    for li, n in enumerate(levels):
        start = [((r * 8) // n) * n for r in range(ngrp)]
        a_start = rows([s - 1 for s in start])
        a_end = rows([s + n - 1 for s in start])
        ql = (q * jnp.exp(cum - a_start)).astype(BF16)
        kr = (kk * jnp.exp(a_end - cum)).astype(BF16)
        scores = scores + msk_ref[li] * _nt(ql, kr)
        if n == 8:
            ql8 = ql
            kb = (kk * jnp.exp(a_start - cum)).astype(BF16)
    if ql8 is None:
        a_start = rows([r * 8 - 1 for r in range(ngrp)])
        ql8 = (q * jnp.exp(cum - a_start)).astype(BF16)
        kb = (kk * jnp.exp(a_start - cum)).astype(BF16)
    scores = scores + msk_ref[len(levels)] * _nt(ql8, kb)

    o = _nt((q * jnp.exp(cum)).astype(BF16), st.astype(BF16)) + _dot(scores.astype(BF16), v.astype(BF16))
    a_last = cum[tc - 1:tc, :]
    khat = (kk * jnp.exp(a_last - cum)).astype(BF16)
    st_new = st * jnp.exp(a_last) + _dot(v.T.astype(BF16), khat)
    on = o * lax.rsqrt(jnp.mean(o * o, axis=-1, keepdims=True) + EPS) * og
    return on * (g * (1.0 / (1.0 + jnp.exp(-g)))), st_new


def _hgrn2_kernel(q_ref, fz_ref, v_ref, g_ref, lb_ref, og_ref, s0_ref, tri_ref, msk_ref, y_ref, so_ref, st_ref, *, tc):
    ct = pl.program_id(2)

    @pl.when(ct == 0)
    def _():
        st_ref[...] = s0_ref[0]

    for e in range(HGRN2_HEADS_PER_STEP):
        sl = slice(e * LANES, (e + 1) * LANES)
        y, st_new = _hgrn2_head(q_ref[0, :, sl], fz_ref[0, :, sl], v_ref[0, :, sl], g_ref[0, :, sl], lb_ref[e],
                                og_ref[...], st_ref[e], tri_ref, msk_ref, tc)
        y_ref[0, :, sl] = y
        st_ref[e] = st_new

    @pl.when(ct == pl.num_programs(2) - 1)
    def _():
        so_ref[0] = st_ref[...]


def hgrn2_recurrence(proj, lb, out_gain, s0_t, *, t):
    bsz = proj.shape[0]
    tc = min(t, 128)
    nlv = len(_hgrn2_levels(tc)) + 1
    tri = (jnp.arange(tc)[:, None] >= jnp.arange(tc)[None, :]).astype(BF16)
    h = C_HEADS
    hps = HGRN2_HEADS_PER_STEP
    ng = h // hps
    kern = functools.partial(_hgrn2_kernel, tc=tc)
    blk = lambda off: pl.BlockSpec((1, tc, hps * LANES), lambda b, hh, c: (b, c, off + hh))
    return pl.pallas_call(
        kern,
        grid=(bsz, ng, t // tc),
        in_specs=[blk(0), blk(ng), blk(2 * ng), blk(3 * ng),
                  pl.BlockSpec((hps, 1, C_DK), lambda b, hh, c: (hh, 0, 0)),
                  pl.BlockSpec((1, C_DV), lambda b, hh, c: (0, 0)),
                  pl.BlockSpec((1, hps, C_DV, C_DK), lambda b, hh, c: (b, hh, 0, 0)),
                  pl.BlockSpec((tc, tc), lambda b, hh, c: (0, 0)),
                  pl.BlockSpec((nlv, tc, tc), lambda b, hh, c: (0, 0, 0))],
        out_specs=[pl.BlockSpec((1, tc, hps * LANES), lambda b, hh, c: (b, c, hh)),
                   pl.BlockSpec((1, hps, C_DV, C_DK), lambda b, hh, c: (b, hh, 0, 0))],
        out_shape=[jax.ShapeDtypeStruct((bsz, t, h * C_DV), F32),
                   jax.ShapeDtypeStruct((bsz, h, C_DV, C_DK), F32)],
        scratch_shapes=[pltpu.VMEM((hps, C_DV, C_DK), F32)],
        compiler_params=_cparams("parallel", "parallel", "arbitrary"),
        name="hgrn2",
    )(proj, proj, proj, proj, lb.reshape(h, 1, C_DK), out_gain.reshape(1, C_DV), s0_t, tri, _hgrn2_masks(tc))


def _route_t(h, wr_t, br_t):
    ah, am, al = _split3(wr_t)
    bh, bm, bl = _split3(h)
    r = _nt(ah, bh) + (_nt(ah, bm) + _nt(am, bh)) + (_nt(ah, bl) + _nt(am, bm) + _nt(al, bh)) + br_t
    row = lax.broadcasted_iota(I32, r.shape, 0)
    rowf = row.astype(F32)
    big = float(ROUTER_ROWS)
    is_g = row < N_GROUPS
    lg = jnp.where(is_g, r, -jnp.inf)
    mg = jnp.max(lg, axis=0, keepdims=True)
    grp = jnp.min(jnp.where(lg == mg, rowf, big), axis=0, keepdims=True)
    p_grp = 1.0 / jnp.sum(jnp.where(is_g, jnp.exp(r - mg), 0.0), axis=0, keepdims=True)
    eg = ((row - N_GROUPS) >> 2).astype(F32)
    in_e = (row >= N_GROUPS) & (row < N_GROUPS + N_EXPERTS) & (eg == grp)
    le = jnp.where(in_e, r, -jnp.inf)
    v1 = jnp.max(le, axis=0, keepdims=True)
    i1 = jnp.min(jnp.where(le == v1, rowf, big), axis=0, keepdims=True)
    le2 = jnp.where(rowf == i1, -jnp.inf, le)
    v2 = jnp.max(le2, axis=0, keepdims=True)
    i2 = jnp.min(jnp.where(le2 == v2, rowf, big), axis=0, keepdims=True)
    e2 = jnp.exp(v2 - v1)
    w1 = 1.0 / (1.0 + e2)
    gates = jnp.where(rowf == i1, w1 * p_grp, 0.0) + jnp.where(rowf == i2, (e2 * w1) * p_grp, 0.0)
    member = jnp.where((rowf == i1) | (rowf == i2), 1.0, 0.0)
    return gates, member


def _moe_kernel(x_ref, gain_ref, sc_ref, sh_ref, g2_ref, wr_ref, br_ref, tri_ref, wg_ref, wu_ref, wd_ref, o_ref,
                hb_ref, rank_ref, rt_ref, gt_ref, y_ref, acc_ref, *, ts, slot):
    e = pl.program_id(2)
    tm = x_ref.shape[1]
    n_sub = tm // ts
    tsp = ts
    n_chunk = tsp // slot
    row = e + N_GROUPS

    @pl.when(e == 0)
    def _():
        for s in range(n_sub):
            rows = slice(s * ts, (s + 1) * ts)
            h = _ln_mod(x_ref[0, rows, :], gain_ref[...], _mod_rows(sc_ref, rows), _mod_rows(sh_ref, rows))
            hb_ref[rows, :] = h.astype(BF16)
            gates_t, member_t = _route_t(h, wr_ref[...], br_ref[...])
            r_t = _dot(member_t.astype(BF16), tri_ref[...])
            r_t = jnp.where(member_t > 0.0, r_t, -1.0)
            rt_ref[s] = r_t
            gt_ref[s] = gates_t
            rank_ref[s] = jnp.concatenate([r_t, jnp.full((LANES - ROUTER_ROWS, ts), -1.0, F32)], axis=0).T
        if n_chunk > 1:
            acc_ref[...] = jnp.zeros_like(acc_ref)

    def expert_rows(k):
        xs, ges, sels = [], [], []
        slot_i = lax.broadcasted_iota(I32, (slot, tsp), 0).astype(F32) + float(k * slot)
        for s in range(n_sub):
            sel = jnp.where(rt_ref[s, pl.ds(row, 1), :] == slot_i, 1.0, 0.0)
            ges.append(jnp.sum(sel * gt_ref[s, pl.ds(row, 1), :], axis=-1, keepdims=True))
            xs.append(_dot(sel.astype(BF16), hb_ref[s * tsp:(s + 1) * tsp, :]).astype(BF16))
            sels.append(sel)
        xa = jnp.concatenate(xs, axis=0)
        a = _dot(xa, wg_ref[0])
        u = _dot(xa, wu_ref[0])
        he = (a * (1.0 / (1.0 + jnp.exp(-a)))) * u * jnp.concatenate(ges, axis=0)
        return _dot(he.astype(BF16), wd_ref[0]), sels

    ye, _ = expert_rows(0)
    for s in range(n_sub):
        y_ref[s, pl.ds(pl.multiple_of(e * slot, slot), slot), :] = ye[s * slot:(s + 1) * slot].astype(BF16)

    for k in range(1, n_chunk):
        last_rank = jnp.max(rt_ref[0, pl.ds(row, 1), :])
        for s in range(1, n_sub):
            last_rank = jnp.maximum(last_rank, jnp.max(rt_ref[s, pl.ds(row, 1), :]))

        @pl.when(last_rank >= float(k * slot))
        def _():
            ye_k, sels = expert_rows(k)
            for s in range(n_sub):
                acc_ref[s * tsp:(s + 1) * tsp, :] += _dot(sels[s].T.astype(BF16),
                                                          ye_k[s * slot:(s + 1) * slot].astype(BF16))

    @pl.when(e == pl.num_programs(2) - 1)
    def _():
        slot_l = lax.broadcasted_iota(I32, (1, slot), 1).astype(F32)
        for s in range(n_sub):
            rows = slice(s * ts, (s + 1) * ts)
            rank = rank_ref[s]
            sel_t = jnp.concatenate(
                [jnp.where(rank[:, N_GROUPS + j:N_GROUPS + j + 1] == slot_l, 1.0, 0.0).astype(BF16)
                 for j in range(N_EXPERTS)], axis=1)
            y = _dot(sel_t, y_ref[s])
            if n_chunk > 1:
                y = y + acc_ref[s * tsp:(s + 1) * tsp, :]
            o_ref[0, rows, :] = x_ref[0, rows, :] + _mod_rows(g2_ref, rows) * y


def moe_layer(x, gain, sc, sh, g2, w_router, b_router, w_gate, w_up, w_down):
    bsz, t, d = x.shape
    if t % LANES:
        y = moe_layer(_merge(x), gain, _per_token(sc, t), _per_token(sh, t), _per_token(g2, t),
                      w_router, b_router, w_gate, w_up, w_down)
        return y.reshape(x.shape)
    tm = min(t, 1024 if sc.shape[1] == 1 else 512)
    ts = min(tm, 512)
    assert ts % LANES == 0 and t % tm == 0, (t, tm, ts)
    tsp = ts
    slot = MOE_SLOT
    ne, _, de = w_gate.shape
    tri = (jnp.arange(tsp)[:, None] < jnp.arange(tsp)[None, :]).astype(BF16)
    kern = functools.partial(_moe_kernel, ts=ts, slot=slot)
    n_sub = tm // ts
    return pl.pallas_call(
        kern,
        grid=(bsz, t // tm, ne),
        in_specs=[pl.BlockSpec((1, tm, d), lambda b, i, e: (b, i, 0)),
                  pl.BlockSpec((1, d), lambda b, i, e: (0, 0)),
                  _mod_spec(sc, tm, 3),
                  _mod_spec(sh, tm, 3),
                  _mod_spec(g2, tm, 3),
                  pl.BlockSpec((ROUTER_ROWS, d), lambda b, i, e: (0, 0)),
                  pl.BlockSpec((ROUTER_ROWS, 1), lambda b, i, e: (0, 0)),
                  pl.BlockSpec((tsp, tsp), lambda b, i, e: (0, 0)),
                  pl.BlockSpec((1, d, de), lambda b, i, e: (e, 0, 0)),
                  pl.BlockSpec((1, d, de), lambda b, i, e: (e, 0, 0)),
                  pl.BlockSpec((1, de, d), lambda b, i, e: (e, 0, 0))],
        out_specs=pl.BlockSpec((1, tm, d), lambda b, i, e: (b, i, 0)),
        out_shape=jax.ShapeDtypeStruct((bsz, t, d), F32),
        scratch_shapes=[pltpu.VMEM((n_sub * tsp, d), BF16),
                        pltpu.VMEM((n_sub, tsp, LANES), F32),
                        pltpu.VMEM((n_sub, ROUTER_ROWS, tsp), F32),
                        pltpu.VMEM((n_sub, ROUTER_ROWS, tsp), F32),
                        pltpu.VMEM((n_sub, ne * slot, d), BF16),
                        pltpu.VMEM((n_sub * tsp if tsp > slot else 8, d), F32)],
        compiler_params=_cparams("parallel", "parallel", "arbitrary"),
        name="moe",
    )(x, gain.reshape(1, d), sc, sh, g2, w_router, b_router, tri, w_gate, w_up, w_down)


def _pad_cols(w, n):
    return jnp.pad(w, ((0, 0), (0, n - w.shape[1])))


def _pad_rows(a, n):
    return jnp.pad(a, ((0, 0), (0, n - a.shape[1])) + ((0, 0),) * (a.ndim - 2))


def _round_up(n, m):
    return -(-n // m) * m


def _head_cols(gain_q, nq, gain_k, nk, npad):
    cg = jnp.concatenate([jnp.tile(gain_q, nq), jnp.tile(gain_k, nk)])
    n = cg.shape[0]
    cgain = jnp.pad(cg, (0, npad - n)).reshape(1, npad)
    cflag = (jnp.arange(npad) < n).astype(F32).reshape(1, npad)
    return cgain, cflag


def _dsa_layer(x, mod, past, prm, bdiag, bsub):
    sh1, sc1, g1 = mod
    past_k, past_v, past_ki = past
    bsz, t, d = x.shape
    p = 0 if past_k is None else past_k.shape[1]
    n_keys = p + t
    topk = min(TOPK_MAX, n_keys // 4)
    tn = 768
    n_in = prm['w_in'].shape[1]
    npad = _round_up(n_in, tn)
    nqk = (A_HEADS + A_KV_HEADS) * HEAD_DIM
    w = _pad_cols(prm['w_in'], npad).astype(BF16)
    cgain, cflag = _head_cols(prm['q_norm'], A_HEADS, prm['k_norm'], A_KV_HEADS, npad)
    proj = ln_proj(x, prm['norm'], sc1, sh1, w, cgain, cflag, -(-nqk // tn), tn)
    o_k = A_HEADS * HEAD_DIM
    o_v = o_k + A_KV_HEADS * HEAD_DIM
    o_qi = o_v + A_KV_HEADS * HEAD_DIM
    o_ki = o_qi + IDX_HEADS * IDX_DIM
    k_new = proj[..., o_k:o_v].reshape(bsz, t, A_KV_HEADS, HEAD_DIM)
    v_new = proj[..., o_v:o_qi].reshape(bsz, t, A_KV_HEADS, HEAD_DIM)
    ki_new = proj[..., o_ki:o_ki + IDX_DIM]
    lp = _round_up(n_keys, KEY_TILE)
    kvw = A_KV_HEADS * HEAD_DIM
    if p:
        keys = _pad_rows(jnp.concatenate([past_k.reshape(bsz, p, kvw), proj[..., o_k:o_v]], axis=1), lp)
        values = _pad_rows(jnp.concatenate([past_v.reshape(bsz, p, kvw), proj[..., o_v:o_qi]], axis=1), lp)
        ikeys = _pad_rows(jnp.concatenate([past_ki, ki_new], axis=1), lp)
        ikeys = jnp.pad(ikeys, ((0, 0), (0, 0), (0, LANES - IDX_DIM)))
        k_blk0, v_blk0, ki_blk = 0, 0, 0
    else:
        keys = values = ikeys = proj
        k_blk0, v_blk0, ki_blk = o_k // LANES, o_v // LANES, o_ki // LANES
    mask = dsa_index_mask(proj, ikeys, lp=lp, t=t, q_off=p, topk=topk, qi_blk=o_qi // (4 * LANES),
                          w_blk=o_ki // LANES, ki_blk=ki_blk)
    o = dsa_attention(proj, keys, values, mask, bdiag, bsub, lp=lp, t=t, q_off=p, k_blk0=k_blk0, v_blk0=v_blk0)
    x = out_proj_residual(o, prm['w_out'].astype(BF16), x, g1)
    return x, (k_new, v_new, ki_new)


def _fox_layer(x, mod, past, prm):
    sh1, sc1, g1 = mod
    past_k, past_v, past_lf = past
    bsz, t, d = x.shape
    p = 0 if past_k is None else past_k.shape[1]
    n_keys = p + t
    hd = B_HEADS * HEAD_DIM
    tn = 512
    npad = _round_up(prm['w_in'].shape[1], tn)
    w = _pad_cols(prm['w_in'], npad).astype(BF16)
    cgain, cflag = _head_cols(prm['q_norm'], B_HEADS, prm['k_norm'], B_HEADS, npad)
    proj = ln_proj(x, prm['norm'], sc1, sh1, w, cgain, cflag, 2 * hd // tn, tn)
    k_new = proj[..., hd:2 * hd]
    v_new = proj[..., 2 * hd:3 * hd]
    fz = proj[..., 4 * hd:4 * hd + B_HEADS]
    tp = _round_up(t, LANES)
    fz_t = _pad_rows(fz, tp).transpose(0, 2, 1)
    past_t = None if not p else past_lf.transpose(0, 2, 1)
    lf_t, nck = fox_decay(fz_t, prm['forget_bias'], past_t, t=t)
    logf_new = lf_t[:, :, :t].transpose(0, 2, 1)
    lp = _round_up(n_keys, KEY_TILE)
    if p:
        keys = _pad_rows(jnp.concatenate([past_k.reshape(bsz, p, hd), k_new], axis=1), lp)
        values = _pad_rows(jnp.concatenate([past_v.reshape(bsz, p, hd), v_new], axis=1), lp)
        k_blk0, v_blk0 = 0, 0
    else:
        keys = values = proj
        k_blk0, v_blk0 = hd // LANES, 2 * hd // LANES
    pieces = jnp.pad(nck, ((0, 0), (0, 0), (0, 0), (0, lp - nck.shape[3]))).transpose(0, 3, 2, 1)
    zl = jnp.zeros((bsz, lp, HEAD_DIM - 3 * B_HEADS // 2), pieces.dtype)
    pz = jnp.concatenate([pieces[:, :, 1::2].reshape(bsz, lp, -1), zl,
                          pieces[:, :, 0::2].reshape(bsz, lp, -1), zl], axis=-1)
    o = fox_attention(proj, keys, values, pz, lp=lp, t=t, q_off=p, g_blk0=3 * hd // LANES,
                      k_blk0=k_blk0, v_blk0=v_blk0)
    x = out_proj_residual(o, prm['w_out'].astype(BF16), x, g1)
    return x, (k_new.reshape(bsz, t, B_HEADS, HEAD_DIM), v_new.reshape(bsz, t, B_HEADS, HEAD_DIM), logf_new)


def _hgrn2_layer(x, mod, s0, prm):
    sh1, sc1, g1 = mod
    bsz, t, d = x.shape
    npad = prm['w_in'].shape[1]
    zeros = jnp.zeros((1, npad), F32)
    proj = ln_proj(x, prm['norm'], sc1, sh1, prm['w_in'].astype(BF16), zeros, zeros, 0, 512)
    y, s_t = hgrn2_recurrence(proj, prm['lb'], prm['out_norm'], jnp.swapaxes(s0, -1, -2), t=t)
    x = out_proj_residual(y, prm['w_out'].astype(BF16), x, g1)
    return x, jnp.swapaxes(s_t, -1, -2)


def _trunk(x, c, a_k, a_v, a_kidx, b_k, b_v, b_logf, c_state, prm):
    bsz, t, d = x.shape
    mod_all = ada_mod(c, prm['w_ada'], prm['b_ada'])
    lb_all = jnp.cumsum(jax.nn.softmax(prm['c_lower_bound'].astype(F32), axis=0), axis=0)
    lb_all = lb_all - lb_all[0]
    bdiag, bsub = dsa_bias_tiles(prm['rel_table'])
    out_a, out_b, out_c = [], [], []
    for i in range(DEPTH):
        j = i // N_MIXERS
        kind = i % N_MIXERS
        sh1, sc1, g1, sh2, sc2, g2 = [m.reshape(bsz, 1, d) for m in jnp.split(mod_all[i], 6, axis=-1)]
        mod = (sh1, sc1, g1)
        if kind == 0:
            past = (None, None, None) if a_k is None else (a_k[j], a_v[j], a_kidx[j])
            lp = dict(norm=prm['norm_mix'][i], w_in=prm['a_w_in'][j], q_norm=prm['a_q_norm'][j],
                      k_norm=prm['a_k_norm'][j], w_out=prm['a_w_out'][j])
            x, new = _dsa_layer(x, mod, past, lp, bdiag, bsub)
            out_a.append(new)
        elif kind == 1:
            past = (None, None, None) if b_k is None else (b_k[j], b_v[j], b_logf[j])
            lp = dict(norm=prm['norm_mix'][i], w_in=prm['b_w_in'][j], forget_bias=prm['b_forget_bias'][j],
                      q_norm=prm['b_q_norm'][j], k_norm=prm['b_k_norm'][j], w_out=prm['b_w_out'][j])
            x, new = _fox_layer(x, mod, past, lp)
            out_b.append(new)
        else:
            s0 = jnp.zeros((bsz, C_HEADS, C_DK, C_DV), F32) if c_state is None else c_state[j]
            lp = dict(norm=prm['norm_mix'][i], w_in=prm['c_w_in'][j], lb=lb_all[i],
                      out_norm=prm['c_out_norm'][j], w_out=prm['c_w_out'][j])
            x, new = _hgrn2_layer(x, mod, s0, lp)
            out_c.append(new)
        w_router = jnp.pad(jnp.concatenate([prm['moe_w_group'][i], prm['moe_w_expert'][i]], axis=1).T,
                           ((0, ROUTER_ROWS - N_GROUPS - N_EXPERTS), (0, 0)))
        b_router = jnp.pad(jnp.concatenate([prm['moe_b_group'][i], prm['moe_b_expert'][i]]),
                           (0, ROUTER_ROWS - N_GROUPS - N_EXPERTS)).reshape(ROUTER_ROWS, 1)
        x = moe_layer(x, prm['norm_ffn'][i], sc2, sh2, g2, w_router, b_router,
                      prm['moe_w_gate'][i].astype(BF16), prm['moe_w_up'][i].astype(BF16),
                      prm['moe_w_down'][i].astype(BF16))
    stack = lambda outs, k: jnp.stack([o[k] for o in outs])
    return (x, stack(out_a, 0), stack(out_a, 1), stack(out_a, 2),
            stack(out_b, 0), stack(out_b, 1), stack(out_b, 2), jnp.stack(out_c))


def kernel(x_prompt, x_sample, cache_a_k, cache_a_v, cache_a_kidx, cache_b_k, cache_b_v, cache_b_logf, state_c,
           c_prompt, c_sample, rel_table, w_ada, b_ada, norm_mix, norm_ffn, a_w_in, a_q_norm, a_k_norm, a_w_out,
           b_w_in, b_forget_bias, b_q_norm, b_k_norm, b_w_out, c_w_in, c_lower_bound, c_out_norm, c_w_out,
           moe_w_group, moe_b_group, moe_w_expert, moe_b_expert, moe_w_gate, moe_w_up, moe_w_down):
    prm = {'rel_table': rel_table, 'w_ada': w_ada, 'b_ada': b_ada, 'norm_mix': norm_mix, 'norm_ffn': norm_ffn,
           'a_w_in': a_w_in, 'a_q_norm': a_q_norm, 'a_k_norm': a_k_norm, 'a_w_out': a_w_out,
           'b_w_in': b_w_in, 'b_forget_bias': b_forget_bias, 'b_q_norm': b_q_norm, 'b_k_norm': b_k_norm,
           'b_w_out': b_w_out, 'c_w_in': c_w_in, 'c_lower_bound': c_lower_bound, 'c_out_norm': c_out_norm,
           'c_w_out': c_w_out, 'moe_w_group': moe_w_group, 'moe_b_group': moe_b_group,
           'moe_w_expert': moe_w_expert, 'moe_b_expert': moe_b_expert, 'moe_w_gate': moe_w_gate,
           'moe_w_up': moe_w_up, 'moe_w_down': moe_w_down}
    (y_p, ak_p, av_p, ai_p, bk_p, bv_p, bl_p, cs_p) = _trunk(
        x_prompt, c_prompt, None, None, None, None, None, None, None, prm)
    (y_s, ak_s, av_s, ai_s, bk_s, bv_s, bl_s, cs_s) = _trunk(
        x_sample, c_sample, cache_a_k, cache_a_v, cache_a_kidx, cache_b_k, cache_b_v, cache_b_logf, state_c, prm)
    return (y_p, y_s, ak_p, av_p, ai_p, ak_s, av_s, ai_s, bk_p, bv_p, bl_p, bk_s, bv_s, bl_s, cs_p, cs_s)
```

```python
import functools

import jax
import jax.numpy as jnp
from jax import lax
from jax.experimental import pallas as pl
from jax.experimental.pallas import tpu as pltpu

F32 = jnp.float32
BF16 = jnp.bfloat16
I32 = jnp.int32

LANES = 128
VMEM_LIMIT_BYTES = 56 * 1024 * 1024

DEPTH = 4
N_MIXERS = 3
CHUNK = 64
EPS = 1e-6
HEAD_DIM = 64
A_HEADS = 16
A_KV_HEADS = 4
A_GROUP = A_HEADS // A_KV_HEADS
IDX_HEADS = 8
IDX_DIM = 64
TOPK_MAX = 256
REL_BUCKETS = 32
B_HEADS = 16
C_HEADS = 8
C_DK = 128
C_DV = 128
N_GROUPS = 4
EXPERTS_PER_GROUP = 4
N_EXPERTS = 16
D_EXPERT = 512

LOG2E = 1.4426950408889634
NEG_BIG = -1e30
M_FLOOR = -1e20
INT_MIN = -2 ** 31
KEY_TILE = 512
SLABS = KEY_TILE // LANES
MOE_SLOT = 128
VT_ROWS = LANES + 16
ROUTER_ROWS = 32
HGRN2_HEADS_PER_STEP = 4
MOE_EXPERTS_PER_STEP = 2


def _cparams(*sem):
    return pltpu.CompilerParams(dimension_semantics=sem, vmem_limit_bytes=VMEM_LIMIT_BYTES)


def _nt(a, b):
    return lax.dot_general(a, b, (((1,), (1,)), ((), ())), preferred_element_type=F32)


def _split3(x):
    hi = x.astype(BF16)
    r = x - hi.astype(F32)
    mid = r.astype(BF16)
    lo = (r - mid.astype(F32)).astype(BF16)
    return hi, mid, lo


def _dot(a, b):
    return jnp.dot(a, b, preferred_element_type=F32)


def _dot_x01(x, m01):
    hi, mid, lo = _split3(x)
    return _dot(hi, m01) + _dot(mid, m01) + _dot(lo, m01)


def _dot_01x(m01, x):
    hi, mid, lo = _split3(x)
    return _dot(m01, hi) + _dot(m01, mid) + _dot(m01, lo)


def _dot_f32(a, b):
    ah, am, al = _split3(a)
    bh, bm, bl = _split3(b)
    return _dot(ah, bh) + (_dot(ah, bm) + _dot(am, bh)) + (_dot(ah, bl) + _dot(am, bm) + _dot(al, bh))


def _pad_q_rows(q, tq, tqp):
    if tqp == tq:
        return q
    return jnp.concatenate([q, jnp.zeros((tqp - tq, q.shape[1]), q.dtype)], axis=0)


def _mod_kernel(c_ref, w_ref, b_ref, o_ref):
    o_ref[0] = _dot(c_ref[...], w_ref[0]) + b_ref[0]


def ada_mod(c, w_ada, b_ada):
    nl, d, n6 = w_ada.shape
    bsz = c.shape[0]
    tn = 512
    return pl.pallas_call(
        _mod_kernel,
        grid=(nl, n6 // tn),
        in_specs=[pl.BlockSpec((bsz, d), lambda l, j: (0, 0)),
                  pl.BlockSpec((1, d, tn), lambda l, j: (l, 0, j)),
                  pl.BlockSpec((1, 1, tn), lambda l, j: (l, 0, j))],
        out_specs=pl.BlockSpec((1, bsz, tn), lambda l, j: (l, 0, j)),
        out_shape=jax.ShapeDtypeStruct((nl, bsz, n6), F32),
        compiler_params=_cparams("parallel", "parallel"),
        name="ada_mod",
    )(c, w_ada, b_ada.reshape(nl, 1, n6))


def _mod_spec(m, tm, grid_rank):
    d = m.shape[2]
    if m.shape[1] == 1:
        return pl.BlockSpec((1, 1, d), (lambda b, i, j: (b, 0, 0)) if grid_rank == 3 else (lambda b, i: (b, 0, 0)))
    return pl.BlockSpec((1, tm, d), (lambda b, i, j: (b, i, 0)) if grid_rank == 3 else (lambda b, i: (b, i, 0)))


def _mod_rows(ref, rows):
    return ref[0] if ref.shape[1] == 1 else ref[0, rows, :]


def _merge(x):
    return x.reshape(1, x.shape[0] * x.shape[1], x.shape[2])


def _per_token(m, t):
    bsz, _, d = m.shape
    return jnp.broadcast_to(m, (bsz, t, d)).reshape(1, bsz * t, d)


def _ln_mod(x, gain, sc, sh):
    ms = jnp.mean(x * x, axis=-1, keepdims=True)
    return (x * lax.rsqrt(ms + EPS) * gain) * (1.0 + sc) + sh


def _ln_proj_kernel(x_ref, gain_ref, sc_ref, sh_ref, w_ref, cgain_ref, cflag_ref, bd_ref, o_ref, h_ref,
                    *, n_norm_tiles, tn):
    j = pl.program_id(2)

    @pl.when(j == 0)
    def _():
        h_ref[...] = _ln_mod(x_ref[0], gain_ref[...], sc_ref[0], sh_ref[0]).astype(BF16)

    y = _dot(h_ref[...], w_ref[...])

    def plain():
        o_ref[0] = y

    def normed():
        y2 = y * y
        hi = y2.astype(BF16)
        lo = (y2 - hi.astype(F32)).astype(BF16)
        bd = bd_ref[...]
        segs = []
        for s in range(tn // LANES):
            sl = slice(s * LANES, (s + 1) * LANES)
            segs.append(_dot(hi[:, sl], bd) + _dot(lo[:, sl], bd))
        seg = jnp.concatenate(segs, axis=1)
        yn = y * lax.rsqrt(seg * (1.0 / HEAD_DIM) + EPS) * cgain_ref[...]
        o_ref[0] = jnp.where(cflag_ref[...] > 0.0, yn, y)

    if n_norm_tiles == 0:
        plain()
    else:
        pl.when(j < n_norm_tiles)(normed)
        pl.when(j >= n_norm_tiles)(plain)


def ln_proj(x, gain, sc, sh, w, cgain, cflag, n_norm_tiles, tn=256):
    bsz, t, d = x.shape
    if t % LANES:
        y = ln_proj(_merge(x), gain, _per_token(sc, t), _per_token(sh, t), w, cgain, cflag, n_norm_tiles, tn)
        return y.reshape(bsz, t, -1)
    npad = w.shape[1]
    tm = min(t, 1024)
    bd = (jnp.arange(LANES)[:, None] // HEAD_DIM == jnp.arange(LANES)[None, :] // HEAD_DIM).astype(BF16)
    kern = functools.partial(_ln_proj_kernel, n_norm_tiles=n_norm_tiles, tn=tn)
    return pl.pallas_call(
        kern,
        grid=(bsz, t // tm, npad // tn),
        in_specs=[pl.BlockSpec((1, tm, d), lambda b, i, j: (b, i, 0)),
                  pl.BlockSpec((1, d), lambda b, i, j: (0, 0)),
                  _mod_spec(sc, tm, 3),
                  _mod_spec(sh, tm, 3),
                  pl.BlockSpec((d, tn), lambda b, i, j: (0, j)),
                  pl.BlockSpec((1, tn), lambda b, i, j: (0, j)),
                  pl.BlockSpec((1, tn), lambda b, i, j: (0, j)),
                  pl.BlockSpec((LANES, LANES), lambda b, i, j: (0, 0))],
        out_specs=pl.BlockSpec((1, tm, tn), lambda b, i, j: (b, i, j)),
        out_shape=jax.ShapeDtypeStruct((bsz, t, npad), F32),
        scratch_shapes=[pltpu.VMEM((tm, d), BF16)],
        compiler_params=_cparams("parallel", "parallel", "arbitrary"),
        name="ln_proj",
    )(x, gain.reshape(1, d), sc, sh, w, cgain, cflag, bd)


def _out_proj_kernel(a_ref, w_ref, x_ref, g_ref, o_ref):
    y = _dot(a_ref[0].astype(BF16), w_ref[...])
    o_ref[0] = x_ref[0] + g_ref[0] * y


def out_proj_residual(a, w, x, gate):
    bsz, t, k = a.shape
    if t % LANES:
        return out_proj_residual(_merge(a), w, _merge(x), _per_token(gate, t)).reshape(x.shape)
    d = w.shape[1]
    tm = min(t, 512)
    return pl.pallas_call(
        _out_proj_kernel,
        grid=(bsz, t // tm),
        in_specs=[pl.BlockSpec((1, tm, k), lambda b, i: (b, i, 0)),
                  pl.BlockSpec((k, d), lambda b, i: (0, 0)),
                  pl.BlockSpec((1, tm, d), lambda b, i: (b, i, 0)),
                  _mod_spec(gate, tm, 2)],
        out_specs=pl.BlockSpec((1, tm, d), lambda b, i: (b, i, 0)),
        out_shape=jax.ShapeDtypeStruct((bsz, t, d), F32),
        compiler_params=_cparams("parallel", "parallel"),
        name="out_proj",
    )(a, w, x, gate)


def _dsa_index_kernel(qi_ref, w_ref, ki_ref, o_ref, key_ref, *, tq, tqp, n_slabs, q_off, topk, idx_bits):
    a = pl.program_id(1)
    tk = KEY_TILE
    q0 = q_off + a * tq
    n_kt = (q0 + tq + tk - 1) // tk
    lane = lax.broadcasted_iota(I32, (tqp, LANES), 1)
    half = lane < IDX_DIM
    krow = lax.broadcasted_iota(I32, (tk, tqp), 0)
    qcol = lax.broadcasted_iota(I32, (tk, tqp), 1)
    qchunk = (q0 + qcol) >> 6
    srow = lax.broadcasted_iota(I32, (LANES, tqp), 0)
    w_t = (_pad_q_rows(w_ref[0], tq, tqp) * (IDX_HEADS ** -0.5)).T
    qs = []
    for p in range(IDX_HEADS // 2):
        qp = _pad_q_rows(qi_ref[0, :, p * LANES:(p + 1) * LANES], tq, tqp) * (IDX_DIM ** -0.5)
        qs.append(jnp.where(half, qp, 0.0).astype(BF16))
        qs.append(pltpu.roll(jnp.where(half, 0.0, qp), IDX_DIM, axis=1).astype(BF16))

    def score_tile(c, carry):
        kt = ki_ref[0, pl.ds(pl.multiple_of(c * tk, tk), tk), :].astype(BF16)
        sc = jnp.zeros((tk, tqp), F32)
        for h in range(IDX_HEADS):
            sc = sc + w_t[IDX_DIM + h:IDX_DIM + h + 1, :] * jnp.maximum(_nt(kt, qs[h]), 0.0)
        bits = lax.bitcast_convert_type(sc, I32)
        key = jnp.where(bits < 0, bits ^ 0x7FFFFFFF, bits)
        key = jnp.where(sc == 0.0, 0, key)
        adm = ((c * tk + krow) >> 6) <= qchunk
        key = jnp.where(adm, key, INT_MIN)
        for s_ in range(SLABS):
            key_ref[c * SLABS + s_] = key[s_ * LANES:(s_ + 1) * LANES, :]
        return carry

    lax.fori_loop(0, n_kt, score_tile, 0)

    def count(pred):
        def body(c, acc):
            for s_ in range(SLABS):
                sidx = c * SLABS + s_
                ind = jnp.where(pred(key_ref[sidx], sidx), 1.0, 0.0)
                acc = acc + jnp.sum(ind.reshape(LANES // 8, 8, tqp), axis=0)
            return acc
        acc = lax.fori_loop(0, n_kt, body, jnp.zeros((8, tqp), F32))
        return jnp.sum(acc, axis=0, keepdims=True)

    kf = float(topk)
    n_adm = count(lambda k, s: k > INT_MIN)

    def all_done(cnt_t):
        done = (cnt_t == kf) | (n_adm < kf)
        return (jnp.min(jnp.where(done, 1.0, 0.0)) > 0.0).astype(I32)

    def bit_cond(st):
        return (st[0] < 32) & (st[3] == 0)

    def bit_body(st):
        i, t_u, cnt_t, _ = st
        cand_u = t_u | lax.shift_left(jnp.int32(1), 31 - i)
        cand_s = cand_u ^ INT_MIN
        cnt = count(lambda k, s: k >= cand_s)
        take = cnt >= kf
        cnt_t = jnp.where(take, cnt, cnt_t)
        return i + 1, jnp.where(take, cand_u, t_u), cnt_t, all_done(cnt_t)

    _, t_u, cnt_ge, _ = lax.while_loop(bit_cond, bit_body,
                                       (jnp.int32(0), jnp.zeros((1, tqp), I32), n_adm, all_done(n_adm)))
    thr = t_u ^ INT_MIN
    excess = jnp.where(cnt_ge > kf, 1.0, 0.0)

    def tie_search():
        need = kf - count(lambda k, s: k > thr)

        def j_body(i, j):
            cand = j | lax.shift_left(jnp.int32(1), idx_bits - 1 - i)
            c = count(lambda k, s: (k == thr) & (srow + s * LANES < cand))
            return jnp.where(c < need, cand, j)
        return lax.fori_loop(0, idx_bits, j_body, jnp.zeros((1, tqp), I32))

    has_ties = jnp.max(excess) > 0.0

    @pl.when(has_ties)
    def _():
        j_last = tie_search()

        def write_ties(c, carry):
            for s_ in range(SLABS):
                sidx = c * SLABS + s_
                k = key_ref[sidx]
                sel = (k > thr) | ((k == thr) & (srow + sidx * LANES <= j_last))
                sel = sel & (k > INT_MIN)
                o_ref[0, sidx] = jnp.where(sel, 0.0, NEG_BIG)
            return carry

        lax.fori_loop(0, n_kt, write_ties, 0)

    @pl.when(jnp.logical_not(has_ties))
    def _():
        thr_adm = jnp.maximum(thr, INT_MIN + 1)

        def write_plain(c, carry):
            for s_ in range(SLABS):
                sidx = c * SLABS + s_
                o_ref[0, sidx] = jnp.where(key_ref[sidx] >= thr_adm, 0.0, NEG_BIG)
            return carry

        lax.fori_loop(0, n_kt, write_plain, 0)

    neg = jnp.full((LANES, tqp), NEG_BIG, F32)

    def write_inactive(s, carry):
        o_ref[0, s] = neg
        return carry

    lax.fori_loop(n_kt * SLABS, n_slabs, write_inactive, 0)


def dsa_index_mask(proj, keys, *, lp, t, q_off, topk, qi_blk, w_blk, ki_blk):
    bsz = proj.shape[0]
    n_slabs = lp // LANES
    tq = min(t, KEY_TILE)
    tqp = max(tq, LANES)
    idx_bits = max(1, (lp - 1).bit_length())
    kern = functools.partial(_dsa_index_kernel, tq=tq, tqp=tqp, n_slabs=n_slabs, q_off=q_off, topk=topk,
                             idx_bits=idx_bits)
    return pl.pallas_call(
        kern,
        grid=(bsz, t // tq),
        in_specs=[pl.BlockSpec((1, tq, 4 * LANES), lambda b, a: (b, a, qi_blk)),
                  pl.BlockSpec((1, tq, LANES), lambda b, a: (b, a, w_blk)),
                  pl.BlockSpec((1, lp, LANES), lambda b, a: (b, 0, ki_blk))],
        out_specs=pl.BlockSpec((1, n_slabs, LANES, tqp), lambda b, a: (b, 0, 0, a)),
        out_shape=jax.ShapeDtypeStruct((bsz, n_slabs, LANES, (t // tq) * tqp), F32),
        scratch_shapes=[pltpu.VMEM((n_slabs, LANES, tqp), I32)],
        compiler_params=_cparams("parallel", "parallel"),
        name="dsa_index",
    )(proj, proj, keys)


def _bias_kernel(tab_ref, diag_ref, sub_ref):
    h = pl.program_id(0)
    far = tab_ref[REL_BUCKETS // 2 - 1, h]

    def bias(rel):
        n = jnp.abs(rel)
        large = jnp.full(rel.shape, 8, I32)
        for th in (12, 16, 23, 32, 46, 64, 91):
            large = large + jnp.where(n >= th, 1, 0)
        bucket = jnp.where(rel > 0, REL_BUCKETS // 2, 0) + jnp.where(n < 8, n, large)
        acc = jnp.zeros(rel.shape, F32)
        for bk in range(REL_BUCKETS):
            acc = jnp.where(bucket == bk, tab_ref[bk, h], acc)
        return (acc - far) * LOG2E

    ik = lax.broadcasted_iota(I32, (KEY_TILE, KEY_TILE), 0)
    iq = lax.broadcasted_iota(I32, (KEY_TILE, KEY_TILE), 1)
    diag_ref[0] = bias(ik - iq)
    ik = lax.broadcasted_iota(I32, (LANES, LANES), 0)
    iq = lax.broadcasted_iota(I32, (LANES, LANES), 1)
    sub_ref[0] = bias(ik - LANES - iq)


def dsa_bias_tiles(rel_table):
    return pl.pallas_call(
        _bias_kernel,
        grid=(A_HEADS,),
        in_specs=[pl.BlockSpec(memory_space=pltpu.SMEM)],
        out_specs=[pl.BlockSpec((1, KEY_TILE, KEY_TILE), lambda h: (h, 0, 0)),
                   pl.BlockSpec((1, LANES, LANES), lambda h: (h, 0, 0))],
        out_shape=[jax.ShapeDtypeStruct((A_HEADS, KEY_TILE, KEY_TILE), F32),
                   jax.ShapeDtypeStruct((A_HEADS, LANES, LANES), F32)],
        compiler_params=_cparams("parallel"),
        name="dsa_bias",
    )(rel_table)


def _vt_rows(v_tile):
    return jnp.concatenate([v_tile.T.astype(BF16), jnp.ones((VT_ROWS - LANES, v_tile.shape[0]), BF16)], axis=0)


def _dsa_attn_kernel(q_ref, k_ref, v_ref, msk_ref, bd_ref, bs_ref, o_ref, acc_ref, m_ref, s0_ref, vt_ref,
                     *, tq, tqp, q_off, nt):
    a = pl.program_id(1)
    g = pl.program_id(2)
    tk = KEY_TILE
    q0 = q_off + a * tq
    cd = q0 // tk
    cs = jnp.maximum(cd - 1, 0)
    par = g % 2
    lane = lax.broadcasted_iota(I32, (tqp, LANES), 1)
    half = lane < HEAD_DIM

    @pl.when(a == 0)
    def _():
        for c in range(nt):
            vt_ref[g, c] = _vt_rows(v_ref[0, c * tk:(c + 1) * tk, :])

    qs = []
    for e in range(A_GROUP):
        qc = _pad_q_rows(q_ref[0, :, (e // 2) * LANES:(e // 2 + 1) * LANES], tq, tqp) * (HEAD_DIM ** -0.5 * LOG2E)
        own = jnp.where(half, qc, 0.0) if e % 2 == 0 else jnp.where(half, 0.0, qc)
        qs.append(jnp.where(par == e % 2, own, pltpu.roll(own, HEAD_DIM, axis=1)).astype(BF16))
    m_ref[...] = jnp.full(m_ref.shape, M_FLOOR, F32)
    acc_ref[...] = jnp.zeros(acc_ref.shape, F32)

    def s_tile(e, c):
        return _nt(k_ref[0, pl.ds(pl.multiple_of(c * tk, tk), tk), :].astype(BF16), qs[e])

    def softmax_pv(e, c, s, after, kind):
        s = s + jnp.concatenate([msk_ref[0, c * SLABS + i] for i in range(SLABS)], axis=0)
        if kind == "diag":
            s = s + bd_ref[e, :, 0:tqp]
        elif kind == "sub":
            corner = bs_ref[e]
            if tqp > LANES:
                corner = jnp.concatenate([corner, jnp.zeros((LANES, tqp - LANES), F32)], axis=1)
            s = s + jnp.concatenate([jnp.zeros((tk - LANES, tqp), F32), corner], axis=0)
            s = jnp.where(cd > 0, s, NEG_BIG)
        m_old = jnp.minimum(m_ref[e], jnp.maximum(after[0:1, :], -NEG_BIG))
        m_new = jnp.maximum(m_old, jnp.max(s, axis=0, keepdims=True))
        p = jnp.exp2(s - m_new).astype(BF16)
        acc_ref[e] = jnp.exp2(m_old - m_new) * acc_ref[e] + _dot(vt_ref[g, c], p)
        m_ref[e] = m_new

    def step(c, kind):
        s_prev = s0_ref[...]
        for e in range(A_GROUP):
            if e + 1 < A_GROUP:
                s_next = s_tile(e + 1, c)
            else:
                s_next = s_tile(0, jnp.minimum(c + 1, cd))
                s0_ref[...] = s_next
            softmax_pv(e, c, s_prev, s_next, kind)
            s_prev = s_next

    def far_body(c, carry):
        step(c, "far")
        return carry

    def far_pair(i, carry):
        step(2 * i, "far")
        step(2 * i + 1, "far")
        return carry

    s0_ref[...] = s_tile(0, 0)
    lax.fori_loop(0, cs // 2, far_pair, 0)
    lax.fori_loop((cs // 2) * 2, cs, far_body, 0)
    step(cs, "sub")
    step(cd, "diag")
    outs = []
    for e in range(A_GROUP):
        o_t = (acc_ref[e, 0:LANES, :] * (1.0 / acc_ref[e, LANES:LANES + 1, :])).T
        outs.append(jnp.where(par == e % 2, o_t, pltpu.roll(o_t, HEAD_DIM, axis=1)))
    for c2 in range(A_GROUP // 2):
        o_ref[0, :, c2 * LANES:(c2 + 1) * LANES] = jnp.where(half, outs[2 * c2], outs[2 * c2 + 1])[:tq]


def dsa_attention(proj, keys, values, mask, bdiag, bsub, *, lp, t, q_off, k_blk0, v_blk0):
    bsz = proj.shape[0]
    tq = min(t, KEY_TILE)
    tqp = max(tq, LANES)
    nt = lp // KEY_TILE
    nsl = mask.shape[1]
    kern = functools.partial(_dsa_attn_kernel, tq=tq, tqp=tqp, q_off=q_off, nt=nt)
    return pl.pallas_call(
        kern,
        grid=(bsz, t // tq, A_KV_HEADS),
        in_specs=[pl.BlockSpec((1, tq, 2 * LANES), lambda b, a, g: (b, a, g)),
                  pl.BlockSpec((1, lp, LANES), lambda b, a, g: (b, 0, k_blk0 + g // 2)),
                  pl.BlockSpec((1, lp, LANES), lambda b, a, g: (b, 0, v_blk0 + g // 2)),
                  pl.BlockSpec((1, nsl, LANES, tqp), lambda b, a, g: (b, 0, 0, a)),
                  pl.BlockSpec((A_GROUP, KEY_TILE, tqp), lambda b, a, g: (g, 0, 0)),
                  pl.BlockSpec((A_GROUP, LANES, LANES), lambda b, a, g: (g, 0, 0))],
        out_specs=pl.BlockSpec((1, tq, 2 * LANES), lambda b, a, g: (b, a, g)),
        out_shape=jax.ShapeDtypeStruct((bsz, t, A_HEADS * HEAD_DIM), F32),
        scratch_shapes=[pltpu.VMEM((A_GROUP, VT_ROWS, tqp), F32), pltpu.VMEM((A_GROUP, 1, tqp), F32),
                        pltpu.VMEM((KEY_TILE, tqp), F32),
                        pltpu.VMEM((A_KV_HEADS, nt, VT_ROWS, KEY_TILE), BF16)],
        compiler_params=_cparams("parallel", "arbitrary", "arbitrary"),
        name="dsa_attn",
    )(proj, keys, values, mask, bdiag, bsub)


def _fox_decay_kernel(*refs, n_past, n_new, t):
    if n_past:
        fz_ref, bf_ref, past_ref, tri_ref, lf_ref, nck_ref = refs
    else:
        fz_ref, bf_ref, tri_ref, lf_ref, nck_ref = refs
        past_ref = None
    tri = tri_ref[...]
    lane = lax.broadcasted_iota(I32, (B_HEADS, LANES), 1)
    carry = jnp.zeros((B_HEADS, 1), F32)
    for blk in range(n_past + n_new):
        sl = slice(blk * LANES, (blk + 1) * LANES)
        if blk < n_past:
            lf = past_ref[0, :, sl]
        else:
            nsl = slice((blk - n_past) * LANES, (blk - n_past + 1) * LANES)
            x = fz_ref[0, :, nsl] + bf_ref[...]
            lf = jnp.minimum(x, 0.0) - jnp.log1p(jnp.exp(-jnp.abs(x)))
            lf = jnp.where(lane + (blk - n_past) * LANES < t, lf, 0.0)
            lf_ref[0, :, nsl] = lf
        cum = _dot_x01(lf, tri) + carry
        for i, piece in enumerate(_split3(cum * -LOG2E)):
            nck_ref[0, i, :, sl] = piece
        carry = cum[:, LANES - 1:LANES]


def fox_decay(fz_t, b_f, past_t, *, t):
    bsz, h, tp = fz_t.shape
    p = 0 if past_t is None else past_t.shape[2]
    n_past, n_new = p // LANES, tp // LANES
    tri = (jnp.arange(LANES)[:, None] <= jnp.arange(LANES)[None, :]).astype(BF16)
    kern = functools.partial(_fox_decay_kernel, n_past=n_past, n_new=n_new, t=t)
    args = [fz_t, b_f.reshape(h, 1)]
    in_specs = [pl.BlockSpec((1, h, tp), lambda b: (b, 0, 0)),
                pl.BlockSpec((h, 1), lambda b: (0, 0))]
    if n_past:
        args.append(past_t)
        in_specs.append(pl.BlockSpec((1, h, p), lambda b: (b, 0, 0)))
    args.append(tri)
    in_specs.append(pl.BlockSpec((LANES, LANES), lambda b: (0, 0)))
    return pl.pallas_call(
        kern,
        grid=(bsz,),
        in_specs=in_specs,
        out_specs=[pl.BlockSpec((1, h, tp), lambda b: (b, 0, 0)),
                   pl.BlockSpec((1, 3, h, p + tp), lambda b: (b, 0, 0, 0))],
        out_shape=[jax.ShapeDtypeStruct((bsz, h, tp), F32),
                   jax.ShapeDtypeStruct((bsz, 3, h, p + tp), BF16)],
        compiler_params=_cparams("parallel"),
        name="fox_decay",
    )(*args)


def _fox_attn_kernel(q_ref, k_ref, v_ref, pz_ref, g_ref, o_ref, acc_ref, m_ref, s0_ref, ka_ref, vt_ref,
                     *, tq, tqp, q_off, nt):
    j = pl.program_id(1)
    a = pl.program_id(2)
    tk = KEY_TILE
    q0 = q_off + a * tq
    n_full = q0 // tk
    n_need = (q0 + tq - 1) // tk + 1

    @pl.when(a == 0)
    def _():
        klane = lax.broadcasted_iota(I32, (tk, LANES), 1)
        for c in range(nt):
            rows = slice(c * tk, (c + 1) * tk)
            kp = k_ref[0, rows, :]
            pz = pz_ref[0, rows, :].astype(F32)
            ka_ref[0, rows, :] = jnp.where(klane < HEAD_DIM, kp, pz).astype(BF16)
            ka_ref[1, rows, :] = jnp.where(klane >= HEAD_DIM, kp, pz).astype(BF16)
            vt_ref[c] = _vt_rows(v_ref[0, rows, :])

    lane = lax.broadcasted_iota(I32, (tqp, LANES), 1)
    qn = _pad_q_rows(q_ref[0], tq, tqp) * (HEAD_DIM ** -0.5 * LOG2E)
    ones_e = (lane >= HEAD_DIM + 3 * j) & (lane < HEAD_DIM + 3 * j + 3)
    ones_o = (lane >= 3 * j) & (lane < 3 * j + 3)
    qs = (jnp.where(lane < HEAD_DIM, qn, jnp.where(ones_e, 1.0, 0.0)).astype(BF16),
          jnp.where(lane >= HEAD_DIM, qn, jnp.where(ones_o, 1.0, 0.0)).astype(BF16))
    m_ref[...] = jnp.full(m_ref.shape, M_FLOOR, F32)
    acc_ref[...] = jnp.zeros(acc_ref.shape, F32)
    krow = lax.broadcasted_iota(I32, (tk, tqp), 0)
    qcol = lax.broadcasted_iota(I32, (tk, tqp), 1)

    def s_tile(e, c):
        return _nt(ka_ref[e, pl.ds(pl.multiple_of(c * tk, tk), tk), :], qs[e])

    def softmax_pv(e, c, s, masked):
        if masked:
            s = jnp.where(c * tk + krow <= q0 + qcol, s, NEG_BIG)
        m_old = m_ref[e]
        m_new = jnp.maximum(m_old, jnp.max(s, axis=0, keepdims=True))
        p = jnp.exp2(s - m_new).astype(BF16)
        acc_ref[e] = jnp.exp2(m_old - m_new) * acc_ref[e] + _dot(vt_ref[c], p)
        m_ref[e] = m_new

    def step(c, masked):
        s1 = s_tile(1, c)
        softmax_pv(0, c, s0_ref[...], masked)
        s0_ref[...] = s_tile(0, jnp.minimum(c + 1, n_need - 1))
        softmax_pv(1, c, s1, masked)

    def full_body(c, carry):
        step(c, False)
        return carry

    def masked_body(c, carry):
        step(c, True)
        return carry

    def pair_body(i, carry):
        step(2 * i, False)
        step(2 * i + 1, False)
        return carry

    s0_ref[...] = s_tile(0, 0)
    lax.fori_loop(0, n_full // 2, pair_body, 0)
    lax.fori_loop((n_full // 2) * 2, n_full, full_body, 0)
    lax.fori_loop(n_full, n_need, masked_body, 0)
    o_e = (acc_ref[0, 0:LANES, :] * (1.0 / acc_ref[0, LANES:LANES + 1, :])).T
    o_o = (acc_ref[1, 0:LANES, :] * (1.0 / acc_ref[1, LANES:LANES + 1, :])).T
    o = jnp.where(lane < HEAD_DIM, o_e, o_o)
    o_ref[0] = o[:tq] * (1.0 / (1.0 + jnp.exp(-g_ref[0])))


def fox_attention(proj, keys, values, pz, *, lp, t, q_off, g_blk0, k_blk0, v_blk0):
    bsz = proj.shape[0]
    tq = min(t, KEY_TILE)
    tqp = max(tq, LANES)
    nt = lp // KEY_TILE
    kern = functools.partial(_fox_attn_kernel, tq=tq, tqp=tqp, q_off=q_off, nt=nt)
    return pl.pallas_call(
        kern,
        grid=(bsz, B_HEADS // 2, t // tq),
        in_specs=[pl.BlockSpec((1, tq, LANES), lambda b, j, a: (b, a, j)),
                  pl.BlockSpec((1, lp, LANES), lambda b, j, a: (b, 0, k_blk0 + j)),
                  pl.BlockSpec((1, lp, LANES), lambda b, j, a: (b, 0, v_blk0 + j)),
                  pl.BlockSpec((1, lp, LANES), lambda b, j, a: (b, 0, 0)),
                  pl.BlockSpec((1, tq, LANES), lambda b, j, a: (b, a, g_blk0 + j))],
        out_specs=pl.BlockSpec((1, tq, LANES), lambda b, j, a: (b, a, j)),
        out_shape=jax.ShapeDtypeStruct((bsz, t, B_HEADS * HEAD_DIM), F32),
        scratch_shapes=[pltpu.VMEM((2, VT_ROWS, tqp), F32), pltpu.VMEM((2, 1, tqp), F32),
                        pltpu.VMEM((KEY_TILE, tqp), F32),
                        pltpu.VMEM((2, lp, LANES), BF16),
                        pltpu.VMEM((nt, VT_ROWS, KEY_TILE), BF16)],
        compiler_params=_cparams("parallel", "parallel", "arbitrary"),
        name="fox_attn",
    )(proj, keys, values, pz, proj)


def _hgrn2_levels(tc):
    lv = []
    n = 8
    while n < tc:
        lv.append(n)
        n *= 2
    return lv


def _hgrn2_masks(tc):
    t = jnp.arange(tc)[:, None]
    s = jnp.arange(tc)[None, :]
    ms = [((t // (2 * n) == s // (2 * n)) & ((t // n) % 2 == 1) & ((s // n) % 2 == 0)) for n in _hgrn2_levels(tc)]
    ms.append((t // 8 == s // 8) & (s <= t))
    return jnp.stack(ms).astype(F32)


def _hgrn2_head(q, z, v, g, lb, og, st, tri_ref, msk_ref, tc):
    ez = jnp.exp(-jnp.abs(z))
    den = 1.0 / (1.0 + ez)
    pos = z >= 0.0
    f = lb + (1.0 - lb) * (jnp.where(pos, 1.0, ez) * den)
    kk = (1.0 - lb) * (jnp.where(pos, ez, 1.0) * den)
    cum = _dot_01x(tri_ref[...], jnp.log(f))

    def rows(idx):
        parts = []
        for i in idx:
            parts.append(jnp.zeros((8, LANES), F32) if i < 0 else jnp.broadcast_to(cum[i:i + 1, :], (8, LANES)))
        return jnp.concatenate(parts, axis=0)

    levels = _hgrn2_levels(tc)
    ngrp = tc // 8
    scores = jnp.zeros((tc, tc), F32)
    ql8 = None
    for li, n in enumerate(levels):
        start = [((r * 8) // n) * n for r in range(ngrp)]
        a_start = rows([s - 1 for s in start])
        a_end = rows([s + n - 1 for s in start])
        ql = (q * jnp.exp(cum - a_start)).astype(BF16)
        kr = (kk * jnp.exp(a_end - cum)).astype(BF16)
        scores = scores + msk_ref[li] * _nt(ql, kr)
        if n == 8:
            ql8 = ql
            kb = (kk * jnp.exp(a_start - cum)).astype(BF16)
    if ql8 is None:
        a_start = rows([r * 8 - 1 for r in range(ngrp)])
        ql8 = (q * jnp.exp(cum - a_start)).astype(BF16)
        kb = (kk * jnp.exp(a_start - cum)).astype(BF16)
    scores = scores + msk_ref[len(levels)] * _nt(ql8, kb)

    o = _nt((q * jnp.exp(cum)).astype(BF16), st.astype(BF16)) + _dot(scores.astype(BF16), v.astype(BF16))
    a_last = cum[tc - 1:tc, :]
    khat = (kk * jnp.exp(a_last - cum)).astype(BF16)
    st_new = st * jnp.exp(a_last) + _dot(v.T.astype(BF16), khat)
    on = o * lax.rsqrt(jnp.mean(o * o, axis=-1, keepdims=True) + EPS) * og
    return on * (g * (1.0 / (1.0 + jnp.exp(-g)))), st_new


def _hgrn2_kernel(q_ref, fz_ref, v_ref, g_ref, lb_ref, og_ref, s0_ref, tri_ref, msk_ref, y_ref, so_ref, st_ref, *, tc):
    ct = pl.program_id(2)

    @pl.when(ct == 0)
    def _():
        st_ref[...] = s0_ref[0]

    for e in range(HGRN2_HEADS_PER_STEP):
        sl = slice(e * LANES, (e + 1) * LANES)
        y, st_new = _hgrn2_head(q_ref[0, :, sl], fz_ref[0, :, sl], v_ref[0, :, sl], g_ref[0, :, sl], lb_ref[e],
                                og_ref[...], st_ref[e], tri_ref, msk_ref, tc)
        y_ref[0, :, sl] = y
        st_ref[e] = st_new

    @pl.when(ct == pl.num_programs(2) - 1)
    def _():
        so_ref[0] = st_ref[...]


def hgrn2_recurrence(proj, lb, out_gain, s0_t, *, t):
    bsz = proj.shape[0]
    tc = min(t, 128)
    nlv = len(_hgrn2_levels(tc)) + 1
    tri = (jnp.arange(tc)[:, None] >= jnp.arange(tc)[None, :]).astype(BF16)
    h = C_HEADS
    hps = HGRN2_HEADS_PER_STEP
    ng = h // hps
    kern = functools.partial(_hgrn2_kernel, tc=tc)
    blk = lambda off: pl.BlockSpec((1, tc, hps * LANES), lambda b, hh, c: (b, c, off + hh))
    return pl.pallas_call(
        kern,
        grid=(bsz, ng, t // tc),
        in_specs=[blk(0), blk(ng), blk(2 * ng), blk(3 * ng),
                  pl.BlockSpec((hps, 1, C_DK), lambda b, hh, c: (hh, 0, 0)),
                  pl.BlockSpec((1, C_DV), lambda b, hh, c: (0, 0)),
                  pl.BlockSpec((1, hps, C_DV, C_DK), lambda b, hh, c: (b, hh, 0, 0)),
                  pl.BlockSpec((tc, tc), lambda b, hh, c: (0, 0)),
                  pl.BlockSpec((nlv, tc, tc), lambda b, hh, c: (0, 0, 0))],
        out_specs=[pl.BlockSpec((1, tc, hps * LANES), lambda b, hh, c: (b, c, hh)),
                   pl.BlockSpec((1, hps, C_DV, C_DK), lambda b, hh, c: (b, hh, 0, 0))],
        out_shape=[jax.ShapeDtypeStruct((bsz, t, h * C_DV), F32),
                   jax.ShapeDtypeStruct((bsz, h, C_DV, C_DK), F32)],
        scratch_shapes=[pltpu.VMEM((hps, C_DV, C_DK), F32)],
        compiler_params=_cparams("parallel", "parallel", "arbitrary"),
        name="hgrn2",
    )(proj, proj, proj, proj, lb.reshape(h, 1, C_DK), out_gain.reshape(1, C_DV), s0_t, tri, _hgrn2_masks(tc))


def _route_t(h, wr_t, br_t):
    ah, am, al = _split3(wr_t)
    bh, bm, bl = _split3(h)
    r = _nt(ah, bh) + (_nt(ah, bm) + _nt(am, bh)) + (_nt(ah, bl) + _nt(am, bm) + _nt(al, bh)) + br_t
    row = lax.broadcasted_iota(I32, r.shape, 0)
    rowf = row.astype(F32)
    big = float(ROUTER_ROWS)
    is_g = row < N_GROUPS
    lg = jnp.where(is_g, r, -jnp.inf)
    mg = jnp.max(lg, axis=0, keepdims=True)
    grp = jnp.min(jnp.where(lg == mg, rowf, big), axis=0, keepdims=True)
    p_grp = 1.0 / jnp.sum(jnp.where(is_g, jnp.exp(r - mg), 0.0), axis=0, keepdims=True)
    eg = ((row - N_GROUPS) >> 2).astype(F32)
    in_e = (row >= N_GROUPS) & (row < N_GROUPS + N_EXPERTS) & (eg == grp)
    le = jnp.where(in_e, r, -jnp.inf)
    v1 = jnp.max(le, axis=0, keepdims=True)
    i1 = jnp.min(jnp.where(le == v1, rowf, big), axis=0, keepdims=True)
    le2 = jnp.where(rowf == i1, -jnp.inf, le)
    v2 = jnp.max(le2, axis=0, keepdims=True)
    i2 = jnp.min(jnp.where(le2 == v2, rowf, big), axis=0, keepdims=True)
    e2 = jnp.exp(v2 - v1)
    w1 = 1.0 / (1.0 + e2)
    gates = jnp.where(rowf == i1, w1 * p_grp, 0.0) + jnp.where(rowf == i2, (e2 * w1) * p_grp, 0.0)
    member = jnp.where((rowf == i1) | (rowf == i2), 1.0, 0.0)
    return gates, member


def _moe_kernel(x_ref, gain_ref, sc_ref, sh_ref, g2_ref, wr_ref, br_ref, tri_ref, wg_ref, wu_ref, wd_ref, o_ref,
                hb_ref, rank_ref, rt_ref, gt_ref, y_ref, acc_ref, *, ts, slot):
    step = pl.program_id(2)
    tm = x_ref.shape[1]
    n_sub = tm // ts
    tsp = ts
    n_chunk = tsp // slot

    @pl.when(step == 0)
    def _():
        for s in range(n_sub):
            rows = slice(s * ts, (s + 1) * ts)
            h = _ln_mod(x_ref[0, rows, :], gain_ref[...], _mod_rows(sc_ref, rows), _mod_rows(sh_ref, rows))
            hb_ref[rows, :] = h.astype(BF16)
            gates_t, member_t = _route_t(h, wr_ref[...], br_ref[...])
            r_t = _dot(member_t.astype(BF16), tri_ref[...])
            r_t = jnp.where(member_t > 0.0, r_t, -1.0)
            rt_ref[s] = r_t
            gt_ref[s] = gates_t
            rank_ref[s] = jnp.concatenate([r_t, jnp.full((LANES - ROUTER_ROWS, ts), -1.0, F32)], axis=0).T
        if n_chunk > 1:
            acc_ref[...] = jnp.zeros_like(acc_ref)

    def expert_rows(k, ee):
        row = step * MOE_EXPERTS_PER_STEP + ee + N_GROUPS
        xs, ges, sels = [], [], []
        slot_i = lax.broadcasted_iota(I32, (slot, tsp), 0).astype(F32) + float(k * slot)
        for s in range(n_sub):
            sel = jnp.where(rt_ref[s, pl.ds(row, 1), :] == slot_i, 1.0, 0.0)
            ges.append(jnp.sum(sel * gt_ref[s, pl.ds(row, 1), :], axis=-1, keepdims=True))
            xs.append(_dot(sel.astype(BF16), hb_ref[s * tsp:(s + 1) * tsp, :]).astype(BF16))
            sels.append(sel)
        xa = jnp.concatenate(xs, axis=0)
        a = _dot(xa, wg_ref[ee])
        u = _dot(xa, wu_ref[ee])
        he = (a * (1.0 / (1.0 + jnp.exp(-a)))) * u * jnp.concatenate(ges, axis=0)
        return _dot(he.astype(BF16), wd_ref[ee]), sels

    for ee in range(MOE_EXPERTS_PER_STEP):
        e = step * MOE_EXPERTS_PER_STEP + ee
        ye, _ = expert_rows(0, ee)
        for s in range(n_sub):
            y_ref[s, pl.ds(pl.multiple_of(e * slot, slot), slot), :] = ye[s * slot:(s + 1) * slot].astype(BF16)

    for ee in range(MOE_EXPERTS_PER_STEP):
        row = step * MOE_EXPERTS_PER_STEP + ee + N_GROUPS
        for k in range(1, n_chunk):
            last_rank = jnp.max(rt_ref[0, pl.ds(row, 1), :])
            for s in range(1, n_sub):
                last_rank = jnp.maximum(last_rank, jnp.max(rt_ref[s, pl.ds(row, 1), :]))

            @pl.when(last_rank >= float(k * slot))
            def _(k=k, ee=ee):
                ye_k, sels = expert_rows(k, ee)
                for s in range(n_sub):
                    acc_ref[s * tsp:(s + 1) * tsp, :] += _dot(sels[s].T.astype(BF16),
                                                              ye_k[s * slot:(s + 1) * slot].astype(BF16))

    @pl.when(step == pl.num_programs(2) - 1)
    def _():
        slot_l = lax.broadcasted_iota(I32, (1, slot), 1).astype(F32)
        for s in range(n_sub):
            rows = slice(s * ts, (s + 1) * ts)
            rank = rank_ref[s]
            sel_t = jnp.concatenate(
                [jnp.where(rank[:, N_GROUPS + j:N_GROUPS + j + 1] == slot_l, 1.0, 0.0).astype(BF16)
                 for j in range(N_EXPERTS)], axis=1)
            y = _dot(sel_t, y_ref[s])
            if n_chunk > 1:
                y = y + acc_ref[s * tsp:(s + 1) * tsp, :]
            o_ref[0, rows, :] = x_ref[0, rows, :] + _mod_rows(g2_ref, rows) * y


def moe_layer(x, gain, sc, sh, g2, w_router, b_router, w_gate, w_up, w_down):
    bsz, t, d = x.shape
    if t % LANES:
        y = moe_layer(_merge(x), gain, _per_token(sc, t), _per_token(sh, t), _per_token(g2, t),
                      w_router, b_router, w_gate, w_up, w_down)
        return y.reshape(x.shape)
    tm = min(t, 1024 if sc.shape[1] == 1 else 512)
    ts = min(tm, 512)
    assert ts % LANES == 0 and t % tm == 0, (t, tm, ts)
    tsp = ts
    slot = MOE_SLOT
    eps = MOE_EXPERTS_PER_STEP
    ne, _, de = w_gate.shape
    tri = (jnp.arange(tsp)[:, None] < jnp.arange(tsp)[None, :]).astype(BF16)
    kern = functools.partial(_moe_kernel, ts=ts, slot=slot)
    n_sub = tm // ts
    return pl.pallas_call(
        kern,
        grid=(bsz, t // tm, ne // eps),
        in_specs=[pl.BlockSpec((1, tm, d), lambda b, i, e: (b, i, 0)),
                  pl.BlockSpec((1, d), lambda b, i, e: (0, 0)),
                  _mod_spec(sc, tm, 3),
                  _mod_spec(sh, tm, 3),
                  _mod_spec(g2, tm, 3),
                  pl.BlockSpec((ROUTER_ROWS, d), lambda b, i, e: (0, 0)),
                  pl.BlockSpec((ROUTER_ROWS, 1), lambda b, i, e: (0, 0)),
                  pl.BlockSpec((tsp, tsp), lambda b, i, e: (0, 0)),
                  pl.BlockSpec((eps, d, de), lambda b, i, e: (e, 0, 0)),
                  pl.BlockSpec((eps, d, de), lambda b, i, e: (e, 0, 0)),
                  pl.BlockSpec((eps, de, d), lambda b, i, e: (e, 0, 0))],
        out_specs=pl.BlockSpec((1, tm, d), lambda b, i, e: (b, i, 0)),
        out_shape=jax.ShapeDtypeStruct((bsz, t, d), F32),
        scratch_shapes=[pltpu.VMEM((n_sub * tsp, d), BF16),
                        pltpu.VMEM((n_sub, tsp, LANES), F32),
                        pltpu.VMEM((n_sub, ROUTER_ROWS, tsp), F32),
                        pltpu.VMEM((n_sub, ROUTER_ROWS, tsp), F32),
                        pltpu.VMEM((n_sub, ne * slot, d), BF16),
                        pltpu.VMEM((n_sub * tsp if tsp > slot else 8, d), F32)],
        compiler_params=_cparams("parallel", "parallel", "arbitrary"),
        name="moe",
    )(x, gain.reshape(1, d), sc, sh, g2, w_router, b_router, tri, w_gate, w_up, w_down)


def _pad_cols(w, n):
    return jnp.pad(w, ((0, 0), (0, n - w.shape[1])))


def _pad_rows(a, n):
    return jnp.pad(a, ((0, 0), (0, n - a.shape[1])) + ((0, 0),) * (a.ndim - 2))


def _round_up(n, m):
    return -(-n // m) * m


def _head_cols(gain_q, nq, gain_k, nk, npad):
    cg = jnp.concatenate([jnp.tile(gain_q, nq), jnp.tile(gain_k, nk)])
    n = cg.shape[0]
    cgain = jnp.pad(cg, (0, npad - n)).reshape(1, npad)
    cflag = (jnp.arange(npad) < n).astype(F32).reshape(1, npad)
    return cgain, cflag


def _dsa_layer(x, mod, past, prm, bdiag, bsub):
    sh1, sc1, g1 = mod
    past_k, past_v, past_ki = past
    bsz, t, d = x.shape
    p = 0 if past_k is None else past_k.shape[1]
    n_keys = p + t
    topk = min(TOPK_MAX, n_keys // 4)
    tn = 768
    n_in = prm['w_in'].shape[1]
    npad = _round_up(n_in, tn)
    nqk = (A_HEADS + A_KV_HEADS) * HEAD_DIM
    w = _pad_cols(prm['w_in'], npad).astype(BF16)
    cgain, cflag = _head_cols(prm['q_norm'], A_HEADS, prm['k_norm'], A_KV_HEADS, npad)
    proj = ln_proj(x, prm['norm'], sc1, sh1, w, cgain, cflag, -(-nqk // tn), tn)
    o_k = A_HEADS * HEAD_DIM
    o_v = o_k + A_KV_HEADS * HEAD_DIM
    o_qi = o_v + A_KV_HEADS * HEAD_DIM
    o_ki = o_qi + IDX_HEADS * IDX_DIM
    k_new = proj[..., o_k:o_v].reshape(bsz, t, A_KV_HEADS, HEAD_DIM)
    v_new = proj[..., o_v:o_qi].reshape(bsz, t, A_KV_HEADS, HEAD_DIM)
    ki_new = proj[..., o_ki:o_ki + IDX_DIM]
    lp = _round_up(n_keys, KEY_TILE)
    kvw = A_KV_HEADS * HEAD_DIM
    if p:
        keys = _pad_rows(jnp.concatenate([past_k.reshape(bsz, p, kvw), proj[..., o_k:o_v]], axis=1), lp)
        values = _pad_rows(jnp.concatenate([past_v.reshape(bsz, p, kvw), proj[..., o_v:o_qi]], axis=1), lp)
        ikeys = _pad_rows(jnp.concatenate([past_ki, ki_new], axis=1), lp)
        ikeys = jnp.pad(ikeys, ((0, 0), (0, 0), (0, LANES - IDX_DIM)))
        k_blk0, v_blk0, ki_blk = 0, 0, 0
    else:
        keys = values = ikeys = proj
        k_blk0, v_blk0, ki_blk = o_k // LANES, o_v // LANES, o_ki // LANES
    mask = dsa_index_mask(proj, ikeys, lp=lp, t=t, q_off=p, topk=topk, qi_blk=o_qi // (4 * LANES),
                          w_blk=o_ki // LANES, ki_blk=ki_blk)
    o = dsa_attention(proj, keys, values, mask, bdiag, bsub, lp=lp, t=t, q_off=p, k_blk0=k_blk0, v_blk0=v_blk0)
    x = out_proj_residual(o, prm['w_out'].astype(BF16), x, g1)
    return x, (k_new, v_new, ki_new)


def _fox_layer(x, mod, past, prm):
    sh1, sc1, g1 = mod
    past_k, past_v, past_lf = past
    bsz, t, d = x.shape
    p = 0 if past_k is None else past_k.shape[1]
    n_keys = p + t
    hd = B_HEADS * HEAD_DIM
    tn = 512
    npad = _round_up(prm['w_in'].shape[1], tn)
    w = _pad_cols(prm['w_in'], npad).astype(BF16)
    cgain, cflag = _head_cols(prm['q_norm'], B_HEADS, prm['k_norm'], B_HEADS, npad)
    proj = ln_proj(x, prm['norm'], sc1, sh1, w, cgain, cflag, 2 * hd // tn, tn)
    k_new = proj[..., hd:2 * hd]
    v_new = proj[..., 2 * hd:3 * hd]
    fz = proj[..., 4 * hd:4 * hd + B_HEADS]
    tp = _round_up(t, LANES)
    fz_t = _pad_rows(fz, tp).transpose(0, 2, 1)
    past_t = None if not p else past_lf.transpose(0, 2, 1)
    lf_t, nck = fox_decay(fz_t, prm['forget_bias'], past_t, t=t)
    logf_new = lf_t[:, :, :t].transpose(0, 2, 1)
    lp = _round_up(n_keys, KEY_TILE)
    if p:
        keys = _pad_rows(jnp.concatenate([past_k.reshape(bsz, p, hd), k_new], axis=1), lp)
        values = _pad_rows(jnp.concatenate([past_v.reshape(bsz, p, hd), v_new], axis=1), lp)
        k_blk0, v_blk0 = 0, 0
    else:
        keys = values = proj
        k_blk0, v_blk0 = hd // LANES, 2 * hd // LANES
    pieces = jnp.pad(nck, ((0, 0), (0, 0), (0, 0), (0, lp - nck.shape[3]))).transpose(0, 3, 2, 1)
    zl = jnp.zeros((bsz, lp, HEAD_DIM - 3 * B_HEADS // 2), pieces.dtype)
    pz = jnp.concatenate([pieces[:, :, 1::2].reshape(bsz, lp, -1), zl,
                          pieces[:, :, 0::2].reshape(bsz, lp, -1), zl], axis=-1)
    o = fox_attention(proj, keys, values, pz, lp=lp, t=t, q_off=p, g_blk0=3 * hd // LANES,
                      k_blk0=k_blk0, v_blk0=v_blk0)
    x = out_proj_residual(o, prm['w_out'].astype(BF16), x, g1)
    return x, (k_new.reshape(bsz, t, B_HEADS, HEAD_DIM), v_new.reshape(bsz, t, B_HEADS, HEAD_DIM), logf_new)


def _hgrn2_layer(x, mod, s0, prm):
    sh1, sc1, g1 = mod
    bsz, t, d = x.shape
    npad = prm['w_in'].shape[1]
    zeros = jnp.zeros((1, npad), F32)
    proj = ln_proj(x, prm['norm'], sc1, sh1, prm['w_in'].astype(BF16), zeros, zeros, 0, 512)
    y, s_t = hgrn2_recurrence(proj, prm['lb'], prm['out_norm'], jnp.swapaxes(s0, -1, -2), t=t)
    x = out_proj_residual(y, prm['w_out'].astype(BF16), x, g1)
    return x, jnp.swapaxes(s_t, -1, -2)


def _trunk(x, c, a_k, a_v, a_kidx, b_k, b_v, b_logf, c_state, prm):
    bsz, t, d = x.shape
    mod_all = ada_mod(c, prm['w_ada'], prm['b_ada'])
    lb_all = jnp.cumsum(jax.nn.softmax(prm['c_lower_bound'].astype(F32), axis=0), axis=0)
    lb_all = lb_all - lb_all[0]
    bdiag, bsub = dsa_bias_tiles(prm['rel_table'])
    out_a, out_b, out_c = [], [], []
    for i in range(DEPTH):
        j = i // N_MIXERS
        kind = i % N_MIXERS
        sh1, sc1, g1, sh2, sc2, g2 = [m.reshape(bsz, 1, d) for m in jnp.split(mod_all[i], 6, axis=-1)]
        mod = (sh1, sc1, g1)
        if kind == 0:
            past = (None, None, None) if a_k is None else (a_k[j], a_v[j], a_kidx[j])
            lp = dict(norm=prm['norm_mix'][i], w_in=prm['a_w_in'][j], q_norm=prm['a_q_norm'][j],
                      k_norm=prm['a_k_norm'][j], w_out=prm['a_w_out'][j])
            x, new = _dsa_layer(x, mod, past, lp, bdiag, bsub)
            out_a.append(new)
        elif kind == 1:
            past = (None, None, None) if b_k is None else (b_k[j], b_v[j], b_logf[j])
            lp = dict(norm=prm['norm_mix'][i], w_in=prm['b_w_in'][j], forget_bias=prm['b_forget_bias'][j],
                      q_norm=prm['b_q_norm'][j], k_norm=prm['b_k_norm'][j], w_out=prm['b_w_out'][j])
            x, new = _fox_layer(x, mod, past, lp)
            out_b.append(new)
        else:
            s0 = jnp.zeros((bsz, C_HEADS, C_DK, C_DV), F32) if c_state is None else c_state[j]
            lp = dict(norm=prm['norm_mix'][i], w_in=prm['c_w_in'][j], lb=lb_all[i],
                      out_norm=prm['c_out_norm'][j], w_out=prm['c_w_out'][j])
            x, new = _hgrn2_layer(x, mod, s0, lp)
            out_c.append(new)
        w_router = jnp.pad(jnp.concatenate([prm['moe_w_group'][i], prm['moe_w_expert'][i]], axis=1).T,
                           ((0, ROUTER_ROWS - N_GROUPS - N_EXPERTS), (0, 0)))
        b_router = jnp.pad(jnp.concatenate([prm['moe_b_group'][i], prm['moe_b_expert'][i]]),
                           (0, ROUTER_ROWS - N_GROUPS - N_EXPERTS)).reshape(ROUTER_ROWS, 1)
        x = moe_layer(x, prm['norm_ffn'][i], sc2, sh2, g2, w_router, b_router,
                      prm['moe_w_gate'][i].astype(BF16), prm['moe_w_up'][i].astype(BF16),
                      prm['moe_w_down'][i].astype(BF16))
    stack = lambda outs, k: jnp.stack([o[k] for o in outs])
    return (x, stack(out_a, 0), stack(out_a, 1), stack(out_a, 2),
            stack(out_b, 0), stack(out_b, 1), stack(out_b, 2), jnp.stack(out_c))


def kernel(x_prompt, x_sample, cache_a_k, cache_a_v, cache_a_kidx, cache_b_k, cache_b_v, cache_b_logf, state_c,
           c_prompt, c_sample, rel_table, w_ada, b_ada, norm_mix, norm_ffn, a_w_in, a_q_norm, a_k_norm, a_w_out,
           b_w_in, b_forget_bias, b_q_norm, b_k_norm, b_w_out, c_w_in, c_lower_bound, c_out_norm, c_w_out,
           moe_w_group, moe_b_group, moe_w_expert, moe_b_expert, moe_w_gate, moe_w_up, moe_w_down):
    prm = {'rel_table': rel_table, 'w_ada': w_ada, 'b_ada': b_ada, 'norm_mix': norm_mix, 'norm_ffn': norm_ffn,
           'a_w_in': a_w_in, 'a_q_norm': a_q_norm, 'a_k_norm': a_k_norm, 'a_w_out': a_w_out,
           'b_w_in': b_w_in, 'b_forget_bias': b_forget_bias, 'b_q_norm': b_q_norm, 'b_k_norm': b_k_norm,
           'b_w_out': b_w_out, 'c_w_in': c_w_in, 'c_lower_bound': c_lower_bound, 'c_out_norm': c_out_norm,
           'c_w_out': c_w_out, 'moe_w_group': moe_w_group, 'moe_b_group': moe_b_group,
           'moe_w_expert': moe_w_expert, 'moe_b_expert': moe_b_expert, 'moe_w_gate': moe_w_gate,
           'moe_w_up': moe_w_up, 'moe_w_down': moe_w_down}
    (y_p, ak_p, av_p, ai_p, bk_p, bv_p, bl_p, cs_p) = _trunk(
        x_prompt, c_prompt, None, None, None, None, None, None, None, prm)
    (y_s, ak_s, av_s, ai_s, bk_s, bv_s, bl_s, cs_s) = _trunk(
        x_sample, c_sample, cache_a_k, cache_a_v, cache_a_kidx, cache_b_k, cache_b_v, cache_b_logf, state_c, prm)
    return (y_p, y_s, ak_p, av_p, ai_p, ak_s, av_s, ai_s, bk_p, bv_p, bl_p, bk_s, bv_s, bl_s, cs_p, cs_s)
```

```python
import functools

import jax
import jax.numpy as jnp
from jax import lax
from jax.experimental import pallas as pl
from jax.experimental.pallas import tpu as pltpu

F32 = jnp.float32
BF16 = jnp.bfloat16
I32 = jnp.int32

LANES = 128
VMEM_LIMIT_BYTES = 56 * 1024 * 1024

DEPTH = 4
N_MIXERS = 3
CHUNK = 64
EPS = 1e-6
HEAD_DIM = 64
A_HEADS = 16
A_KV_HEADS = 4
A_GROUP = A_HEADS // A_KV_HEADS
IDX_HEADS = 8
IDX_DIM = 64
TOPK_MAX = 256
REL_BUCKETS = 32
B_HEADS = 16
C_HEADS = 8
C_DK = 128
C_DV = 128
N_GROUPS = 4
EXPERTS_PER_GROUP = 4
N_EXPERTS = 16
D_EXPERT = 512

LOG2E = 1.4426950408889634
NEG_BIG = -1e30
M_FLOOR = -1e20
INT_MIN = -2 ** 31
KEY_TILE = 512
SLABS = KEY_TILE // LANES
MOE_SLOT = 128
VT_ROWS = LANES + 16
ROUTER_ROWS = 32
HGRN2_HEADS_PER_STEP = 8
MOE_EXPERTS_PER_STEP = 2


def _cparams(*sem):
    return pltpu.CompilerParams(dimension_semantics=sem, vmem_limit_bytes=VMEM_LIMIT_BYTES)


def _nt(a, b):
    return lax.dot_general(a, b, (((1,), (1,)), ((), ())), preferred_element_type=F32)


def _split3(x):
    hi = x.astype(BF16)
    r = x - hi.astype(F32)
    mid = r.astype(BF16)
    lo = (r - mid.astype(F32)).astype(BF16)
    return hi, mid, lo


def _dot(a, b):
    return jnp.dot(a, b, preferred_element_type=F32)


def _dot_x01(x, m01):
    hi, mid, lo = _split3(x)
    return _dot(hi, m01) + _dot(mid, m01) + _dot(lo, m01)


def _dot_01x(m01, x):
    hi, mid, lo = _split3(x)
    return _dot(m01, hi) + _dot(m01, mid) + _dot(m01, lo)


def _dot_f32(a, b):
    ah, am, al = _split3(a)
    bh, bm, bl = _split3(b)
    return _dot(ah, bh) + (_dot(ah, bm) + _dot(am, bh)) + (_dot(ah, bl) + _dot(am, bm) + _dot(al, bh))


def _pad_q_rows(q, tq, tqp):
    if tqp == tq:
        return q
    return jnp.concatenate([q, jnp.zeros((tqp - tq, q.shape[1]), q.dtype)], axis=0)


def _mod_kernel(c_ref, w_ref, b_ref, o_ref):
    o_ref[0] = _dot(c_ref[...], w_ref[0]) + b_ref[0]


def ada_mod(c, w_ada, b_ada):
    nl, d, n6 = w_ada.shape
    bsz = c.shape[0]
    tn = 512
    return pl.pallas_call(
        _mod_kernel,
        grid=(nl, n6 // tn),
        in_specs=[pl.BlockSpec((bsz, d), lambda l, j: (0, 0)),
                  pl.BlockSpec((1, d, tn), lambda l, j: (l, 0, j)),
                  pl.BlockSpec((1, 1, tn), lambda l, j: (l, 0, j))],
        out_specs=pl.BlockSpec((1, bsz, tn), lambda l, j: (l, 0, j)),
        out_shape=jax.ShapeDtypeStruct((nl, bsz, n6), F32),
        compiler_params=_cparams("parallel", "parallel"),
        name="ada_mod",
    )(c, w_ada, b_ada.reshape(nl, 1, n6))


def _mod_spec(m, tm, grid_rank):
    d = m.shape[2]
    if m.shape[1] == 1:
        return pl.BlockSpec((1, 1, d), (lambda b, i, j: (b, 0, 0)) if grid_rank == 3 else (lambda b, i: (b, 0, 0)))
    return pl.BlockSpec((1, tm, d), (lambda b, i, j: (b, i, 0)) if grid_rank == 3 else (lambda b, i: (b, i, 0)))


def _mod_rows(ref, rows):
    return ref[0] if ref.shape[1] == 1 else ref[0, rows, :]


def _merge(x):
    return x.reshape(1, x.shape[0] * x.shape[1], x.shape[2])


def _per_token(m, t):
    bsz, _, d = m.shape
    return jnp.broadcast_to(m, (bsz, t, d)).reshape(1, bsz * t, d)


def _ln_mod(x, gain, sc, sh):
    ms = jnp.mean(x * x, axis=-1, keepdims=True)
    return (x * lax.rsqrt(ms + EPS) * gain) * (1.0 + sc) + sh


def _ln_proj_kernel(x_ref, gain_ref, sc_ref, sh_ref, w_ref, cgain_ref, cflag_ref, bd_ref, o_ref, h_ref,
                    *, n_norm_tiles, tn):
    j = pl.program_id(2)

    @pl.when(j == 0)
    def _():
        h_ref[...] = _ln_mod(x_ref[0], gain_ref[...], sc_ref[0], sh_ref[0]).astype(BF16)

    y = _dot(h_ref[...], w_ref[...])

    def plain():
        o_ref[0] = y

    def normed():
        y2 = y * y
        hi = y2.astype(BF16)
        lo = (y2 - hi.astype(F32)).astype(BF16)
        bd = bd_ref[...]
        segs = []
        for s in range(tn // LANES):
            sl = slice(s * LANES, (s + 1) * LANES)
            segs.append(_dot(hi[:, sl], bd) + _dot(lo[:, sl], bd))
        seg = jnp.concatenate(segs, axis=1)
        yn = y * lax.rsqrt(seg * (1.0 / HEAD_DIM) + EPS) * cgain_ref[...]
        o_ref[0] = jnp.where(cflag_ref[...] > 0.0, yn, y)

    if n_norm_tiles == 0:
        plain()
    else:
        pl.when(j < n_norm_tiles)(normed)
        pl.when(j >= n_norm_tiles)(plain)


def ln_proj(x, gain, sc, sh, w, cgain, cflag, n_norm_tiles, tn=256):
    bsz, t, d = x.shape
    if t % LANES:
        y = ln_proj(_merge(x), gain, _per_token(sc, t), _per_token(sh, t), w, cgain, cflag, n_norm_tiles, tn)
        return y.reshape(bsz, t, -1)
    npad = w.shape[1]
    tm = min(t, 1024)
    bd = (jnp.arange(LANES)[:, None] // HEAD_DIM == jnp.arange(LANES)[None, :] // HEAD_DIM).astype(BF16)
    kern = functools.partial(_ln_proj_kernel, n_norm_tiles=n_norm_tiles, tn=tn)
    return pl.pallas_call(
        kern,
        grid=(bsz, t // tm, npad // tn),
        in_specs=[pl.BlockSpec((1, tm, d), lambda b, i, j: (b, i, 0)),
                  pl.BlockSpec((1, d), lambda b, i, j: (0, 0)),
                  _mod_spec(sc, tm, 3),
                  _mod_spec(sh, tm, 3),
                  pl.BlockSpec((d, tn), lambda b, i, j: (0, j)),
                  pl.BlockSpec((1, tn), lambda b, i, j: (0, j)),
                  pl.BlockSpec((1, tn), lambda b, i, j: (0, j)),
                  pl.BlockSpec((LANES, LANES), lambda b, i, j: (0, 0))],
        out_specs=pl.BlockSpec((1, tm, tn), lambda b, i, j: (b, i, j)),
        out_shape=jax.ShapeDtypeStruct((bsz, t, npad), F32),
        scratch_shapes=[pltpu.VMEM((tm, d), BF16)],
        compiler_params=_cparams("parallel", "parallel", "arbitrary"),
        name="ln_proj",
    )(x, gain.reshape(1, d), sc, sh, w, cgain, cflag, bd)


def _out_proj_kernel(a_ref, w_ref, x_ref, g_ref, o_ref):
    y = _dot(a_ref[0].astype(BF16), w_ref[...])
    o_ref[0] = x_ref[0] + g_ref[0] * y


def out_proj_residual(a, w, x, gate):
    bsz, t, k = a.shape
    if t % LANES:
        return out_proj_residual(_merge(a), w, _merge(x), _per_token(gate, t)).reshape(x.shape)
    d = w.shape[1]
    tm = min(t, 1024)
    return pl.pallas_call(
        _out_proj_kernel,
        grid=(bsz, t // tm),
        in_specs=[pl.BlockSpec((1, tm, k), lambda b, i: (b, i, 0)),
                  pl.BlockSpec((k, d), lambda b, i: (0, 0)),
                  pl.BlockSpec((1, tm, d), lambda b, i: (b, i, 0)),
                  _mod_spec(gate, tm, 2)],
        out_specs=pl.BlockSpec((1, tm, d), lambda b, i: (b, i, 0)),
        out_shape=jax.ShapeDtypeStruct((bsz, t, d), F32),
        compiler_params=_cparams("parallel", "parallel"),
        name="out_proj",
    )(a, w, x, gate)


def _dsa_index_kernel(qi_ref, w_ref, ki_ref, o_ref, key_ref, *, tq, tqp, n_slabs, q_off, topk, idx_bits):
    a = pl.program_id(1)
    tk = KEY_TILE
    q0 = q_off + a * tq
    n_kt = (q0 + tq + tk - 1) // tk
    lane = lax.broadcasted_iota(I32, (tqp, LANES), 1)
    half = lane < IDX_DIM
    krow = lax.broadcasted_iota(I32, (tk, tqp), 0)
    qcol = lax.broadcasted_iota(I32, (tk, tqp), 1)
    qchunk = (q0 + qcol) >> 6
    srow = lax.broadcasted_iota(I32, (LANES, tqp), 0)
    w_t = (_pad_q_rows(w_ref[0], tq, tqp) * (IDX_HEADS ** -0.5)).T
    qs = []
    for p in range(IDX_HEADS // 2):
        qp = _pad_q_rows(qi_ref[0, :, p * LANES:(p + 1) * LANES], tq, tqp) * (IDX_DIM ** -0.5)
        qs.append(jnp.where(half, qp, 0.0).astype(BF16))
        qs.append(pltpu.roll(jnp.where(half, 0.0, qp), IDX_DIM, axis=1).astype(BF16))

    def score_tile(c, carry):
        kt = ki_ref[0, pl.ds(pl.multiple_of(c * tk, tk), tk), :].astype(BF16)
        sc = jnp.zeros((tk, tqp), F32)
        for h in range(IDX_HEADS):
            sc = sc + w_t[IDX_DIM + h:IDX_DIM + h + 1, :] * jnp.maximum(_nt(kt, qs[h]), 0.0)
        bits = lax.bitcast_convert_type(sc, I32)
        key = jnp.where(bits < 0, bits ^ 0x7FFFFFFF, bits)
        key = jnp.where(sc == 0.0, 0, key)
        adm = ((c * tk + krow) >> 6) <= qchunk
        key = jnp.where(adm, key, INT_MIN)
        for s_ in range(SLABS):
            key_ref[c * SLABS + s_] = key[s_ * LANES:(s_ + 1) * LANES, :]
        return carry

    lax.fori_loop(0, n_kt, score_tile, 0)

    def count(pred):
        def body(c, acc):
            for s_ in range(SLABS):
                sidx = c * SLABS + s_
                ind = jnp.where(pred(key_ref[sidx], sidx), 1.0, 0.0)
                acc = acc + jnp.sum(ind.reshape(LANES // 8, 8, tqp), axis=0)
            return acc
        acc = lax.fori_loop(0, n_kt, body, jnp.zeros((8, tqp), F32))
        return jnp.sum(acc, axis=0, keepdims=True)

    kf = float(topk)
    n_adm = count(lambda k, s: k > INT_MIN)

    def all_done(cnt_t):
        done = (cnt_t == kf) | (n_adm < kf)
        return (jnp.min(jnp.where(done, 1.0, 0.0)) > 0.0).astype(I32)

    def bit_cond(st):
        return (st[0] < 32) & (st[3] == 0)

    def bit_body(st):
        i, t_u, cnt_t, _ = st
        cand_u = t_u | lax.shift_left(jnp.int32(1), 31 - i)
        cand_s = cand_u ^ INT_MIN
        cnt = count(lambda k, s: k >= cand_s)
        take = cnt >= kf
        cnt_t = jnp.where(take, cnt, cnt_t)
        return i + 1, jnp.where(take, cand_u, t_u), cnt_t, all_done(cnt_t)

    _, t_u, cnt_ge, _ = lax.while_loop(bit_cond, bit_body,
                                       (jnp.int32(0), jnp.zeros((1, tqp), I32), n_adm, all_done(n_adm)))
    thr = t_u ^ INT_MIN
    excess = jnp.where(cnt_ge > kf, 1.0, 0.0)

    def tie_search():
        need = kf - count(lambda k, s: k > thr)

        def j_body(i, j):
            cand = j | lax.shift_left(jnp.int32(1), idx_bits - 1 - i)
            c = count(lambda k, s: (k == thr) & (srow + s * LANES < cand))
            return jnp.where(c < need, cand, j)
        return lax.fori_loop(0, idx_bits, j_body, jnp.zeros((1, tqp), I32))

    has_ties = jnp.max(excess) > 0.0

    @pl.when(has_ties)
    def _():
        j_last = tie_search()

        def write_ties(c, carry):
            for s_ in range(SLABS):
                sidx = c * SLABS + s_
                k = key_ref[sidx]
                sel = (k > thr) | ((k == thr) & (srow + sidx * LANES <= j_last))
                sel = sel & (k > INT_MIN)
                o_ref[0, sidx] = jnp.where(sel, 0.0, NEG_BIG)
            return carry

        lax.fori_loop(0, n_kt, write_ties, 0)

    @pl.when(jnp.logical_not(has_ties))
    def _():
        thr_adm = jnp.maximum(thr, INT_MIN + 1)

        def write_plain(c, carry):
            for s_ in range(SLABS):
                sidx = c * SLABS + s_
                o_ref[0, sidx] = jnp.where(key_ref[sidx] >= thr_adm, 0.0, NEG_BIG)
            return carry

        lax.fori_loop(0, n_kt, write_plain, 0)

    neg = jnp.full((LANES, tqp), NEG_BIG, F32)

    def write_inactive(s, carry):
        o_ref[0, s] = neg
        return carry

    lax.fori_loop(n_kt * SLABS, n_slabs, write_inactive, 0)


def dsa_index_mask(proj, keys, *, lp, t, q_off, topk, qi_blk, w_blk, ki_blk):
    bsz = proj.shape[0]
    n_slabs = lp // LANES
    tq = min(t, KEY_TILE)
    tqp = max(tq, LANES)
    idx_bits = max(1, (lp - 1).bit_length())
    kern = functools.partial(_dsa_index_kernel, tq=tq, tqp=tqp, n_slabs=n_slabs, q_off=q_off, topk=topk,
                             idx_bits=idx_bits)
    return pl.pallas_call(
        kern,
        grid=(bsz, t // tq),
        in_specs=[pl.BlockSpec((1, tq, 4 * LANES), lambda b, a: (b, a, qi_blk)),
                  pl.BlockSpec((1, tq, LANES), lambda b, a: (b, a, w_blk)),
                  pl.BlockSpec((1, lp, LANES), lambda b, a: (b, 0, ki_blk))],
        out_specs=pl.BlockSpec((1, n_slabs, LANES, tqp), lambda b, a: (b, 0, 0, a)),
        out_shape=jax.ShapeDtypeStruct((bsz, n_slabs, LANES, (t // tq) * tqp), F32),
        scratch_shapes=[pltpu.VMEM((n_slabs, LANES, tqp), I32)],
        compiler_params=_cparams("parallel", "parallel"),
        name="dsa_index",
    )(proj, proj, keys)


def _bias_kernel(tab_ref, diag_ref, sub_ref):
    h = pl.program_id(0)
    far = tab_ref[REL_BUCKETS // 2 - 1, h]

    def bias(rel):
        n = jnp.abs(rel)
        large = jnp.full(rel.shape, 8, I32)
        for th in (12, 16, 23, 32, 46, 64, 91):
            large = large + jnp.where(n >= th, 1, 0)
        bucket = jnp.where(rel > 0, REL_BUCKETS // 2, 0) + jnp.where(n < 8, n, large)
        acc = jnp.zeros(rel.shape, F32)
        for bk in range(REL_BUCKETS):
            acc = jnp.where(bucket == bk, tab_ref[bk, h], acc)
        return (acc - far) * LOG2E

    ik = lax.broadcasted_iota(I32, (KEY_TILE, KEY_TILE), 0)
    iq = lax.broadcasted_iota(I32, (KEY_TILE, KEY_TILE), 1)
    diag_ref[0] = bias(ik - iq)
    ik = lax.broadcasted_iota(I32, (LANES, LANES), 0)
    iq = lax.broadcasted_iota(I32, (LANES, LANES), 1)
    sub_ref[0] = bias(ik - LANES - iq)


def dsa_bias_tiles(rel_table):
    return pl.pallas_call(
        _bias_kernel,
        grid=(A_HEADS,),
        in_specs=[pl.BlockSpec(memory_space=pltpu.SMEM)],
        out_specs=[pl.BlockSpec((1, KEY_TILE, KEY_TILE), lambda h: (h, 0, 0)),
                   pl.BlockSpec((1, LANES, LANES), lambda h: (h, 0, 0))],
        out_shape=[jax.ShapeDtypeStruct((A_HEADS, KEY_TILE, KEY_TILE), F32),
                   jax.ShapeDtypeStruct((A_HEADS, LANES, LANES), F32)],
        compiler_params=_cparams("parallel"),
        name="dsa_bias",
    )(rel_table)


def _vt_rows(v_tile):
    return jnp.concatenate([v_tile.T.astype(BF16), jnp.ones((VT_ROWS - LANES, v_tile.shape[0]), BF16)], axis=0)


def _dsa_attn_kernel(q_ref, k_ref, v_ref, msk_ref, bd_ref, bs_ref, o_ref, acc_ref, m_ref, s0_ref, vt_ref,
                     *, tq, tqp, q_off, nt):
    a = pl.program_id(1)
    g = pl.program_id(2)
    tk = KEY_TILE
    q0 = q_off + a * tq
    cd = q0 // tk
    cs = jnp.maximum(cd - 1, 0)
    par = g % 2
    lane = lax.broadcasted_iota(I32, (tqp, LANES), 1)
    half = lane < HEAD_DIM

    @pl.when(a == 0)
    def _():
        for c in range(nt):
            vt_ref[g, c] = _vt_rows(v_ref[0, c * tk:(c + 1) * tk, :])

    qs = []
    for e in range(A_GROUP):
        qc = _pad_q_rows(q_ref[0, :, (e // 2) * LANES:(e // 2 + 1) * LANES], tq, tqp) * (HEAD_DIM ** -0.5 * LOG2E)
        own = jnp.where(half, qc, 0.0) if e % 2 == 0 else jnp.where(half, 0.0, qc)
        qs.append(jnp.where(par == e % 2, own, pltpu.roll(own, HEAD_DIM, axis=1)).astype(BF16))
    m_ref[...] = jnp.full(m_ref.shape, M_FLOOR, F32)
    acc_ref[...] = jnp.zeros(acc_ref.shape, F32)

    def s_tile(e, c):
        return _nt(k_ref[0, pl.ds(pl.multiple_of(c * tk, tk), tk), :].astype(BF16), qs[e])

    def softmax_pv(e, c, s, after, kind):
        s = s + jnp.concatenate([msk_ref[0, c * SLABS + i] for i in range(SLABS)], axis=0)
        if kind == "diag":
            s = s + bd_ref[e, :, 0:tqp]
        elif kind == "sub":
            corner = bs_ref[e]
            if tqp > LANES:
                corner = jnp.concatenate([corner, jnp.zeros((LANES, tqp - LANES), F32)], axis=1)
            s = s + jnp.concatenate([jnp.zeros((tk - LANES, tqp), F32), corner], axis=0)
            s = jnp.where(cd > 0, s, NEG_BIG)
        m_old = jnp.minimum(m_ref[e], jnp.maximum(after[0:1, :], -NEG_BIG))
        m_new = jnp.maximum(m_old, jnp.max(s, axis=0, keepdims=True))
        p = jnp.exp2(s - m_new).astype(BF16)
        acc_ref[e] = jnp.exp2(m_old - m_new) * acc_ref[e] + _dot(vt_ref[g, c], p)
        m_ref[e] = m_new

    def step(c, kind):
        s_prev = s0_ref[...]
        for e in range(A_GROUP):
            if e + 1 < A_GROUP:
                s_next = s_tile(e + 1, c)
            else:
                s_next = s_tile(0, jnp.minimum(c + 1, cd))
                s0_ref[...] = s_next
            softmax_pv(e, c, s_prev, s_next, kind)
            s_prev = s_next

    def far_body(c, carry):
        step(c, "far")
        return carry

    def far_pair(i, carry):
        step(2 * i, "far")
        step(2 * i + 1, "far")
        return carry

    s0_ref[...] = s_tile(0, 0)
    lax.fori_loop(0, cs // 2, far_pair, 0)
    lax.fori_loop((cs // 2) * 2, cs, far_body, 0)
    step(cs, "sub")
    step(cd, "diag")
    outs = []
    for e in range(A_GROUP):
        o_t = (acc_ref[e, 0:LANES, :] * (1.0 / acc_ref[e, LANES:LANES + 1, :])).T
        outs.append(jnp.where(par == e % 2, o_t, pltpu.roll(o_t, HEAD_DIM, axis=1)))
    for c2 in range(A_GROUP // 2):
        o_ref[0, :, c2 * LANES:(c2 + 1) * LANES] = jnp.where(half, outs[2 * c2], outs[2 * c2 + 1])[:tq]


def dsa_attention(proj, keys, values, mask, bdiag, bsub, *, lp, t, q_off, k_blk0, v_blk0):
    bsz = proj.shape[0]
    tq = min(t, KEY_TILE)
    tqp = max(tq, LANES)
    nt = lp // KEY_TILE
    nsl = mask.shape[1]
    kern = functools.partial(_dsa_attn_kernel, tq=tq, tqp=tqp, q_off=q_off, nt=nt)
    return pl.pallas_call(
        kern,
        grid=(bsz, t // tq, A_KV_HEADS),
        in_specs=[pl.BlockSpec((1, tq, 2 * LANES), lambda b, a, g: (b, a, g)),
                  pl.BlockSpec((1, lp, LANES), lambda b, a, g: (b, 0, k_blk0 + g // 2)),
                  pl.BlockSpec((1, lp, LANES), lambda b, a, g: (b, 0, v_blk0 + g // 2)),
                  pl.BlockSpec((1, nsl, LANES, tqp), lambda b, a, g: (b, 0, 0, a)),
                  pl.BlockSpec((A_GROUP, KEY_TILE, tqp), lambda b, a, g: (g, 0, 0)),
                  pl.BlockSpec((A_GROUP, LANES, LANES), lambda b, a, g: (g, 0, 0))],
        out_specs=pl.BlockSpec((1, tq, 2 * LANES), lambda b, a, g: (b, a, g)),
        out_shape=jax.ShapeDtypeStruct((bsz, t, A_HEADS * HEAD_DIM), F32),
        scratch_shapes=[pltpu.VMEM((A_GROUP, VT_ROWS, tqp), F32), pltpu.VMEM((A_GROUP, 1, tqp), F32),
                        pltpu.VMEM((KEY_TILE, tqp), F32),
                        pltpu.VMEM((A_KV_HEADS, nt, VT_ROWS, KEY_TILE), BF16)],
        compiler_params=_cparams("parallel", "arbitrary", "arbitrary"),
        name="dsa_attn",
    )(proj, keys, values, mask, bdiag, bsub)


def _fox_decay_kernel(*refs, n_past, n_new, t):
    if n_past:
        fz_ref, bf_ref, past_ref, tri_ref, lf_ref, nck_ref = refs
    else:
        fz_ref, bf_ref, tri_ref, lf_ref, nck_ref = refs
        past_ref = None
    tri = tri_ref[...]
    lane = lax.broadcasted_iota(I32, (B_HEADS, LANES), 1)
    carry = jnp.zeros((B_HEADS, 1), F32)
    for blk in range(n_past + n_new):
        sl = slice(blk * LANES, (blk + 1) * LANES)
        if blk < n_past:
            lf = past_ref[0, :, sl]
        else:
            nsl = slice((blk - n_past) * LANES, (blk - n_past + 1) * LANES)
            x = fz_ref[0, :, nsl] + bf_ref[...]
            lf = jnp.minimum(x, 0.0) - jnp.log1p(jnp.exp(-jnp.abs(x)))
            lf = jnp.where(lane + (blk - n_past) * LANES < t, lf, 0.0)
            lf_ref[0, :, nsl] = lf
        cum = _dot_x01(lf, tri) + carry
        for i, piece in enumerate(_split3(cum * -LOG2E)):
            nck_ref[0, i, :, sl] = piece
        carry = cum[:, LANES - 1:LANES]


def fox_decay(fz_t, b_f, past_t, *, t):
    bsz, h, tp = fz_t.shape
    p = 0 if past_t is None else past_t.shape[2]
    n_past, n_new = p // LANES, tp // LANES
    tri = (jnp.arange(LANES)[:, None] <= jnp.arange(LANES)[None, :]).astype(BF16)
    kern = functools.partial(_fox_decay_kernel, n_past=n_past, n_new=n_new, t=t)
    args = [fz_t, b_f.reshape(h, 1)]
    in_specs = [pl.BlockSpec((1, h, tp), lambda b: (b, 0, 0)),
                pl.BlockSpec((h, 1), lambda b: (0, 0))]
    if n_past:
        args.append(past_t)
        in_specs.append(pl.BlockSpec((1, h, p), lambda b: (b, 0, 0)))
    args.append(tri)
    in_specs.append(pl.BlockSpec((LANES, LANES), lambda b: (0, 0)))
    return pl.pallas_call(
        kern,
        grid=(bsz,),
        in_specs=in_specs,
        out_specs=[pl.BlockSpec((1, h, tp), lambda b: (b, 0, 0)),
                   pl.BlockSpec((1, 3, h, p + tp), lambda b: (b, 0, 0, 0))],
        out_shape=[jax.ShapeDtypeStruct((bsz, h, tp), F32),
                   jax.ShapeDtypeStruct((bsz, 3, h, p + tp), BF16)],
        compiler_params=_cparams("parallel"),
        name="fox_decay",
    )(*args)


def _fox_attn_kernel(q_ref, k_ref, v_ref, pz_ref, g_ref, o_ref, acc_ref, m_ref, s0_ref, ka_ref, vt_ref,
                     *, tq, tqp, q_off, nt):
    j = pl.program_id(1)
    a = pl.program_id(2)
    tk = KEY_TILE
    q0 = q_off + a * tq
    n_full = q0 // tk
    n_need = (q0 + tq - 1) // tk + 1

    @pl.when(a == 0)
    def _():
        klane = lax.broadcasted_iota(I32, (tk, LANES), 1)
        for c in range(nt):
            rows = slice(c * tk, (c + 1) * tk)
            kp = k_ref[0, rows, :]
            pz = pz_ref[0, rows, :].astype(F32)
            ka_ref[0, rows, :] = jnp.where(klane < HEAD_DIM, kp, pz).astype(BF16)
            ka_ref[1, rows, :] = jnp.where(klane >= HEAD_DIM, kp, pz).astype(BF16)
            vt_ref[c] = _vt_rows(v_ref[0, rows, :])

    lane = lax.broadcasted_iota(I32, (tqp, LANES), 1)
    qn = _pad_q_rows(q_ref[0], tq, tqp) * (HEAD_DIM ** -0.5 * LOG2E)
    ones_e = (lane >= HEAD_DIM + 3 * j) & (lane < HEAD_DIM + 3 * j + 3)
    ones_o = (lane >= 3 * j) & (lane < 3 * j + 3)
    qs = (jnp.where(lane < HEAD_DIM, qn, jnp.where(ones_e, 1.0, 0.0)).astype(BF16),
          jnp.where(lane >= HEAD_DIM, qn, jnp.where(ones_o, 1.0, 0.0)).astype(BF16))
    m_ref[...] = jnp.full(m_ref.shape, M_FLOOR, F32)
    acc_ref[...] = jnp.zeros(acc_ref.shape, F32)
    krow = lax.broadcasted_iota(I32, (tk, tqp), 0)
    qcol = lax.broadcasted_iota(I32, (tk, tqp), 1)

    def s_tile(e, c):
        return _nt(ka_ref[e, pl.ds(pl.multiple_of(c * tk, tk), tk), :], qs[e])

    def softmax_pv(e, c, s, masked):
        if masked:
            s = jnp.where(c * tk + krow <= q0 + qcol, s, NEG_BIG)
        m_old = m_ref[e]
        m_new = jnp.maximum(m_old, jnp.max(s, axis=0, keepdims=True))
        p = jnp.exp2(s - m_new).astype(BF16)
        acc_ref[e] = jnp.exp2(m_old - m_new) * acc_ref[e] + _dot(vt_ref[c], p)
        m_ref[e] = m_new

    def step(c, masked):
        s1 = s_tile(1, c)
        softmax_pv(0, c, s0_ref[...], masked)
        s0_ref[...] = s_tile(0, jnp.minimum(c + 1, n_need - 1))
        softmax_pv(1, c, s1, masked)

    def full_body(c, carry):
        step(c, False)
        return carry

    def masked_body(c, carry):
        step(c, True)
        return carry

    def pair_body(i, carry):
        step(2 * i, False)
        step(2 * i + 1, False)
        return carry

    s0_ref[...] = s_tile(0, 0)
    lax.fori_loop(0, n_full // 2, pair_body, 0)
    lax.fori_loop((n_full // 2) * 2, n_full, full_body, 0)
    lax.fori_loop(n_full, n_need, masked_body, 0)
    o_e = (acc_ref[0, 0:LANES, :] * (1.0 / acc_ref[0, LANES:LANES + 1, :])).T
    o_o = (acc_ref[1, 0:LANES, :] * (1.0 / acc_ref[1, LANES:LANES + 1, :])).T
    o = jnp.where(lane < HEAD_DIM, o_e, o_o)
    o_ref[0] = o[:tq] * (1.0 / (1.0 + jnp.exp(-g_ref[0])))


def fox_attention(proj, keys, values, pz, *, lp, t, q_off, g_blk0, k_blk0, v_blk0):
    bsz = proj.shape[0]
    tq = min(t, KEY_TILE)
    tqp = max(tq, LANES)
    nt = lp // KEY_TILE
    kern = functools.partial(_fox_attn_kernel, tq=tq, tqp=tqp, q_off=q_off, nt=nt)
    return pl.pallas_call(
        kern,
        grid=(bsz, B_HEADS // 2, t // tq),
        in_specs=[pl.BlockSpec((1, tq, LANES), lambda b, j, a: (b, a, j)),
                  pl.BlockSpec((1, lp, LANES), lambda b, j, a: (b, 0, k_blk0 + j)),
                  pl.BlockSpec((1, lp, LANES), lambda b, j, a: (b, 0, v_blk0 + j)),
                  pl.BlockSpec((1, lp, LANES), lambda b, j, a: (b, 0, 0)),
                  pl.BlockSpec((1, tq, LANES), lambda b, j, a: (b, a, g_blk0 + j))],
        out_specs=pl.BlockSpec((1, tq, LANES), lambda b, j, a: (b, a, j)),
        out_shape=jax.ShapeDtypeStruct((bsz, t, B_HEADS * HEAD_DIM), F32),
        scratch_shapes=[pltpu.VMEM((2, VT_ROWS, tqp), F32), pltpu.VMEM((2, 1, tqp), F32),
                        pltpu.VMEM((KEY_TILE, tqp), F32),
                        pltpu.VMEM((2, lp, LANES), BF16),
                        pltpu.VMEM((nt, VT_ROWS, KEY_TILE), BF16)],
        compiler_params=_cparams("parallel", "parallel", "arbitrary"),
        name="fox_attn",
    )(proj, keys, values, pz, proj)


def _hgrn2_levels(tc):
    lv = []
    n = 8
    while n < tc:
        lv.append(n)
        n *= 2
    return lv


def _hgrn2_masks(tc):
    t = jnp.arange(tc)[:, None]
    s = jnp.arange(tc)[None, :]
    ms = [((t // (2 * n) == s // (2 * n)) & ((t // n) % 2 == 1) & ((s // n) % 2 == 0)) for n in _hgrn2_levels(tc)]
    ms.append((t // 8 == s // 8) & (s <= t))
    return jnp.stack(ms).astype(F32)


def _hgrn2_head(q, z, v, g, lb, og, st, tri_ref, msk_ref, tc):
    ez = jnp.exp(-jnp.abs(z))
    den = 1.0 / (1.0 + ez)
    pos = z >= 0.0
    f = lb + (1.0 - lb) * (jnp.where(pos, 1.0, ez) * den)
    kk = (1.0 - lb) * (jnp.where(pos, ez, 1.0) * den)
    cum = _dot_01x(tri_ref[...], jnp.log(f))

    def rows(idx):
        parts = []
        for i in idx:
            parts.append(jnp.zeros((8, LANES), F32) if i < 0 else jnp.broadcast_to(cum[i:i + 1, :], (8, LANES)))
        return jnp.concatenate(parts, axis=0)

    levels = _hgrn2_levels(tc)
    ngrp = tc // 8
    scores = jnp.zeros((tc, tc), F32)
    ql8 = None
    for li, n in enumerate(levels):
        start = [((r * 8) // n) * n for r in range(ngrp)]
        a_start = rows([s - 1 for s in start])
        a_end = rows([s + n - 1 for s in start])
        ql = (q * jnp.exp(cum - a_start)).astype(BF16)
        kr = (kk * jnp.exp(a_end - cum)).astype(BF16)
        scores = scores + msk_ref[li] * _nt(ql, kr)
        if n == 8:
            ql8 = ql
            kb = (kk * jnp.exp(a_start - cum)).astype(BF16)
    if ql8 is None:
        a_start = rows([r * 8 - 1 for r in range(ngrp)])
        ql8 = (q * jnp.exp(cum - a_start)).astype(BF16)
        kb = (kk * jnp.exp(a_start - cum)).astype(BF16)
    scores = scores + msk_ref[len(levels)] * _nt(ql8, kb)

    o = _nt((q * jnp.exp(cum)).astype(BF16), st.astype(BF16)) + _dot(scores.astype(BF16), v.astype(BF16))
    a_last = cum[tc - 1:tc, :]
    khat = (kk * jnp.exp(a_last - cum)).astype(BF16)
    st_new = st * jnp.exp(a_last) + _dot(v.T.astype(BF16), khat)
    on = o * lax.rsqrt(jnp.mean(o * o, axis=-1, keepdims=True) + EPS) * og
    return on * (g * (1.0 / (1.0 + jnp.exp(-g)))), st_new


def _hgrn2_kernel(q_ref, fz_ref, v_ref, g_ref, lb_ref, og_ref, s0_ref, tri_ref, msk_ref, y_ref, so_ref, st_ref, *, tc):
    ct = pl.program_id(2)

    @pl.when(ct == 0)
    def _():
        st_ref[...] = s0_ref[0]

    for e in range(HGRN2_HEADS_PER_STEP):
        sl = slice(e * LANES, (e + 1) * LANES)
        y, st_new = _hgrn2_head(q_ref[0, :, sl], fz_ref[0, :, sl], v_ref[0, :, sl], g_ref[0, :, sl], lb_ref[e],
                                og_ref[...], st_ref[e], tri_ref, msk_ref, tc)
        y_ref[0, :, sl] = y
        st_ref[e] = st_new

    @pl.when(ct == pl.num_programs(2) - 1)
    def _():
        so_ref[0] = st_ref[...]


def hgrn2_recurrence(proj, lb, out_gain, s0_t, *, t):
    bsz = proj.shape[0]
    tc = min(t, 128)
    nlv = len(_hgrn2_levels(tc)) + 1
    tri = (jnp.arange(tc)[:, None] >= jnp.arange(tc)[None, :]).astype(BF16)
    h = C_HEADS
    hps = HGRN2_HEADS_PER_STEP
    ng = h // hps
    kern = functools.partial(_hgrn2_kernel, tc=tc)
    blk = lambda off: pl.BlockSpec((1, tc, hps * LANES), lambda b, hh, c: (b, c, off + hh))
    return pl.pallas_call(
        kern,
        grid=(bsz, ng, t // tc),
        in_specs=[blk(0), blk(ng), blk(2 * ng), blk(3 * ng),
                  pl.BlockSpec((hps, 1, C_DK), lambda b, hh, c: (hh, 0, 0)),
                  pl.BlockSpec((1, C_DV), lambda b, hh, c: (0, 0)),
                  pl.BlockSpec((1, hps, C_DV, C_DK), lambda b, hh, c: (b, hh, 0, 0)),
                  pl.BlockSpec((tc, tc), lambda b, hh, c: (0, 0)),
                  pl.BlockSpec((nlv, tc, tc), lambda b, hh, c: (0, 0, 0))],
        out_specs=[pl.BlockSpec((1, tc, hps * LANES), lambda b, hh, c: (b, c, hh)),
                   pl.BlockSpec((1, hps, C_DV, C_DK), lambda b, hh, c: (b, hh, 0, 0))],
        out_shape=[jax.ShapeDtypeStruct((bsz, t, h * C_DV), F32),
                   jax.ShapeDtypeStruct((bsz, h, C_DV, C_DK), F32)],
        scratch_shapes=[pltpu.VMEM((hps, C_DV, C_DK), F32)],
        compiler_params=_cparams("parallel", "parallel", "arbitrary"),
        name="hgrn2",
    )(proj, proj, proj, proj, lb.reshape(h, 1, C_DK), out_gain.reshape(1, C_DV), s0_t, tri, _hgrn2_masks(tc))


def _route_t(h, wr_t, br_t):
    ah, am, al = _split3(wr_t)
    bh, bm, bl = _split3(h)
    r = _nt(ah, bh) + (_nt(ah, bm) + _nt(am, bh)) + (_nt(ah, bl) + _nt(am, bm) + _nt(al, bh)) + br_t
    row = lax.broadcasted_iota(I32, r.shape, 0)
    rowf = row.astype(F32)
    big = float(ROUTER_ROWS)
    is_g = row < N_GROUPS
    lg = jnp.where(is_g, r, -jnp.inf)
    mg = jnp.max(lg, axis=0, keepdims=True)
    grp = jnp.min(jnp.where(lg == mg, rowf, big), axis=0, keepdims=True)
    p_grp = 1.0 / jnp.sum(jnp.where(is_g, jnp.exp(r - mg), 0.0), axis=0, keepdims=True)
    eg = ((row - N_GROUPS) >> 2).astype(F32)
    in_e = (row >= N_GROUPS) & (row < N_GROUPS + N_EXPERTS) & (eg == grp)
    le = jnp.where(in_e, r, -jnp.inf)
    v1 = jnp.max(le, axis=0, keepdims=True)
    i1 = jnp.min(jnp.where(le == v1, rowf, big), axis=0, keepdims=True)
    le2 = jnp.where(rowf == i1, -jnp.inf, le)
    v2 = jnp.max(le2, axis=0, keepdims=True)
    i2 = jnp.min(jnp.where(le2 == v2, rowf, big), axis=0, keepdims=True)
    e2 = jnp.exp(v2 - v1)
    w1 = 1.0 / (1.0 + e2)
    gates = jnp.where(rowf == i1, w1 * p_grp, 0.0) + jnp.where(rowf == i2, (e2 * w1) * p_grp, 0.0)
    member = jnp.where((rowf == i1) | (rowf == i2), 1.0, 0.0)
    return gates, member


def _moe_kernel(x_ref, gain_ref, sc_ref, sh_ref, g2_ref, wr_ref, br_ref, tri_ref, wg_ref, wu_ref, wd_ref, o_ref,
                hb_ref, rank_ref, rt_ref, gt_ref, y_ref, acc_ref, *, ts, slot):
    step = pl.program_id(2)
    tm = x_ref.shape[1]
    n_sub = tm // ts
    tsp = ts
    n_chunk = tsp // slot

    @pl.when(step == 0)
    def _():
        for s in range(n_sub):
            rows = slice(s * ts, (s + 1) * ts)
            h = _ln_mod(x_ref[0, rows, :], gain_ref[...], _mod_rows(sc_ref, rows), _mod_rows(sh_ref, rows))
            hb_ref[rows, :] = h.astype(BF16)
            gates_t, member_t = _route_t(h, wr_ref[...], br_ref[...])
            r_t = _dot(member_t.astype(BF16), tri_ref[...])
            r_t = jnp.where(member_t > 0.0, r_t, -1.0)
            rt_ref[s] = r_t
            gt_ref[s] = gates_t
            rank_ref[s] = jnp.concatenate([r_t, jnp.full((LANES - ROUTER_ROWS, ts), -1.0, F32)], axis=0).T
        if n_chunk > 1:
            acc_ref[...] = jnp.zeros_like(acc_ref)

    def expert_rows(k, ee):
        row = step * MOE_EXPERTS_PER_STEP + ee + N_GROUPS
        xs, ges, sels = [], [], []
        slot_i = lax.broadcasted_iota(I32, (slot, tsp), 0).astype(F32) + float(k * slot)
        for s in range(n_sub):
            sel = jnp.where(rt_ref[s, pl.ds(row, 1), :] == slot_i, 1.0, 0.0)
            ges.append(jnp.sum(sel * gt_ref[s, pl.ds(row, 1), :], axis=-1, keepdims=True))
            xs.append(_dot(sel.astype(BF16), hb_ref[s * tsp:(s + 1) * tsp, :]).astype(BF16))
            sels.append(sel)
        xa = jnp.concatenate(xs, axis=0)
        a = _dot(xa, wg_ref[ee])
        u = _dot(xa, wu_ref[ee])
        he = (a * (1.0 / (1.0 + jnp.exp(-a)))) * u * jnp.concatenate(ges, axis=0)
        return _dot(he.astype(BF16), wd_ref[ee]), sels

    for ee in range(MOE_EXPERTS_PER_STEP):
        e = step * MOE_EXPERTS_PER_STEP + ee
        ye, _ = expert_rows(0, ee)
        for s in range(n_sub):
            y_ref[s, pl.ds(pl.multiple_of(e * slot, slot), slot), :] = ye[s * slot:(s + 1) * slot].astype(BF16)

    for ee in range(MOE_EXPERTS_PER_STEP):
        row = step * MOE_EXPERTS_PER_STEP + ee + N_GROUPS
        for k in range(1, n_chunk):
            last_rank = jnp.max(rt_ref[0, pl.ds(row, 1), :])
            for s in range(1, n_sub):
                last_rank = jnp.maximum(last_rank, jnp.max(rt_ref[s, pl.ds(row, 1), :]))

            @pl.when(last_rank >= float(k * slot))
            def _(k=k, ee=ee):
                ye_k, sels = expert_rows(k, ee)
                for s in range(n_sub):
                    acc_ref[s * tsp:(s + 1) * tsp, :] += _dot(sels[s].T.astype(BF16),
                                                              ye_k[s * slot:(s + 1) * slot].astype(BF16))

    @pl.when(step == pl.num_programs(2) - 1)
    def _():
        slot_l = lax.broadcasted_iota(I32, (1, slot), 1).astype(F32)
        for s in range(n_sub):
            rows = slice(s * ts, (s + 1) * ts)
            rank = rank_ref[s]
            sel_t = jnp.concatenate(
                [jnp.where(rank[:, N_GROUPS + j:N_GROUPS + j + 1] == slot_l, 1.0, 0.0).astype(BF16)
                 for j in range(N_EXPERTS)], axis=1)
            y = _dot(sel_t, y_ref[s])
            if n_chunk > 1:
                y = y + acc_ref[s * tsp:(s + 1) * tsp, :]
            o_ref[0, rows, :] = x_ref[0, rows, :] + _mod_rows(g2_ref, rows) * y


def moe_layer(x, gain, sc, sh, g2, w_router, b_router, w_gate, w_up, w_down):
    bsz, t, d = x.shape
    if t % LANES:
        y = moe_layer(_merge(x), gain, _per_token(sc, t), _per_token(sh, t), _per_token(g2, t),
                      w_router, b_router, w_gate, w_up, w_down)
        return y.reshape(x.shape)
    tm = min(t, 1024 if sc.shape[1] == 1 else 512)
    ts = min(tm, 512)
    assert ts % LANES == 0 and t % tm == 0, (t, tm, ts)
    tsp = ts
    slot = MOE_SLOT
    eps = MOE_EXPERTS_PER_STEP
    ne, _, de = w_gate.shape
    tri = (jnp.arange(tsp)[:, None] < jnp.arange(tsp)[None, :]).astype(BF16)
    kern = functools.partial(_moe_kernel, ts=ts, slot=slot)
    n_sub = tm // ts
    return pl.pallas_call(
        kern,
        grid=(bsz, t // tm, ne // eps),
        in_specs=[pl.BlockSpec((1, tm, d), lambda b, i, e: (b, i, 0)),
                  pl.BlockSpec((1, d), lambda b, i, e: (0, 0)),
                  _mod_spec(sc, tm, 3),
                  _mod_spec(sh, tm, 3),
                  _mod_spec(g2, tm, 3),
                  pl.BlockSpec((ROUTER_ROWS, d), lambda b, i, e: (0, 0)),
                  pl.BlockSpec((ROUTER_ROWS, 1), lambda b, i, e: (0, 0)),
                  pl.BlockSpec((tsp, tsp), lambda b, i, e: (0, 0)),
                  pl.BlockSpec((eps, d, de), lambda b, i, e: (e, 0, 0)),
                  pl.BlockSpec((eps, d, de), lambda b, i, e: (e, 0, 0)),
                  pl.BlockSpec((eps, de, d), lambda b, i, e: (e, 0, 0))],
        out_specs=pl.BlockSpec((1, tm, d), lambda b, i, e: (b, i, 0)),
        out_shape=jax.ShapeDtypeStruct((bsz, t, d), F32),
        scratch_shapes=[pltpu.VMEM((n_sub * tsp, d), BF16),
                        pltpu.VMEM((n_sub, tsp, LANES), F32),
                        pltpu.VMEM((n_sub, ROUTER_ROWS, tsp), F32),
                        pltpu.VMEM((n_sub, ROUTER_ROWS, tsp), F32),
                        pltpu.VMEM((n_sub, ne * slot, d), BF16),
                        pltpu.VMEM((n_sub * tsp if tsp > slot else 8, d), F32)],
        compiler_params=_cparams("parallel", "parallel", "arbitrary"),
        name="moe",
    )(x, gain.reshape(1, d), sc, sh, g2, w_router, b_router, tri, w_gate, w_up, w_down)


def _pad_cols(w, n):
    return jnp.pad(w, ((0, 0), (0, n - w.shape[1])))


def _pad_rows(a, n):
    return jnp.pad(a, ((0, 0), (0, n - a.shape[1])) + ((0, 0),) * (a.ndim - 2))


def _round_up(n, m):
    return -(-n // m) * m


def _head_cols(gain_q, nq, gain_k, nk, npad):
    cg = jnp.concatenate([jnp.tile(gain_q, nq), jnp.tile(gain_k, nk)])
    n = cg.shape[0]
    cgain = jnp.pad(cg, (0, npad - n)).reshape(1, npad)
    cflag = (jnp.arange(npad) < n).astype(F32).reshape(1, npad)
    return cgain, cflag


def _dsa_layer(x, mod, past, prm, bdiag, bsub):
    sh1, sc1, g1 = mod
    past_k, past_v, past_ki = past
    bsz, t, d = x.shape
    p = 0 if past_k is None else past_k.shape[1]
    n_keys = p + t
    topk = min(TOPK_MAX, n_keys // 4)
    tn = 768
    n_in = prm['w_in'].shape[1]
    npad = _round_up(n_in, tn)
    nqk = (A_HEADS + A_KV_HEADS) * HEAD_DIM
    w = _pad_cols(prm['w_in'], npad).astype(BF16)
    cgain, cflag = _head_cols(prm['q_norm'], A_HEADS, prm['k_norm'], A_KV_HEADS, npad)
    proj = ln_proj(x, prm['norm'], sc1, sh1, w, cgain, cflag, -(-nqk // tn), tn)
    o_k = A_HEADS * HEAD_DIM
    o_v = o_k + A_KV_HEADS * HEAD_DIM
    o_qi = o_v + A_KV_HEADS * HEAD_DIM
    o_ki = o_qi + IDX_HEADS * IDX_DIM
    k_new = proj[..., o_k:o_v].reshape(bsz, t, A_KV_HEADS, HEAD_DIM)
    v_new = proj[..., o_v:o_qi].reshape(bsz, t, A_KV_HEADS, HEAD_DIM)
    ki_new = proj[..., o_ki:o_ki + IDX_DIM]
    lp = _round_up(n_keys, KEY_TILE)
    kvw = A_KV_HEADS * HEAD_DIM
    if p:
        keys = _pad_rows(jnp.concatenate([past_k.reshape(bsz, p, kvw), proj[..., o_k:o_v]], axis=1), lp)
        values = _pad_rows(jnp.concatenate([past_v.reshape(bsz, p, kvw), proj[..., o_v:o_qi]], axis=1), lp)
        ikeys = _pad_rows(jnp.concatenate([past_ki, ki_new], axis=1), lp)
        ikeys = jnp.pad(ikeys, ((0, 0), (0, 0), (0, LANES - IDX_DIM)))
        k_blk0, v_blk0, ki_blk = 0, 0, 0
    else:
        keys = values = ikeys = proj
        k_blk0, v_blk0, ki_blk = o_k // LANES, o_v // LANES, o_ki // LANES
    mask = dsa_index_mask(proj, ikeys, lp=lp, t=t, q_off=p, topk=topk, qi_blk=o_qi // (4 * LANES),
                          w_blk=o_ki // LANES, ki_blk=ki_blk)
    o = dsa_attention(proj, keys, values, mask, bdiag, bsub, lp=lp, t=t, q_off=p, k_blk0=k_blk0, v_blk0=v_blk0)
    x = out_proj_residual(o, prm['w_out'].astype(BF16), x, g1)
    return x, (k_new, v_new, ki_new)


def _fox_layer(x, mod, past, prm):
    sh1, sc1, g1 = mod
    past_k, past_v, past_lf = past
    bsz, t, d = x.shape
    p = 0 if past_k is None else past_k.shape[1]
    n_keys = p + t
    hd = B_HEADS * HEAD_DIM
    tn = 512
    npad = _round_up(prm['w_in'].shape[1], tn)
    w = _pad_cols(prm['w_in'], npad).astype(BF16)
    cgain, cflag = _head_cols(prm['q_norm'], B_HEADS, prm['k_norm'], B_HEADS, npad)
    proj = ln_proj(x, prm['norm'], sc1, sh1, w, cgain, cflag, 2 * hd // tn, tn)
    k_new = proj[..., hd:2 * hd]
    v_new = proj[..., 2 * hd:3 * hd]
    fz = proj[..., 4 * hd:4 * hd + B_HEADS]
    tp = _round_up(t, LANES)
    fz_t = _pad_rows(fz, tp).transpose(0, 2, 1)
    past_t = None if not p else past_lf.transpose(0, 2, 1)
    lf_t, nck = fox_decay(fz_t, prm['forget_bias'], past_t, t=t)
    logf_new = lf_t[:, :, :t].transpose(0, 2, 1)
    lp = _round_up(n_keys, KEY_TILE)
    if p:
        keys = _pad_rows(jnp.concatenate([past_k.reshape(bsz, p, hd), k_new], axis=1), lp)
        values = _pad_rows(jnp.concatenate([past_v.reshape(bsz, p, hd), v_new], axis=1), lp)
        k_blk0, v_blk0 = 0, 0
    else:
        keys = values = proj
        k_blk0, v_blk0 = hd // LANES, 2 * hd // LANES
    pieces = jnp.pad(nck, ((0, 0), (0, 0), (0, 0), (0, lp - nck.shape[3]))).transpose(0, 3, 2, 1)
    zl = jnp.zeros((bsz, lp, HEAD_DIM - 3 * B_HEADS // 2), pieces.dtype)
    pz = jnp.concatenate([pieces[:, :, 1::2].reshape(bsz, lp, -1), zl,
                          pieces[:, :, 0::2].reshape(bsz, lp, -1), zl], axis=-1)
    o = fox_attention(proj, keys, values, pz, lp=lp, t=t, q_off=p, g_blk0=3 * hd // LANES,
                      k_blk0=k_blk0, v_blk0=v_blk0)
    x = out_proj_residual(o, prm['w_out'].astype(BF16), x, g1)
    return x, (k_new.reshape(bsz, t, B_HEADS, HEAD_DIM), v_new.reshape(bsz, t, B_HEADS, HEAD_DIM), logf_new)


def _hgrn2_layer(x, mod, s0, prm):
    sh1, sc1, g1 = mod
    bsz, t, d = x.shape
    npad = prm['w_in'].shape[1]
    zeros = jnp.zeros((1, npad), F32)
    proj = ln_proj(x, prm['norm'], sc1, sh1, prm['w_in'].astype(BF16), zeros, zeros, 0, 512)
    y, s_t = hgrn2_recurrence(proj, prm['lb'], prm['out_norm'], jnp.swapaxes(s0, -1, -2), t=t)
    x = out_proj_residual(y, prm['w_out'].astype(BF16), x, g1)
    return x, jnp.swapaxes(s_t, -1, -2)


def _trunk(x, c, a_k, a_v, a_kidx, b_k, b_v, b_logf, c_state, prm):
    bsz, t, d = x.shape
    mod_all = ada_mod(c, prm['w_ada'], prm['b_ada'])
    lb_all = jnp.cumsum(jax.nn.softmax(prm['c_lower_bound'].astype(F32), axis=0), axis=0)
    lb_all = lb_all - lb_all[0]
    bdiag, bsub = dsa_bias_tiles(prm['rel_table'])
    out_a, out_b, out_c = [], [], []
    for i in range(DEPTH):
        j = i // N_MIXERS
        kind = i % N_MIXERS
        sh1, sc1, g1, sh2, sc2, g2 = [m.reshape(bsz, 1, d) for m in jnp.split(mod_all[i], 6, axis=-1)]
        mod = (sh1, sc1, g1)
        if kind == 0:
            past = (None, None, None) if a_k is None else (a_k[j], a_v[j], a_kidx[j])
            lp = dict(norm=prm['norm_mix'][i], w_in=prm['a_w_in'][j], q_norm=prm['a_q_norm'][j],
                      k_norm=prm['a_k_norm'][j], w_out=prm['a_w_out'][j])
            x, new = _dsa_layer(x, mod, past, lp, bdiag, bsub)
            out_a.append(new)
        elif kind == 1:
            past = (None, None, None) if b_k is None else (b_k[j], b_v[j], b_logf[j])
            lp = dict(norm=prm['norm_mix'][i], w_in=prm['b_w_in'][j], forget_bias=prm['b_forget_bias'][j],
                      q_norm=prm['b_q_norm'][j], k_norm=prm['b_k_norm'][j], w_out=prm['b_w_out'][j])
            x, new = _fox_layer(x, mod, past, lp)
            out_b.append(new)
        else:
            s0 = jnp.zeros((bsz, C_HEADS, C_DK, C_DV), F32) if c_state is None else c_state[j]
            lp = dict(norm=prm['norm_mix'][i], w_in=prm['c_w_in'][j], lb=lb_all[i],
                      out_norm=prm['c_out_norm'][j], w_out=prm['c_w_out'][j])
            x, new = _hgrn2_layer(x, mod, s0, lp)
            out_c.append(new)
        w_router = jnp.pad(jnp.concatenate([prm['moe_w_group'][i], prm['moe_w_expert'][i]], axis=1).T,
                           ((0, ROUTER_ROWS - N_GROUPS - N_EXPERTS), (0, 0)))
        b_router = jnp.pad(jnp.concatenate([prm['moe_b_group'][i], prm['moe_b_expert'][i]]),
                           (0, ROUTER_ROWS - N_GROUPS - N_EXPERTS)).reshape(ROUTER_ROWS, 1)
        x = moe_layer(x, prm['norm_ffn'][i], sc2, sh2, g2, w_router, b_router,
                      prm['moe_w_gate'][i].astype(BF16), prm['moe_w_up'][i].astype(BF16),
                      prm['moe_w_down'][i].astype(BF16))
    stack = lambda outs, k: jnp.stack([o[k] for o in outs])
    return (x, stack(out_a, 0), stack(out_a, 1), stack(out_a, 2),
            stack(out_b, 0), stack(out_b, 1), stack(out_b, 2), jnp.stack(out_c))


def kernel(x_prompt, x_sample, cache_a_k, cache_a_v, cache_a_kidx, cache_b_k, cache_b_v, cache_b_logf, state_c,
           c_prompt, c_sample, rel_table, w_ada, b_ada, norm_mix, norm_ffn, a_w_in, a_q_norm, a_k_norm, a_w_out,
           b_w_in, b_forget_bias, b_q_norm, b_k_norm, b_w_out, c_w_in, c_lower_bound, c_out_norm, c_w_out,
           moe_w_group, moe_b_group, moe_w_expert, moe_b_expert, moe_w_gate, moe_w_up, moe_w_down):
    prm = {'rel_table': rel_table, 'w_ada': w_ada, 'b_ada': b_ada, 'norm_mix': norm_mix, 'norm_ffn': norm_ffn,
           'a_w_in': a_w_in, 'a_q_norm': a_q_norm, 'a_k_norm': a_k_norm, 'a_w_out': a_w_out,
           'b_w_in': b_w_in, 'b_forget_bias': b_forget_bias, 'b_q_norm': b_q_norm, 'b_k_norm': b_k_norm,
           'b_w_out': b_w_out, 'c_w_in': c_w_in, 'c_lower_bound': c_lower_bound, 'c_out_norm': c_out_norm,
           'c_w_out': c_w_out, 'moe_w_group': moe_w_group, 'moe_b_group': moe_b_group,
           'moe_w_expert': moe_w_expert, 'moe_b_expert': moe_b_expert, 'moe_w_gate': moe_w_gate,
           'moe_w_up': moe_w_up, 'moe_w_down': moe_w_down}
    (y_p, ak_p, av_p, ai_p, bk_p, bv_p, bl_p, cs_p) = _trunk(
        x_prompt, c_prompt, None, None, None, None, None, None, None, prm)
    (y_s, ak_s, av_s, ai_s, bk_s, bv_s, bl_s, cs_s) = _trunk(
        x_sample, c_sample, cache_a_k, cache_a_v, cache_a_kidx, cache_b_k, cache_b_v, cache_b_logf, state_c, prm)
    return (y_p, y_s, ak_p, av_p, ai_p, ak_s, av_s, ai_s, bk_p, bv_p, bl_p, bk_s, bv_s, bl_s, cs_p, cs_s)
```

```python
import functools

import jax
import jax.numpy as jnp
from jax import lax
from jax.experimental import pallas as pl
from jax.experimental.pallas import tpu as pltpu

F32 = jnp.float32
BF16 = jnp.bfloat16
I32 = jnp.int32

LANES = 128
VMEM_LIMIT_BYTES = 56 * 1024 * 1024

DEPTH = 4
N_MIXERS = 3
CHUNK = 64
EPS = 1e-6
HEAD_DIM = 64
A_HEADS = 16
A_KV_HEADS = 4
A_GROUP = A_HEADS // A_KV_HEADS
IDX_HEADS = 8
IDX_DIM = 64
TOPK_MAX = 256
REL_BUCKETS = 32
B_HEADS = 16
C_HEADS = 8
C_DK = 128
C_DV = 128
N_GROUPS = 4
EXPERTS_PER_GROUP = 4
N_EXPERTS = 16
D_EXPERT = 512

LOG2E = 1.4426950408889634
NEG_BIG = -1e30
M_FLOOR = -1e20
INT_MIN = -2 ** 31
KEY_TILE = 512
SLABS = KEY_TILE // LANES
MOE_SLOT = 128
VT_ROWS = LANES + 16
ROUTER_ROWS = 32
HGRN2_HEADS_PER_STEP = 8
HGRN2_CHUNK = 256
MOE_EXPERTS_PER_STEP = 2


def _cparams(*sem):
    return pltpu.CompilerParams(dimension_semantics=sem, vmem_limit_bytes=VMEM_LIMIT_BYTES)


def _nt(a, b):
    return lax.dot_general(a, b, (((1,), (1,)), ((), ())), preferred_element_type=F32)


def _split3(x):
    hi = x.astype(BF16)
    r = x - hi.astype(F32)
    mid = r.astype(BF16)
    lo = (r - mid.astype(F32)).astype(BF16)
    return hi, mid, lo


def _dot(a, b):
    return jnp.dot(a, b, preferred_element_type=F32)


def _dot_x01(x, m01):
    hi, mid, lo = _split3(x)
    return _dot(hi, m01) + _dot(mid, m01) + _dot(lo, m01)


def _dot_01x(m01, x):
    hi, mid, lo = _split3(x)
    return _dot(m01, hi) + _dot(m01, mid) + _dot(m01, lo)


def _dot_f32(a, b):
    ah, am, al = _split3(a)
    bh, bm, bl = _split3(b)
    return _dot(ah, bh) + (_dot(ah, bm) + _dot(am, bh)) + (_dot(ah, bl) + _dot(am, bm) + _dot(al, bh))


def _pad_q_rows(q, tq, tqp):
    if tqp == tq:
        return q
    return jnp.concatenate([q, jnp.zeros((tqp - tq, q.shape[1]), q.dtype)], axis=0)


def _mod_kernel(c_ref, w_ref, b_ref, o_ref):
    o_ref[0] = _dot(c_ref[...], w_ref[0]) + b_ref[0]


def ada_mod(c, w_ada, b_ada):
    nl, d, n6 = w_ada.shape
    bsz = c.shape[0]
    tn = 512
    return pl.pallas_call(
        _mod_kernel,
        grid=(nl, n6 // tn),
        in_specs=[pl.BlockSpec((bsz, d), lambda l, j: (0, 0)),
                  pl.BlockSpec((1, d, tn), lambda l, j: (l, 0, j)),
                  pl.BlockSpec((1, 1, tn), lambda l, j: (l, 0, j))],
        out_specs=pl.BlockSpec((1, bsz, tn), lambda l, j: (l, 0, j)),
        out_shape=jax.ShapeDtypeStruct((nl, bsz, n6), F32),
        compiler_params=_cparams("parallel", "parallel"),
        name="ada_mod",
    )(c, w_ada, b_ada.reshape(nl, 1, n6))


def _mod_spec(m, tm, grid_rank):
    d = m.shape[2]
    if m.shape[1] == 1:
        return pl.BlockSpec((1, 1, d), (lambda b, i, j: (b, 0, 0)) if grid_rank == 3 else (lambda b, i: (b, 0, 0)))
    return pl.BlockSpec((1, tm, d), (lambda b, i, j: (b, i, 0)) if grid_rank == 3 else (lambda b, i: (b, i, 0)))


def _mod_rows(ref, rows):
    return ref[0] if ref.shape[1] == 1 else ref[0, rows, :]


def _merge(x):
    return x.reshape(1, x.shape[0] * x.shape[1], x.shape[2])


def _per_token(m, t):
    bsz, _, d = m.shape
    return jnp.broadcast_to(m, (bsz, t, d)).reshape(1, bsz * t, d)


def _ln_mod(x, gain, sc, sh):
    ms = jnp.mean(x * x, axis=-1, keepdims=True)
    return (x * lax.rsqrt(ms + EPS) * gain) * (1.0 + sc) + sh


def _ln_proj_kernel(x_ref, gain_ref, sc_ref, sh_ref, w_ref, cgain_ref, cflag_ref, bd_ref, o_ref, h_ref,
                    *, n_norm_tiles, tn):
    j = pl.program_id(2)

    @pl.when(j == 0)
    def _():
        h_ref[...] = _ln_mod(x_ref[0], gain_ref[...], sc_ref[0], sh_ref[0]).astype(BF16)

    y = _dot(h_ref[...], w_ref[...])

    def plain():
        o_ref[0] = y

    def normed():
        y2 = y * y
        hi = y2.astype(BF16)
        lo = (y2 - hi.astype(F32)).astype(BF16)
        bd = bd_ref[...]
        segs = []
        for s in range(tn // LANES):
            sl = slice(s * LANES, (s + 1) * LANES)
            segs.append(_dot(hi[:, sl], bd) + _dot(lo[:, sl], bd))
        seg = jnp.concatenate(segs, axis=1)
        yn = y * lax.rsqrt(seg * (1.0 / HEAD_DIM) + EPS) * cgain_ref[...]
        o_ref[0] = jnp.where(cflag_ref[...] > 0.0, yn, y)

    if n_norm_tiles == 0:
        plain()
    else:
        pl.when(j < n_norm_tiles)(normed)
        pl.when(j >= n_norm_tiles)(plain)


def ln_proj(x, gain, sc, sh, w, cgain, cflag, n_norm_tiles, tn=256):
    bsz, t, d = x.shape
    if t % LANES:
        y = ln_proj(_merge(x), gain, _per_token(sc, t), _per_token(sh, t), w, cgain, cflag, n_norm_tiles, tn)
        return y.reshape(bsz, t, -1)
    npad = w.shape[1]
    tm = min(t, 1024)
    bd = (jnp.arange(LANES)[:, None] // HEAD_DIM == jnp.arange(LANES)[None, :] // HEAD_DIM).astype(BF16)
    kern = functools.partial(_ln_proj_kernel, n_norm_tiles=n_norm_tiles, tn=tn)
    return pl.pallas_call(
        kern,
        grid=(bsz, t // tm, npad // tn),
        in_specs=[pl.BlockSpec((1, tm, d), lambda b, i, j: (b, i, 0)),
                  pl.BlockSpec((1, d), lambda b, i, j: (0, 0)),
                  _mod_spec(sc, tm, 3),
                  _mod_spec(sh, tm, 3),
                  pl.BlockSpec((d, tn), lambda b, i, j: (0, j)),
                  pl.BlockSpec((1, tn), lambda b, i, j: (0, j)),
                  pl.BlockSpec((1, tn), lambda b, i, j: (0, j)),
                  pl.BlockSpec((LANES, LANES), lambda b, i, j: (0, 0))],
        out_specs=pl.BlockSpec((1, tm, tn), lambda b, i, j: (b, i, j)),
        out_shape=jax.ShapeDtypeStruct((bsz, t, npad), F32),
        scratch_shapes=[pltpu.VMEM((tm, d), BF16)],
        compiler_params=_cparams("parallel", "parallel", "arbitrary"),
        name="ln_proj",
    )(x, gain.reshape(1, d), sc, sh, w, cgain, cflag, bd)


def _out_proj_kernel(a_ref, w_ref, x_ref, g_ref, o_ref):
    y = _dot(a_ref[0].astype(BF16), w_ref[...])
    o_ref[0] = x_ref[0] + g_ref[0] * y


def out_proj_residual(a, w, x, gate):
    bsz, t, k = a.shape
    if t % LANES:
        return out_proj_residual(_merge(a), w, _merge(x), _per_token(gate, t)).reshape(x.shape)
    d = w.shape[1]
    tm = min(t, 1024)
    return pl.pallas_call(
        _out_proj_kernel,
        grid=(bsz, t // tm),
        in_specs=[pl.BlockSpec((1, tm, k), lambda b, i: (b, i, 0)),
                  pl.BlockSpec((k, d), lambda b, i: (0, 0)),
                  pl.BlockSpec((1, tm, d), lambda b, i: (b, i, 0)),
                  _mod_spec(gate, tm, 2)],
        out_specs=pl.BlockSpec((1, tm, d), lambda b, i: (b, i, 0)),
        out_shape=jax.ShapeDtypeStruct((bsz, t, d), F32),
        compiler_params=_cparams("parallel", "parallel"),
        name="out_proj",
    )(a, w, x, gate)


def _dsa_index_kernel(qi_ref, w_ref, ki_ref, o_ref, key_ref, *, tq, tqp, n_slabs, q_off, topk, idx_bits):
    a = pl.program_id(1)
    tk = KEY_TILE
    q0 = q_off + a * tq
    n_kt = (q0 + tq + tk - 1) // tk
    lane = lax.broadcasted_iota(I32, (tqp, LANES), 1)
    half = lane < IDX_DIM
    krow = lax.broadcasted_iota(I32, (tk, tqp), 0)
    qcol = lax.broadcasted_iota(I32, (tk, tqp), 1)
    qchunk = (q0 + qcol) >> 6
    srow = lax.broadcasted_iota(I32, (LANES, tqp), 0)
    w_t = (_pad_q_rows(w_ref[0], tq, tqp) * (IDX_HEADS ** -0.5)).T
    qs = []
    for p in range(IDX_HEADS // 2):
        qp = _pad_q_rows(qi_ref[0, :, p * LANES:(p + 1) * LANES], tq, tqp) * (IDX_DIM ** -0.5)
        qs.append(jnp.where(half, qp, 0.0).astype(BF16))
        qs.append(pltpu.roll(jnp.where(half, 0.0, qp), IDX_DIM, axis=1).astype(BF16))

    def score_tile(c, carry):
        kt = ki_ref[0, pl.ds(pl.multiple_of(c * tk, tk), tk), :].astype(BF16)
        sc = jnp.zeros((tk, tqp), F32)
        for h in range(IDX_HEADS):
            sc = sc + w_t[IDX_DIM + h:IDX_DIM + h + 1, :] * jnp.maximum(_nt(kt, qs[h]), 0.0)
        bits = lax.bitcast_convert_type(sc, I32)
        key = jnp.where(bits < 0, bits ^ 0x7FFFFFFF, bits)
        key = jnp.where(sc == 0.0, 0, key)
        adm = ((c * tk + krow) >> 6) <= qchunk
        key = jnp.where(adm, key, INT_MIN)
        for s_ in range(SLABS):
            key_ref[c * SLABS + s_] = key[s_ * LANES:(s_ + 1) * LANES, :]
        return carry

    lax.fori_loop(0, n_kt, score_tile, 0)

    def count(pred):
        def body(c, acc):
            for s_ in range(SLABS):
                sidx = c * SLABS + s_
                ind = jnp.where(pred(key_ref[sidx], sidx), 1.0, 0.0)
                acc = acc + jnp.sum(ind.reshape(LANES // 8, 8, tqp), axis=0)
            return acc
        acc = lax.fori_loop(0, n_kt, body, jnp.zeros((8, tqp), F32))
        return jnp.sum(acc, axis=0, keepdims=True)

    kf = float(topk)
    n_adm = count(lambda k, s: k > INT_MIN)

    def all_done(cnt_t):
        done = (cnt_t == kf) | (n_adm < kf)
        return (jnp.min(jnp.where(done, 1.0, 0.0)) > 0.0).astype(I32)

    def bit_cond(st):
        return (st[0] < 32) & (st[3] == 0)

    def bit_body(st):
        i, t_u, cnt_t, _ = st
        cand_u = t_u | lax.shift_left(jnp.int32(1), 31 - i)
        cand_s = cand_u ^ INT_MIN
        cnt = count(lambda k, s: k >= cand_s)
        take = cnt >= kf
        cnt_t = jnp.where(take, cnt, cnt_t)
        return i + 1, jnp.where(take, cand_u, t_u), cnt_t, all_done(cnt_t)

    _, t_u, cnt_ge, _ = lax.while_loop(bit_cond, bit_body,
                                       (jnp.int32(0), jnp.zeros((1, tqp), I32), n_adm, all_done(n_adm)))
    thr = t_u ^ INT_MIN
    excess = jnp.where(cnt_ge > kf, 1.0, 0.0)

    def tie_search():
        need = kf - count(lambda k, s: k > thr)

        def j_body(i, j):
            cand = j | lax.shift_left(jnp.int32(1), idx_bits - 1 - i)
            c = count(lambda k, s: (k == thr) & (srow + s * LANES < cand))
            return jnp.where(c < need, cand, j)
        return lax.fori_loop(0, idx_bits, j_body, jnp.zeros((1, tqp), I32))

    has_ties = jnp.max(excess) > 0.0

    @pl.when(has_ties)
    def _():
        j_last = tie_search()

        def write_ties(c, carry):
            for s_ in range(SLABS):
                sidx = c * SLABS + s_
                k = key_ref[sidx]
                sel = (k > thr) | ((k == thr) & (srow + sidx * LANES <= j_last))
                sel = sel & (k > INT_MIN)
                o_ref[0, sidx] = jnp.where(sel, 0.0, NEG_BIG)
            return carry

        lax.fori_loop(0, n_kt, write_ties, 0)

    @pl.when(jnp.logical_not(has_ties))
    def _():
        thr_adm = jnp.maximum(thr, INT_MIN + 1)

        def write_plain(c, carry):
            for s_ in range(SLABS):
                sidx = c * SLABS + s_
                o_ref[0, sidx] = jnp.where(key_ref[sidx] >= thr_adm, 0.0, NEG_BIG)
            return carry

        lax.fori_loop(0, n_kt, write_plain, 0)

    neg = jnp.full((LANES, tqp), NEG_BIG, F32)

    def write_inactive(s, carry):
        o_ref[0, s] = neg
        return carry

    lax.fori_loop(n_kt * SLABS, n_slabs, write_inactive, 0)


def dsa_index_mask(proj, keys, *, lp, t, q_off, topk, qi_blk, w_blk, ki_blk):
    bsz = proj.shape[0]
    n_slabs = lp // LANES
    tq = min(t, KEY_TILE)
    tqp = max(tq, LANES)
    idx_bits = max(1, (lp - 1).bit_length())
    kern = functools.partial(_dsa_index_kernel, tq=tq, tqp=tqp, n_slabs=n_slabs, q_off=q_off, topk=topk,
                             idx_bits=idx_bits)
    return pl.pallas_call(
        kern,
        grid=(bsz, t // tq),
        in_specs=[pl.BlockSpec((1, tq, 4 * LANES), lambda b, a: (b, a, qi_blk)),
                  pl.BlockSpec((1, tq, LANES), lambda b, a: (b, a, w_blk)),
                  pl.BlockSpec((1, lp, LANES), lambda b, a: (b, 0, ki_blk))],
        out_specs=pl.BlockSpec((1, n_slabs, LANES, tqp), lambda b, a: (b, 0, 0, a)),
        out_shape=jax.ShapeDtypeStruct((bsz, n_slabs, LANES, (t // tq) * tqp), F32),
        scratch_shapes=[pltpu.VMEM((n_slabs, LANES, tqp), I32)],
        compiler_params=_cparams("parallel", "parallel"),
        name="dsa_index",
    )(proj, proj, keys)


def _bias_kernel(tab_ref, diag_ref, sub_ref):
    h = pl.program_id(0)
    far = tab_ref[REL_BUCKETS // 2 - 1, h]

    def bias(rel):
        n = jnp.abs(rel)
        large = jnp.full(rel.shape, 8, I32)
        for th in (12, 16, 23, 32, 46, 64, 91):
            large = large + jnp.where(n >= th, 1, 0)
        bucket = jnp.where(rel > 0, REL_BUCKETS // 2, 0) + jnp.where(n < 8, n, large)
        acc = jnp.zeros(rel.shape, F32)
        for bk in range(REL_BUCKETS):
            acc = jnp.where(bucket == bk, tab_ref[bk, h], acc)
        return (acc - far) * LOG2E

    ik = lax.broadcasted_iota(I32, (KEY_TILE, KEY_TILE), 0)
    iq = lax.broadcasted_iota(I32, (KEY_TILE, KEY_TILE), 1)
    diag_ref[0] = bias(ik - iq)
    ik = lax.broadcasted_iota(I32, (LANES, LANES), 0)
    iq = lax.broadcasted_iota(I32, (LANES, LANES), 1)
    sub_ref[0] = bias(ik - LANES - iq)


def dsa_bias_tiles(rel_table):
    return pl.pallas_call(
        _bias_kernel,
        grid=(A_HEADS,),
        in_specs=[pl.BlockSpec(memory_space=pltpu.SMEM)],
        out_specs=[pl.BlockSpec((1, KEY_TILE, KEY_TILE), lambda h: (h, 0, 0)),
                   pl.BlockSpec((1, LANES, LANES), lambda h: (h, 0, 0))],
        out_shape=[jax.ShapeDtypeStruct((A_HEADS, KEY_TILE, KEY_TILE), F32),
                   jax.ShapeDtypeStruct((A_HEADS, LANES, LANES), F32)],
        compiler_params=_cparams("parallel"),
        name="dsa_bias",
    )(rel_table)


def _vt_rows(v_tile):
    return jnp.concatenate([v_tile.T.astype(BF16), jnp.ones((VT_ROWS - LANES, v_tile.shape[0]), BF16)], axis=0)


def _dsa_attn_kernel(q_ref, k_ref, v_ref, msk_ref, bd_ref, bs_ref, o_ref, acc_ref, m_ref, s0_ref, vt_ref,
                     *, tq, tqp, q_off, nt):
    a = pl.program_id(1)
    g = pl.program_id(2)
    tk = KEY_TILE
    q0 = q_off + a * tq
    cd = q0 // tk
    cs = jnp.maximum(cd - 1, 0)
    par = g % 2
    lane = lax.broadcasted_iota(I32, (tqp, LANES), 1)
    half = lane < HEAD_DIM

    @pl.when(a == 0)
    def _():
        for c in range(nt):
            vt_ref[g, c] = _vt_rows(v_ref[0, c * tk:(c + 1) * tk, :])

    qs = []
    for e in range(A_GROUP):
        qc = _pad_q_rows(q_ref[0, :, (e // 2) * LANES:(e // 2 + 1) * LANES], tq, tqp) * (HEAD_DIM ** -0.5 * LOG2E)
        own = jnp.where(half, qc, 0.0) if e % 2 == 0 else jnp.where(half, 0.0, qc)
        qs.append(jnp.where(par == e % 2, own, pltpu.roll(own, HEAD_DIM, axis=1)).astype(BF16))
    m_ref[...] = jnp.full(m_ref.shape, M_FLOOR, F32)
    acc_ref[...] = jnp.zeros(acc_ref.shape, F32)

    def s_tile(e, c):
        return _nt(k_ref[0, pl.ds(pl.multiple_of(c * tk, tk), tk), :].astype(BF16), qs[e])

    def softmax_pv(e, c, s, after, kind):
        s = s + jnp.concatenate([msk_ref[0, c * SLABS + i] for i in range(SLABS)], axis=0)
        if kind == "diag":
            s = s + bd_ref[e, :, 0:tqp]
        elif kind == "sub":
            corner = bs_ref[e]
            if tqp > LANES:
                corner = jnp.concatenate([corner, jnp.zeros((LANES, tqp - LANES), F32)], axis=1)
            s = s + jnp.concatenate([jnp.zeros((tk - LANES, tqp), F32), corner], axis=0)
            s = jnp.where(cd > 0, s, NEG_BIG)
        m_old = jnp.minimum(m_ref[e], jnp.maximum(after[0:1, :], -NEG_BIG))
        m_new = jnp.maximum(m_old, jnp.max(s, axis=0, keepdims=True))
        p = jnp.exp2(s - m_new).astype(BF16)
        acc_ref[e] = jnp.exp2(m_old - m_new) * acc_ref[e] + _dot(vt_ref[g, c], p)
        m_ref[e] = m_new

    def step(c, kind):
        s_prev = s0_ref[...]
        for e in range(A_GROUP):
            if e + 1 < A_GROUP:
                s_next = s_tile(e + 1, c)
            else:
                s_next = s_tile(0, jnp.minimum(c + 1, cd))
                s0_ref[...] = s_next
            softmax_pv(e, c, s_prev, s_next, kind)
            s_prev = s_next

    def far_body(c, carry):
        step(c, "far")
        return carry

    def far_pair(i, carry):
        step(2 * i, "far")
        step(2 * i + 1, "far")
        return carry

    s0_ref[...] = s_tile(0, 0)
    lax.fori_loop(0, cs // 2, far_pair, 0)
    lax.fori_loop((cs // 2) * 2, cs, far_body, 0)
    step(cs, "sub")
    step(cd, "diag")
    outs = []
    for e in range(A_GROUP):
        o_t = (acc_ref[e, 0:LANES, :] * (1.0 / acc_ref[e, LANES:LANES + 1, :])).T
        outs.append(jnp.where(par == e % 2, o_t, pltpu.roll(o_t, HEAD_DIM, axis=1)))
    for c2 in range(A_GROUP // 2):
        o_ref[0, :, c2 * LANES:(c2 + 1) * LANES] = jnp.where(half, outs[2 * c2], outs[2 * c2 + 1])[:tq]


def dsa_attention(proj, keys, values, mask, bdiag, bsub, *, lp, t, q_off, k_blk0, v_blk0):
    bsz = proj.shape[0]
    tq = min(t, KEY_TILE)
    tqp = max(tq, LANES)
    nt = lp // KEY_TILE
    nsl = mask.shape[1]
    kern = functools.partial(_dsa_attn_kernel, tq=tq, tqp=tqp, q_off=q_off, nt=nt)
    return pl.pallas_call(
        kern,
        grid=(bsz, t // tq, A_KV_HEADS),
        in_specs=[pl.BlockSpec((1, tq, 2 * LANES), lambda b, a, g: (b, a, g)),
                  pl.BlockSpec((1, lp, LANES), lambda b, a, g: (b, 0, k_blk0 + g // 2)),
                  pl.BlockSpec((1, lp, LANES), lambda b, a, g: (b, 0, v_blk0 + g // 2)),
                  pl.BlockSpec((1, nsl, LANES, tqp), lambda b, a, g: (b, 0, 0, a)),
                  pl.BlockSpec((A_GROUP, KEY_TILE, tqp), lambda b, a, g: (g, 0, 0)),
                  pl.BlockSpec((A_GROUP, LANES, LANES), lambda b, a, g: (g, 0, 0))],
        out_specs=pl.BlockSpec((1, tq, 2 * LANES), lambda b, a, g: (b, a, g)),
        out_shape=jax.ShapeDtypeStruct((bsz, t, A_HEADS * HEAD_DIM), F32),
        scratch_shapes=[pltpu.VMEM((A_GROUP, VT_ROWS, tqp), F32), pltpu.VMEM((A_GROUP, 1, tqp), F32),
                        pltpu.VMEM((KEY_TILE, tqp), F32),
                        pltpu.VMEM((A_KV_HEADS, nt, VT_ROWS, KEY_TILE), BF16)],
        compiler_params=_cparams("parallel", "arbitrary", "arbitrary"),
        name="dsa_attn",
    )(proj, keys, values, mask, bdiag, bsub)


def _fox_decay_kernel(*refs, n_past, n_new, t):
    if n_past:
        fz_ref, bf_ref, past_ref, tri_ref, lf_ref, nck_ref = refs
    else:
        fz_ref, bf_ref, tri_ref, lf_ref, nck_ref = refs
        past_ref = None
    tri = tri_ref[...]
    lane = lax.broadcasted_iota(I32, (B_HEADS, LANES), 1)
    carry = jnp.zeros((B_HEADS, 1), F32)
    for blk in range(n_past + n_new):
        sl = slice(blk * LANES, (blk + 1) * LANES)
        if blk < n_past:
            lf = past_ref[0, :, sl]
        else:
            nsl = slice((blk - n_past) * LANES, (blk - n_past + 1) * LANES)
            x = fz_ref[0, :, nsl] + bf_ref[...]
            lf = jnp.minimum(x, 0.0) - jnp.log1p(jnp.exp(-jnp.abs(x)))
            lf = jnp.where(lane + (blk - n_past) * LANES < t, lf, 0.0)
            lf_ref[0, :, nsl] = lf
        cum = _dot_x01(lf, tri) + carry
        for i, piece in enumerate(_split3(cum * -LOG2E)):
            nck_ref[0, i, :, sl] = piece
        carry = cum[:, LANES - 1:LANES]


def fox_decay(fz_t, b_f, past_t, *, t):
    bsz, h, tp = fz_t.shape
    p = 0 if past_t is None else past_t.shape[2]
    n_past, n_new = p // LANES, tp // LANES
    tri = (jnp.arange(LANES)[:, None] <= jnp.arange(LANES)[None, :]).astype(BF16)
    kern = functools.partial(_fox_decay_kernel, n_past=n_past, n_new=n_new, t=t)
    args = [fz_t, b_f.reshape(h, 1)]
    in_specs = [pl.BlockSpec((1, h, tp), lambda b: (b, 0, 0)),
                pl.BlockSpec((h, 1), lambda b: (0, 0))]
    if n_past:
        args.append(past_t)
        in_specs.append(pl.BlockSpec((1, h, p), lambda b: (b, 0, 0)))
    args.append(tri)
    in_specs.append(pl.BlockSpec((LANES, LANES), lambda b: (0, 0)))
    return pl.pallas_call(
        kern,
        grid=(bsz,),
        in_specs=in_specs,
        out_specs=[pl.BlockSpec((1, h, tp), lambda b: (b, 0, 0)),
                   pl.BlockSpec((1, 3, h, p + tp), lambda b: (b, 0, 0, 0))],
        out_shape=[jax.ShapeDtypeStruct((bsz, h, tp), F32),
                   jax.ShapeDtypeStruct((bsz, 3, h, p + tp), BF16)],
        compiler_params=_cparams("parallel"),
        name="fox_decay",
    )(*args)


def _fox_attn_kernel(q_ref, k_ref, v_ref, pz_ref, g_ref, o_ref, acc_ref, m_ref, s0_ref, ka_ref, vt_ref,
                     *, tq, tqp, q_off, nt):
    j = pl.program_id(1)
    a = pl.program_id(2)
    tk = KEY_TILE
    q0 = q_off + a * tq
    n_full = q0 // tk
    n_need = (q0 + tq - 1) // tk + 1

    @pl.when(a == 0)
    def _():
        klane = lax.broadcasted_iota(I32, (tk, LANES), 1)
        for c in range(nt):
            rows = slice(c * tk, (c + 1) * tk)
            kp = k_ref[0, rows, :]
            pz = pz_ref[0, rows, :].astype(F32)
            ka_ref[0, rows, :] = jnp.where(klane < HEAD_DIM, kp, pz).astype(BF16)
            ka_ref[1, rows, :] = jnp.where(klane >= HEAD_DIM, kp, pz).astype(BF16)
            vt_ref[c] = _vt_rows(v_ref[0, rows, :])

    lane = lax.broadcasted_iota(I32, (tqp, LANES), 1)
    qn = _pad_q_rows(q_ref[0], tq, tqp) * (HEAD_DIM ** -0.5 * LOG2E)
    ones_e = (lane >= HEAD_DIM + 3 * j) & (lane < HEAD_DIM + 3 * j + 3)
    ones_o = (lane >= 3 * j) & (lane < 3 * j + 3)
    qs = (jnp.where(lane < HEAD_DIM, qn, jnp.where(ones_e, 1.0, 0.0)).astype(BF16),
          jnp.where(lane >= HEAD_DIM, qn, jnp.where(ones_o, 1.0, 0.0)).astype(BF16))
    m_ref[...] = jnp.full(m_ref.shape, M_FLOOR, F32)
    acc_ref[...] = jnp.zeros(acc_ref.shape, F32)
    krow = lax.broadcasted_iota(I32, (tk, tqp), 0)
    qcol = lax.broadcasted_iota(I32, (tk, tqp), 1)

    def s_tile(e, c):
        return _nt(ka_ref[e, pl.ds(pl.multiple_of(c * tk, tk), tk), :], qs[e])

    def softmax_pv(e, c, s, masked):
        if masked:
            s = jnp.where(c * tk + krow <= q0 + qcol, s, NEG_BIG)
        m_old = m_ref[e]
        m_new = jnp.maximum(m_old, jnp.max(s, axis=0, keepdims=True))
        p = jnp.exp2(s - m_new).astype(BF16)
        acc_ref[e] = jnp.exp2(m_old - m_new) * acc_ref[e] + _dot(vt_ref[c], p)
        m_ref[e] = m_new

    def step(c, masked):
        s1 = s_tile(1, c)
        softmax_pv(0, c, s0_ref[...], masked)
        s0_ref[...] = s_tile(0, jnp.minimum(c + 1, n_need - 1))
        softmax_pv(1, c, s1, masked)

    def full_body(c, carry):
        step(c, False)
        return carry

    def masked_body(c, carry):
        step(c, True)
        return carry

    def pair_body(i, carry):
        step(2 * i, False)
        step(2 * i + 1, False)
        return carry

    s0_ref[...] = s_tile(0, 0)
    lax.fori_loop(0, n_full // 2, pair_body, 0)
    lax.fori_loop((n_full // 2) * 2, n_full, full_body, 0)
    lax.fori_loop(n_full, n_need, masked_body, 0)
    o_e = (acc_ref[0, 0:LANES, :] * (1.0 / acc_ref[0, LANES:LANES + 1, :])).T
    o_o = (acc_ref[1, 0:LANES, :] * (1.0 / acc_ref[1, LANES:LANES + 1, :])).T
    o = jnp.where(lane < HEAD_DIM, o_e, o_o)
    o_ref[0] = o[:tq] * (1.0 / (1.0 + jnp.exp(-g_ref[0])))


def fox_attention(proj, keys, values, pz, *, lp, t, q_off, g_blk0, k_blk0, v_blk0):
    bsz = proj.shape[0]
    tq = min(t, KEY_TILE)
    tqp = max(tq, LANES)
    nt = lp // KEY_TILE
    kern = functools.partial(_fox_attn_kernel, tq=tq, tqp=tqp, q_off=q_off, nt=nt)
    return pl.pallas_call(
        kern,
        grid=(bsz, B_HEADS // 2, t // tq),
        in_specs=[pl.BlockSpec((1, tq, LANES), lambda b, j, a: (b, a, j)),
                  pl.BlockSpec((1, lp, LANES), lambda b, j, a: (b, 0, k_blk0 + j)),
                  pl.BlockSpec((1, lp, LANES), lambda b, j, a: (b, 0, v_blk0 + j)),
                  pl.BlockSpec((1, lp, LANES), lambda b, j, a: (b, 0, 0)),
                  pl.BlockSpec((1, tq, LANES), lambda b, j, a: (b, a, g_blk0 + j))],
        out_specs=pl.BlockSpec((1, tq, LANES), lambda b, j, a: (b, a, j)),
        out_shape=jax.ShapeDtypeStruct((bsz, t, B_HEADS * HEAD_DIM), F32),
        scratch_shapes=[pltpu.VMEM((2, VT_ROWS, tqp), F32), pltpu.VMEM((2, 1, tqp), F32),
                        pltpu.VMEM((KEY_TILE, tqp), F32),
                        pltpu.VMEM((2, lp, LANES), BF16),
                        pltpu.VMEM((nt, VT_ROWS, KEY_TILE), BF16)],
        compiler_params=_cparams("parallel", "parallel", "arbitrary"),
        name="fox_attn",
    )(proj, keys, values, pz, proj)


def _hgrn2_levels(tc):
    lv = []
    n = 8
    while n < tc:
        lv.append(n)
        n *= 2
    return lv


def _hgrn2_masks(tc):
    t = jnp.arange(tc)[:, None]
    s = jnp.arange(tc)[None, :]
    ms = [((t // (2 * n) == s // (2 * n)) & ((t // n) % 2 == 1) & ((s // n) % 2 == 0)) for n in _hgrn2_levels(tc)]
    ms.append((t // 8 == s // 8) & (s <= t))
    return jnp.stack(ms).astype(F32)


def _hgrn2_head(q, z, v, g, lb, og, st, tri_ref, msk_ref, tc):
    ez = jnp.exp(-jnp.abs(z))
    den = 1.0 / (1.0 + ez)
    pos = z >= 0.0
    f = lb + (1.0 - lb) * (jnp.where(pos, 1.0, ez) * den)
    kk = (1.0 - lb) * (jnp.where(pos, ez, 1.0) * den)
    cum = _dot_01x(tri_ref[...], jnp.log(f))

    def rows(idx):
        parts = []
        for i in idx:
            parts.append(jnp.zeros((8, LANES), F32) if i < 0 else jnp.broadcast_to(cum[i:i + 1, :], (8, LANES)))
        return jnp.concatenate(parts, axis=0)

    levels = _hgrn2_levels(tc)
    ngrp = tc // 8
    scores = jnp.zeros((tc, tc), F32)
    ql8 = None
    for li, n in enumerate(levels):
        start = [((r * 8) // n) * n for r in range(ngrp)]
        a_start = rows([s - 1 for s in start])
        a_end = rows([s + n - 1 for s in start])
        ql = (q * jnp.exp(cum - a_start)).astype(BF16)
        kr = (kk * jnp.exp(a_end - cum)).astype(BF16)
        scores = scores + msk_ref[li] * _nt(ql, kr)
        if n == 8:
            ql8 = ql
            kb = (kk * jnp.exp(a_start - cum)).astype(BF16)
    if ql8 is None:
        a_start = rows([r * 8 - 1 for r in range(ngrp)])
        ql8 = (q * jnp.exp(cum - a_start)).astype(BF16)
        kb = (kk * jnp.exp(a_start - cum)).astype(BF16)
    scores = scores + msk_ref[len(levels)] * _nt(ql8, kb)

    o = _nt((q * jnp.exp(cum)).astype(BF16), st.astype(BF16)) + _dot(scores.astype(BF16), v.astype(BF16))
    a_last = cum[tc - 1:tc, :]
    khat = (kk * jnp.exp(a_last - cum)).astype(BF16)
    st_new = st * jnp.exp(a_last) + _dot(v.T.astype(BF16), khat)
    on = o * lax.rsqrt(jnp.mean(o * o, axis=-1, keepdims=True) + EPS) * og
    return on * (g * (1.0 / (1.0 + jnp.exp(-g)))), st_new


def _hgrn2_kernel(q_ref, fz_ref, v_ref, g_ref, lb_ref, og_ref, s0_ref, tri_ref, msk_ref, y_ref, so_ref, st_ref, *, tc):
    ct = pl.program_id(2)

    @pl.when(ct == 0)
    def _():
        st_ref[...] = s0_ref[0]

    for e in range(HGRN2_HEADS_PER_STEP):
        sl = slice(e * LANES, (e + 1) * LANES)
        y, st_new = _hgrn2_head(q_ref[0, :, sl], fz_ref[0, :, sl], v_ref[0, :, sl], g_ref[0, :, sl], lb_ref[e],
                                og_ref[...], st_ref[e], tri_ref, msk_ref, tc)
        y_ref[0, :, sl] = y
        st_ref[e] = st_new

    @pl.when(ct == pl.num_programs(2) - 1)
    def _():
        so_ref[0] = st_ref[...]


def hgrn2_recurrence(proj, lb, out_gain, s0_t, *, t):
    bsz = proj.shape[0]
    tc = min(t, HGRN2_CHUNK)
    nlv = len(_hgrn2_levels(tc)) + 1
    tri = (jnp.arange(tc)[:, None] >= jnp.arange(tc)[None, :]).astype(BF16)
    h = C_HEADS
    hps = HGRN2_HEADS_PER_STEP
    ng = h // hps
    kern = functools.partial(_hgrn2_kernel, tc=tc)
    blk = lambda off: pl.BlockSpec((1, tc, hps * LANES), lambda b, hh, c: (b, c, off + hh))
    return pl.pallas_call(
        kern,
        grid=(bsz, ng, t // tc),
        in_specs=[blk(0), blk(ng), blk(2 * ng), blk(3 * ng),
                  pl.BlockSpec((hps, 1, C_DK), lambda b, hh, c: (hh, 0, 0)),
                  pl.BlockSpec((1, C_DV), lambda b, hh, c: (0, 0)),
                  pl.BlockSpec((1, hps, C_DV, C_DK), lambda b, hh, c: (b, hh, 0, 0)),
                  pl.BlockSpec((tc, tc), lambda b, hh, c: (0, 0)),
                  pl.BlockSpec((nlv, tc, tc), lambda b, hh, c: (0, 0, 0))],
        out_specs=[pl.BlockSpec((1, tc, hps * LANES), lambda b, hh, c: (b, c, hh)),
                   pl.BlockSpec((1, hps, C_DV, C_DK), lambda b, hh, c: (b, hh, 0, 0))],
        out_shape=[jax.ShapeDtypeStruct((bsz, t, h * C_DV), F32),
                   jax.ShapeDtypeStruct((bsz, h, C_DV, C_DK), F32)],
        scratch_shapes=[pltpu.VMEM((hps, C_DV, C_DK), F32)],
        compiler_params=_cparams("parallel", "parallel", "arbitrary"),
        name="hgrn2",
    )(proj, proj, proj, proj, lb.reshape(h, 1, C_DK), out_gain.reshape(1, C_DV), s0_t, tri, _hgrn2_masks(tc))


def _route_t(h, wr_t, br_t):
    ah, am, al = _split3(wr_t)
    bh, bm, bl = _split3(h)
    r = _nt(ah, bh) + (_nt(ah, bm) + _nt(am, bh)) + (_nt(ah, bl) + _nt(am, bm) + _nt(al, bh)) + br_t
    row = lax.broadcasted_iota(I32, r.shape, 0)
    rowf = row.astype(F32)
    big = float(ROUTER_ROWS)
    is_g = row < N_GROUPS
    lg = jnp.where(is_g, r, -jnp.inf)
    mg = jnp.max(lg, axis=0, keepdims=True)
    grp = jnp.min(jnp.where(lg == mg, rowf, big), axis=0, keepdims=True)
    p_grp = 1.0 / jnp.sum(jnp.where(is_g, jnp.exp(r - mg), 0.0), axis=0, keepdims=True)
    eg = ((row - N_GROUPS) >> 2).astype(F32)
    in_e = (row >= N_GROUPS) & (row < N_GROUPS + N_EXPERTS) & (eg == grp)
    le = jnp.where(in_e, r, -jnp.inf)
    v1 = jnp.max(le, axis=0, keepdims=True)
    i1 = jnp.min(jnp.where(le == v1, rowf, big), axis=0, keepdims=True)
    le2 = jnp.where(rowf == i1, -jnp.inf, le)
    v2 = jnp.max(le2, axis=0, keepdims=True)
    i2 = jnp.min(jnp.where(le2 == v2, rowf, big), axis=0, keepdims=True)
    e2 = jnp.exp(v2 - v1)
    w1 = 1.0 / (1.0 + e2)
    gates = jnp.where(rowf == i1, w1 * p_grp, 0.0) + jnp.where(rowf == i2, (e2 * w1) * p_grp, 0.0)
    member = jnp.where((rowf == i1) | (rowf == i2), 1.0, 0.0)
    return gates, member


def _moe_kernel(x_ref, gain_ref, sc_ref, sh_ref, g2_ref, wr_ref, br_ref, tri_ref, wg_ref, wu_ref, wd_ref, o_ref,
                hb_ref, rank_ref, rt_ref, gt_ref, y_ref, acc_ref, *, ts, slot):
    step = pl.program_id(2)
    tm = x_ref.shape[1]
    n_sub = tm // ts
    tsp = ts
    n_chunk = tsp // slot

    @pl.when(step == 0)
    def _():
        for s in range(n_sub):
            rows = slice(s * ts, (s + 1) * ts)
            h = _ln_mod(x_ref[0, rows, :], gain_ref[...], _mod_rows(sc_ref, rows), _mod_rows(sh_ref, rows))
            hb_ref[rows, :] = h.astype(BF16)
            gates_t, member_t = _route_t(h, wr_ref[...], br_ref[...])
            r_t = _dot(member_t.astype(BF16), tri_ref[...])
            r_t = jnp.where(member_t > 0.0, r_t, -1.0)
            rt_ref[s] = r_t
            gt_ref[s] = gates_t
            rank_ref[s] = jnp.concatenate([r_t, jnp.full((LANES - ROUTER_ROWS, ts), -1.0, F32)], axis=0).T
        if n_chunk > 1:
            acc_ref[...] = jnp.zeros_like(acc_ref)

    def expert_rows(k, ee):
        row = step * MOE_EXPERTS_PER_STEP + ee + N_GROUPS
        xs, ges, sels = [], [], []
        slot_i = lax.broadcasted_iota(I32, (slot, tsp), 0).astype(F32) + float(k * slot)
        for s in range(n_sub):
            sel = jnp.where(rt_ref[s, pl.ds(row, 1), :] == slot_i, 1.0, 0.0)
            ges.append(jnp.sum(sel * gt_ref[s, pl.ds(row, 1), :], axis=-1, keepdims=True))
            xs.append(_dot(sel.astype(BF16), hb_ref[s * tsp:(s + 1) * tsp, :]).astype(BF16))
            sels.append(sel)
        xa = jnp.concatenate(xs, axis=0)
        a = _dot(xa, wg_ref[ee])
        u = _dot(xa, wu_ref[ee])
        he = (a * (1.0 / (1.0 + jnp.exp(-a)))) * u * jnp.concatenate(ges, axis=0)
        return _dot(he.astype(BF16), wd_ref[ee]), sels

    for ee in range(MOE_EXPERTS_PER_STEP):
        e = step * MOE_EXPERTS_PER_STEP + ee
        ye, _ = expert_rows(0, ee)
        for s in range(n_sub):
            y_ref[s, pl.ds(pl.multiple_of(e * slot, slot), slot), :] = ye[s * slot:(s + 1) * slot].astype(BF16)

    for ee in range(MOE_EXPERTS_PER_STEP):
        row = step * MOE_EXPERTS_PER_STEP + ee + N_GROUPS
        for k in range(1, n_chunk):
            last_rank = jnp.max(rt_ref[0, pl.ds(row, 1), :])
            for s in range(1, n_sub):
                last_rank = jnp.maximum(last_rank, jnp.max(rt_ref[s, pl.ds(row, 1), :]))

            @pl.when(last_rank >= float(k * slot))
            def _(k=k, ee=ee):
                ye_k, sels = expert_rows(k, ee)
                for s in range(n_sub):
                    acc_ref[s * tsp:(s + 1) * tsp, :] += _dot(sels[s].T.astype(BF16),
                                                              ye_k[s * slot:(s + 1) * slot].astype(BF16))

    @pl.when(step == pl.num_programs(2) - 1)
    def _():
        slot_l = lax.broadcasted_iota(I32, (1, slot), 1).astype(F32)
        for s in range(n_sub):
            rows = slice(s * ts, (s + 1) * ts)
            rank = rank_ref[s]
            sel_t = jnp.concatenate(
                [jnp.where(rank[:, N_GROUPS + j:N_GROUPS + j + 1] == slot_l, 1.0, 0.0).astype(BF16)
                 for j in range(N_EXPERTS)], axis=1)
            y = _dot(sel_t, y_ref[s])
            if n_chunk > 1:
                y = y + acc_ref[s * tsp:(s + 1) * tsp, :]
            o_ref[0, rows, :] = x_ref[0, rows, :] + _mod_rows(g2_ref, rows) * y


def moe_layer(x, gain, sc, sh, g2, w_router, b_router, w_gate, w_up, w_down):
    bsz, t, d = x.shape
    if t % LANES:
        y = moe_layer(_merge(x), gain, _per_token(sc, t), _per_token(sh, t), _per_token(g2, t),
                      w_router, b_router, w_gate, w_up, w_down)
        return y.reshape(x.shape)
    tm = min(t, 1024 if sc.shape[1] == 1 else 512)
    ts = min(tm, 512)
    assert ts % LANES == 0 and t % tm == 0, (t, tm, ts)
    tsp = ts
    slot = MOE_SLOT
    eps = MOE_EXPERTS_PER_STEP
    ne, _, de = w_gate.shape
    tri = (jnp.arange(tsp)[:, None] < jnp.arange(tsp)[None, :]).astype(BF16)
    kern = functools.partial(_moe_kernel, ts=ts, slot=slot)
    n_sub = tm // ts
    return pl.pallas_call(
        kern,
        grid=(bsz, t // tm, ne // eps),
        in_specs=[pl.BlockSpec((1, tm, d), lambda b, i, e: (b, i, 0)),
                  pl.BlockSpec((1, d), lambda b, i, e: (0, 0)),
                  _mod_spec(sc, tm, 3),
                  _mod_spec(sh, tm, 3),
                  _mod_spec(g2, tm, 3),
                  pl.BlockSpec((ROUTER_ROWS, d), lambda b, i, e: (0, 0)),
                  pl.BlockSpec((ROUTER_ROWS, 1), lambda b, i, e: (0, 0)),
                  pl.BlockSpec((tsp, tsp), lambda b, i, e: (0, 0)),
                  pl.BlockSpec((eps, d, de), lambda b, i, e: (e, 0, 0)),
                  pl.BlockSpec((eps, d, de), lambda b, i, e: (e, 0, 0)),
                  pl.BlockSpec((eps, de, d), lambda b, i, e: (e, 0, 0))],
        out_specs=pl.BlockSpec((1, tm, d), lambda b, i, e: (b, i, 0)),
        out_shape=jax.ShapeDtypeStruct((bsz, t, d), F32),
        scratch_shapes=[pltpu.VMEM((n_sub * tsp, d), BF16),
                        pltpu.VMEM((n_sub, tsp, LANES), F32),
                        pltpu.VMEM((n_sub, ROUTER_ROWS, tsp), F32),
                        pltpu.VMEM((n_sub, ROUTER_ROWS, tsp), F32),
                        pltpu.VMEM((n_sub, ne * slot, d), BF16),
                        pltpu.VMEM((n_sub * tsp if tsp > slot else 8, d), F32)],
        compiler_params=_cparams("parallel", "parallel", "arbitrary"),
        name="moe",
    )(x, gain.reshape(1, d), sc, sh, g2, w_router, b_router, tri, w_gate, w_up, w_down)


def _pad_cols(w, n):
    return jnp.pad(w, ((0, 0), (0, n - w.shape[1])))


def _pad_rows(a, n):
    return jnp.pad(a, ((0, 0), (0, n - a.shape[1])) + ((0, 0),) * (a.ndim - 2))


def _round_up(n, m):
    return -(-n // m) * m


def _head_cols(gain_q, nq, gain_k, nk, npad):
    cg = jnp.concatenate([jnp.tile(gain_q, nq), jnp.tile(gain_k, nk)])
    n = cg.shape[0]
    cgain = jnp.pad(cg, (0, npad - n)).reshape(1, npad)
    cflag = (jnp.arange(npad) < n).astype(F32).reshape(1, npad)
    return cgain, cflag


def _dsa_layer(x, mod, past, prm, bdiag, bsub):
    sh1, sc1, g1 = mod
    past_k, past_v, past_ki = past
    bsz, t, d = x.shape
    p = 0 if past_k is None else past_k.shape[1]
    n_keys = p + t
    topk = min(TOPK_MAX, n_keys // 4)
    tn = 768
    n_in = prm['w_in'].shape[1]
    npad = _round_up(n_in, tn)
    nqk = (A_HEADS + A_KV_HEADS) * HEAD_DIM
    w = _pad_cols(prm['w_in'], npad).astype(BF16)
    cgain, cflag = _head_cols(prm['q_norm'], A_HEADS, prm['k_norm'], A_KV_HEADS, npad)
    proj = ln_proj(x, prm['norm'], sc1, sh1, w, cgain, cflag, -(-nqk // tn), tn)
    o_k = A_HEADS * HEAD_DIM
    o_v = o_k + A_KV_HEADS * HEAD_DIM
    o_qi = o_v + A_KV_HEADS * HEAD_DIM
    o_ki = o_qi + IDX_HEADS * IDX_DIM
    k_new = proj[..., o_k:o_v].reshape(bsz, t, A_KV_HEADS, HEAD_DIM)
    v_new = proj[..., o_v:o_qi].reshape(bsz, t, A_KV_HEADS, HEAD_DIM)
    ki_new = proj[..., o_ki:o_ki + IDX_DIM]
    lp = _round_up(n_keys, KEY_TILE)
    kvw = A_KV_HEADS * HEAD_DIM
    if p:
        keys = _pad_rows(jnp.concatenate([past_k.reshape(bsz, p, kvw), proj[..., o_k:o_v]], axis=1), lp)
        values = _pad_rows(jnp.concatenate([past_v.reshape(bsz, p, kvw), proj[..., o_v:o_qi]], axis=1), lp)
        ikeys = _pad_rows(jnp.concatenate([past_ki, ki_new], axis=1), lp)
        ikeys = jnp.pad(ikeys, ((0, 0), (0, 0), (0, LANES - IDX_DIM)))
        k_blk0, v_blk0, ki_blk = 0, 0, 0
    else:
        keys = values = ikeys = proj
        k_blk0, v_blk0, ki_blk = o_k // LANES, o_v // LANES, o_ki // LANES
    mask = dsa_index_mask(proj, ikeys, lp=lp, t=t, q_off=p, topk=topk, qi_blk=o_qi // (4 * LANES),
                          w_blk=o_ki // LANES, ki_blk=ki_blk)
    o = dsa_attention(proj, keys, values, mask, bdiag, bsub, lp=lp, t=t, q_off=p, k_blk0=k_blk0, v_blk0=v_blk0)
    x = out_proj_residual(o, prm['w_out'].astype(BF16), x, g1)
    return x, (k_new, v_new, ki_new)


def _fox_layer(x, mod, past, prm):
    sh1, sc1, g1 = mod
    past_k, past_v, past_lf = past
    bsz, t, d = x.shape
    p = 0 if past_k is None else past_k.shape[1]
    n_keys = p + t
    hd = B_HEADS * HEAD_DIM
    tn = 512
    npad = _round_up(prm['w_in'].shape[1], tn)
    w = _pad_cols(prm['w_in'], npad).astype(BF16)
    cgain, cflag = _head_cols(prm['q_norm'], B_HEADS, prm['k_norm'], B_HEADS, npad)
    proj = ln_proj(x, prm['norm'], sc1, sh1, w, cgain, cflag, 2 * hd // tn, tn)
    k_new = proj[..., hd:2 * hd]
    v_new = proj[..., 2 * hd:3 * hd]
    fz = proj[..., 4 * hd:4 * hd + B_HEADS]
    tp = _round_up(t, LANES)
    fz_t = _pad_rows(fz, tp).transpose(0, 2, 1)
    past_t = None if not p else past_lf.transpose(0, 2, 1)
    lf_t, nck = fox_decay(fz_t, prm['forget_bias'], past_t, t=t)
    logf_new = lf_t[:, :, :t].transpose(0, 2, 1)
    lp = _round_up(n_keys, KEY_TILE)
    if p:
        keys = _pad_rows(jnp.concatenate([past_k.reshape(bsz, p, hd), k_new], axis=1), lp)
        values = _pad_rows(jnp.concatenate([past_v.reshape(bsz, p, hd), v_new], axis=1), lp)
        k_blk0, v_blk0 = 0, 0
    else:
        keys = values = proj
        k_blk0, v_blk0 = hd // LANES, 2 * hd // LANES
    pieces = jnp.pad(nck, ((0, 0), (0, 0), (0, 0), (0, lp - nck.shape[3]))).transpose(0, 3, 2, 1)
    zl = jnp.zeros((bsz, lp, HEAD_DIM - 3 * B_HEADS // 2), pieces.dtype)
    pz = jnp.concatenate([pieces[:, :, 1::2].reshape(bsz, lp, -1), zl,
                          pieces[:, :, 0::2].reshape(bsz, lp, -1), zl], axis=-1)
    o = fox_attention(proj, keys, values, pz, lp=lp, t=t, q_off=p, g_blk0=3 * hd // LANES,
                      k_blk0=k_blk0, v_blk0=v_blk0)
    x = out_proj_residual(o, prm['w_out'].astype(BF16), x, g1)
    return x, (k_new.reshape(bsz, t, B_HEADS, HEAD_DIM), v_new.reshape(bsz, t, B_HEADS, HEAD_DIM), logf_new)


def _hgrn2_layer(x, mod, s0, prm):
    sh1, sc1, g1 = mod
    bsz, t, d = x.shape
    npad = prm['w_in'].shape[1]
    zeros = jnp.zeros((1, npad), F32)
    proj = ln_proj(x, prm['norm'], sc1, sh1, prm['w_in'].astype(BF16), zeros, zeros, 0, 512)
    y, s_t = hgrn2_recurrence(proj, prm['lb'], prm['out_norm'], jnp.swapaxes(s0, -1, -2), t=t)
    x = out_proj_residual(y, prm['w_out'].astype(BF16), x, g1)
    return x, jnp.swapaxes(s_t, -1, -2)


def _trunk(x, c, a_k, a_v, a_kidx, b_k, b_v, b_logf, c_state, prm):
    bsz, t, d = x.shape
    mod_all = ada_mod(c, prm['w_ada'], prm['b_ada'])
    lb_all = jnp.cumsum(jax.nn.softmax(prm['c_lower_bound'].astype(F32), axis=0), axis=0)
    lb_all = lb_all - lb_all[0]
    bdiag, bsub = dsa_bias_tiles(prm['rel_table'])
    out_a, out_b, out_c = [], [], []
    for i in range(DEPTH):
        j = i // N_MIXERS
        kind = i % N_MIXERS
        sh1, sc1, g1, sh2, sc2, g2 = [m.reshape(bsz, 1, d) for m in jnp.split(mod_all[i], 6, axis=-1)]
        mod = (sh1, sc1, g1)
        if kind == 0:
            past = (None, None, None) if a_k is None else (a_k[j], a_v[j], a_kidx[j])
            lp = dict(norm=prm['norm_mix'][i], w_in=prm['a_w_in'][j], q_norm=prm['a_q_norm'][j],
                      k_norm=prm['a_k_norm'][j], w_out=prm['a_w_out'][j])
            x, new = _dsa_layer(x, mod, past, lp, bdiag, bsub)
            out_a.append(new)
        elif kind == 1:
            past = (None, None, None) if b_k is None else (b_k[j], b_v[j], b_logf[j])
            lp = dict(norm=prm['norm_mix'][i], w_in=prm['b_w_in'][j], forget_bias=prm['b_forget_bias'][j],
                      q_norm=prm['b_q_norm'][j], k_norm=prm['b_k_norm'][j], w_out=prm['b_w_out'][j])
            x, new = _fox_layer(x, mod, past, lp)
            out_b.append(new)
        else:
            s0 = jnp.zeros((bsz, C_HEADS, C_DK, C_DV), F32) if c_state is None else c_state[j]
            lp = dict(norm=prm['norm_mix'][i], w_in=prm['c_w_in'][j], lb=lb_all[i],
                      out_norm=prm['c_out_norm'][j], w_out=prm['c_w_out'][j])
            x, new = _hgrn2_layer(x, mod, s0, lp)
            out_c.append(new)
        w_router = jnp.pad(jnp.concatenate([prm['moe_w_group'][i], prm['moe_w_expert'][i]], axis=1).T,
                           ((0, ROUTER_ROWS - N_GROUPS - N_EXPERTS), (0, 0)))
        b_router = jnp.pad(jnp.concatenate([prm['moe_b_group'][i], prm['moe_b_expert'][i]]),
                           (0, ROUTER_ROWS - N_GROUPS - N_EXPERTS)).reshape(ROUTER_ROWS, 1)
        x = moe_layer(x, prm['norm_ffn'][i], sc2, sh2, g2, w_router, b_router,
                      prm['moe_w_gate'][i].astype(BF16), prm['moe_w_up'][i].astype(BF16),
                      prm['moe_w_down'][i].astype(BF16))
    stack = lambda outs, k: jnp.stack([o[k] for o in outs])
    return (x, stack(out_a, 0), stack(out_a, 1), stack(out_a, 2),
            stack(out_b, 0), stack(out_b, 1), stack(out_b, 2), jnp.stack(out_c))


def kernel(x_prompt, x_sample, cache_a_k, cache_a_v, cache_a_kidx, cache_b_k, cache_b_v, cache_b_logf, state_c,
           c_prompt, c_sample, rel_table, w_ada, b_ada, norm_mix, norm_ffn, a_w_in, a_q_norm, a_k_norm, a_w_out,
           b_w_in, b_forget_bias, b_q_norm, b_k_norm, b_w_out, c_w_in, c_lower_bound, c_out_norm, c_w_out,
           moe_w_group, moe_b_group, moe_w_expert, moe_b_expert, moe_w_gate, moe_w_up, moe_w_down):
    prm = {'rel_table': rel_table, 'w_ada': w_ada, 'b_ada': b_ada, 'norm_mix': norm_mix, 'norm_ffn': norm_ffn,
           'a_w_in': a_w_in, 'a_q_norm': a_q_norm, 'a_k_norm': a_k_norm, 'a_w_out': a_w_out,
           'b_w_in': b_w_in, 'b_forget_bias': b_forget_bias, 'b_q_norm': b_q_norm, 'b_k_norm': b_k_norm,
           'b_w_out': b_w_out, 'c_w_in': c_w_in, 'c_lower_bound': c_lower_bound, 'c_out_norm': c_out_norm,
           'c_w_out': c_w_out, 'moe_w_group': moe_w_group, 'moe_b_group': moe_b_group,
           'moe_w_expert': moe_w_expert, 'moe_b_expert': moe_b_expert, 'moe_w_gate': moe_w_gate,
           'moe_w_up': moe_w_up, 'moe_w_down': moe_w_down}
    (y_p, ak_p, av_p, ai_p, bk_p, bv_p, bl_p, cs_p) = _trunk(
        x_prompt, c_prompt, None, None, None, None, None, None, None, prm)
    (y_s, ak_s, av_s, ai_s, bk_s, bv_s, bl_s, cs_s) = _trunk(
        x_sample, c_sample, cache_a_k, cache_a_v, cache_a_kidx, cache_b_k, cache_b_v, cache_b_logf, state_c, prm)
    return (y_p, y_s, ak_p, av_p, ai_p, ak_s, av_s, ai_s, bk_p, bv_p, bl_p, bk_s, bv_s, bl_s, cs_p, cs_s)
```

```python
import functools

import jax
import jax.numpy as jnp
from jax import lax
from jax.experimental import pallas as pl
from jax.experimental.pallas import tpu as pltpu

F32 = jnp.float32
BF16 = jnp.bfloat16
I32 = jnp.int32

LANES = 128
VMEM_LIMIT_BYTES = 56 * 1024 * 1024

DEPTH = 4
N_MIXERS = 3
CHUNK = 64
EPS = 1e-6
HEAD_DIM = 64
A_HEADS = 16
A_KV_HEADS = 4
A_GROUP = A_HEADS // A_KV_HEADS
IDX_HEADS = 8
IDX_DIM = 64
TOPK_MAX = 256
REL_BUCKETS = 32
B_HEADS = 16
C_HEADS = 8
C_DK = 128
C_DV = 128
N_GROUPS = 4
EXPERTS_PER_GROUP = 4
N_EXPERTS = 16
D_EXPERT = 512

LOG2E = 1.4426950408889634
NEG_BIG = -1e30
M_FLOOR = -1e20
INT_MIN = -2 ** 31
KEY_TILE = 512
SLABS = KEY_TILE // LANES
MOE_SLOT = 128
VT_ROWS = LANES + 16
ROUTER_ROWS = 32
HGRN2_HEADS_PER_STEP = 8
HGRN2_CHUNK = 256
MOE_EXPERTS_PER_STEP = 2


def _cparams(*sem):
    return pltpu.CompilerParams(dimension_semantics=sem, vmem_limit_bytes=VMEM_LIMIT_BYTES)


def _nt(a, b):
    return lax.dot_general(a, b, (((1,), (1,)), ((), ())), preferred_element_type=F32)


def _split3(x):
    hi = x.astype(BF16)
    r = x - hi.astype(F32)
    mid = r.astype(BF16)
    lo = (r - mid.astype(F32)).astype(BF16)
    return hi, mid, lo


def _dot(a, b):
    return jnp.dot(a, b, preferred_element_type=F32)


def _dot_x01(x, m01):
    hi, mid, lo = _split3(x)
    return _dot(hi, m01) + _dot(mid, m01) + _dot(lo, m01)


def _dot_01x(m01, x):
    hi, mid, lo = _split3(x)
    return _dot(m01, hi) + _dot(m01, mid) + _dot(m01, lo)


def _dot_f32(a, b):
    ah, am, al = _split3(a)
    bh, bm, bl = _split3(b)
    return _dot(ah, bh) + (_dot(ah, bm) + _dot(am, bh)) + (_dot(ah, bl) + _dot(am, bm) + _dot(al, bh))


def _pad_q_rows(q, tq, tqp):
    if tqp == tq:
        return q
    return jnp.concatenate([q, jnp.zeros((tqp - tq, q.shape[1]), q.dtype)], axis=0)


def _mod_kernel(c_ref, w_ref, b_ref, o_ref):
    o_ref[0] = _dot(c_ref[...], w_ref[0]) + b_ref[0]


def ada_mod(c, w_ada, b_ada):
    nl, d, n6 = w_ada.shape
    bsz = c.shape[0]
    tn = 512
    return pl.pallas_call(
        _mod_kernel,
        grid=(nl, n6 // tn),
        in_specs=[pl.BlockSpec((bsz, d), lambda l, j: (0, 0)),
                  pl.BlockSpec((1, d, tn), lambda l, j: (l, 0, j)),
                  pl.BlockSpec((1, 1, tn), lambda l, j: (l, 0, j))],
        out_specs=pl.BlockSpec((1, bsz, tn), lambda l, j: (l, 0, j)),
        out_shape=jax.ShapeDtypeStruct((nl, bsz, n6), F32),
        compiler_params=_cparams("parallel", "parallel"),
        name="ada_mod",
    )(c, w_ada, b_ada.reshape(nl, 1, n6))


def _mod_spec(m, tm, grid_rank):
    d = m.shape[2]
    if m.shape[1] == 1:
        return pl.BlockSpec((1, 1, d), (lambda b, i, j: (b, 0, 0)) if grid_rank == 3 else (lambda b, i: (b, 0, 0)))
    return pl.BlockSpec((1, tm, d), (lambda b, i, j: (b, i, 0)) if grid_rank == 3 else (lambda b, i: (b, i, 0)))


def _mod_rows(ref, rows):
    return ref[0] if ref.shape[1] == 1 else ref[0, rows, :]


def _merge(x):
    return x.reshape(1, x.shape[0] * x.shape[1], x.shape[2])


def _per_token(m, t):
    bsz, _, d = m.shape
    return jnp.broadcast_to(m, (bsz, t, d)).reshape(1, bsz * t, d)


def _ln_mod(x, gain, sc, sh):
    ms = jnp.mean(x * x, axis=-1, keepdims=True)
    return (x * lax.rsqrt(ms + EPS) * gain) * (1.0 + sc) + sh


def _ln_proj_kernel(x_ref, gain_ref, sc_ref, sh_ref, w_ref, cgain_ref, cflag_ref, bd_ref, o_ref, h_ref,
                    *, n_norm_tiles, tn):
    j = pl.program_id(2)

    @pl.when(j == 0)
    def _():
        h_ref[...] = _ln_mod(x_ref[0], gain_ref[...], sc_ref[0], sh_ref[0]).astype(BF16)

    y = _dot(h_ref[...], w_ref[...])

    def plain():
        o_ref[0] = y

    def normed():
        y2 = y * y
        hi = y2.astype(BF16)
        lo = (y2 - hi.astype(F32)).astype(BF16)
        bd = bd_ref[...]
        segs = []
        for s in range(tn // LANES):
            sl = slice(s * LANES, (s + 1) * LANES)
            segs.append(_dot(hi[:, sl], bd) + _dot(lo[:, sl], bd))
        seg = jnp.concatenate(segs, axis=1)
        yn = y * lax.rsqrt(seg * (1.0 / HEAD_DIM) + EPS) * cgain_ref[...]
        o_ref[0] = jnp.where(cflag_ref[...] > 0.0, yn, y)

    if n_norm_tiles == 0:
        plain()
    else:
        pl.when(j < n_norm_tiles)(normed)
        pl.when(j >= n_norm_tiles)(plain)


def ln_proj(x, gain, sc, sh, w, cgain, cflag, n_norm_tiles, tn=256):
    bsz, t, d = x.shape
    if t % LANES:
        y = ln_proj(_merge(x), gain, _per_token(sc, t), _per_token(sh, t), w, cgain, cflag, n_norm_tiles, tn)
        return y.reshape(bsz, t, -1)
    npad = w.shape[1]
    tm = min(t, 2048 if sc.shape[1] == 1 else 1024)
    bd =(jnp.arange(LANES)[:, None] // HEAD_DIM == jnp.arange(LANES)[None, :] // HEAD_DIM).astype(BF16)
    kern = functools.partial(_ln_proj_kernel, n_norm_tiles=n_norm_tiles, tn=tn)
    return pl.pallas_call(
        kern,
        grid=(bsz, t // tm, npad // tn),
        in_specs=[pl.BlockSpec((1, tm, d), lambda b, i, j: (b, i, 0)),
                  pl.BlockSpec((1, d), lambda b, i, j: (0, 0)),
                  _mod_spec(sc, tm, 3),
                  _mod_spec(sh, tm, 3),
                  pl.BlockSpec((d, tn), lambda b, i, j: (0, j)),
                  pl.BlockSpec((1, tn), lambda b, i, j: (0, j)),
                  pl.BlockSpec((1, tn), lambda b, i, j: (0, j)),
                  pl.BlockSpec((LANES, LANES), lambda b, i, j: (0, 0))],
        out_specs=pl.BlockSpec((1, tm, tn), lambda b, i, j: (b, i, j)),
        out_shape=jax.ShapeDtypeStruct((bsz, t, npad), F32),
        scratch_shapes=[pltpu.VMEM((tm, d), BF16)],
        compiler_params=_cparams("parallel", "parallel", "arbitrary"),
        name="ln_proj",
    )(x, gain.reshape(1, d), sc, sh, w, cgain, cflag, bd)


def _out_proj_kernel(a_ref, w_ref, x_ref, g_ref, o_ref):
    y = _dot(a_ref[0].astype(BF16), w_ref[...])
    o_ref[0] = x_ref[0] + g_ref[0] * y


def out_proj_residual(a, w, x, gate):
    bsz, t, k = a.shape
    if t % LANES:
        return out_proj_residual(_merge(a), w, _merge(x), _per_token(gate, t)).reshape(x.shape)
    d = w.shape[1]
    tm = min(t, 1024)
    return pl.pallas_call(
        _out_proj_kernel,
        grid=(bsz, t // tm),
        in_specs=[pl.BlockSpec((1, tm, k), lambda b, i: (b, i, 0)),
                  pl.BlockSpec((k, d), lambda b, i: (0, 0)),
                  pl.BlockSpec((1, tm, d), lambda b, i: (b, i, 0)),
                  _mod_spec(gate, tm, 2)],
        out_specs=pl.BlockSpec((1, tm, d), lambda b, i: (b, i, 0)),
        out_shape=jax.ShapeDtypeStruct((bsz, t, d), F32),
        compiler_params=_cparams("parallel", "parallel"),
        name="out_proj",
    )(a, w, x, gate)


def _dsa_index_kernel(qi_ref, w_ref, ki_ref, o_ref, key_ref, *, tq, tqp, n_slabs, q_off, topk, idx_bits):
    a = pl.program_id(1)
    tk = KEY_TILE
    q0 = q_off + a * tq
    n_kt = (q0 + tq + tk - 1) // tk
    lane = lax.broadcasted_iota(I32, (tqp, LANES), 1)
    half = lane < IDX_DIM
    krow = lax.broadcasted_iota(I32, (tk, tqp), 0)
    qcol = lax.broadcasted_iota(I32, (tk, tqp), 1)
    qchunk = (q0 + qcol) >> 6
    srow = lax.broadcasted_iota(I32, (LANES, tqp), 0)
    w_t = (_pad_q_rows(w_ref[0], tq, tqp) * (IDX_HEADS ** -0.5)).T
    qs = []
    for p in range(IDX_HEADS // 2):
        qp = _pad_q_rows(qi_ref[0, :, p * LANES:(p + 1) * LANES], tq, tqp) * (IDX_DIM ** -0.5)
        qs.append(jnp.where(half, qp, 0.0).astype(BF16))
        qs.append(pltpu.roll(jnp.where(half, 0.0, qp), IDX_DIM, axis=1).astype(BF16))

    def score_tile(c, carry):
        kt = ki_ref[0, pl.ds(pl.multiple_of(c * tk, tk), tk), :].astype(BF16)
        sc = jnp.zeros((tk, tqp), F32)
        for h in range(IDX_HEADS):
            sc = sc + w_t[IDX_DIM + h:IDX_DIM + h + 1, :] * jnp.maximum(_nt(kt, qs[h]), 0.0)
        bits = lax.bitcast_convert_type(sc, I32)
        key = jnp.where(bits < 0, bits ^ 0x7FFFFFFF, bits)
        key = jnp.where(sc == 0.0, 0, key)
        adm = ((c * tk + krow) >> 6) <= qchunk
        key = jnp.where(adm, key, INT_MIN)
        for s_ in range(SLABS):
            key_ref[c * SLABS + s_] = key[s_ * LANES:(s_ + 1) * LANES, :]
        return carry

    lax.fori_loop(0, n_kt, score_tile, 0)

    def count(pred):
        def body(c, acc):
            for s_ in range(SLABS):
                sidx = c * SLABS + s_
                ind = jnp.where(pred(key_ref[sidx], sidx), 1.0, 0.0)
                acc = acc + jnp.sum(ind.reshape(LANES // 8, 8, tqp), axis=0)
            return acc
        acc = lax.fori_loop(0, n_kt, body, jnp.zeros((8, tqp), F32))
        return jnp.sum(acc, axis=0, keepdims=True)

    kf = float(topk)
    n_adm = count(lambda k, s: k > INT_MIN)

    def all_done(cnt_t):
        done = (cnt_t == kf) | (n_adm < kf)
        return (jnp.min(jnp.where(done, 1.0, 0.0)) > 0.0).astype(I32)

    def bit_cond(st):
        return (st[0] < 32) & (st[3] == 0)

    def bit_body(st):
        i, t_u, cnt_t, _ = st
        cand_u = t_u | lax.shift_left(jnp.int32(1), 31 - i)
        cand_s = cand_u ^ INT_MIN
        cnt = count(lambda k, s: k >= cand_s)
        take = cnt >= kf
        cnt_t = jnp.where(take, cnt, cnt_t)
        return i + 1, jnp.where(take, cand_u, t_u), cnt_t, all_done(cnt_t)

    _, t_u, cnt_ge, _ = lax.while_loop(bit_cond, bit_body,
                                       (jnp.int32(0), jnp.zeros((1, tqp), I32), n_adm, all_done(n_adm)))
    thr = t_u ^ INT_MIN
    excess = jnp.where(cnt_ge > kf, 1.0, 0.0)

    def tie_search():
        need = kf - count(lambda k, s: k > thr)

        def j_body(i, j):
            cand = j | lax.shift_left(jnp.int32(1), idx_bits - 1 - i)
            c = count(lambda k, s: (k == thr) & (srow + s * LANES < cand))
            return jnp.where(c < need, cand, j)
        return lax.fori_loop(0, idx_bits, j_body, jnp.zeros((1, tqp), I32))

    has_ties = jnp.max(excess) > 0.0

    @pl.when(has_ties)
    def _():
        j_last = tie_search()

        def write_ties(c, carry):
            for s_ in range(SLABS):
                sidx = c * SLABS + s_
                k = key_ref[sidx]
                sel = (k > thr) | ((k == thr) & (srow + sidx * LANES <= j_last))
                sel = sel & (k > INT_MIN)
                o_ref[0, sidx] = jnp.where(sel, 0.0, NEG_BIG)
            return carry

        lax.fori_loop(0, n_kt, write_ties, 0)

    @pl.when(jnp.logical_not(has_ties))
    def _():
        thr_adm = jnp.maximum(thr, INT_MIN + 1)

        def write_plain(c, carry):
            for s_ in range(SLABS):
                sidx = c * SLABS + s_
                o_ref[0, sidx] = jnp.where(key_ref[sidx] >= thr_adm, 0.0, NEG_BIG)
            return carry

        lax.fori_loop(0, n_kt, write_plain, 0)

    neg = jnp.full((LANES, tqp), NEG_BIG, F32)

    def write_inactive(s, carry):
        o_ref[0, s] = neg
        return carry

    lax.fori_loop(n_kt * SLABS, n_slabs, write_inactive, 0)


def dsa_index_mask(proj, keys, *, lp, t, q_off, topk, qi_blk, w_blk, ki_blk):
    bsz = proj.shape[0]
    n_slabs = lp // LANES
    tq = min(t, KEY_TILE)
    tqp = max(tq, LANES)
    idx_bits = max(1, (lp - 1).bit_length())
    kern = functools.partial(_dsa_index_kernel, tq=tq, tqp=tqp, n_slabs=n_slabs, q_off=q_off, topk=topk,
                             idx_bits=idx_bits)
    return pl.pallas_call(
        kern,
        grid=(bsz, t // tq),
        in_specs=[pl.BlockSpec((1, tq, 4 * LANES), lambda b, a: (b, a, qi_blk)),
                  pl.BlockSpec((1, tq, LANES), lambda b, a: (b, a, w_blk)),
                  pl.BlockSpec((1, lp, LANES), lambda b, a: (b, 0, ki_blk))],
        out_specs=pl.BlockSpec((1, n_slabs, LANES, tqp), lambda b, a: (b, 0, 0, a)),
        out_shape=jax.ShapeDtypeStruct((bsz, n_slabs, LANES, (t // tq) * tqp), F32),
        scratch_shapes=[pltpu.VMEM((n_slabs, LANES, tqp), I32)],
        compiler_params=_cparams("parallel", "parallel"),
        name="dsa_index",
    )(proj, proj, keys)


def _bias_kernel(tab_ref, diag_ref, sub_ref):
    h = pl.program_id(0)
    far = tab_ref[REL_BUCKETS // 2 - 1, h]

    def bias(rel):
        n = jnp.abs(rel)
        large = jnp.full(rel.shape, 8, I32)
        for th in (12, 16, 23, 32, 46, 64, 91):
            large = large + jnp.where(n >= th, 1, 0)
        bucket = jnp.where(rel > 0, REL_BUCKETS // 2, 0) + jnp.where(n < 8, n, large)
        acc = jnp.zeros(rel.shape, F32)
        for bk in range(REL_BUCKETS):
            acc = jnp.where(bucket == bk, tab_ref[bk, h], acc)
        return (acc - far) * LOG2E

    ik = lax.broadcasted_iota(I32, (KEY_TILE, KEY_TILE), 0)
    iq = lax.broadcasted_iota(I32, (KEY_TILE, KEY_TILE), 1)
    diag_ref[0] = bias(ik - iq)
    ik = lax.broadcasted_iota(I32, (LANES, LANES), 0)
    iq = lax.broadcasted_iota(I32, (LANES, LANES), 1)
    sub_ref[0] = bias(ik - LANES - iq)


def dsa_bias_tiles(rel_table):
    return pl.pallas_call(
        _bias_kernel,
        grid=(A_HEADS,),
        in_specs=[pl.BlockSpec(memory_space=pltpu.SMEM)],
        out_specs=[pl.BlockSpec((1, KEY_TILE, KEY_TILE), lambda h: (h, 0, 0)),
                   pl.BlockSpec((1, LANES, LANES), lambda h: (h, 0, 0))],
        out_shape=[jax.ShapeDtypeStruct((A_HEADS, KEY_TILE, KEY_TILE), F32),
                   jax.ShapeDtypeStruct((A_HEADS, LANES, LANES), F32)],
        compiler_params=_cparams("parallel"),
        name="dsa_bias",
    )(rel_table)


def _vt_rows(v_tile):
    return jnp.concatenate([v_tile.T.astype(BF16), jnp.ones((VT_ROWS - LANES, v_tile.shape[0]), BF16)], axis=0)


def _dsa_attn_kernel(q_ref, k_ref, v_ref, msk_ref, bd_ref, bs_ref, o_ref, acc_ref, m_ref, s0_ref, vt_ref,
                     *, tq, tqp, q_off, nt):
    a = pl.program_id(1)
    g = pl.program_id(2)
    tk = KEY_TILE
    q0 = q_off + a * tq
    cd = q0 // tk
    cs = jnp.maximum(cd - 1, 0)
    par = g % 2
    lane = lax.broadcasted_iota(I32, (tqp, LANES), 1)
    half = lane < HEAD_DIM

    @pl.when(a == 0)
    def _():
        for c in range(nt):
            vt_ref[g, c] = _vt_rows(v_ref[0, c * tk:(c + 1) * tk, :])

    qs = []
    for e in range(A_GROUP):
        qc = _pad_q_rows(q_ref[0, :, (e // 2) * LANES:(e // 2 + 1) * LANES], tq, tqp) * (HEAD_DIM ** -0.5 * LOG2E)
        own = jnp.where(half, qc, 0.0) if e % 2 == 0 else jnp.where(half, 0.0, qc)
        qs.append(jnp.where(par == e % 2, own, pltpu.roll(own, HEAD_DIM, axis=1)).astype(BF16))
    m_ref[...] = jnp.full(m_ref.shape, M_FLOOR, F32)
    acc_ref[...] = jnp.zeros(acc_ref.shape, F32)

    def s_tile(e, c):
        return _nt(k_ref[0, pl.ds(pl.multiple_of(c * tk, tk), tk), :].astype(BF16), qs[e])

    def softmax_pv(e, c, s, after, kind):
        s = s + jnp.concatenate([msk_ref[0, c * SLABS + i] for i in range(SLABS)], axis=0)
        if kind == "diag":
            s = s + bd_ref[e, :, 0:tqp]
        elif kind == "sub":
            corner = bs_ref[e]
            if tqp > LANES:
                corner = jnp.concatenate([corner, jnp.zeros((LANES, tqp - LANES), F32)], axis=1)
            s = s + jnp.concatenate([jnp.zeros((tk - LANES, tqp), F32), corner], axis=0)
            s = jnp.where(cd > 0, s, NEG_BIG)
        m_old = jnp.minimum(m_ref[e], jnp.maximum(after[0:1, :], -NEG_BIG))
        m_new = jnp.maximum(m_old, jnp.max(s, axis=0, keepdims=True))
        p = jnp.exp2(s - m_new).astype(BF16)
        acc_ref[e] = jnp.exp2(m_old - m_new) * acc_ref[e] + _dot(vt_ref[g, c], p)
        m_ref[e] = m_new

    def step(c, kind):
        s_prev = s0_ref[...]
        for e in range(A_GROUP):
            if e + 1 < A_GROUP:
                s_next = s_tile(e + 1, c)
            else:
                s_next = s_tile(0, jnp.minimum(c + 1, cd))
                s0_ref[...] = s_next
            softmax_pv(e, c, s_prev, s_next, kind)
            s_prev = s_next

    def far_body(c, carry):
        step(c, "far")
        return carry

    def far_pair(i, carry):
        step(2 * i, "far")
        step(2 * i + 1, "far")
        return carry

    s0_ref[...] = s_tile(0, 0)
    lax.fori_loop(0, cs // 2, far_pair, 0)
    lax.fori_loop((cs // 2) * 2, cs, far_body, 0)
    step(cs, "sub")
    step(cd, "diag")
    outs = []
    for e in range(A_GROUP):
        o_t = (acc_ref[e, 0:LANES, :] * (1.0 / acc_ref[e, LANES:LANES + 1, :])).T
        outs.append(jnp.where(par == e % 2, o_t, pltpu.roll(o_t, HEAD_DIM, axis=1)))
    for c2 in range(A_GROUP // 2):
        o_ref[0, :, c2 * LANES:(c2 + 1) * LANES] = jnp.where(half, outs[2 * c2], outs[2 * c2 + 1])[:tq]


def dsa_attention(proj, keys, values, mask, bdiag, bsub, *, lp, t, q_off, k_blk0, v_blk0):
    bsz = proj.shape[0]
    tq = min(t, KEY_TILE)
    tqp = max(tq, LANES)
    nt = lp // KEY_TILE
    nsl = mask.shape[1]
    kern = functools.partial(_dsa_attn_kernel, tq=tq, tqp=tqp, q_off=q_off, nt=nt)
    return pl.pallas_call(
        kern,
        grid=(bsz, t // tq, A_KV_HEADS),
        in_specs=[pl.BlockSpec((1, tq, 2 * LANES), lambda b, a, g: (b, a, g)),
                  pl.BlockSpec((1, lp, LANES), lambda b, a, g: (b, 0, k_blk0 + g // 2)),
                  pl.BlockSpec((1, lp, LANES), lambda b, a, g: (b, 0, v_blk0 + g // 2)),
                  pl.BlockSpec((1, nsl, LANES, tqp), lambda b, a, g: (b, 0, 0, a)),
                  pl.BlockSpec((A_GROUP, KEY_TILE, tqp), lambda b, a, g: (g, 0, 0)),
                  pl.BlockSpec((A_GROUP, LANES, LANES), lambda b, a, g: (g, 0, 0))],
        out_specs=pl.BlockSpec((1, tq, 2 * LANES), lambda b, a, g: (b, a, g)),
        out_shape=jax.ShapeDtypeStruct((bsz, t, A_HEADS * HEAD_DIM), F32),
        scratch_shapes=[pltpu.VMEM((A_GROUP, VT_ROWS, tqp), F32), pltpu.VMEM((A_GROUP, 1, tqp), F32),
                        pltpu.VMEM((KEY_TILE, tqp), F32),
                        pltpu.VMEM((A_KV_HEADS, nt, VT_ROWS, KEY_TILE), BF16)],
        compiler_params=_cparams("parallel", "arbitrary", "arbitrary"),
        name="dsa_attn",
    )(proj, keys, values, mask, bdiag, bsub)


def _fox_decay_kernel(*refs, n_past, n_new, t):
    if n_past:
        fz_ref, bf_ref, past_ref, tri_ref, lf_ref, nck_ref = refs
    else:
        fz_ref, bf_ref, tri_ref, lf_ref, nck_ref = refs
        past_ref = None
    tri = tri_ref[...]
    lane = lax.broadcasted_iota(I32, (B_HEADS, LANES), 1)
    carry = jnp.zeros((B_HEADS, 1), F32)
    for blk in range(n_past + n_new):
        sl = slice(blk * LANES, (blk + 1) * LANES)
        if blk < n_past:
            lf = past_ref[0, :, sl]
        else:
            nsl = slice((blk - n_past) * LANES, (blk - n_past + 1) * LANES)
            x = fz_ref[0, :, nsl] + bf_ref[...]
            lf = jnp.minimum(x, 0.0) - jnp.log1p(jnp.exp(-jnp.abs(x)))
            lf = jnp.where(lane + (blk - n_past) * LANES < t, lf, 0.0)
            lf_ref[0, :, nsl] = lf
        cum = _dot_x01(lf, tri) + carry
        for i, piece in enumerate(_split3(cum * -LOG2E)):
            nck_ref[0, i, :, sl] = piece
        carry = cum[:, LANES - 1:LANES]


def fox_decay(fz_t, b_f, past_t, *, t):
    bsz, h, tp = fz_t.shape
    p = 0 if past_t is None else past_t.shape[2]
    n_past, n_new = p // LANES, tp // LANES
    tri = (jnp.arange(LANES)[:, None] <= jnp.arange(LANES)[None, :]).astype(BF16)
    kern = functools.partial(_fox_decay_kernel, n_past=n_past, n_new=n_new, t=t)
    args = [fz_t, b_f.reshape(h, 1)]
    in_specs = [pl.BlockSpec((1, h, tp), lambda b: (b, 0, 0)),
                pl.BlockSpec((h, 1), lambda b: (0, 0))]
    if n_past:
        args.append(past_t)
        in_specs.append(pl.BlockSpec((1, h, p), lambda b: (b, 0, 0)))
    args.append(tri)
    in_specs.append(pl.BlockSpec((LANES, LANES), lambda b: (0, 0)))
    return pl.pallas_call(
        kern,
        grid=(bsz,),
        in_specs=in_specs,
        out_specs=[pl.BlockSpec((1, h, tp), lambda b: (b, 0, 0)),
                   pl.BlockSpec((1, 3, h, p + tp), lambda b: (b, 0, 0, 0))],
        out_shape=[jax.ShapeDtypeStruct((bsz, h, tp), F32),
                   jax.ShapeDtypeStruct((bsz, 3, h, p + tp), BF16)],
        compiler_params=_cparams("parallel"),
        name="fox_decay",
    )(*args)


def _fox_attn_kernel(q_ref, k_ref, v_ref, pz_ref, g_ref, o_ref, acc_ref, m_ref, s0_ref, ka_ref, vt_ref,
                     *, tq, tqp, q_off, nt):
    j = pl.program_id(1)
    a = pl.program_id(2)
    tk = KEY_TILE
    q0 = q_off + a * tq
    n_full = q0 // tk
    n_need = (q0 + tq - 1) // tk + 1

    @pl.when(a == 0)
    def _():
        klane = lax.broadcasted_iota(I32, (tk, LANES), 1)
        for c in range(nt):
            rows = slice(c * tk, (c + 1) * tk)
            kp = k_ref[0, rows, :]
            pz = pz_ref[0, rows, :].astype(F32)
            ka_ref[0, rows, :] = jnp.where(klane < HEAD_DIM, kp, pz).astype(BF16)
            ka_ref[1, rows, :] = jnp.where(klane >= HEAD_DIM, kp, pz).astype(BF16)
            vt_ref[c] = _vt_rows(v_ref[0, rows, :])

    lane = lax.broadcasted_iota(I32, (tqp, LANES), 1)
    qn = _pad_q_rows(q_ref[0], tq, tqp) * (HEAD_DIM ** -0.5 * LOG2E)
    ones_e = (lane >= HEAD_DIM + 3 * j) & (lane < HEAD_DIM + 3 * j + 3)
    ones_o = (lane >= 3 * j) & (lane < 3 * j + 3)
    qs = (jnp.where(lane < HEAD_DIM, qn, jnp.where(ones_e, 1.0, 0.0)).astype(BF16),
          jnp.where(lane >= HEAD_DIM, qn, jnp.where(ones_o, 1.0, 0.0)).astype(BF16))
    m_ref[...] = jnp.full(m_ref.shape, M_FLOOR, F32)
    acc_ref[...] = jnp.zeros(acc_ref.shape, F32)
    krow = lax.broadcasted_iota(I32, (tk, tqp), 0)
    qcol = lax.broadcasted_iota(I32, (tk, tqp), 1)

    def s_tile(e, c):
        return _nt(ka_ref[e, pl.ds(pl.multiple_of(c * tk, tk), tk), :], qs[e])

    def softmax_pv(e, c, s, masked):
        if masked:
            s = jnp.where(c * tk + krow <= q0 + qcol, s, NEG_BIG)
        m_old = m_ref[e]
        m_new = jnp.maximum(m_old, jnp.max(s, axis=0, keepdims=True))
        p = jnp.exp2(s - m_new).astype(BF16)
        acc_ref[e] = jnp.exp2(m_old - m_new) * acc_ref[e] + _dot(vt_ref[c], p)
        m_ref[e] = m_new

    def step(c, masked):
        s1 = s_tile(1, c)
        softmax_pv(0, c, s0_ref[...], masked)
        s0_ref[...] = s_tile(0, jnp.minimum(c + 1, n_need - 1))
        softmax_pv(1, c, s1, masked)

    def full_body(c, carry):
        step(c, False)
        return carry

    def masked_body(c, carry):
        step(c, True)
        return carry

    def pair_body(i, carry):
        step(2 * i, False)
        step(2 * i + 1, False)
        return carry

    s0_ref[...] = s_tile(0, 0)
    lax.fori_loop(0, n_full // 2, pair_body, 0)
    lax.fori_loop((n_full // 2) * 2, n_full, full_body, 0)
    lax.fori_loop(n_full, n_need, masked_body, 0)
    o_e = (acc_ref[0, 0:LANES, :] * (1.0 / acc_ref[0, LANES:LANES + 1, :])).T
    o_o = (acc_ref[1, 0:LANES, :] * (1.0 / acc_ref[1, LANES:LANES + 1, :])).T
    o = jnp.where(lane < HEAD_DIM, o_e, o_o)
    o_ref[0] = o[:tq] * (1.0 / (1.0 + jnp.exp(-g_ref[0])))


def fox_attention(proj, keys, values, pz, *, lp, t, q_off, g_blk0, k_blk0, v_blk0):
    bsz = proj.shape[0]
    tq = min(t, KEY_TILE)
    tqp = max(tq, LANES)
    nt = lp // KEY_TILE
    kern = functools.partial(_fox_attn_kernel, tq=tq, tqp=tqp, q_off=q_off, nt=nt)
    return pl.pallas_call(
        kern,
        grid=(bsz, B_HEADS // 2, t // tq),
        in_specs=[pl.BlockSpec((1, tq, LANES), lambda b, j, a: (b, a, j)),
                  pl.BlockSpec((1, lp, LANES), lambda b, j, a: (b, 0, k_blk0 + j)),
                  pl.BlockSpec((1, lp, LANES), lambda b, j, a: (b, 0, v_blk0 + j)),
                  pl.BlockSpec((1, lp, LANES), lambda b, j, a: (b, 0, 0)),
                  pl.BlockSpec((1, tq, LANES), lambda b, j, a: (b, a, g_blk0 + j))],
        out_specs=pl.BlockSpec((1, tq, LANES), lambda b, j, a: (b, a, j)),
        out_shape=jax.ShapeDtypeStruct((bsz, t, B_HEADS * HEAD_DIM), F32),
        scratch_shapes=[pltpu.VMEM((2, VT_ROWS, tqp), F32), pltpu.VMEM((2, 1, tqp), F32),
                        pltpu.VMEM((KEY_TILE, tqp), F32),
                        pltpu.VMEM((2, lp, LANES), BF16),
                        pltpu.VMEM((nt, VT_ROWS, KEY_TILE), BF16)],
        compiler_params=_cparams("parallel", "parallel", "arbitrary"),
        name="fox_attn",
    )(proj, keys, values, pz, proj)


def _hgrn2_levels(tc):
    lv = []
    n = 8
    while n < tc:
        lv.append(n)
        n *= 2
    return lv


def _hgrn2_masks(tc):
    t = jnp.arange(tc)[:, None]
    s = jnp.arange(tc)[None, :]
    ms = [((t // (2 * n) == s // (2 * n)) & ((t // n) % 2 == 1) & ((s // n) % 2 == 0)) for n in _hgrn2_levels(tc)]
    ms.append((t // 8 == s // 8) & (s <= t))
    return jnp.stack(ms).astype(F32)


def _hgrn2_head(q, z, v, g, lb, og, st, tri_ref, msk_ref, tc):
    ez = jnp.exp(-jnp.abs(z))
    den = 1.0 / (1.0 + ez)
    pos = z >= 0.0
    f = lb + (1.0 - lb) * (jnp.where(pos, 1.0, ez) * den)
    kk = (1.0 - lb) * (jnp.where(pos, ez, 1.0) * den)
    cum = _dot_01x(tri_ref[...], jnp.log(f))

    def rows(idx):
        parts = []
        for i in idx:
            parts.append(jnp.zeros((8, LANES), F32) if i < 0 else jnp.broadcast_to(cum[i:i + 1, :], (8, LANES)))
        return jnp.concatenate(parts, axis=0)

    levels = _hgrn2_levels(tc)
    ngrp = tc // 8
    scores = jnp.zeros((tc, tc), F32)
    ql8 = None
    for li, n in enumerate(levels):
        start = [((r * 8) // n) * n for r in range(ngrp)]
        a_start = rows([s - 1 for s in start])
        a_end = rows([s + n - 1 for s in start])
        ql = (q * jnp.exp(cum - a_start)).astype(BF16)
        kr = (kk * jnp.exp(a_end - cum)).astype(BF16)
        scores = scores + msk_ref[li] * _nt(ql, kr)
        if n == 8:
            ql8 = ql
            kb = (kk * jnp.exp(a_start - cum)).astype(BF16)
    if ql8 is None:
        a_start = rows([r * 8 - 1 for r in range(ngrp)])
        ql8 = (q * jnp.exp(cum - a_start)).astype(BF16)
        kb = (kk * jnp.exp(a_start - cum)).astype(BF16)
    scores = scores + msk_ref[len(levels)] * _nt(ql8, kb)

    o = _nt((q * jnp.exp(cum)).astype(BF16), st.astype(BF16)) + _dot(scores.astype(BF16), v.astype(BF16))
    a_last = cum[tc - 1:tc, :]
    khat = (kk * jnp.exp(a_last - cum)).astype(BF16)
    st_new = st * jnp.exp(a_last) + _dot(v.T.astype(BF16), khat)
    on = o * lax.rsqrt(jnp.mean(o * o, axis=-1, keepdims=True) + EPS) * og
    return on * (g * (1.0 / (1.0 + jnp.exp(-g)))), st_new


def _hgrn2_kernel(q_ref, fz_ref, v_ref, g_ref, lb_ref, og_ref, s0_ref, tri_ref, msk_ref, y_ref, so_ref, st_ref, *, tc):
    ct = pl.program_id(2)

    @pl.when(ct == 0)
    def _():
        st_ref[...] = s0_ref[0]

    for e in range(HGRN2_HEADS_PER_STEP):
        sl = slice(e * LANES, (e + 1) * LANES)
        y, st_new = _hgrn2_head(q_ref[0, :, sl], fz_ref[0, :, sl], v_ref[0, :, sl], g_ref[0, :, sl], lb_ref[e],
                                og_ref[...], st_ref[e], tri_ref, msk_ref, tc)
        y_ref[0, :, sl] = y
        st_ref[e] = st_new

    @pl.when(ct == pl.num_programs(2) - 1)
    def _():
        so_ref[0] = st_ref[...]


def hgrn2_recurrence(proj, lb, out_gain, s0_t, *, t):
    bsz = proj.shape[0]
    tc = min(t, HGRN2_CHUNK)
    nlv = len(_hgrn2_levels(tc)) + 1
    tri = (jnp.arange(tc)[:, None] >= jnp.arange(tc)[None, :]).astype(BF16)
    h = C_HEADS
    hps = HGRN2_HEADS_PER_STEP
    ng = h // hps
    kern = functools.partial(_hgrn2_kernel, tc=tc)
    blk = lambda off: pl.BlockSpec((1, tc, hps * LANES), lambda b, hh, c: (b, c, off + hh))
    return pl.pallas_call(
        kern,
        grid=(bsz, ng, t // tc),
        in_specs=[blk(0), blk(ng), blk(2 * ng), blk(3 * ng),
                  pl.BlockSpec((hps, 1, C_DK), lambda b, hh, c: (hh, 0, 0)),
                  pl.BlockSpec((1, C_DV), lambda b, hh, c: (0, 0)),
                  pl.BlockSpec((1, hps, C_DV, C_DK), lambda b, hh, c: (b, hh, 0, 0)),
                  pl.BlockSpec((tc, tc), lambda b, hh, c: (0, 0)),
                  pl.BlockSpec((nlv, tc, tc), lambda b, hh, c: (0, 0, 0))],
        out_specs=[pl.BlockSpec((1, tc, hps * LANES), lambda b, hh, c: (b, c, hh)),
                   pl.BlockSpec((1, hps, C_DV, C_DK), lambda b, hh, c: (b, hh, 0, 0))],
        out_shape=[jax.ShapeDtypeStruct((bsz, t, h * C_DV), F32),
                   jax.ShapeDtypeStruct((bsz, h, C_DV, C_DK), F32)],
        scratch_shapes=[pltpu.VMEM((hps, C_DV, C_DK), F32)],
        compiler_params=_cparams("parallel", "parallel", "arbitrary"),
        name="hgrn2",
    )(proj, proj, proj, proj, lb.reshape(h, 1, C_DK), out_gain.reshape(1, C_DV), s0_t, tri, _hgrn2_masks(tc))


def _route_t(h, wr_t, br_t):
    ah, am, al = _split3(wr_t)
    bh, bm, bl = _split3(h)
    r = _nt(ah, bh) + (_nt(ah, bm) + _nt(am, bh)) + (_nt(ah, bl) + _nt(am, bm) + _nt(al, bh)) + br_t
    row = lax.broadcasted_iota(I32, r.shape, 0)
    rowf = row.astype(F32)
    big = float(ROUTER_ROWS)
    is_g = row < N_GROUPS
    lg = jnp.where(is_g, r, -jnp.inf)
    mg = jnp.max(lg, axis=0, keepdims=True)
    grp = jnp.min(jnp.where(lg == mg, rowf, big), axis=0, keepdims=True)
    p_grp = 1.0 / jnp.sum(jnp.where(is_g, jnp.exp(r - mg), 0.0), axis=0, keepdims=True)
    eg = ((row - N_GROUPS) >> 2).astype(F32)
    in_e = (row >= N_GROUPS) & (row < N_GROUPS + N_EXPERTS) & (eg == grp)
    le = jnp.where(in_e, r, -jnp.inf)
    v1 = jnp.max(le, axis=0, keepdims=True)
    i1 = jnp.min(jnp.where(le == v1, rowf, big), axis=0, keepdims=True)
    le2 = jnp.where(rowf == i1, -jnp.inf, le)
    v2 = jnp.max(le2, axis=0, keepdims=True)
    i2 = jnp.min(jnp.where(le2 == v2, rowf, big), axis=0, keepdims=True)
    e2 = jnp.exp(v2 - v1)
    w1 = 1.0 / (1.0 + e2)
    gates = jnp.where(rowf == i1, w1 * p_grp, 0.0) + jnp.where(rowf == i2, (e2 * w1) * p_grp, 0.0)
    member = jnp.where((rowf == i1) | (rowf == i2), 1.0, 0.0)
    return gates, member


def _moe_kernel(x_ref, gain_ref, sc_ref, sh_ref, g2_ref, wr_ref, br_ref, tri_ref, wg_ref, wu_ref, wd_ref, o_ref,
                hb_ref, rank_ref, rt_ref, gt_ref, y_ref, acc_ref, *, ts, slot):
    step = pl.program_id(2)
    tm = x_ref.shape[1]
    n_sub = tm // ts
    tsp = ts
    n_chunk = tsp // slot

    @pl.when(step == 0)
    def _():
        for s in range(n_sub):
            rows = slice(s * ts, (s + 1) * ts)
            h = _ln_mod(x_ref[0, rows, :], gain_ref[...], _mod_rows(sc_ref, rows), _mod_rows(sh_ref, rows))
            hb_ref[rows, :] = h.astype(BF16)
            gates_t, member_t = _route_t(h, wr_ref[...], br_ref[...])
            r_t = _dot(member_t.astype(BF16), tri_ref[...])
            r_t = jnp.where(member_t > 0.0, r_t, -1.0)
            rt_ref[s] = r_t
            gt_ref[s] = gates_t
            rank_ref[s] = jnp.concatenate([r_t, jnp.full((LANES - ROUTER_ROWS, ts), -1.0, F32)], axis=0).T
        if n_chunk > 1:
            acc_ref[...] = jnp.zeros_like(acc_ref)

    def expert_rows(k, ee):
        row = step * MOE_EXPERTS_PER_STEP + ee + N_GROUPS
        xs, ges, sels = [], [], []
        slot_i = lax.broadcasted_iota(I32, (slot, tsp), 0).astype(F32) + float(k * slot)
        for s in range(n_sub):
            sel = jnp.where(rt_ref[s, pl.ds(row, 1), :] == slot_i, 1.0, 0.0)
            ges.append(jnp.sum(sel * gt_ref[s, pl.ds(row, 1), :], axis=-1, keepdims=True))
            xs.append(_dot(sel.astype(BF16), hb_ref[s * tsp:(s + 1) * tsp, :]).astype(BF16))
            sels.append(sel)
        xa = jnp.concatenate(xs, axis=0)
        a = _dot(xa, wg_ref[ee])
        u = _dot(xa, wu_ref[ee])
        he = (a * (1.0 / (1.0 + jnp.exp(-a)))) * u * jnp.concatenate(ges, axis=0)
        return _dot(he.astype(BF16), wd_ref[ee]), sels

    for ee in range(MOE_EXPERTS_PER_STEP):
        e = step * MOE_EXPERTS_PER_STEP + ee
        ye, _ = expert_rows(0, ee)
        for s in range(n_sub):
            y_ref[s, pl.ds(pl.multiple_of(e * slot, slot), slot), :] = ye[s * slot:(s + 1) * slot].astype(BF16)

    for ee in range(MOE_EXPERTS_PER_STEP):
        row = step * MOE_EXPERTS_PER_STEP + ee + N_GROUPS
        for k in range(1, n_chunk):
            last_rank = jnp.max(rt_ref[0, pl.ds(row, 1), :])
            for s in range(1, n_sub):
                last_rank = jnp.maximum(last_rank, jnp.max(rt_ref[s, pl.ds(row, 1), :]))

            @pl.when(last_rank >= float(k * slot))
            def _(k=k, ee=ee):
                ye_k, sels = expert_rows(k, ee)
                for s in range(n_sub):
                    acc_ref[s * tsp:(s + 1) * tsp, :] += _dot(sels[s].T.astype(BF16),
                                                              ye_k[s * slot:(s + 1) * slot].astype(BF16))

    @pl.when(step == pl.num_programs(2) - 1)
    def _():
        slot_l = lax.broadcasted_iota(I32, (1, slot), 1).astype(F32)
        for s in range(n_sub):
            rows = slice(s * ts, (s + 1) * ts)
            rank = rank_ref[s]
            sel_t = jnp.concatenate(
                [jnp.where(rank[:, N_GROUPS + j:N_GROUPS + j + 1] == slot_l, 1.0, 0.0).astype(BF16)
                 for j in range(N_EXPERTS)], axis=1)
            y = _dot(sel_t, y_ref[s])
            if n_chunk > 1:
                y = y + acc_ref[s * tsp:(s + 1) * tsp, :]
            o_ref[0, rows, :] = x_ref[0, rows, :] + _mod_rows(g2_ref, rows) * y


def moe_layer(x, gain, sc, sh, g2, w_router, b_router, w_gate, w_up, w_down):
    bsz, t, d = x.shape
    if t % LANES:
        y = moe_layer(_merge(x), gain, _per_token(sc, t), _per_token(sh, t), _per_token(g2, t),
                      w_router, b_router, w_gate, w_up, w_down)
        return y.reshape(x.shape)
    tm = min(t, 1024 if sc.shape[1] == 1 else 512)
    ts = min(tm, 512)
    assert ts % LANES == 0 and t % tm == 0, (t, tm, ts)
    tsp = ts
    slot = MOE_SLOT
    eps = MOE_EXPERTS_PER_STEP
    ne, _, de = w_gate.shape
    tri = (jnp.arange(tsp)[:, None] < jnp.arange(tsp)[None, :]).astype(BF16)
    kern = functools.partial(_moe_kernel, ts=ts, slot=slot)
    n_sub = tm // ts
    return pl.pallas_call(
        kern,
        grid=(bsz, t // tm, ne // eps),
        in_specs=[pl.BlockSpec((1, tm, d), lambda b, i, e: (b, i, 0)),
                  pl.BlockSpec((1, d), lambda b, i, e: (0, 0)),
                  _mod_spec(sc, tm, 3),
                  _mod_spec(sh, tm, 3),
                  _mod_spec(g2, tm, 3),
                  pl.BlockSpec((ROUTER_ROWS, d), lambda b, i, e: (0, 0)),
                  pl.BlockSpec((ROUTER_ROWS, 1), lambda b, i, e: (0, 0)),
                  pl.BlockSpec((tsp, tsp), lambda b, i, e: (0, 0)),
                  pl.BlockSpec((eps, d, de), lambda b, i, e: (e, 0, 0)),
                  pl.BlockSpec((eps, d, de), lambda b, i, e: (e, 0, 0)),
                  pl.BlockSpec((eps, de, d), lambda b, i, e: (e, 0, 0))],
        out_specs=pl.BlockSpec((1, tm, d), lambda b, i, e: (b, i, 0)),
        out_shape=jax.ShapeDtypeStruct((bsz, t, d), F32),
        scratch_shapes=[pltpu.VMEM((n_sub * tsp, d), BF16),
                        pltpu.VMEM((n_sub, tsp, LANES), F32),
                        pltpu.VMEM((n_sub, ROUTER_ROWS, tsp), F32),
                        pltpu.VMEM((n_sub, ROUTER_ROWS, tsp), F32),
                        pltpu.VMEM((n_sub, ne * slot, d), BF16),
                        pltpu.VMEM((n_sub * tsp if tsp > slot else 8, d), F32)],
        compiler_params=_cparams("parallel", "parallel", "arbitrary"),
        name="moe",
    )(x, gain.reshape(1, d), sc, sh, g2, w_router, b_router, tri, w_gate, w_up, w_down)


def _pad_cols(w, n):
    return jnp.pad(w, ((0, 0), (0, n - w.shape[1])))


def _pad_rows(a, n):
    return jnp.pad(a, ((0, 0), (0, n - a.shape[1])) + ((0, 0),) * (a.ndim - 2))


def _round_up(n, m):
    return -(-n // m) * m


def _head_cols(gain_q, nq, gain_k, nk, npad):
    cg = jnp.concatenate([jnp.tile(gain_q, nq), jnp.tile(gain_k, nk)])
    n = cg.shape[0]
    cgain = jnp.pad(cg, (0, npad - n)).reshape(1, npad)
    cflag = (jnp.arange(npad) < n).astype(F32).reshape(1, npad)
    return cgain, cflag


def _dsa_layer(x, mod, past, prm, bdiag, bsub):
    sh1, sc1, g1 = mod
    past_k, past_v, past_ki = past
    bsz, t, d = x.shape
    p = 0 if past_k is None else past_k.shape[1]
    n_keys = p + t
    topk = min(TOPK_MAX, n_keys // 4)
    tn = 768
    n_in = prm['w_in'].shape[1]
    npad = _round_up(n_in, tn)
    nqk = (A_HEADS + A_KV_HEADS) * HEAD_DIM
    w = _pad_cols(prm['w_in'], npad).astype(BF16)
    cgain, cflag = _head_cols(prm['q_norm'], A_HEADS, prm['k_norm'], A_KV_HEADS, npad)
    proj = ln_proj(x, prm['norm'], sc1, sh1, w, cgain, cflag, -(-nqk // tn), tn)
    o_k = A_HEADS * HEAD_DIM
    o_v = o_k + A_KV_HEADS * HEAD_DIM
    o_qi = o_v + A_KV_HEADS * HEAD_DIM
    o_ki = o_qi + IDX_HEADS * IDX_DIM
    k_new = proj[..., o_k:o_v].reshape(bsz, t, A_KV_HEADS, HEAD_DIM)
    v_new = proj[..., o_v:o_qi].reshape(bsz, t, A_KV_HEADS, HEAD_DIM)
    ki_new = proj[..., o_ki:o_ki + IDX_DIM]
    lp = _round_up(n_keys, KEY_TILE)
    kvw = A_KV_HEADS * HEAD_DIM
    if p:
        keys = _pad_rows(jnp.concatenate([past_k.reshape(bsz, p, kvw), proj[..., o_k:o_v]], axis=1), lp)
        values = _pad_rows(jnp.concatenate([past_v.reshape(bsz, p, kvw), proj[..., o_v:o_qi]], axis=1), lp)
        ikeys = _pad_rows(jnp.concatenate([past_ki, ki_new], axis=1), lp)
        ikeys = jnp.pad(ikeys, ((0, 0), (0, 0), (0, LANES - IDX_DIM)))
        k_blk0, v_blk0, ki_blk = 0, 0, 0
    else:
        keys = values = ikeys = proj
        k_blk0, v_blk0, ki_blk = o_k // LANES, o_v // LANES, o_ki // LANES
    mask = dsa_index_mask(proj, ikeys, lp=lp, t=t, q_off=p, topk=topk, qi_blk=o_qi // (4 * LANES),
                          w_blk=o_ki // LANES, ki_blk=ki_blk)
    o = dsa_attention(proj, keys, values, mask, bdiag, bsub, lp=lp, t=t, q_off=p, k_blk0=k_blk0, v_blk0=v_blk0)
    x = out_proj_residual(o, prm['w_out'].astype(BF16), x, g1)
    return x, (k_new, v_new, ki_new)


def _fox_layer(x, mod, past, prm):
    sh1, sc1, g1 = mod
    past_k, past_v, past_lf = past
    bsz, t, d = x.shape
    p = 0 if past_k is None else past_k.shape[1]
    n_keys = p + t
    hd = B_HEADS * HEAD_DIM
    tn = 512
    npad = _round_up(prm['w_in'].shape[1], tn)
    w = _pad_cols(prm['w_in'], npad).astype(BF16)
    cgain, cflag = _head_cols(prm['q_norm'], B_HEADS, prm['k_norm'], B_HEADS, npad)
    proj = ln_proj(x, prm['norm'], sc1, sh1, w, cgain, cflag, 2 * hd // tn, tn)
    k_new = proj[..., hd:2 * hd]
    v_new = proj[..., 2 * hd:3 * hd]
    fz = proj[..., 4 * hd:4 * hd + B_HEADS]
    tp = _round_up(t, LANES)
    fz_t = _pad_rows(fz, tp).transpose(0, 2, 1)
    past_t = None if not p else past_lf.transpose(0, 2, 1)
    lf_t, nck = fox_decay(fz_t, prm['forget_bias'], past_t, t=t)
    logf_new = lf_t[:, :, :t].transpose(0, 2, 1)
    lp = _round_up(n_keys, KEY_TILE)
    if p:
        keys = _pad_rows(jnp.concatenate([past_k.reshape(bsz, p, hd), k_new], axis=1), lp)
        values = _pad_rows(jnp.concatenate([past_v.reshape(bsz, p, hd), v_new], axis=1), lp)
        k_blk0, v_blk0 = 0, 0
    else:
        keys = values = proj
        k_blk0, v_blk0 = hd // LANES, 2 * hd // LANES
    pieces = jnp.pad(nck, ((0, 0), (0, 0), (0, 0), (0, lp - nck.shape[3]))).transpose(0, 3, 2, 1)
    zl = jnp.zeros((bsz, lp, HEAD_DIM - 3 * B_HEADS // 2), pieces.dtype)
    pz = jnp.concatenate([pieces[:, :, 1::2].reshape(bsz, lp, -1), zl,
                          pieces[:, :, 0::2].reshape(bsz, lp, -1), zl], axis=-1)
    o = fox_attention(proj, keys, values, pz, lp=lp, t=t, q_off=p, g_blk0=3 * hd // LANES,
                      k_blk0=k_blk0, v_blk0=v_blk0)
    x = out_proj_residual(o, prm['w_out'].astype(BF16), x, g1)
    return x, (k_new.reshape(bsz, t, B_HEADS, HEAD_DIM), v_new.reshape(bsz, t, B_HEADS, HEAD_DIM), logf_new)


def _hgrn2_layer(x, mod, s0, prm):
    sh1, sc1, g1 = mod
    bsz, t, d = x.shape
    npad = prm['w_in'].shape[1]
    zeros = jnp.zeros((1, npad), F32)
    proj = ln_proj(x, prm['norm'], sc1, sh1, prm['w_in'].astype(BF16), zeros, zeros, 0, 512)
    y, s_t = hgrn2_recurrence(proj, prm['lb'], prm['out_norm'], jnp.swapaxes(s0, -1, -2), t=t)
    x = out_proj_residual(y, prm['w_out'].astype(BF16), x, g1)
    return x, jnp.swapaxes(s_t, -1, -2)


def _trunk(x, c, a_k, a_v, a_kidx, b_k, b_v, b_logf, c_state, prm):
    bsz, t, d = x.shape
    mod_all = ada_mod(c, prm['w_ada'], prm['b_ada'])
    lb_all = jnp.cumsum(jax.nn.softmax(prm['c_lower_bound'].astype(F32), axis=0), axis=0)
    lb_all = lb_all - lb_all[0]
    bdiag, bsub = dsa_bias_tiles(prm['rel_table'])
    out_a, out_b, out_c = [], [], []
    for i in range(DEPTH):
        j = i // N_MIXERS
        kind = i % N_MIXERS
        sh1, sc1, g1, sh2, sc2, g2 = [m.reshape(bsz, 1, d) for m in jnp.split(mod_all[i], 6, axis=-1)]
        mod = (sh1, sc1, g1)
        if kind == 0:
            past = (None, None, None) if a_k is None else (a_k[j], a_v[j], a_kidx[j])
            lp = dict(norm=prm['norm_mix'][i], w_in=prm['a_w_in'][j], q_norm=prm['a_q_norm'][j],
                      k_norm=prm['a_k_norm'][j], w_out=prm['a_w_out'][j])
            x, new = _dsa_layer(x, mod, past, lp, bdiag, bsub)
            out_a.append(new)
        elif kind == 1:
            past = (None, None, None) if b_k is None else (b_k[j], b_v[j], b_logf[j])
            lp = dict(norm=prm['norm_mix'][i], w_in=prm['b_w_in'][j], forget_bias=prm['b_forget_bias'][j],
                      q_norm=prm['b_q_norm'][j], k_norm=prm['b_k_norm'][j], w_out=prm['b_w_out'][j])
            x, new = _fox_layer(x, mod, past, lp)
            out_b.append(new)
        else:
            s0 = jnp.zeros((bsz, C_HEADS, C_DK, C_DV), F32) if c_state is None else c_state[j]
            lp = dict(norm=prm['norm_mix'][i], w_in=prm['c_w_in'][j], lb=lb_all[i],
                      out_norm=prm['c_out_norm'][j], w_out=prm['c_w_out'][j])
            x, new = _hgrn2_layer(x, mod, s0, lp)
            out_c.append(new)
        w_router = jnp.pad(jnp.concatenate([prm['moe_w_group'][i], prm['moe_w_expert'][i]], axis=1).T,
                           ((0, ROUTER_ROWS - N_GROUPS - N_EXPERTS), (0, 0)))
        b_router = jnp.pad(jnp.concatenate([prm['moe_b_group'][i], prm['moe_b_expert'][i]]),
                           (0, ROUTER_ROWS - N_GROUPS - N_EXPERTS)).reshape(ROUTER_ROWS, 1)
        x = moe_layer(x, prm['norm_ffn'][i], sc2, sh2, g2, w_router, b_router,
                      prm['moe_w_gate'][i].astype(BF16), prm['moe_w_up'][i].astype(BF16),
                      prm['moe_w_down'][i].astype(BF16))
    stack = lambda outs, k: jnp.stack([o[k] for o in outs])
    return (x, stack(out_a, 0), stack(out_a, 1), stack(out_a, 2),
            stack(out_b, 0), stack(out_b, 1), stack(out_b, 2), jnp.stack(out_c))


def kernel(x_prompt, x_sample, cache_a_k, cache_a_v, cache_a_kidx, cache_b_k, cache_b_v, cache_b_logf, state_c,
           c_prompt, c_sample, rel_table, w_ada, b_ada, norm_mix, norm_ffn, a_w_in, a_q_norm, a_k_norm, a_w_out,
           b_w_in, b_forget_bias, b_q_norm, b_k_norm, b_w_out, c_w_in, c_lower_bound, c_out_norm, c_w_out,
           moe_w_group, moe_b_group, moe_w_expert, moe_b_expert, moe_w_gate, moe_w_up, moe_w_down):
    prm = {'rel_table': rel_table, 'w_ada': w_ada, 'b_ada': b_ada, 'norm_mix': norm_mix, 'norm_ffn': norm_ffn,
           'a_w_in': a_w_in, 'a_q_norm': a_q_norm, 'a_k_norm': a_k_norm, 'a_w_out': a_w_out,
           'b_w_in': b_w_in, 'b_forget_bias': b_forget_bias, 'b_q_norm': b_q_norm, 'b_k_norm': b_k_norm,
           'b_w_out': b_w_out, 'c_w_in': c_w_in, 'c_lower_bound': c_lower_bound, 'c_out_norm': c_out_norm,
           'c_w_out': c_w_out, 'moe_w_group': moe_w_group, 'moe_b_group': moe_b_group,
           'moe_w_expert': moe_w_expert, 'moe_b_expert': moe_b_expert, 'moe_w_gate': moe_w_gate,
           'moe_w_up': moe_w_up, 'moe_w_down': moe_w_down}
    (y_p, ak_p, av_p, ai_p, bk_p, bv_p, bl_p, cs_p) = _trunk(
        x_prompt, c_prompt, None, None, None, None, None, None, None, prm)
    (y_s, ak_s, av_s, ai_s, bk_s, bv_s, bl_s, cs_s) = _trunk(
        x_sample, c_sample, cache_a_k, cache_a_v, cache_a_kidx, cache_b_k, cache_b_v, cache_b_logf, state_c, prm)
    return (y_p, y_s, ak_p, av_p, ai_p, ak_s, av_s, ai_s, bk_p, bv_p, bl_p, bk_s, bv_s, bl_s, cs_p, cs_s)
```
